```python
import math
import jax
import jax.numpy as jnp
from jax import lax
import numpy as np

D_MODEL = 1024
BATCH = 1
SEQ = 16384
DEPTH = 2
DEC_BATCH = 8
DEC_SEQ = 64
PAST_LEN = 4096

CHUNK = 64
Q_BLOCK = 128

MLA_HEADS = 8
QK_NOPE = 64
QK_ROPE = 32
QK_HEAD = QK_NOPE + QK_ROPE
V_HEAD = 64
Q_LORA = 256
KV_LORA = 128
ROPE_THETA = 10000.0
MLA_SCALE = QK_HEAD ** -0.5
NEG_INF = -1e30

LRU_WIDTH = 512
LRU_BLOCKS = 8
LRU_BLOCK = LRU_WIDTH // LRU_BLOCKS
CONV_WIDTH = 4
RG_C = 8.0

IN_AB = Q_LORA + KV_LORA + QK_ROPE + 2 * LRU_WIDTH
AB_SPLITS = (Q_LORA, Q_LORA + KV_LORA, Q_LORA + KV_LORA + QK_ROPE,
             Q_LORA + KV_LORA + QK_ROPE + LRU_WIDTH)
MIX_AB = MLA_HEADS * V_HEAD + LRU_WIDTH

S5_WIDTH = 512
S5_GROUP = 16
S5_GROUPS = S5_WIDTH // S5_GROUP
S5_STATE = 64

N_GROUPS = 4
EXPERTS_PER_GROUP = 4
N_EXPERTS = N_GROUPS * EXPERTS_PER_GROUP
TOP_K = 2
EXPERT_HIDDEN = 256

N_AB_LAYERS = (DEPTH + 1) // 2
N_C_LAYERS = DEPTH // 2
ALPHA = (2 * DEPTH) ** 0.25
BETA = (8 * DEPTH) ** -0.25
LN_EPS = 1e-5
RMS_EPS = 1e-6

kernel_name = 'hybrid_stream_mla_rglru_s5_hmoe_step'


def layer_norm(x, g, b):
    xf = x.astype(jnp.float32)
    mu = jnp.mean(xf, -1, keepdims=True)
    var = jnp.mean(jnp.square(xf - mu), -1, keepdims=True)
    return ((xf - mu) * lax.rsqrt(var + LN_EPS)).astype(x.dtype) * g + b


def rms_norm(x, g):
    xf = x.astype(jnp.float32)
    ms = jnp.mean(jnp.square(xf), -1, keepdims=True)
    return (xf * lax.rsqrt(ms + RMS_EPS)).astype(x.dtype) * g


def rope(x, pos):
    half = x.shape[-1] // 2
    inv = ROPE_THETA ** (-jnp.arange(half, dtype=jnp.float32) / half)
    ang = pos.astype(jnp.float32)[:, None] * inv[None, :]
    ang = ang.reshape((ang.shape[0],) + (1,) * (x.ndim - 3) + (half,))
    cos, sin = jnp.cos(ang), jnp.sin(ang)
    xf = x.astype(jnp.float32)
    x1, x2 = xf[..., :half], xf[..., half:]
    return jnp.concatenate([x1 * cos - x2 * sin, x2 * cos + x1 * sin], -1).astype(x.dtype)


def attend_block(q, k, v, q_pos, k_pos):
    s = jnp.einsum('bqhd,bkhd->bhqk', q, k, preferred_element_type=jnp.float32) * MLA_SCALE
    visible = (k_pos[None, :] // CHUNK) <= (q_pos[:, None] // CHUNK)
    s = jnp.where(visible[None, None], s, NEG_INF)
    p = jax.nn.softmax(s, axis=-1).astype(v.dtype)
    return jnp.einsum('bhqk,bkhd->bqhd', p, v)


def mla(q_lat, kv_lat, kr_raw, pos, past_ckv, past_krope, q_g, kv_g, w_uq, w_ukv):
    b, t, _ = q_lat.shape
    q = (rms_norm(q_lat, q_g) @ w_uq).reshape(b, t, MLA_HEADS, QK_HEAD)
    q = jnp.concatenate([q[..., :QK_NOPE], rope(q[..., QK_NOPE:], pos)], -1)
    c_new = rms_norm(kv_lat, kv_g)
    kr_new = rope(kr_raw, pos)
    if past_ckv is None:
        c_all, kr_all = c_new, kr_new
    else:
        c_all = jnp.concatenate([past_ckv.astype(c_new.dtype), c_new], axis=1)
        kr_all = jnp.concatenate([past_krope.astype(kr_new.dtype), kr_new], axis=1)
    tk = c_all.shape[1]
    k_pos = jnp.arange(tk)
    kv = (c_all @ w_ukv).reshape(b, tk, MLA_HEADS, QK_NOPE + V_HEAD)
    k = jnp.concatenate(
        [kv[..., :QK_NOPE], jnp.broadcast_to(kr_all[:, :, None, :], (b, tk, MLA_HEADS, QK_ROPE))], -1)
    v = kv[..., QK_NOPE:]
    if t % Q_BLOCK == 0:
        nb = t // Q_BLOCK
        qb = jnp.moveaxis(q.reshape(b, nb, Q_BLOCK, MLA_HEADS, QK_HEAD), 1, 0)
        pb = pos.reshape(nb, Q_BLOCK)
        o = lax.map(lambda a: attend_block(a[0], k, v, a[1], k_pos), (qb, pb))
        o = jnp.moveaxis(o, 0, 1).reshape(b, t, MLA_HEADS * V_HEAD)
    else:
        o = attend_block(q, k, v, pos, k_pos).reshape(b, t, MLA_HEADS * V_HEAD)
    return o, c_new, kr_new


def _linear_combine(e1, e2):
    a1, b1 = e1
    a2, b2 = e2
    return a1 * a2, a2 * b1 + b2


def _complex_combine(e1, e2):
    ar1, ai1, br1, bi1 = e1
    ar2, ai2, br2, bi2 = e2
    return (ar1 * ar2 - ai1 * ai2, ar1 * ai2 + ai1 * ar2,
            ar2 * br1 - ai2 * bi1 + br2, ar2 * bi1 + ai2 * br1 + bi2)


def rg_lru(xb, yb, conv_buf, h0, conv_w, conv_b, wa, ba, wx, bx, a_param):
    b, t, w = xb.shape
    xcat = jnp.concatenate([conv_buf.astype(xb.dtype), xb], axis=1)
    xc = conv_b + xcat[:, 0:t] * conv_w[0]
    for tap in range(1, CONV_WIDTH):
        xc = xc + xcat[:, tap:tap + t] * conv_w[tap]
    xh = xc.reshape(b, t, LRU_BLOCKS, LRU_BLOCK)
    r = jax.nn.sigmoid(jnp.einsum('bthi,hij->bthj', xh, wa).reshape(b, t, w) + ba)
    i = jax.nn.sigmoid(jnp.einsum('bthi,hij->bthj', xh, wx).reshape(b, t, w) + bx)
    log_a = -RG_C * r.astype(jnp.float32) * jax.nn.softplus(-a_param.astype(jnp.float32))
    a = jnp.exp(log_a)
    u = jnp.sqrt(-jnp.expm1(2.0 * log_a)) * (i * xc).astype(jnp.float32)
    u = u.at[:, 0].add(a[:, 0] * h0.astype(jnp.float32))
    _, h = lax.associative_scan(_linear_combine, (a, u), axis=1)
    out = h.astype(xb.dtype) * jax.nn.gelu(yb)
    return out, xcat[:, t:], h[:, -1]


def mixer_ab(x, pos, past_ckv, past_krope, conv_buf, h0, w_in, q_g, kv_g, w_uq, w_ukv,
             conv_w, conv_b, wa, ba, wx, bx, a_param, w_out):
    proj = x @ w_in
    q_lat, kv_lat, kr_raw, xb, yb = jnp.split(proj, AB_SPLITS, axis=-1)
    att, c_new, kr_new = mla(q_lat, kv_lat, kr_raw, pos, past_ckv, past_krope, q_g, kv_g, w_uq, w_ukv)
    rec, conv_new, h_new = rg_lru(xb, yb, conv_buf, h0, conv_w, conv_b, wa, ba, wx, bx, a_param)
    y = jnp.concatenate([att, rec], axis=-1) @ w_out
    return y, c_new, kr_new, conv_new, h_new


def s5_ssm(u, h0_re, h0_im, lam_re, lam_im, log_step, b_re, b_im, c_re, c_im, d_skip):
    f32 = jnp.float32
    b, t, w = u.shape
    uf = u.astype(f32)
    ug = uf.reshape(b, t, S5_GROUPS, S5_GROUP)
    lr, li = lam_re.astype(f32), lam_im.astype(f32)
    dt = jnp.exp(log_step.astype(f32))[:, None]
    mag = jnp.exp(lr * dt)
    ar, ai = mag * jnp.cos(li * dt), mag * jnp.sin(li * dt)
    den = lr * lr + li * li
    nr = ar - 1.0
    cr = (nr * lr + ai * li) / den
    ci = (ai * lr - nr * li) / den
    br, bi = b_re.astype(f32), b_im.astype(f32)
    bbr = cr[..., None] * br - ci[..., None] * bi
    bbi = cr[..., None] * bi + ci[..., None] * br
    xr = jnp.einsum('btgc,gpc->btgp', ug, bbr)
    xi = jnp.einsum('btgc,gpc->btgp', ug, bbi)
    h0r, h0i = h0_re.astype(f32), h0_im.astype(f32)
    xr = xr.at[:, 0].add(ar * h0r - ai * h0i)
    xi = xi.at[:, 0].add(ar * h0i + ai * h0r)
    arb = jnp.broadcast_to(ar, xr.shape)
    aib = jnp.broadcast_to(ai, xr.shape)
    _, _, hr, hi = lax.associative_scan(_complex_combine, (arb, aib, xr, xi), axis=1)
    y = (jnp.einsum('gcp,btgp->btgc', c_re.astype(f32), hr)
         - jnp.einsum('gcp,btgp->btgc', c_im.astype(f32), hi))
    y = y.reshape(b, t, w) + d_skip.astype(f32) * uf
    return y.astype(u.dtype), hr[:, -1], hi[:, -1]


def mixer_c(x, h0_re, h0_im, w_in, lam_re, lam_im, log_step, b_re, b_im, c_re, c_im, d_skip,
            w_glu, w_out):
    u = x @ w_in
    y, hr, hi = s5_ssm(u, h0_re, h0_im, lam_re, lam_im, log_step, b_re, b_im, c_re, c_im, d_skip)
    g = jax.nn.gelu(y)
    val, gate = jnp.split(g @ w_glu, 2, axis=-1)
    return (val * jax.nn.sigmoid(gate)) @ w_out, hr, hi


def hier_moe(x, w_group, w_expert, w_gate, w_up, w_down):
    f32 = jnp.float32
    b, t, _ = x.shape
    lg = (x @ w_group).astype(f32)
    pg = jax.nn.softmax(lg, axis=-1)
    gi = jnp.argmax(lg, axis=-1)
    p_top = jnp.max(pg, axis=-1, keepdims=True)
    le = (x @ w_expert).astype(f32).reshape(b, t, N_GROUPS, EXPERTS_PER_GROUP)
    le_sel = jnp.sum(le * jax.nn.one_hot(gi, N_GROUPS, dtype=f32)[..., None], axis=2)
    top_v, top_i = lax.top_k(le_sel, TOP_K)
    wts = jax.nn.softmax(top_v, axis=-1) * p_top
    eid = gi[..., None] * EXPERTS_PER_GROUP + top_i
    gates = jnp.sum(jax.nn.one_hot(eid, N_EXPERTS, dtype=f32) * wts[..., None], axis=-2).astype(x.dtype)
    h = jnp.einsum('btd,edf->btef', x, w_gate)
    u = jnp.einsum('btd,edf->btef', x, w_up)
    act = jax.nn.silu(h) * u * gates[..., None]
    return jnp.einsum('btef,efd->btd', act, w_down)


def setup_inputs(seed: int = 0) -> dict:
    key = jax.random.key(seed)
    ks = iter(jax.random.split(key, 64))

    def nrm(shape, scale=1.0):
        return scale * jax.random.normal(next(ks), shape, jnp.float32)

    def gain(shape):
        return 1.0 + nrm(shape, 0.02)

    na, nc = N_AB_LAYERS, N_C_LAYERS
    x_prompt = nrm((BATCH, SEQ, D_MODEL))
    x_sample = nrm((DEC_BATCH, DEC_SEQ, D_MODEL))
    cache_mla_ckv = nrm((na, DEC_BATCH, PAST_LEN, KV_LORA))
    cache_mla_krope = nrm((na, DEC_BATCH, PAST_LEN, QK_ROPE))
    state_lru_conv = nrm((na, DEC_BATCH, CONV_WIDTH - 1, LRU_WIDTH))
    state_lru_h = nrm((na, DEC_BATCH, LRU_WIDTH), 0.5)
    state_s5_re = nrm((nc, DEC_BATCH, S5_GROUPS, S5_STATE), 0.3)
    state_s5_im = nrm((nc, DEC_BATCH, S5_GROUPS, S5_STATE), 0.3)

    w_in_ab = nrm((na, D_MODEL, IN_AB), D_MODEL ** -0.5)
    q_norm_g = gain((na, Q_LORA))
    kv_norm_g = gain((na, KV_LORA))
    w_uq = nrm((na, Q_LORA, MLA_HEADS * QK_HEAD), Q_LORA ** -0.5)
    w_ukv = nrm((na, KV_LORA, MLA_HEADS * (QK_NOPE + V_HEAD)), KV_LORA ** -0.5)
    lru_conv_w = nrm((na, CONV_WIDTH, LRU_WIDTH), CONV_WIDTH ** -0.5)
    lru_conv_b = nrm((na, LRU_WIDTH), 0.02)
    lru_w_a = nrm((na, LRU_BLOCKS, LRU_BLOCK, LRU_BLOCK), LRU_BLOCK ** -0.5)
    lru_b_a = nrm((na, LRU_WIDTH), 0.02)
    lru_w_x = nrm((na, LRU_BLOCKS, LRU_BLOCK, LRU_BLOCK), LRU_BLOCK ** -0.5)
    lru_b_x = nrm((na, LRU_WIDTH), 0.02)
    a0 = jax.random.uniform(next(ks), (na, LRU_WIDTH), jnp.float32, 0.9, 0.999)
    lru_a_param = jnp.log(a0) - jnp.log1p(-a0)
    w_out_ab = nrm((na, MIX_AB, D_MODEL), BETA * math.sqrt(2.0 / (MIX_AB + D_MODEL)))

    w_in_c = nrm((nc, D_MODEL, S5_WIDTH), D_MODEL ** -0.5)
    s5_lam_re = -0.5 * jnp.exp(nrm((nc, S5_GROUPS, S5_STATE), 0.02))
    s5_lam_im = math.pi * jnp.arange(S5_STATE, dtype=jnp.float32) + nrm((nc, S5_GROUPS, S5_STATE), 0.01)
    s5_log_step = jax.random.uniform(next(ks), (nc, S5_GROUPS), jnp.float32,
                                     math.log(1e-3), math.log(1e-1))
    s5_b_re = nrm((nc, S5_GROUPS, S5_STATE, S5_GROUP), (2 * S5_GROUP) ** -0.5)
    s5_b_im = nrm((nc, S5_GROUPS, S5_STATE, S5_GROUP), (2 * S5_GROUP) ** -0.5)
    s5_c_re = nrm((nc, S5_GROUPS, S5_GROUP, S5_STATE), S5_STATE ** -0.5)
    s5_c_im = nrm((nc, S5_GROUPS, S5_GROUP, S5_STATE), S5_STATE ** -0.5)
    s5_d = nrm((nc, S5_WIDTH))
    s5_w_glu = nrm((nc, S5_WIDTH, 2 * S5_WIDTH), S5_WIDTH ** -0.5)
    w_out_c = nrm((nc, S5_WIDTH, D_MODEL), BETA * math.sqrt(2.0 / (S5_WIDTH + D_MODEL)))

    ln_mix_g = gain((DEPTH, D_MODEL))
    ln_mix_b = nrm((DEPTH, D_MODEL), 0.02)
    ln_ffn_g = gain((DEPTH, D_MODEL))
    ln_ffn_b = nrm((DEPTH, D_MODEL), 0.02)
    moe_w_group = nrm((DEPTH, D_MODEL, N_GROUPS), D_MODEL ** -0.5)
    moe_w_expert = nrm((DEPTH, D_MODEL, N_EXPERTS), D_MODEL ** -0.5)
    moe_w_gate = nrm((DEPTH, N_EXPERTS, D_MODEL, EXPERT_HIDDEN), D_MODEL ** -0.5)
    moe_w_up = nrm((DEPTH, N_EXPERTS, D_MODEL, EXPERT_HIDDEN), D_MODEL ** -0.5)
    moe_w_down = nrm((DEPTH, N_EXPERTS, EXPERT_HIDDEN, D_MODEL),
                     BETA * math.sqrt(2.0 / (EXPERT_HIDDEN + D_MODEL)))
    return {
        'x_prompt': x_prompt, 'x_sample': x_sample,
        'cache_mla_ckv': cache_mla_ckv, 'cache_mla_krope': cache_mla_krope,
        'state_lru_conv': state_lru_conv, 'state_lru_h': state_lru_h,
        'state_s5_re': state_s5_re, 'state_s5_im': state_s5_im,
        'w_in_ab': w_in_ab, 'q_norm_g': q_norm_g, 'kv_norm_g': kv_norm_g,
        'w_uq': w_uq, 'w_ukv': w_ukv,
        'lru_conv_w': lru_conv_w, 'lru_conv_b': lru_conv_b,
        'lru_w_a': lru_w_a, 'lru_b_a': lru_b_a, 'lru_w_x': lru_w_x, 'lru_b_x': lru_b_x,
        'lru_a_param': lru_a_param, 'w_out_ab': w_out_ab,
        'w_in_c': w_in_c, 's5_lam_re': s5_lam_re, 's5_lam_im': s5_lam_im,
        's5_log_step': s5_log_step, 's5_b_re': s5_b_re, 's5_b_im': s5_b_im,
        's5_c_re': s5_c_re, 's5_c_im': s5_c_im, 's5_d': s5_d,
        's5_w_glu': s5_w_glu, 'w_out_c': w_out_c,
        'ln_mix_g': ln_mix_g, 'ln_mix_b': ln_mix_b, 'ln_ffn_g': ln_ffn_g, 'ln_ffn_b': ln_ffn_b,
        'moe_w_group': moe_w_group, 'moe_w_expert': moe_w_expert,
        'moe_w_gate': moe_w_gate, 'moe_w_up': moe_w_up, 'moe_w_down': moe_w_down,
    }


def reference(x_prompt, x_sample, cache_mla_ckv, cache_mla_krope, state_lru_conv, state_lru_h,
              state_s5_re, state_s5_im, w_in_ab, q_norm_g, kv_norm_g, w_uq, w_ukv,
              lru_conv_w, lru_conv_b, lru_w_a, lru_b_a, lru_w_x, lru_b_x, lru_a_param, w_out_ab,
              w_in_c, s5_lam_re, s5_lam_im, s5_log_step, s5_b_re, s5_b_im, s5_c_re, s5_c_im,
              s5_d, s5_w_glu, w_out_c, ln_mix_g, ln_mix_b, ln_ffn_g, ln_ffn_b,
              moe_w_group, moe_w_expert, moe_w_gate, moe_w_up, moe_w_down):
    f32 = jnp.float32
    bp, tp = x_prompt.shape[0], x_prompt.shape[1]
    bs, ts = x_sample.shape[0], x_sample.shape[1]
    past = cache_mla_ckv.shape[2]
    pos_p = jnp.arange(tp)
    pos_s = past + jnp.arange(ts)
    hp, hs = x_prompt, x_sample
    ckv_p, ckv_s, kr_p, kr_s, cv_p, cv_s, lh_p, lh_s = [], [], [], [], [], [], [], []
    s5r_p, s5r_s, s5i_p, s5i_s = [], [], [], []
    for layer in range(DEPTH):
        j = layer // 2
        if layer % 2 == 0:
            wab = (w_in_ab[j], q_norm_g[j], kv_norm_g[j], w_uq[j], w_ukv[j], lru_conv_w[j],
                   lru_conv_b[j], lru_w_a[j], lru_b_a[j], lru_w_x[j], lru_b_x[j],
                   lru_a_param[j], w_out_ab[j])
            yp, c1, k1, v1, h1 = mixer_ab(
                hp, pos_p, None, None,
                jnp.zeros((bp, CONV_WIDTH - 1, LRU_WIDTH), hp.dtype),
                jnp.zeros((bp, LRU_WIDTH), f32), *wab)
            ys, c2, k2, v2, h2 = mixer_ab(
                hs, pos_s, cache_mla_ckv[j], cache_mla_krope[j],
                state_lru_conv[j], state_lru_h[j], *wab)
            ckv_p.append(c1); ckv_s.append(c2)
            kr_p.append(k1); kr_s.append(k2)
            cv_p.append(v1); cv_s.append(v2)
            lh_p.append(h1); lh_s.append(h2)
        else:
            wc = (w_in_c[j], s5_lam_re[j], s5_lam_im[j], s5_log_step[j], s5_b_re[j], s5_b_im[j],
                  s5_c_re[j], s5_c_im[j], s5_d[j], s5_w_glu[j], w_out_c[j])
            zero_state = jnp.zeros((bp, S5_GROUPS, S5_STATE), f32)
            yp, r1, i1 = mixer_c(hp, zero_state, zero_state, *wc)
            ys, r2, i2 = mixer_c(hs, state_s5_re[j], state_s5_im[j], *wc)
            s5r_p.append(r1); s5r_s.append(r2)
            s5i_p.append(i1); s5i_s.append(i2)
        hp = layer_norm(ALPHA * hp + yp, ln_mix_g[layer], ln_mix_b[layer])
        hs = layer_norm(ALPHA * hs + ys, ln_mix_g[layer], ln_mix_b[layer])
        wm = (moe_w_group[layer], moe_w_expert[layer], moe_w_gate[layer], moe_w_up[layer],
              moe_w_down[layer])
        hp = layer_norm(ALPHA * hp + hier_moe(hp, *wm), ln_ffn_g[layer], ln_ffn_b[layer])
        hs = layer_norm(ALPHA * hs + hier_moe(hs, *wm), ln_ffn_g[layer], ln_ffn_b[layer])
    new_ckv_prompt = jnp.stack(ckv_p)
    new_ckv_sample = jnp.stack(ckv_s)
    new_krope_prompt = jnp.stack(kr_p)
    new_krope_sample = jnp.stack(kr_s)
    new_conv_prompt = jnp.stack(cv_p)
    new_conv_sample = jnp.stack(cv_s)
    new_lru_h_prompt = jnp.stack(lh_p)
    new_lru_h_sample = jnp.stack(lh_s)
    new_s5_re_prompt = jnp.stack(s5r_p)
    new_s5_re_sample = jnp.stack(s5r_s)
    new_s5_im_prompt = jnp.stack(s5i_p)
    new_s5_im_sample = jnp.stack(s5i_s)
    return (hp, hs, new_ckv_prompt, new_ckv_sample, new_krope_prompt, new_krope_sample,
            new_conv_prompt, new_conv_sample, new_lru_h_prompt, new_lru_h_sample,
            new_s5_re_prompt, new_s5_re_sample, new_s5_im_prompt, new_s5_im_sample)
```

```python
import functools
import math

import jax
import jax.numpy as jnp
from jax import lax
from jax.experimental import pallas as pl
from jax.experimental.pallas import tpu as pltpu

F32 = jnp.float32
BF16 = jnp.bfloat16

CHUNK = 64
MLA_HEADS = 8
QK_NOPE = 64
QK_ROPE = 32
QK_HEAD = QK_NOPE + QK_ROPE
V_HEAD = 64
Q_LORA = 256
KV_LORA = 128
ROPE_THETA = 10000.0
MLA_SCALE = QK_HEAD ** -0.5
NEG_INF = -1e30
LRU_WIDTH = 512
LRU_BLOCKS = 8
LRU_BLOCK = LRU_WIDTH // LRU_BLOCKS
CONV_WIDTH = 4
RG_C = 8.0
S5_WIDTH = 512
S5_GROUP = 16
S5_GROUPS = S5_WIDTH // S5_GROUP
S5_STATE = 64
N_GROUPS = 4
EXPERTS_PER_GROUP = 4
N_EXPERTS = N_GROUPS * EXPERTS_PER_GROUP
EXPERT_HIDDEN = 256
LN_EPS = 1e-5
RMS_EPS = 1e-6

LANES = 128
SUBLANES = 8
VMEM_LIMIT_BYTES = 48 * 1024 * 1024

HEAD_PAD = LANES
ROW_TILE = 512
ATTN_TQ = 256
ATTN_TK = 256
LRU_TILE = 512
S5_L = 16
S5_PAIRS = S5_GROUPS // 2
S5_PAIR_IN = 2 * S5_L * S5_GROUP
S5_STATE_W = S5_GROUPS * S5_STATE
S5_SCAN_LANES = 512
ROUTER_W = LANES


def _cparams(semantics):
    return pltpu.CompilerParams(dimension_semantics=semantics,
                                vmem_limit_bytes=VMEM_LIMIT_BYTES)


def _full_spec(a):
    nd = a.ndim
    return pl.BlockSpec(a.shape, lambda *_: (0,) * nd)


def _row_spec(tm, width):
    return pl.BlockSpec((tm, width), lambda i: (i, 0))


def _layer_norm(z, g, b):
    mu = jnp.mean(z, axis=-1, keepdims=True)
    zc = z - mu
    var = jnp.mean(zc * zc, axis=-1, keepdims=True)
    return zc * lax.rsqrt(var + LN_EPS) * g + b


def _rms_norm(z, g):
    ms = jnp.mean(z * z, axis=-1, keepdims=True)
    return z * lax.rsqrt(ms + RMS_EPS) * g


def _bdot(a, b):
    return jnp.dot(a.astype(BF16), b.astype(BF16), preferred_element_type=F32)


def _ab_in_kernel(x_ref, win_ref, qg_ref, kvg_ref, wuq_ref, wuk_ref, wuv_ref,
                  rc_ref, ra_ref, rb_ref,
                  q_out, k_out, v_out, c_out, kr_out, xb_out, yb_out, *, dm):
    proj = _bdot(x_ref[...], win_ref[...])
    o0 = Q_LORA
    o1 = o0 + KV_LORA
    o2 = o1 + LRU_WIDTH
    o3 = o2 + LRU_WIDTH
    q_lat = proj[:, 0:o0]
    kv_lat = proj[:, o0:o1]
    xb_out[...] = proj[:, o1:o2]
    yb_out[...] = proj[:, o2:o3]
    krp = proj[:, o3:o3 + HEAD_PAD]

    qn = _rms_norm(q_lat, qg_ref[...])
    cn = _rms_norm(kv_lat, kvg_ref[...])
    c_out[...] = cn
    q = _bdot(qn, wuq_ref[...])
    kk = _bdot(cn, wuk_ref[...])
    v_out[...] = _bdot(cn, wuv_ref[...]).astype(v_out.dtype)

    rc = rc_ref[...]
    ra = ra_ref[...]
    rb = rb_ref[...]
    half = QK_ROPE // 2

    def rope(z):
        return (z * rc + pltpu.roll(z, half, 1) * ra
                + pltpu.roll(z, HEAD_PAD - half, 1) * rb)

    kr = rope(krp)
    kr_out[...] = kr[:, QK_NOPE:QK_NOPE + QK_ROPE]
    for h in range(MLA_HEADS):
        sl = slice(h * HEAD_PAD, (h + 1) * HEAD_PAD)
        q_out[:, sl] = (rope(q[:, sl]) * MLA_SCALE).astype(q_out.dtype)
        k_out[:, sl] = (kk[:, sl] + kr).astype(k_out.dtype)


def _ab_in(x, w, rope_tabs):
    n, dm = x.shape
    tm = min(ROW_TILE, n)
    rc, ra, rb = rope_tabs
    consts = [w['w_in'], w['q_g'], w['kv_g'], w['w_uq'], w['w_uk'], w['w_uv']]
    hp = MLA_HEADS * HEAD_PAD
    out_shape = [
        jax.ShapeDtypeStruct((n, hp), BF16),
        jax.ShapeDtypeStruct((n, hp), BF16),
        jax.ShapeDtypeStruct((n, MLA_HEADS * V_HEAD), BF16),
        jax.ShapeDtypeStruct((n, KV_LORA), F32),
        jax.ShapeDtypeStruct((n, QK_ROPE), F32),
        jax.ShapeDtypeStruct((n, LRU_WIDTH), F32),
        jax.ShapeDtypeStruct((n, LRU_WIDTH), F32),
    ]
    return pl.pallas_call(
        functools.partial(_ab_in_kernel, dm=dm),
        grid=(n // tm,),
        in_specs=([_row_spec(tm, dm)] + [_full_spec(c) for c in consts]
                  + [_row_spec(tm, HEAD_PAD)] * 3),
        out_specs=[_row_spec(tm, s.shape[1]) for s in out_shape],
        out_shape=out_shape,
        compiler_params=_cparams(("parallel",)),
    )(x, *consts, rc, ra, rb)


def _attn_prompt_kernel(q_ref, k_ref, v_ref, o_ref):
    i = pl.program_id(1)
    tq, tk = ATTN_TQ, ATTN_TK
    row_chunk = lax.broadcasted_iota(jnp.int32, (tq, tk), 0) // CHUNK
    col_chunk = lax.broadcasted_iota(jnp.int32, (tq, tk), 1) // CHUNK
    visible = col_chunk <= row_chunk
    outs = []
    for hh in range(2):
        q = q_ref[:, hh * HEAD_PAD:(hh + 1) * HEAD_PAD]

        def scores(j):
            start = pl.multiple_of(j * tk, tk)
            k = k_ref[pl.ds(start, tk), hh * HEAD_PAD:(hh + 1) * HEAD_PAD]
            v = v_ref[pl.ds(start, tk), hh * V_HEAD:(hh + 1) * V_HEAD]
            s = lax.dot_general(q, k, (((1,), (1,)), ((), ())),
                                preferred_element_type=F32)
            return s, v

        s, v = scores(i)
        s = jnp.where(visible, s, NEG_INF)
        m = jnp.max(s, axis=-1, keepdims=True)
        p = jnp.exp(s - m)
        l = jnp.sum(p, axis=-1, keepdims=True)
        acc = jnp.dot(p.astype(BF16), v, preferred_element_type=F32)

        def body(j, carry):
            m, l, acc = carry
            s, v = scores(j)
            m_new = jnp.maximum(m, jnp.max(s, axis=-1, keepdims=True))
            alpha = jnp.exp(m - m_new)
            p = jnp.exp(s - m_new)
            l = alpha * l + jnp.sum(p, axis=-1, keepdims=True)
            acc = alpha * acc + jnp.dot(p.astype(BF16), v, preferred_element_type=F32)
            return m_new, l, acc

        m, l, acc = lax.fori_loop(0, i, body, (m, l, acc))
        outs.append(acc / l)
    o_ref[...] = jnp.concatenate(outs, axis=-1).astype(o_ref.dtype)


def _attn_prompt(q, k, v):
    t = q.shape[0]
    pairs = MLA_HEADS // 2
    return pl.pallas_call(
        _attn_prompt_kernel,
        grid=(pairs, t // ATTN_TQ),
        in_specs=[
            pl.BlockSpec((ATTN_TQ, 2 * HEAD_PAD), lambda p, i: (i, p)),
            pl.BlockSpec((t, 2 * HEAD_PAD), lambda p, i: (0, p)),
            pl.BlockSpec((t, 2 * V_HEAD), lambda p, i: (0, p)),
        ],
        out_specs=pl.BlockSpec((ATTN_TQ, 2 * V_HEAD), lambda p, i: (i, p)),
        out_shape=jax.ShapeDtypeStruct((t, MLA_HEADS * V_HEAD), BF16),
        compiler_params=_cparams(("parallel", "parallel")),
    )(q, k, v)


def _attn_sample_kernel(q_ref, cn_ref, krn_ref, cp_ref, krp_ref, wukt_ref, wuv_ref, o_ref):
    cp = cp_ref[...].astype(BF16)
    krp = krp_ref[...].astype(BF16)
    cn = cn_ref[...].astype(BF16)
    krn = krn_ref[...].astype(BF16)
    ts = cn.shape[0]
    dn = (((1,), (1,)), ((), ()))
    outs = []
    for hp in range(MLA_HEADS // 2):
        qa, qr = [], []
        for hh in range(2):
            h = 2 * hp + hh
            qh = q_ref[:, h * HEAD_PAD:(h + 1) * HEAD_PAD]
            qa.append(jnp.dot(qh[:, :QK_NOPE], wukt_ref[h], preferred_element_type=F32))
            qr.append(qh[:, QK_NOPE:QK_NOPE + QK_ROPE])
        qa = jnp.concatenate(qa, axis=0).astype(BF16)
        qr = jnp.concatenate(qr, axis=0)
        s_past = (lax.dot_general(qa, cp, dn, preferred_element_type=F32)
                  + lax.dot_general(qr, krp, dn, preferred_element_type=F32))
        s_new = (lax.dot_general(qa, cn, dn, preferred_element_type=F32)
                 + lax.dot_general(qr, krn, dn, preferred_element_type=F32))
        m = jnp.maximum(jnp.max(s_past, axis=-1, keepdims=True),
                        jnp.max(s_new, axis=-1, keepdims=True))
        p_past = jnp.exp(s_past - m)
        p_new = jnp.exp(s_new - m)
        l = (jnp.sum(p_past, axis=-1, keepdims=True)
             + jnp.sum(p_new, axis=-1, keepdims=True))
        o_lat = (jnp.dot(p_past.astype(BF16), cp, preferred_element_type=F32)
                 + jnp.dot(p_new.astype(BF16), cn, preferred_element_type=F32)) / l
        for hh in range(2):
            h = 2 * hp + hh
            outs.append(_bdot(o_lat[hh * ts:(hh + 1) * ts], wuv_ref[h]))
    o_ref[...] = jnp.concatenate(outs, axis=-1).astype(o_ref.dtype)


def _attn_sample(q, c_new, kr_new, c_past, kr_past, wukt, wuv):
    bs, past, _ = c_past.shape
    ts = q.shape[0] // bs
    return pl.pallas_call(
        _attn_sample_kernel,
        grid=(bs,),
        in_specs=[
            pl.BlockSpec((ts, MLA_HEADS * HEAD_PAD), lambda b: (b, 0)),
            pl.BlockSpec((ts, KV_LORA), lambda b: (b, 0)),
            pl.BlockSpec((ts, QK_ROPE), lambda b: (b, 0)),
            pl.BlockSpec((None, past, KV_LORA), lambda b: (b, 0, 0)),
            pl.BlockSpec((None, past, QK_ROPE), lambda b: (b, 0, 0)),
            _full_spec(wukt), _full_spec(wuv),
        ],
        out_specs=pl.BlockSpec((ts, MLA_HEADS * V_HEAD), lambda b: (b, 0)),
        out_shape=jax.ShapeDtypeStruct((bs * ts, MLA_HEADS * V_HEAD), BF16),
        compiler_params=_cparams(("parallel",)),
    )(q, c_new, kr_new, c_past, kr_past, wukt, wuv)


def _lru_kernel(xb_ref, yb_ref, conv0_ref, h0_ref, cw_ref, cb_ref, wax_ref, bax_ref, ap_ref,
                rec_ref, conv_out_ref, h_out_ref, xcat, hc, *, tb):
    t = pl.program_id(1)
    nt = pl.num_programs(1)
    tail = CONV_WIDTH - 1
    base = SUBLANES

    @pl.when(t == 0)
    def _():
        xcat[base - tail:base, :] = conv0_ref[...]
        hc[...] = h0_ref[...]

    @pl.when(t > 0)
    def _():
        xcat[0:base, :] = xcat[tb:tb + base, :]

    xcat[base:base + tb, :] = xb_ref[...]
    xc = cb_ref[...]
    for tap in range(CONV_WIDTH):
        xc = xc + xcat[base - tail + tap:base - tail + tap + tb, :] * cw_ref[tap:tap + 1, :]

    gates = _bdot(xc, wax_ref[...]) + bax_ref[...]
    r = jax.nn.sigmoid(gates[:, :LRU_WIDTH])
    ig = jax.nn.sigmoid(gates[:, LRU_WIDTH:])
    log_a = -RG_C * r * ap_ref[...]
    a = jnp.exp(log_a)
    u = jnp.sqrt(jnp.tanh(-log_a) * (1.0 + a * a)) * (ig * xc)

    row = lax.broadcasted_iota(jnp.int32, (tb, LRU_WIDTH), 0)
    s = 1
    while s < tb:
        keep = row >= s
        u = jnp.where(keep, a * pltpu.roll(u, s, 0) + u, u)
        a = jnp.where(keep, a * pltpu.roll(a, s, 0), a)
        s *= 2
    h = a * hc[...] + u
    hc[...] = h[tb - 1:tb, :]
    rec_ref[...] = (h * jax.nn.gelu(yb_ref[...])).astype(rec_ref.dtype)

    @pl.when(t == nt - 1)
    def _():
        conv_out_ref[...] = xcat[base + tb - tail:base + tb, :]
        h_out_ref[...] = h[tb - 1:tb, :]


def _lru(xb, yb, conv0, h0, w):
    b, t, wd = xb.shape
    tb = min(LRU_TILE, t)
    tail = CONV_WIDTH - 1
    consts = [w['conv_w'], w['conv_b'], w['w_ax'], w['b_ax'], w['a_param']]
    seq = pl.BlockSpec((None, tb, wd), lambda bi, ti: (bi, ti, 0))
    return pl.pallas_call(
        functools.partial(_lru_kernel, tb=tb),
        grid=(b, t // tb),
        in_specs=[seq, seq,
                  pl.BlockSpec((None, tail, wd), lambda bi, ti: (bi, 0, 0)),
                  pl.BlockSpec((None, 1, wd), lambda bi, ti: (bi, 0, 0))]
                 + [pl.BlockSpec(c.shape, lambda bi, ti: (0, 0)) for c in consts],
        out_specs=[seq,
                   pl.BlockSpec((None, tail, wd), lambda bi, ti: (bi, 0, 0)),
                   pl.BlockSpec((None, 1, wd), lambda bi, ti: (bi, 0, 0))],
        out_shape=[jax.ShapeDtypeStruct((b, t, wd), BF16),
                   jax.ShapeDtypeStruct((b, tail, wd), F32),
                   jax.ShapeDtypeStruct((b, 1, wd), F32)],
        scratch_shapes=[pltpu.VMEM((tb + 2 * SUBLANES, wd), F32), pltpu.VMEM((1, wd), F32)],
        compiler_params=_cparams(("parallel", "arbitrary")),
    )(xb, yb, conv0, h0, *consts)


def _out_ln_kernel(a_ref, b_ref, x_ref, wa_ref, wb_ref, g_ref, beta_ref, o_ref, *, alpha):
    y = (jnp.dot(a_ref[...], wa_ref[...], preferred_element_type=F32)
         + jnp.dot(b_ref[...], wb_ref[...], preferred_element_type=F32))
    o_ref[...] = _layer_norm(alpha * x_ref[...] + y, g_ref[...], beta_ref[...])


def _out_ln(att, rec, x, wa, wb, g, beta, alpha):
    n, dm = x.shape
    tm = min(ROW_TILE, n)
    consts = [wa, wb, g, beta]
    return pl.pallas_call(
        functools.partial(_out_ln_kernel, alpha=alpha),
        grid=(n // tm,),
        in_specs=[_row_spec(tm, att.shape[1]), _row_spec(tm, rec.shape[1]), _row_spec(tm, dm)]
                 + [_full_spec(c) for c in consts],
        out_specs=_row_spec(tm, dm),
        out_shape=jax.ShapeDtypeStruct((n, dm), F32),
        compiler_params=_cparams(("parallel",)),
    )(att, rec, x, *consts)


def _moe_ln_kernel(x_ref, wr_ref, wg_ref, wu_ref, wd_ref, g_ref, beta_ref, o_ref,
                   acc, gates, xb16, *, alpha):
    e = pl.program_id(1)
    tm = x_ref.shape[0]
    lane = lax.broadcasted_iota(jnp.int32, (tm, ROUTER_W), 1).astype(F32)

    @pl.when(e == 0)
    def _():
        x = x_ref[...]
        xb16[...] = x.astype(BF16)
        logits = jnp.dot(x, wr_ref[...], preferred_element_type=F32,
                         precision=lax.Precision.HIGHEST)
        big = float(ROUTER_W)
        lg = jnp.where(lane < N_GROUPS, logits, -jnp.inf)
        mg = jnp.max(lg, axis=-1, keepdims=True)
        gi = jnp.min(jnp.where(lg == mg, lane, big), axis=-1, keepdims=True)
        p_top = 1.0 / jnp.sum(jnp.exp(lg - mg), axis=-1, keepdims=True)
        lo = N_GROUPS + gi * EXPERTS_PER_GROUP
        le = jnp.where((lane >= lo) & (lane < lo + EXPERTS_PER_GROUP), logits, -jnp.inf)
        v1 = jnp.max(le, axis=-1, keepdims=True)
        i1 = jnp.min(jnp.where(le == v1, lane, big), axis=-1, keepdims=True)
        le2 = jnp.where(lane == i1, -jnp.inf, le)
        v2 = jnp.max(le2, axis=-1, keepdims=True)
        i2 = jnp.min(jnp.where(le2 == v2, lane, big), axis=-1, keepdims=True)
        e2 = jnp.exp(v2 - v1)
        w1 = p_top / (1.0 + e2)
        w2 = p_top * e2 / (1.0 + e2)
        gates[...] = jnp.where(lane == i1, w1, 0.0) + jnp.where(lane == i2, w2, 0.0)
        acc[...] = jnp.zeros_like(acc)

    ge = jnp.sum(jnp.where(lane == (N_GROUPS + e).astype(F32), gates[...], 0.0),
                 axis=-1, keepdims=True)
    xb = xb16[...]
    h = jnp.dot(xb, wg_ref[...], preferred_element_type=F32)
    u = jnp.dot(xb, wu_ref[...], preferred_element_type=F32)
    act = jax.nn.silu(h) * u * ge
    acc[...] += jnp.dot(act.astype(BF16), wd_ref[...], preferred_element_type=F32)

    @pl.when(e == pl.num_programs(1) - 1)
    def _():
        o_ref[...] = _layer_norm(alpha * x_ref[...] + acc[...], g_ref[...], beta_ref[...])


def _moe_ln(x, w, g, beta, alpha):
    n, dm = x.shape
    tm = min(ROW_TILE, n)
    ne, _, eh = w['w_gate'].shape
    return pl.pallas_call(
        functools.partial(_moe_ln_kernel, alpha=alpha),
        grid=(n // tm, ne),
        in_specs=[
            pl.BlockSpec((tm, dm), lambda i, e: (i, 0)),
            pl.BlockSpec(w['w_router'].shape, lambda i, e: (0, 0)),
            pl.BlockSpec((None, dm, eh), lambda i, e: (e, 0, 0)),
            pl.BlockSpec((None, dm, eh), lambda i, e: (e, 0, 0)),
            pl.BlockSpec((None, eh, dm), lambda i, e: (e, 0, 0)),
            pl.BlockSpec(g.shape, lambda i, e: (0, 0)),
            pl.BlockSpec(beta.shape, lambda i, e: (0, 0)),
        ],
        out_specs=pl.BlockSpec((tm, dm), lambda i, e: (i, 0)),
        out_shape=jax.ShapeDtypeStruct((n, dm), F32),
        scratch_shapes=[pltpu.VMEM((tm, dm), F32), pltpu.VMEM((tm, ROUTER_W), F32),
                        pltpu.VMEM((tm, dm), BF16)],
        compiler_params=_cparams(("parallel", "arbitrary")),
    )(x, w['w_router'], w['w_gate'], w['w_up'], w['w_down'], g, beta)


def _c_in_kernel(x_ref, w_ref, u_ref):
    u_ref[...] = _bdot(x_ref[...], w_ref[...])


def _c_in(x, w_in):
    n, dm = x.shape
    tm = min(ROW_TILE, n)
    return pl.pallas_call(
        _c_in_kernel,
        grid=(n // tm,),
        in_specs=[_row_spec(tm, dm), _full_spec(w_in)],
        out_specs=_row_spec(tm, w_in.shape[1]),
        out_shape=jax.ShapeDtypeStruct((n, w_in.shape[1]), F32),
        compiler_params=_cparams(("parallel",)),
    )(x, w_in)


def _s5_e_kernel(u_ref, ere_ref, eim_ref, ere_out, eim_out):
    u = u_ref[...]
    ere_out[...] = jnp.dot(u, ere_ref[...], preferred_element_type=F32)
    eim_out[...] = jnp.dot(u, eim_ref[...], preferred_element_type=F32)


def _s5_e(up, tabs):
    b, npair, nc, wp = up.shape
    two_s = 2 * S5_STATE
    out = jax.ShapeDtypeStruct((b, nc, npair * two_s), F32)
    return pl.pallas_call(
        _s5_e_kernel,
        grid=(b, npair),
        in_specs=[pl.BlockSpec((None, None, nc, wp), lambda bi, p: (bi, p, 0, 0)),
                  pl.BlockSpec((None, wp, two_s), lambda bi, p: (p, 0, 0)),
                  pl.BlockSpec((None, wp, two_s), lambda bi, p: (p, 0, 0))],
        out_specs=[pl.BlockSpec((None, nc, two_s), lambda bi, p: (bi, 0, p))] * 2,
        out_shape=[out, out],
        compiler_params=_cparams(("parallel", "parallel")),
    )(up, tabs['e_re'], tabs['e_im'])


def _s5_scan_kernel(ere_ref, eim_ref, h0r_ref, h0i_ref, stp_re_ref, stp_im_ref,
                    apw_re_ref, apw_im_ref, hpr_ref, hpi_ref, hfr_ref, hfi_ref, *, nc, last_row):
    w = ere_ref.shape[-1]
    row = lax.broadcasted_iota(jnp.int32, (SUBLANES, w), 0)
    apr = apw_re_ref[...]
    api = apw_im_ref[...]

    def tile(i, carry):
        cr, ci = carry
        start = pl.multiple_of(i * SUBLANES, SUBLANES)
        sr = ere_ref[pl.ds(start, SUBLANES), :]
        si = eim_ref[pl.ds(start, SUBLANES), :]
        for k in range(3):
            s = 1 << k
            keep = row >= s
            ar = stp_re_ref[k:k + 1, :]
            ai = stp_im_ref[k:k + 1, :]
            pr = pltpu.roll(sr, s, 0)
            pi = pltpu.roll(si, s, 0)
            sr, si = (jnp.where(keep, sr + ar * pr - ai * pi, sr),
                      jnp.where(keep, si + ar * pi + ai * pr, si))
        hr = sr + apr * cr - api * ci
        hi = si + apr * ci + api * cr
        first = row == 0
        hpr_ref[pl.ds(start, SUBLANES), :] = jnp.where(first, cr, pltpu.roll(hr, 1, 0))
        hpi_ref[pl.ds(start, SUBLANES), :] = jnp.where(first, ci, pltpu.roll(hi, 1, 0))
        return hr, hi

    def body(i, carry):
        hr, hi = tile(i, carry)
        return hr[SUBLANES - 1:SUBLANES, :], hi[SUBLANES - 1:SUBLANES, :]

    ntile = nc // SUBLANES
    carry = lax.fori_loop(0, ntile - 1, body, (h0r_ref[...], h0i_ref[...]))
    hr, hi = tile(ntile - 1, carry)
    hfr_ref[...] = hr[last_row:last_row + 1, :]
    hfi_ref[...] = hi[last_row:last_row + 1, :]


def _s5_scan(e_re, e_im, h0r, h0i, tabs, n_chunks):
    b, nc, w = e_re.shape
    last_row = (n_chunks - 1) % SUBLANES
    wb = S5_SCAN_LANES
    consts = [tabs['step_re'], tabs['step_im'], tabs['apow_re'], tabs['apow_im']]
    seq = pl.BlockSpec((None, nc, wb), lambda bi, li: (bi, 0, li))
    vec = pl.BlockSpec((None, 1, wb), lambda bi, li: (bi, 0, li))
    return pl.pallas_call(
        functools.partial(_s5_scan_kernel, nc=nc, last_row=last_row),
        grid=(b, w // wb),
        in_specs=[seq, seq, vec, vec]
                 + [pl.BlockSpec((c.shape[0], wb), lambda bi, li: (0, li)) for c in consts],
        out_specs=[seq, seq, vec, vec],
        out_shape=[jax.ShapeDtypeStruct((b, nc, w), F32)] * 2
                  + [jax.ShapeDtypeStruct((b, 1, w), F32)] * 2,
        compiler_params=_cparams(("parallel", "parallel")),
    )(e_re, e_im, h0r, h0i, *consts)


def _s5_y_kernel(u_ref, hpr_ref, hpi_ref, m_ref, fre_ref, fim_ref, y_ref):
    y_ref[...] = (jnp.dot(u_ref[...], m_ref[...], preferred_element_type=F32)
                  + _bdot(hpr_ref[...], fre_ref[...])
                  + _bdot(hpi_ref[...], fim_ref[...]))


def _s5_y(up, hp_re, hp_im, tabs):
    b, npair, nc, wp = up.shape
    two_s = 2 * S5_STATE
    return pl.pallas_call(
        _s5_y_kernel,
        grid=(b, npair),
        in_specs=[pl.BlockSpec((None, None, nc, wp), lambda bi, p: (bi, p, 0, 0)),
                  pl.BlockSpec((None, nc, two_s), lambda bi, p: (bi, 0, p)),
                  pl.BlockSpec((None, nc, two_s), lambda bi, p: (bi, 0, p)),
                  pl.BlockSpec((None, wp, wp), lambda bi, p: (p, 0, 0)),
                  pl.BlockSpec((None, two_s, wp), lambda bi, p: (p, 0, 0)),
                  pl.BlockSpec((None, two_s, wp), lambda bi, p: (p, 0, 0))],
        out_specs=pl.BlockSpec((None, None, nc, wp), lambda bi, p: (bi, p, 0, 0)),
        out_shape=jax.ShapeDtypeStruct((b, npair, nc, wp), F32),
        compiler_params=_cparams(("parallel", "parallel")),
    )(up, hp_re, hp_im, tabs['m'], tabs['f_re'], tabs['f_im'])


def _c_out_kernel(y_ref, u_ref, x_ref, d_ref, wglu_ref, wout_ref, g_ref, beta_ref, o_ref, *, alpha):
    y = y_ref[...] + d_ref[...] * u_ref[...]
    vg = _bdot(jax.nn.gelu(y), wglu_ref[...])
    half = vg.shape[1] // 2
    z = vg[:, :half] * jax.nn.sigmoid(vg[:, half:])
    o = _bdot(z, wout_ref[...])
    o_ref[...] = _layer_norm(alpha * x_ref[...] + o, g_ref[...], beta_ref[...])


def _c_out(y, u, x, w, g, beta, alpha):
    n, dm = x.shape
    tm = min(ROW_TILE, n)
    consts = [w['d'], w['w_glu'], w['w_out'], g, beta]
    return pl.pallas_call(
        functools.partial(_c_out_kernel, alpha=alpha),
        grid=(n // tm,),
        in_specs=[_row_spec(tm, y.shape[1]), _row_spec(tm, u.shape[1]), _row_spec(tm, dm)]
                 + [_full_spec(c) for c in consts],
        out_specs=_row_spec(tm, dm),
        out_shape=jax.ShapeDtypeStruct((n, dm), F32),
        compiler_params=_cparams(("parallel",)),
    )(y, u, x, *consts)


def _prep_ab(w_in, q_g, kv_g, w_uq, w_ukv, conv_w, conv_b, wa, ba, wx, bx, a_param, w_out):
    dm = w_in.shape[0]
    s0 = Q_LORA
    s1 = s0 + KV_LORA
    s2 = s1 + QK_ROPE
    s3 = s2 + LRU_WIDTH
    pad_lo = jnp.zeros((dm, QK_NOPE), F32)
    pad_hi = jnp.zeros((dm, HEAD_PAD - QK_HEAD), F32)
    w_in_p = jnp.concatenate([w_in[:, :s1], w_in[:, s2:s3], w_in[:, s3:],
                              pad_lo, w_in[:, s1:s2], pad_hi], axis=1).astype(BF16)
    wq = w_uq.reshape(Q_LORA, MLA_HEADS, QK_HEAD)
    wq = jnp.pad(wq, ((0, 0), (0, 0), (0, HEAD_PAD - QK_HEAD)))
    wq = wq.reshape(Q_LORA, MLA_HEADS * HEAD_PAD).astype(BF16)
    wkv = w_ukv.reshape(KV_LORA, MLA_HEADS, QK_NOPE + V_HEAD)
    wk = jnp.pad(wkv[:, :, :QK_NOPE], ((0, 0), (0, 0), (0, HEAD_PAD - QK_NOPE)))
    wk = wk.reshape(KV_LORA, MLA_HEADS * HEAD_PAD).astype(BF16)
    wv = wkv[:, :, QK_NOPE:].reshape(KV_LORA, MLA_HEADS * V_HEAD).astype(BF16)
    wukt = jnp.transpose(wkv[:, :, :QK_NOPE], (1, 2, 0)).astype(BF16)
    wuv_h = jnp.transpose(wkv[:, :, QK_NOPE:], (1, 0, 2)).astype(BF16)
    eye = jnp.eye(LRU_BLOCKS, dtype=F32)

    def blockdiag(wb):
        return jnp.einsum('hij,hk->hikj', wb, eye).reshape(LRU_WIDTH, LRU_WIDTH)

    w_ax = jnp.concatenate([blockdiag(wa), blockdiag(wx)], axis=1).astype(BF16)
    b_ax = jnp.concatenate([ba, bx])[None, :]
    att_w = MLA_HEADS * V_HEAD
    return dict(w_in=w_in_p, q_g=q_g[None, :], kv_g=kv_g[None, :], w_uq=wq, w_uk=wk, w_uv=wv,
                wukt=wukt, wuv_h=wuv_h, conv_w=conv_w, conv_b=conv_b[None, :], w_ax=w_ax,
                b_ax=b_ax, a_param=jax.nn.softplus(-a_param)[None, :],
                w_out_att=w_out[:att_w].astype(BF16), w_out_rec=w_out[att_w:].astype(BF16))


def _rope_tables(pos):
    half = QK_ROPE // 2
    inv = ROPE_THETA ** (-jnp.arange(half, dtype=F32) / half)
    ang = pos.astype(F32)[:, None] * inv[None, :]
    cos, sin = jnp.cos(ang), jnp.sin(ang)
    n = pos.shape[0]
    z = lambda k: jnp.zeros((n, k), F32)
    rc = jnp.concatenate([jnp.ones((n, QK_NOPE), F32), cos, cos, z(HEAD_PAD - QK_HEAD)], axis=1)
    ra = jnp.concatenate([z(QK_NOPE + half), sin, z(HEAD_PAD - QK_HEAD)], axis=1)
    rb = jnp.concatenate([z(QK_NOPE), -sin, z(half + HEAD_PAD - QK_HEAD)], axis=1)
    return rc, ra, rb


def _prep_moe(w_group, w_expert, w_gate, w_up, w_down):
    dm = w_group.shape[0]
    pad = jnp.zeros((dm, ROUTER_W - N_GROUPS - N_EXPERTS), F32)
    w_router = jnp.concatenate([w_group, w_expert, pad], axis=1)
    return dict(w_router=w_router, w_gate=w_gate.astype(BF16), w_up=w_up.astype(BF16),
                w_down=w_down.astype(BF16))


def _cmul(ar, ai, br, bi):
    return ar * br - ai * bi, ar * bi + ai * br


def _prep_s5(lam_re, lam_im, log_step, b_re, b_im, c_re, c_im):
    hi = lax.Precision.HIGHEST
    L, G, P, C = S5_L, S5_GROUPS, S5_STATE, S5_GROUP
    lr, li = lam_re, lam_im
    dt = jnp.exp(log_step)[:, None]
    mag = jnp.exp(lr * dt)
    ar, ai = mag * jnp.cos(li * dt), mag * jnp.sin(li * dt)
    den = lr * lr + li * li
    nr = ar - 1.0
    cr = (nr * lr + ai * li) / den
    ci = (ai * lr - nr * li) / den
    bbr = cr[..., None] * b_re - ci[..., None] * b_im
    bbi = cr[..., None] * b_im + ci[..., None] * b_re
    prs, pis = [jnp.ones_like(ar)], [jnp.zeros_like(ar)]
    for _ in range(L):
        nr_, ni_ = _cmul(prs[-1], pis[-1], ar, ai)
        prs.append(nr_)
        pis.append(ni_)
    pr = jnp.stack(prs)
    pi = jnp.stack(pis)
    xr = pr[:L, :, :, None] * bbr - pi[:L, :, :, None] * bbi
    xi = pr[:L, :, :, None] * bbi + pi[:L, :, :, None] * bbr
    kk = (jnp.einsum('gcp,kgpd->kgcd', c_re, xr, precision=hi)
          - jnp.einsum('gcp,kgpd->kgcd', c_im, xi, precision=hi))
    s_idx = jnp.arange(L)[:, None]
    t_idx = jnp.arange(L)[None, :]
    lag = t_idx - s_idx
    km = jnp.where((lag >= 0)[:, :, None, None, None], kk[jnp.clip(lag, 0, L - 1)], 0.0)
    m = jnp.transpose(km, (2, 0, 4, 1, 3)).reshape(G, L * C, L * C)
    e_re = jnp.transpose(xr[::-1], (1, 0, 3, 2)).reshape(G, L * C, P)
    e_im = jnp.transpose(xi[::-1], (1, 0, 3, 2)).reshape(G, L * C, P)
    p1r = jnp.transpose(pr[1:], (1, 2, 0))[:, :, :, None]
    p1i = jnp.transpose(pi[1:], (1, 2, 0))[:, :, :, None]
    cre_t = jnp.transpose(c_re, (0, 2, 1))[:, :, None, :]
    cim_t = jnp.transpose(c_im, (0, 2, 1))[:, :, None, :]
    f_re = (cre_t * p1r - cim_t * p1i).reshape(G, P, L * C)
    f_im = (-cre_t * p1i - cim_t * p1r).reshape(G, P, L * C)

    def pair_diag(a):
        g, r, c = a.shape
        a = a.reshape(g // 2, 2, r, c)
        eye2 = jnp.eye(2, dtype=a.dtype)
        return jnp.einsum('pgrc,gh->pgrhc', a, eye2).reshape(g // 2, 2 * r, 2 * c)

    alr, ali = pr[L].reshape(1, G * P), pi[L].reshape(1, G * P)
    qr, qi = [alr], [ali]
    for _ in range(SUBLANES - 1):
        nr_, ni_ = _cmul(qr[-1], qi[-1], alr, ali)
        qr.append(nr_)
        qi.append(ni_)
    apow_re = jnp.concatenate(qr, axis=0)
    apow_im = jnp.concatenate(qi, axis=0)
    step_re = jnp.concatenate([qr[0], qr[1], qr[3]], axis=0)
    step_im = jnp.concatenate([qi[0], qi[1], qi[3]], axis=0)
    return dict(m=pair_diag(m).astype(BF16), e_re=pair_diag(e_re).astype(BF16),
                e_im=pair_diag(e_im).astype(BF16), f_re=pair_diag(f_re).astype(BF16),
                f_im=pair_diag(f_im).astype(BF16), apow_re=apow_re, apow_im=apow_im,
                step_re=step_re, step_im=step_im)


def _mixer_ab(x, pos, past_ckv, past_krope, conv0, h0, w):
    b, t, dm = x.shape
    xf = x.reshape(b * t, dm)
    tabs = _rope_tables(pos)
    if b > 1:
        tabs = tuple(jnp.tile(tb, (b, 1)) for tb in tabs)
    q, k, v, c_new, kr_new, xb, yb = _ab_in(xf, w, tabs)
    if past_ckv is None:
        att = _attn_prompt(q, k, v)
    else:
        att = _attn_sample(q, c_new, kr_new, past_ckv, past_krope, w['wukt'], w['wuv_h'])
    rec, conv_new, h_new = _lru(xb.reshape(b, t, -1), yb.reshape(b, t, -1), conv0,
                                h0[:, None, :], w)
    return (att, rec.reshape(b * t, -1), c_new.reshape(b, t, -1), kr_new.reshape(b, t, -1),
            conv_new, h_new[:, 0, :])


def _s5_chunks(u, b, t):
    nc = t // S5_L
    up = u.astype(BF16).reshape(b, nc, S5_L, S5_PAIRS, 2, S5_GROUP)
    up = jnp.transpose(up, (0, 3, 1, 4, 2, 5)).reshape(b, S5_PAIRS, nc, S5_PAIR_IN)
    ncp = -(-nc // SUBLANES) * SUBLANES
    if ncp != nc:
        up = jnp.pad(up, ((0, 0), (0, 0), (0, ncp - nc), (0, 0)))
    return up, nc


def _s5_unchunk(y, b, t, nc):
    y = y[:, :, :nc].reshape(b, S5_PAIRS, nc, 2, S5_L, S5_GROUP)
    return jnp.transpose(y, (0, 2, 4, 1, 3, 5)).reshape(b * t, S5_WIDTH)


def _mixer_c(x, h0_re, h0_im, w):
    b, t, dm = x.shape
    xf = x.reshape(b * t, dm)
    u = _c_in(xf, w['w_in'])
    up, nc = _s5_chunks(u, b, t)
    e_re, e_im = _s5_e(up, w['tabs'])
    hp_re, hp_im, hf_re, hf_im = _s5_scan(
        e_re, e_im, h0_re.reshape(b, 1, S5_STATE_W), h0_im.reshape(b, 1, S5_STATE_W),
        w['tabs'], nc)
    y = _s5_y(up, hp_re, hp_im, w['tabs'])
    y = _s5_unchunk(y, b, t, nc)
    return (y, u, hf_re.reshape(b, S5_GROUPS, S5_STATE), hf_im.reshape(b, S5_GROUPS, S5_STATE))


def kernel(x_prompt, x_sample, cache_mla_ckv, cache_mla_krope, state_lru_conv, state_lru_h,
           state_s5_re, state_s5_im, w_in_ab, q_norm_g, kv_norm_g, w_uq, w_ukv,
           lru_conv_w, lru_conv_b, lru_w_a, lru_b_a, lru_w_x, lru_b_x, lru_a_param, w_out_ab,
           w_in_c, s5_lam_re, s5_lam_im, s5_log_step, s5_b_re, s5_b_im, s5_c_re, s5_c_im,
           s5_d, s5_w_glu, w_out_c, ln_mix_g, ln_mix_b, ln_ffn_g, ln_ffn_b,
           moe_w_group, moe_w_expert, moe_w_gate, moe_w_up, moe_w_down):
    bp, tp, dm = x_prompt.shape
    bs, ts, _ = x_sample.shape
    past = cache_mla_ckv.shape[2]
    depth = ln_mix_g.shape[0]
    alpha = (2 * depth) ** 0.25
    pos_p = jnp.arange(tp)
    pos_s = past + jnp.arange(ts)
    assert (past + ts - 1) // CHUNK <= past // CHUNK
    hp, hs = x_prompt, x_sample
    outs = {k: [] for k in ('ckv_p', 'ckv_s', 'kr_p', 'kr_s', 'cv_p', 'cv_s', 'lh_p', 'lh_s',
                            's5r_p', 's5r_s', 's5i_p', 's5i_s')}
    for layer in range(depth):
        j = layer // 2
        g_mix, b_mix = ln_mix_g[layer][None, :], ln_mix_b[layer][None, :]
        g_ffn, b_ffn = ln_ffn_g[layer][None, :], ln_ffn_b[layer][None, :]
        if layer % 2 == 0:
            w = _prep_ab(w_in_ab[j], q_norm_g[j], kv_norm_g[j], w_uq[j], w_ukv[j], lru_conv_w[j],
                         lru_conv_b[j], lru_w_a[j], lru_b_a[j], lru_w_x[j], lru_b_x[j],
                         lru_a_param[j], w_out_ab[j])
            att_p, rec_p, c1, k1, v1, h1 = _mixer_ab(
                hp, pos_p, None, None, jnp.zeros((bp, CONV_WIDTH - 1, LRU_WIDTH), F32),
                jnp.zeros((bp, LRU_WIDTH), F32), w)
            att_s, rec_s, c2, k2, v2, h2 = _mixer_ab(
                hs, pos_s, cache_mla_ckv[j], cache_mla_krope[j], state_lru_conv[j],
                state_lru_h[j], w)
            outs['ckv_p'].append(c1); outs['ckv_s'].append(c2)
            outs['kr_p'].append(k1); outs['kr_s'].append(k2)
            outs['cv_p'].append(v1); outs['cv_s'].append(v2)
            outs['lh_p'].append(h1); outs['lh_s'].append(h2)
            hp = _out_ln(att_p, rec_p, hp.reshape(bp * tp, dm), w['w_out_att'], w['w_out_rec'],
                         g_mix, b_mix, alpha)
            hs = _out_ln(att_s, rec_s, hs.reshape(bs * ts, dm), w['w_out_att'], w['w_out_rec'],
                         g_mix, b_mix, alpha)
        else:
            w = dict(w_in=w_in_c[j].astype(BF16), d=s5_d[j][None, :],
                     w_glu=s5_w_glu[j].astype(BF16), w_out=w_out_c[j].astype(BF16),
                     tabs=_prep_s5(s5_lam_re[j], s5_lam_im[j], s5_log_step[j], s5_b_re[j],
                                   s5_b_im[j], s5_c_re[j], s5_c_im[j]))
            zero_state = jnp.zeros((bp, S5_GROUPS, S5_STATE), F32)
            y_p, u_p, r1, i1 = _mixer_c(hp, zero_state, zero_state, w)
            y_s, u_s, r2, i2 = _mixer_c(hs, state_s5_re[j], state_s5_im[j], w)
            outs['s5r_p'].append(r1); outs['s5r_s'].append(r2)
            outs['s5i_p'].append(i1); outs['s5i_s'].append(i2)
            hp = _c_out(y_p, u_p, hp.reshape(bp * tp, dm), w, g_mix, b_mix, alpha)
            hs = _c_out(y_s, u_s, hs.reshape(bs * ts, dm), w, g_mix, b_mix, alpha)
        wm = _prep_moe(moe_w_group[layer], moe_w_expert[layer], moe_w_gate[layer],
                       moe_w_up[layer], moe_w_down[layer])
        hp = _moe_ln(hp, wm, g_ffn, b_ffn, alpha).reshape(bp, tp, dm)
        hs = _moe_ln(hs, wm, g_ffn, b_ffn, alpha).reshape(bs, ts, dm)
    st = lambda k: jnp.stack(outs[k])
    return (hp, hs, st('ckv_p'), st('ckv_s'), st('kr_p'), st('kr_s'), st('cv_p'), st('cv_s'),
            st('lh_p'), st('lh_s'), st('s5r_p'), st('s5r_s'), st('s5i_p'), st('s5i_s'))
```

```python
import functools
import math

import jax
import jax.numpy as jnp
from jax import lax
from jax.experimental import pallas as pl
from jax.experimental.pallas import tpu as pltpu

F32 = jnp.float32
BF16 = jnp.bfloat16

CHUNK = 64
MLA_HEADS = 8
QK_NOPE = 64
QK_ROPE = 32
QK_HEAD = QK_NOPE + QK_ROPE
V_HEAD = 64
Q_LORA = 256
KV_LORA = 128
ROPE_THETA = 10000.0
MLA_SCALE = QK_HEAD ** -0.5
NEG_INF = -1e30
LRU_WIDTH = 512
LRU_BLOCKS = 8
LRU_BLOCK = LRU_WIDTH // LRU_BLOCKS
CONV_WIDTH = 4
RG_C = 8.0
S5_WIDTH = 512
S5_GROUP = 16
S5_GROUPS = S5_WIDTH // S5_GROUP
S5_STATE = 64
N_GROUPS = 4
EXPERTS_PER_GROUP = 4
N_EXPERTS = N_GROUPS * EXPERTS_PER_GROUP
EXPERT_HIDDEN = 256
LN_EPS = 1e-5
RMS_EPS = 1e-6

LANES = 128
SUBLANES = 8
VMEM_LIMIT_BYTES = 48 * 1024 * 1024

HEAD_PAD = LANES
ROW_TILE = 512
ATTN_TQ = 512
Q_SCALE = MLA_SCALE * math.log2(math.e)
LRU_TILE = 512
S5_L = 8
S5_SUPER = LANES // S5_GROUP
S5_NSUPER = S5_GROUPS // S5_SUPER
S5_SUPER_IN = S5_L * LANES
S5_SUPER_STATE = S5_SUPER * S5_STATE
S5_STATE_W = S5_GROUPS * S5_STATE
S5_SCAN_LANES = 256
S5_ROW_TILE = 256
ROUTER_W = LANES


def _cparams(semantics):
    return pltpu.CompilerParams(dimension_semantics=semantics,
                                vmem_limit_bytes=VMEM_LIMIT_BYTES)


def _full_spec(a):
    nd = a.ndim
    return pl.BlockSpec(a.shape, lambda *_: (0,) * nd)


def _row_spec(tm, width):
    return pl.BlockSpec((tm, width), lambda i: (i, 0))


def _layer_norm(z, g, b):
    mu = jnp.mean(z, axis=-1, keepdims=True)
    zc = z - mu
    var = jnp.mean(zc * zc, axis=-1, keepdims=True)
    return zc * lax.rsqrt(var + LN_EPS) * g + b


def _rms_norm(z, g):
    ms = jnp.mean(z * z, axis=-1, keepdims=True)
    return z * lax.rsqrt(ms + RMS_EPS) * g


def _bdot(a, b):
    return jnp.dot(a.astype(BF16), b.astype(BF16), preferred_element_type=F32)


def _ab_in_kernel(x_ref, win_ref, qg_ref, kvg_ref, wuq_ref, wuk_ref, wuv_ref,
                  rc_ref, ra_ref, rb_ref,
                  q_out, k_out, v_out, c_out, kr_out, xb_out, yb_out, *, dm):
    proj = _bdot(x_ref[...], win_ref[...])
    o0 = Q_LORA
    o1 = o0 + KV_LORA
    o2 = o1 + LRU_WIDTH
    o3 = o2 + LRU_WIDTH
    q_lat = proj[:, 0:o0]
    kv_lat = proj[:, o0:o1]
    xb_out[...] = proj[:, o1:o2]
    yb_out[...] = proj[:, o2:o3]
    krp = proj[:, o3:o3 + HEAD_PAD]

    qn = _rms_norm(q_lat, qg_ref[...])
    cn = _rms_norm(kv_lat, kvg_ref[...])
    c_out[...] = cn
    q = _bdot(qn, wuq_ref[...])
    kk = _bdot(cn, wuk_ref[...])
    v_out[...] = _bdot(cn, wuv_ref[...]).astype(v_out.dtype)

    rc = rc_ref[...]
    ra = ra_ref[...]
    rb = rb_ref[...]
    half = QK_ROPE // 2

    def rope(z):
        return (z * rc + pltpu.roll(z, half, 1) * ra
                + pltpu.roll(z, HEAD_PAD - half, 1) * rb)

    kr = rope(krp)
    kr_out[...] = kr[:, QK_NOPE:QK_NOPE + QK_ROPE]
    for h in range(MLA_HEADS):
        sl = slice(h * HEAD_PAD, (h + 1) * HEAD_PAD)
        q_out[:, sl] = (rope(q[:, sl]) * Q_SCALE).astype(q_out.dtype)
        k_out[:, sl] = (kk[:, sl] + kr).astype(k_out.dtype)


def _ab_in(x, w, rope_tabs):
    n, dm = x.shape
    tm = min(ROW_TILE, n)
    rc, ra, rb = rope_tabs
    consts = [w['w_in'], w['q_g'], w['kv_g'], w['w_uq'], w['w_uk'], w['w_uv']]
    hp = MLA_HEADS * HEAD_PAD
    out_shape = [
        jax.ShapeDtypeStruct((n, hp), BF16),
        jax.ShapeDtypeStruct((n, hp), BF16),
        jax.ShapeDtypeStruct((n, MLA_HEADS * V_HEAD), BF16),
        jax.ShapeDtypeStruct((n, KV_LORA), F32),
        jax.ShapeDtypeStruct((n, QK_ROPE), F32),
        jax.ShapeDtypeStruct((n, LRU_WIDTH), F32),
        jax.ShapeDtypeStruct((n, LRU_WIDTH), F32),
    ]
    return pl.pallas_call(
        functools.partial(_ab_in_kernel, dm=dm), name="ab_in",
        grid=(n // tm,),
        in_specs=([_row_spec(tm, dm)] + [_full_spec(c) for c in consts]
                  + [_row_spec(tm, HEAD_PAD)] * 3),
        out_specs=[_row_spec(tm, s.shape[1]) for s in out_shape],
        out_shape=out_shape,
        compiler_params=_cparams(("parallel",)),
    )(x, *consts, rc, ra, rb)


def _attn_prompt_kernel(q_ref, k_ref, v_ref, o_ref):
    i = pl.program_id(1)
    tq = tk = ATTN_TQ
    row_chunk = lax.broadcasted_iota(jnp.int32, (tq, tk), 0) // CHUNK
    col_chunk = lax.broadcasted_iota(jnp.int32, (tq, tk), 1) // CHUNK
    visible = col_chunk <= row_chunk
    qs = [q_ref[:, hh * HEAD_PAD:(hh + 1) * HEAD_PAD] for hh in range(2)]

    def scores(hh, j):
        start = pl.multiple_of(j * tk, tk)
        k = k_ref[pl.ds(start, tk), hh * HEAD_PAD:(hh + 1) * HEAD_PAD]
        v = v_ref[pl.ds(start, tk), hh * V_HEAD:(hh + 1) * V_HEAD]
        s = lax.dot_general(qs[hh], k, (((1,), (1,)), ((), ())), preferred_element_type=F32)
        return s, v

    state = []
    for hh in range(2):
        s, v = scores(hh, i)
        s = jnp.where(visible, s, NEG_INF)
        m = jnp.max(s, axis=-1, keepdims=True)
        p = jnp.exp2(s - m)
        l = jnp.sum(p, axis=-1, keepdims=True)
        state += [m, l, jnp.dot(p.astype(BF16), v, preferred_element_type=F32)]

    def body(j, state):
        new = []
        for hh in range(2):
            m, l, acc = state[3 * hh:3 * hh + 3]
            s, v = scores(hh, j)
            m_new = jnp.maximum(m, jnp.max(s, axis=-1, keepdims=True))
            alpha = jnp.exp2(m - m_new)
            p = jnp.exp2(s - m_new)
            l = alpha * l + jnp.sum(p, axis=-1, keepdims=True)
            acc = alpha * acc + jnp.dot(p.astype(BF16), v, preferred_element_type=F32)
            new += [m_new, l, acc]
        return tuple(new)

    state = lax.fori_loop(0, i, body, tuple(state))
    outs = [state[3 * hh + 2] / state[3 * hh + 1] for hh in range(2)]
    o_ref[...] = jnp.concatenate(outs, axis=-1).astype(o_ref.dtype)


def _attn_prompt(q, k, v):
    t = q.shape[0]
    pairs = MLA_HEADS // 2
    return pl.pallas_call(
        _attn_prompt_kernel, name="attn_prompt",
        grid=(pairs, t // ATTN_TQ),
        in_specs=[
            pl.BlockSpec((ATTN_TQ, 2 * HEAD_PAD), lambda p, i: (i, p)),
            pl.BlockSpec((t, 2 * HEAD_PAD), lambda p, i: (0, p)),
            pl.BlockSpec((t, 2 * V_HEAD), lambda p, i: (0, p)),
        ],
        out_specs=pl.BlockSpec((ATTN_TQ, 2 * V_HEAD), lambda p, i: (i, p)),
        out_shape=jax.ShapeDtypeStruct((t, MLA_HEADS * V_HEAD), BF16),
        compiler_params=_cparams(("parallel", "parallel")),
    )(q, k, v)


def _attn_sample_kernel(q_ref, cn_ref, krn_ref, cp_ref, krp_ref, wukt_ref, wuv_ref, o_ref):
    cp = cp_ref[...].astype(BF16)
    krp = krp_ref[...].astype(BF16)
    cn = cn_ref[...].astype(BF16)
    krn = krn_ref[...].astype(BF16)
    ts = cn.shape[0]
    dn = (((1,), (1,)), ((), ()))
    outs = []
    for hp in range(MLA_HEADS // 2):
        qa, qr = [], []
        for hh in range(2):
            h = 2 * hp + hh
            qh = q_ref[:, h * HEAD_PAD:(h + 1) * HEAD_PAD]
            qa.append(jnp.dot(qh[:, :QK_NOPE], wukt_ref[h], preferred_element_type=F32))
            qr.append(qh[:, QK_NOPE:QK_NOPE + QK_ROPE])
        qa = jnp.concatenate(qa, axis=0).astype(BF16)
        qr = jnp.concatenate(qr, axis=0)
        s_past = (lax.dot_general(qa, cp, dn, preferred_element_type=F32)
                  + lax.dot_general(qr, krp, dn, preferred_element_type=F32))
        s_new = (lax.dot_general(qa, cn, dn, preferred_element_type=F32)
                 + lax.dot_general(qr, krn, dn, preferred_element_type=F32))
        m = jnp.maximum(jnp.max(s_past, axis=-1, keepdims=True),
                        jnp.max(s_new, axis=-1, keepdims=True))
        p_past = jnp.exp2(s_past - m)
        p_new = jnp.exp2(s_new - m)
        l = (jnp.sum(p_past, axis=-1, keepdims=True)
             + jnp.sum(p_new, axis=-1, keepdims=True))
        o_lat = (jnp.dot(p_past.astype(BF16), cp, preferred_element_type=F32)
                 + jnp.dot(p_new.astype(BF16), cn, preferred_element_type=F32)) / l
        for hh in range(2):
            h = 2 * hp + hh
            outs.append(_bdot(o_lat[hh * ts:(hh + 1) * ts], wuv_ref[h]))
    o_ref[...] = jnp.concatenate(outs, axis=-1).astype(o_ref.dtype)


def _attn_sample(q, c_new, kr_new, c_past, kr_past, wukt, wuv):
    bs, past, _ = c_past.shape
    ts = q.shape[0] // bs
    return pl.pallas_call(
        _attn_sample_kernel, name="attn_sample",
        grid=(bs,),
        in_specs=[
            pl.BlockSpec((ts, MLA_HEADS * HEAD_PAD), lambda b: (b, 0)),
            pl.BlockSpec((ts, KV_LORA), lambda b: (b, 0)),
            pl.BlockSpec((ts, QK_ROPE), lambda b: (b, 0)),
            pl.BlockSpec((None, past, KV_LORA), lambda b: (b, 0, 0)),
            pl.BlockSpec((None, past, QK_ROPE), lambda b: (b, 0, 0)),
            _full_spec(wukt), _full_spec(wuv),
        ],
        out_specs=pl.BlockSpec((ts, MLA_HEADS * V_HEAD), lambda b: (b, 0)),
        out_shape=jax.ShapeDtypeStruct((bs * ts, MLA_HEADS * V_HEAD), BF16),
        compiler_params=_cparams(("parallel",)),
    )(q, c_new, kr_new, c_past, kr_past, wukt, wuv)


def _lru_kernel(xb_ref, yb_ref, conv0_ref, h0_ref, cw_ref, cb_ref, wax_ref, bax_ref, ap_ref,
                rec_ref, conv_out_ref, h_out_ref, xcat, hc, *, tb):
    t = pl.program_id(1)
    nt = pl.num_programs(1)
    tail = CONV_WIDTH - 1
    base = SUBLANES

    @pl.when(t == 0)
    def _():
        xcat[base - tail:base, :] = conv0_ref[...]
        hc[...] = h0_ref[...]

    @pl.when(t > 0)
    def _():
        xcat[0:base, :] = xcat[tb:tb + base, :]

    xcat[base:base + tb, :] = xb_ref[...]
    xc = cb_ref[...]
    for tap in range(CONV_WIDTH):
        xc = xc + xcat[base - tail + tap:base - tail + tap + tb, :] * cw_ref[tap:tap + 1, :]

    gates = _bdot(xc, wax_ref[...]) + bax_ref[...]
    r = jax.nn.sigmoid(gates[:, :LRU_WIDTH])
    ig = jax.nn.sigmoid(gates[:, LRU_WIDTH:])
    log_a = -RG_C * r * ap_ref[...]
    a = jnp.exp(log_a)
    u = jnp.sqrt(jnp.tanh(-log_a) * (1.0 + a * a)) * (ig * xc)

    row = lax.broadcasted_iota(jnp.int32, (tb, LRU_WIDTH), 0)
    s = 1
    while s < tb:
        keep = row >= s
        u = jnp.where(keep, a * pltpu.roll(u, s, 0) + u, u)
        a = jnp.where(keep, a * pltpu.roll(a, s, 0), a)
        s *= 2
    h = a * hc[...] + u
    hc[...] = h[tb - 1:tb, :]
    rec_ref[...] = (h * jax.nn.gelu(yb_ref[...])).astype(rec_ref.dtype)

    @pl.when(t == nt - 1)
    def _():
        conv_out_ref[...] = xcat[base + tb - tail:base + tb, :]
        h_out_ref[...] = h[tb - 1:tb, :]


def _lru(xb, yb, conv0, h0, w):
    b, t, wd = xb.shape
    tb = min(LRU_TILE, t)
    tail = CONV_WIDTH - 1
    consts = [w['conv_w'], w['conv_b'], w['w_ax'], w['b_ax'], w['a_param']]
    seq = pl.BlockSpec((None, tb, wd), lambda bi, ti: (bi, ti, 0))
    return pl.pallas_call(
        functools.partial(_lru_kernel, tb=tb), name="rg_lru",
        grid=(b, t // tb),
        in_specs=[seq, seq,
                  pl.BlockSpec((None, tail, wd), lambda bi, ti: (bi, 0, 0)),
                  pl.BlockSpec((None, 1, wd), lambda bi, ti: (bi, 0, 0))]
                 + [pl.BlockSpec(c.shape, lambda bi, ti: (0, 0)) for c in consts],
        out_specs=[seq,
                   pl.BlockSpec((None, tail, wd), lambda bi, ti: (bi, 0, 0)),
                   pl.BlockSpec((None, 1, wd), lambda bi, ti: (bi, 0, 0))],
        out_shape=[jax.ShapeDtypeStruct((b, t, wd), BF16),
                   jax.ShapeDtypeStruct((b, tail, wd), F32),
                   jax.ShapeDtypeStruct((b, 1, wd), F32)],
        scratch_shapes=[pltpu.VMEM((tb + 2 * SUBLANES, wd), F32), pltpu.VMEM((1, wd), F32)],
        compiler_params=_cparams(("parallel", "arbitrary")),
    )(xb, yb, conv0, h0, *consts)


def _out_ln_kernel(a_ref, b_ref, x_ref, wa_ref, wb_ref, g_ref, beta_ref, o_ref, *, alpha):
    y = (jnp.dot(a_ref[...], wa_ref[...], preferred_element_type=F32)
         + jnp.dot(b_ref[...], wb_ref[...], preferred_element_type=F32))
    o_ref[...] = _layer_norm(alpha * x_ref[...] + y, g_ref[...], beta_ref[...])


def _out_ln(att, rec, x, wa, wb, g, beta, alpha):
    n, dm = x.shape
    tm = min(ROW_TILE, n)
    consts = [wa, wb, g, beta]
    return pl.pallas_call(
        functools.partial(_out_ln_kernel, alpha=alpha), name="out_ln",
        grid=(n // tm,),
        in_specs=[_row_spec(tm, att.shape[1]), _row_spec(tm, rec.shape[1]), _row_spec(tm, dm)]
                 + [_full_spec(c) for c in consts],
        out_specs=_row_spec(tm, dm),
        out_shape=jax.ShapeDtypeStruct((n, dm), F32),
        compiler_params=_cparams(("parallel",)),
    )(att, rec, x, *consts)


def _moe_ln_kernel(x_ref, wr_ref, wg_ref, wu_ref, wd_ref, g_ref, beta_ref, o_ref,
                   acc, gates, xb16, *, alpha):
    e = pl.program_id(1)
    tm = x_ref.shape[0]
    lane = lax.broadcasted_iota(jnp.int32, (tm, ROUTER_W), 1).astype(F32)

    @pl.when(e == 0)
    def _():
        x = x_ref[...]
        xb16[...] = x.astype(BF16)
        logits = jnp.dot(x, wr_ref[...], preferred_element_type=F32,
                         precision=lax.Precision.HIGHEST)
        big = float(ROUTER_W)
        lg = jnp.where(lane < N_GROUPS, logits, -jnp.inf)
        mg = jnp.max(lg, axis=-1, keepdims=True)
        gi = jnp.min(jnp.where(lg == mg, lane, big), axis=-1, keepdims=True)
        p_top = 1.0 / jnp.sum(jnp.exp(lg - mg), axis=-1, keepdims=True)
        lo = N_GROUPS + gi * EXPERTS_PER_GROUP
        le = jnp.where((lane >= lo) & (lane < lo + EXPERTS_PER_GROUP), logits, -jnp.inf)
        v1 = jnp.max(le, axis=-1, keepdims=True)
        i1 = jnp.min(jnp.where(le == v1, lane, big), axis=-1, keepdims=True)
        le2 = jnp.where(lane == i1, -jnp.inf, le)
        v2 = jnp.max(le2, axis=-1, keepdims=True)
        i2 = jnp.min(jnp.where(le2 == v2, lane, big), axis=-1, keepdims=True)
        e2 = jnp.exp(v2 - v1)
        w1 = p_top / (1.0 + e2)
        w2 = p_top * e2 / (1.0 + e2)
        gates[...] = jnp.where(lane == i1, w1, 0.0) + jnp.where(lane == i2, w2, 0.0)
        acc[...] = jnp.zeros_like(acc)

    ge = jnp.sum(jnp.where(lane == (N_GROUPS + e).astype(F32), gates[...], 0.0),
                 axis=-1, keepdims=True)
    xb = xb16[...]
    h = jnp.dot(xb, wg_ref[...], preferred_element_type=F32)
    u = jnp.dot(xb, wu_ref[...], preferred_element_type=F32)
    act = jax.nn.silu(h) * u * ge
    acc[...] += jnp.dot(act.astype(BF16), wd_ref[...], preferred_element_type=F32)

    @pl.when(e == pl.num_programs(1) - 1)
    def _():
        o_ref[...] = _layer_norm(alpha * x_ref[...] + acc[...], g_ref[...], beta_ref[...])


def _moe_ln(x, w, g, beta, alpha):
    n, dm = x.shape
    tm = min(ROW_TILE, n)
    ne, _, eh = w['w_gate'].shape
    return pl.pallas_call(
        functools.partial(_moe_ln_kernel, alpha=alpha), name="moe_ln",
        grid=(n // tm, ne),
        in_specs=[
            pl.BlockSpec((tm, dm), lambda i, e: (i, 0)),
            pl.BlockSpec(w['w_router'].shape, lambda i, e: (0, 0)),
            pl.BlockSpec((None, dm, eh), lambda i, e: (e, 0, 0)),
            pl.BlockSpec((None, dm, eh), lambda i, e: (e, 0, 0)),
            pl.BlockSpec((None, eh, dm), lambda i, e: (e, 0, 0)),
            pl.BlockSpec(g.shape, lambda i, e: (0, 0)),
            pl.BlockSpec(beta.shape, lambda i, e: (0, 0)),
        ],
        out_specs=pl.BlockSpec((tm, dm), lambda i, e: (i, 0)),
        out_shape=jax.ShapeDtypeStruct((n, dm), F32),
        scratch_shapes=[pltpu.VMEM((tm, dm), F32), pltpu.VMEM((tm, ROUTER_W), F32),
                        pltpu.VMEM((tm, dm), BF16)],
        compiler_params=_cparams(("parallel", "arbitrary")),
    )(x, w['w_router'], w['w_gate'], w['w_up'], w['w_down'], g, beta)


def _c_in_kernel(x_ref, w_ref, u_ref):
    u_ref[...] = _bdot(x_ref[...], w_ref[...])


def _c_in(x, w_in):
    n, dm = x.shape
    tm = min(ROW_TILE, n)
    return pl.pallas_call(
        _c_in_kernel, name="c_in",
        grid=(n // tm,),
        in_specs=[_row_spec(tm, dm), _full_spec(w_in)],
        out_specs=_row_spec(tm, w_in.shape[1]),
        out_shape=jax.ShapeDtypeStruct((n, w_in.shape[1]), F32),
        compiler_params=_cparams(("parallel",)),
    )(x, w_in)


def _s5_super_rows(u_ref, q):
    tiles = [u_ref[:, s * S5_WIDTH + q * LANES:s * S5_WIDTH + (q + 1) * LANES]
             for s in range(S5_L)]
    return jnp.concatenate(tiles, axis=1).astype(BF16)


def _s5_e_kernel(u_ref, ere_ref, eim_ref, ere_out, eim_out):
    for q in range(S5_NSUPER):
        uq = _s5_super_rows(u_ref, q)
        sl = slice(q * S5_SUPER_STATE, (q + 1) * S5_SUPER_STATE)
        ere_out[:, sl] = jnp.dot(uq, ere_ref[q], preferred_element_type=F32)
        eim_out[:, sl] = jnp.dot(uq, eim_ref[q], preferred_element_type=F32)


def _const_spec(a):
    nd = a.ndim
    return pl.BlockSpec(a.shape, lambda *_: (0,) * nd, pipeline_mode=pl.Buffered(1))


def _s5_e(u2, tabs):
    rows, wu = u2.shape
    tr = min(S5_ROW_TILE, rows)
    out = jax.ShapeDtypeStruct((rows, S5_STATE_W), F32)
    return pl.pallas_call(
        _s5_e_kernel, name="s5_e",
        grid=(rows // tr,),
        in_specs=[_row_spec(tr, wu), _const_spec(tabs['e_re']), _const_spec(tabs['e_im'])],
        out_specs=[_row_spec(tr, S5_STATE_W)] * 2,
        out_shape=[out, out],
        compiler_params=_cparams(("parallel",)),
    )(u2, tabs['e_re'], tabs['e_im'])


def _s5_scan_kernel(ere_ref, eim_ref, h0r_ref, h0i_ref, stp_re_ref, stp_im_ref,
                    apw_re_ref, apw_im_ref, hpr_ref, hpi_ref, hfr_ref, hfi_ref, *, nc, last_row):
    w = ere_ref.shape[-1]
    row = lax.broadcasted_iota(jnp.int32, (SUBLANES, w), 0)
    apr = apw_re_ref[...]
    api = apw_im_ref[...]

    def tile(i, carry):
        cr, ci = carry
        start = pl.multiple_of(i * SUBLANES, SUBLANES)
        sr = ere_ref[pl.ds(start, SUBLANES), :]
        si = eim_ref[pl.ds(start, SUBLANES), :]
        for k in range(3):
            s = 1 << k
            keep = row >= s
            ar = stp_re_ref[k:k + 1, :]
            ai = stp_im_ref[k:k + 1, :]
            pr = pltpu.roll(sr, s, 0)
            pi = pltpu.roll(si, s, 0)
            sr, si = (jnp.where(keep, sr + ar * pr - ai * pi, sr),
                      jnp.where(keep, si + ar * pi + ai * pr, si))
        hr = sr + apr * cr - api * ci
        hi = si + apr * ci + api * cr
        first = row == 0
        hpr_ref[pl.ds(start, SUBLANES), :] = jnp.where(first, cr, pltpu.roll(hr, 1, 0))
        hpi_ref[pl.ds(start, SUBLANES), :] = jnp.where(first, ci, pltpu.roll(hi, 1, 0))
        return hr, hi

    def body(i, carry):
        hr, hi = tile(i, carry)
        return hr[SUBLANES - 1:SUBLANES, :], hi[SUBLANES - 1:SUBLANES, :]

    ntile = nc // SUBLANES
    carry = lax.fori_loop(0, ntile - 1, body, (h0r_ref[...], h0i_ref[...]))
    hr, hi = tile(ntile - 1, carry)
    hfr_ref[...] = hr[last_row:last_row + 1, :]
    hfi_ref[...] = hi[last_row:last_row + 1, :]


def _s5_scan(e_re, e_im, h0r, h0i, tabs, n_chunks):
    b, nc, w = e_re.shape
    last_row = (n_chunks - 1) % SUBLANES
    wb = S5_SCAN_LANES
    consts = [tabs['step_re'], tabs['step_im'], tabs['apow_re'], tabs['apow_im']]
    seq = pl.BlockSpec((None, nc, wb), lambda bi, li: (bi, 0, li))
    vec = pl.BlockSpec((None, 1, wb), lambda bi, li: (bi, 0, li))
    return pl.pallas_call(
        functools.partial(_s5_scan_kernel, nc=nc, last_row=last_row), name="s5_scan",
        grid=(b, w // wb),
        in_specs=[seq, seq, vec, vec]
                 + [pl.BlockSpec((c.shape[0], wb), lambda bi, li: (0, li)) for c in consts],
        out_specs=[seq, seq, vec, vec],
        out_shape=[jax.ShapeDtypeStruct((b, nc, w), F32)] * 2
                  + [jax.ShapeDtypeStruct((b, 1, w), F32)] * 2,
        compiler_params=_cparams(("parallel", "parallel")),
    )(e_re, e_im, h0r, h0i, *consts)


def _s5_y_kernel(u_ref, hpr_ref, hpi_ref, m_ref, fre_ref, fim_ref, y_ref):
    for q in range(S5_NSUPER):
        uq = _s5_super_rows(u_ref, q)
        sl = slice(q * S5_SUPER_STATE, (q + 1) * S5_SUPER_STATE)
        yq = (jnp.dot(uq, m_ref[q], preferred_element_type=F32)
              + _bdot(hpr_ref[:, sl], fre_ref[q])
              + _bdot(hpi_ref[:, sl], fim_ref[q]))
        for t in range(S5_L):
            y_ref[:, t * S5_WIDTH + q * LANES:t * S5_WIDTH + (q + 1) * LANES] = (
                yq[:, t * LANES:(t + 1) * LANES])


def _s5_y(u2, hp_re, hp_im, tabs):
    rows, wu = u2.shape
    tr = min(S5_ROW_TILE, rows)
    consts = [tabs['m'], tabs['f_re'], tabs['f_im']]
    return pl.pallas_call(
        _s5_y_kernel, name="s5_y",
        grid=(rows // tr,),
        in_specs=[_row_spec(tr, wu), _row_spec(tr, S5_STATE_W), _row_spec(tr, S5_STATE_W)]
                 + [_const_spec(c) for c in consts],
        out_specs=_row_spec(tr, wu),
        out_shape=jax.ShapeDtypeStruct((rows, wu), F32),
        compiler_params=_cparams(("parallel",)),
    )(u2, hp_re, hp_im, *consts)


def _c_out_kernel(y_ref, u_ref, x_ref, d_ref, wglu_ref, wout_ref, g_ref, beta_ref, o_ref, *, alpha):
    y = y_ref[...] + d_ref[...] * u_ref[...]
    vg = _bdot(jax.nn.gelu(y), wglu_ref[...])
    half = vg.shape[1] // 2
    z = vg[:, :half] * jax.nn.sigmoid(vg[:, half:])
    o = _bdot(z, wout_ref[...])
    o_ref[...] = _layer_norm(alpha * x_ref[...] + o, g_ref[...], beta_ref[...])


def _c_out(y, u, x, w, g, beta, alpha):
    n, dm = x.shape
    tm = min(ROW_TILE, n)
    consts = [w['d'], w['w_glu'], w['w_out'], g, beta]
    return pl.pallas_call(
        functools.partial(_c_out_kernel, alpha=alpha), name="c_out",
        grid=(n // tm,),
        in_specs=[_row_spec(tm, y.shape[1]), _row_spec(tm, u.shape[1]), _row_spec(tm, dm)]
                 + [_full_spec(c) for c in consts],
        out_specs=_row_spec(tm, dm),
        out_shape=jax.ShapeDtypeStruct((n, dm), F32),
        compiler_params=_cparams(("parallel",)),
    )(y, u, x, *consts)


def _prep_ab(w_in, q_g, kv_g, w_uq, w_ukv, conv_w, conv_b, wa, ba, wx, bx, a_param, w_out):
    dm = w_in.shape[0]
    s0 = Q_LORA
    s1 = s0 + KV_LORA
    s2 = s1 + QK_ROPE
    s3 = s2 + LRU_WIDTH
    pad_lo = jnp.zeros((dm, QK_NOPE), F32)
    pad_hi = jnp.zeros((dm, HEAD_PAD - QK_HEAD), F32)
    w_in_p = jnp.concatenate([w_in[:, :s1], w_in[:, s2:s3], w_in[:, s3:],
                              pad_lo, w_in[:, s1:s2], pad_hi], axis=1).astype(BF16)
    wq = w_uq.reshape(Q_LORA, MLA_HEADS, QK_HEAD)
    wq = jnp.pad(wq, ((0, 0), (0, 0), (0, HEAD_PAD - QK_HEAD)))
    wq = wq.reshape(Q_LORA, MLA_HEADS * HEAD_PAD).astype(BF16)
    wkv = w_ukv.reshape(KV_LORA, MLA_HEADS, QK_NOPE + V_HEAD)
    wk = jnp.pad(wkv[:, :, :QK_NOPE], ((0, 0), (0, 0), (0, HEAD_PAD - QK_NOPE)))
    wk = wk.reshape(KV_LORA, MLA_HEADS * HEAD_PAD).astype(BF16)
    wv = wkv[:, :, QK_NOPE:].reshape(KV_LORA, MLA_HEADS * V_HEAD).astype(BF16)
    wukt = jnp.transpose(wkv[:, :, :QK_NOPE], (1, 2, 0)).astype(BF16)
    wuv_h = jnp.transpose(wkv[:, :, QK_NOPE:], (1, 0, 2)).astype(BF16)
    eye = jnp.eye(LRU_BLOCKS, dtype=F32)

    def blockdiag(wb):
        return jnp.einsum('hij,hk->hikj', wb, eye).reshape(LRU_WIDTH, LRU_WIDTH)

    w_ax = jnp.concatenate([blockdiag(wa), blockdiag(wx)], axis=1).astype(BF16)
    b_ax = jnp.concatenate([ba, bx])[None, :]
    att_w = MLA_HEADS * V_HEAD
    return dict(w_in=w_in_p, q_g=q_g[None, :], kv_g=kv_g[None, :], w_uq=wq, w_uk=wk, w_uv=wv,
                wukt=wukt, wuv_h=wuv_h, conv_w=conv_w, conv_b=conv_b[None, :], w_ax=w_ax,
                b_ax=b_ax, a_param=jax.nn.softplus(-a_param)[None, :],
                w_out_att=w_out[:att_w].astype(BF16), w_out_rec=w_out[att_w:].astype(BF16))


def _rope_tables(pos):
    half = QK_ROPE // 2
    inv = ROPE_THETA ** (-jnp.arange(half, dtype=F32) / half)
    ang = pos.astype(F32)[:, None] * inv[None, :]
    cos, sin = jnp.cos(ang), jnp.sin(ang)
    n = pos.shape[0]
    z = lambda k: jnp.zeros((n, k), F32)
    rc = jnp.concatenate([jnp.ones((n, QK_NOPE), F32), cos, cos, z(HEAD_PAD - QK_HEAD)], axis=1)
    ra = jnp.concatenate([z(QK_NOPE + half), sin, z(HEAD_PAD - QK_HEAD)], axis=1)
    rb = jnp.concatenate([z(QK_NOPE), -sin, z(half + HEAD_PAD - QK_HEAD)], axis=1)
    return rc, ra, rb


def _prep_moe(w_group, w_expert, w_gate, w_up, w_down):
    dm = w_group.shape[0]
    pad = jnp.zeros((dm, ROUTER_W - N_GROUPS - N_EXPERTS), F32)
    w_router = jnp.concatenate([w_group, w_expert, pad], axis=1)
    return dict(w_router=w_router, w_gate=w_gate.astype(BF16), w_up=w_up.astype(BF16),
                w_down=w_down.astype(BF16))


def _cmul(ar, ai, br, bi):
    return ar * br - ai * bi, ar * bi + ai * br


def _prep_s5(lam_re, lam_im, log_step, b_re, b_im, c_re, c_im):
    hi = lax.Precision.HIGHEST
    L, G, P, C = S5_L, S5_GROUPS, S5_STATE, S5_GROUP
    lr, li = lam_re, lam_im
    dt = jnp.exp(log_step)[:, None]
    mag = jnp.exp(lr * dt)
    ar, ai = mag * jnp.cos(li * dt), mag * jnp.sin(li * dt)
    den = lr * lr + li * li
    nr = ar - 1.0
    cr = (nr * lr + ai * li) / den
    ci = (ai * lr - nr * li) / den
    bbr = cr[..., None] * b_re - ci[..., None] * b_im
    bbi = cr[..., None] * b_im + ci[..., None] * b_re
    prs, pis = [jnp.ones_like(ar)], [jnp.zeros_like(ar)]
    for _ in range(L):
        nr_, ni_ = _cmul(prs[-1], pis[-1], ar, ai)
        prs.append(nr_)
        pis.append(ni_)
    pr = jnp.stack(prs)
    pi = jnp.stack(pis)
    xr = pr[:L, :, :, None] * bbr - pi[:L, :, :, None] * bbi
    xi = pr[:L, :, :, None] * bbi + pi[:L, :, :, None] * bbr
    kk = (jnp.einsum('gcp,kgpd->kgcd', c_re, xr, precision=hi)
          - jnp.einsum('gcp,kgpd->kgcd', c_im, xi, precision=hi))
    s_idx = jnp.arange(L)[:, None]
    t_idx = jnp.arange(L)[None, :]
    lag = t_idx - s_idx
    km = jnp.where((lag >= 0)[:, :, None, None, None], kk[jnp.clip(lag, 0, L - 1)], 0.0)
    m = jnp.transpose(km, (2, 0, 4, 1, 3))
    e_re = jnp.transpose(xr[::-1], (1, 0, 3, 2))
    e_im = jnp.transpose(xi[::-1], (1, 0, 3, 2))
    p1r = jnp.transpose(pr[1:], (1, 2, 0))[:, :, :, None]
    p1i = jnp.transpose(pi[1:], (1, 2, 0))[:, :, :, None]
    cre_t = jnp.transpose(c_re, (0, 2, 1))[:, :, None, :]
    cim_t = jnp.transpose(c_im, (0, 2, 1))[:, :, None, :]
    f_re = cre_t * p1r - cim_t * p1i
    f_im = -cre_t * p1i - cim_t * p1r

    nq, sg = S5_NSUPER, S5_SUPER
    eye = jnp.eye(sg, dtype=F32)
    m = jnp.einsum('qgsdtc,gh->qsgdthc', m.reshape(nq, sg, L, C, L, C), eye)
    m = m.reshape(nq, L * sg * C, L * sg * C)

    def pack_e(e):
        e = jnp.einsum('qgsdp,gh->qsgdhp', e.reshape(nq, sg, L, C, P), eye)
        return e.reshape(nq, L * sg * C, sg * P).astype(BF16)

    def pack_f(f):
        f = jnp.einsum('qgptc,gh->qgpthc', f.reshape(nq, sg, P, L, C), eye)
        return f.reshape(nq, sg * P, L * sg * C).astype(BF16)

    alr, ali = pr[L].reshape(1, G * P), pi[L].reshape(1, G * P)
    qr, qi = [alr], [ali]
    for _ in range(SUBLANES - 1):
        nr_, ni_ = _cmul(qr[-1], qi[-1], alr, ali)
        qr.append(nr_)
        qi.append(ni_)
    apow_re = jnp.concatenate(qr, axis=0)
    apow_im = jnp.concatenate(qi, axis=0)
    step_re = jnp.concatenate([qr[0], qr[1], qr[3]], axis=0)
    step_im = jnp.concatenate([qi[0], qi[1], qi[3]], axis=0)
    return dict(m=m.astype(BF16), e_re=pack_e(e_re), e_im=pack_e(e_im), f_re=pack_f(f_re),
                f_im=pack_f(f_im), apow_re=apow_re, apow_im=apow_im,
                step_re=step_re, step_im=step_im)


def _mixer_ab(x, pos, past_ckv, past_krope, conv0, h0, w):
    b, t, dm = x.shape
    xf = x.reshape(b * t, dm)
    tabs = _rope_tables(pos)
    if b > 1:
        tabs = tuple(jnp.tile(tb, (b, 1)) for tb in tabs)
    q, k, v, c_new, kr_new, xb, yb = _ab_in(xf, w, tabs)
    if past_ckv is None:
        att = _attn_prompt(q, k, v)
    else:
        att = _attn_sample(q, c_new, kr_new, past_ckv, past_krope, w['wukt'], w['wuv_h'])
    rec, conv_new, h_new = _lru(xb.reshape(b, t, -1), yb.reshape(b, t, -1), conv0,
                                h0[:, None, :], w)
    return (att, rec.reshape(b * t, -1), c_new.reshape(b, t, -1), kr_new.reshape(b, t, -1),
            conv_new, h_new[:, 0, :])


def _mixer_c(x, h0_re, h0_im, w):
    b, t, dm = x.shape
    xf = x.reshape(b * t, dm)
    u = _c_in(xf, w['w_in'])
    nc = t // S5_L
    assert t % S5_L == 0 and nc % SUBLANES == 0
    u2 = u.reshape(b * nc, S5_L * S5_WIDTH)
    e_re, e_im = _s5_e(u2, w['tabs'])
    hp_re, hp_im, hf_re, hf_im = _s5_scan(
        e_re.reshape(b, nc, S5_STATE_W), e_im.reshape(b, nc, S5_STATE_W),
        h0_re.reshape(b, 1, S5_STATE_W), h0_im.reshape(b, 1, S5_STATE_W), w['tabs'], nc)
    y2 = _s5_y(u2, hp_re.reshape(b * nc, S5_STATE_W), hp_im.reshape(b * nc, S5_STATE_W),
               w['tabs'])
    y = y2.reshape(b * t, S5_WIDTH)
    return (y, u, hf_re.reshape(b, S5_GROUPS, S5_STATE), hf_im.reshape(b, S5_GROUPS, S5_STATE))


def kernel(x_prompt, x_sample, cache_mla_ckv, cache_mla_krope, state_lru_conv, state_lru_h,
           state_s5_re, state_s5_im, w_in_ab, q_norm_g, kv_norm_g, w_uq, w_ukv,
           lru_conv_w, lru_conv_b, lru_w_a, lru_b_a, lru_w_x, lru_b_x, lru_a_param, w_out_ab,
           w_in_c, s5_lam_re, s5_lam_im, s5_log_step, s5_b_re, s5_b_im, s5_c_re, s5_c_im,
           s5_d, s5_w_glu, w_out_c, ln_mix_g, ln_mix_b, ln_ffn_g, ln_ffn_b,
           moe_w_group, moe_w_expert, moe_w_gate, moe_w_up, moe_w_down):
    bp, tp, dm = x_prompt.shape
    bs, ts, _ = x_sample.shape
    past = cache_mla_ckv.shape[2]
    depth = ln_mix_g.shape[0]
    alpha = (2 * depth) ** 0.25
    pos_p = jnp.arange(tp)
    pos_s = past + jnp.arange(ts)
    assert (past + ts - 1) // CHUNK <= past // CHUNK
    hp, hs = x_prompt, x_sample
    outs = {k: [] for k in ('ckv_p', 'ckv_s', 'kr_p', 'kr_s', 'cv_p', 'cv_s', 'lh_p', 'lh_s',
                            's5r_p', 's5r_s', 's5i_p', 's5i_s')}
    for layer in range(depth):
        j = layer // 2
        g_mix, b_mix = ln_mix_g[layer][None, :], ln_mix_b[layer][None, :]
        g_ffn, b_ffn = ln_ffn_g[layer][None, :], ln_ffn_b[layer][None, :]
        if layer % 2 == 0:
            w = _prep_ab(w_in_ab[j], q_norm_g[j], kv_norm_g[j], w_uq[j], w_ukv[j], lru_conv_w[j],
                         lru_conv_b[j], lru_w_a[j], lru_b_a[j], lru_w_x[j], lru_b_x[j],
                         lru_a_param[j], w_out_ab[j])
            att_p, rec_p, c1, k1, v1, h1 = _mixer_ab(
                hp, pos_p, None, None, jnp.zeros((bp, CONV_WIDTH - 1, LRU_WIDTH), F32),
                jnp.zeros((bp, LRU_WIDTH), F32), w)
            att_s, rec_s, c2, k2, v2, h2 = _mixer_ab(
                hs, pos_s, cache_mla_ckv[j], cache_mla_krope[j], state_lru_conv[j],
                state_lru_h[j], w)
            outs['ckv_p'].append(c1); outs['ckv_s'].append(c2)
            outs['kr_p'].append(k1); outs['kr_s'].append(k2)
            outs['cv_p'].append(v1); outs['cv_s'].append(v2)
            outs['lh_p'].append(h1); outs['lh_s'].append(h2)
            hp = _out_ln(att_p, rec_p, hp.reshape(bp * tp, dm), w['w_out_att'], w['w_out_rec'],
                         g_mix, b_mix, alpha)
            hs = _out_ln(att_s, rec_s, hs.reshape(bs * ts, dm), w['w_out_att'], w['w_out_rec'],
                         g_mix, b_mix, alpha)
        else:
            w = dict(w_in=w_in_c[j].astype(BF16), d=s5_d[j][None, :],
                     w_glu=s5_w_glu[j].astype(BF16), w_out=w_out_c[j].astype(BF16),
                     tabs=_prep_s5(s5_lam_re[j], s5_lam_im[j], s5_log_step[j], s5_b_re[j],
                                   s5_b_im[j], s5_c_re[j], s5_c_im[j]))
            zero_state = jnp.zeros((bp, S5_GROUPS, S5_STATE), F32)
            y_p, u_p, r1, i1 = _mixer_c(hp, zero_state, zero_state, w)
            y_s, u_s, r2, i2 = _mixer_c(hs, state_s5_re[j], state_s5_im[j], w)
            outs['s5r_p'].append(r1); outs['s5r_s'].append(r2)
            outs['s5i_p'].append(i1); outs['s5i_s'].append(i2)
            hp = _c_out(y_p, u_p, hp.reshape(bp * tp, dm), w, g_mix, b_mix, alpha)
            hs = _c_out(y_s, u_s, hs.reshape(bs * ts, dm), w, g_mix, b_mix, alpha)
        wm = _prep_moe(moe_w_group[layer], moe_w_expert[layer], moe_w_gate[layer],
                       moe_w_up[layer], moe_w_down[layer])
        hp = _moe_ln(hp, wm, g_ffn, b_ffn, alpha).reshape(bp, tp, dm)
        hs = _moe_ln(hs, wm, g_ffn, b_ffn, alpha).reshape(bs, ts, dm)
    st = lambda k: jnp.stack(outs[k])
    return (hp, hs, st('ckv_p'), st('ckv_s'), st('kr_p'), st('kr_s'), st('cv_p'), st('cv_s'),
            st('lh_p'), st('lh_s'), st('s5r_p'), st('s5r_s'), st('s5i_p'), st('s5i_s'))
```

```python
import functools
import math

import jax
import jax.numpy as jnp
from jax import lax
from jax.experimental import pallas as pl
from jax.experimental.pallas import tpu as pltpu

F32 = jnp.float32
BF16 = jnp.bfloat16

CHUNK = 64
MLA_HEADS = 8
QK_NOPE = 64
QK_ROPE = 32
QK_HEAD = QK_NOPE + QK_ROPE
V_HEAD = 64
Q_LORA = 256
KV_LORA = 128
ROPE_THETA = 10000.0
MLA_SCALE = QK_HEAD ** -0.5
NEG_INF = -1e30
LRU_WIDTH = 512
LRU_BLOCKS = 8
LRU_BLOCK = LRU_WIDTH // LRU_BLOCKS
CONV_WIDTH = 4
RG_C = 8.0
S5_WIDTH = 512
S5_GROUP = 16
S5_GROUPS = S5_WIDTH // S5_GROUP
S5_STATE = 64
N_GROUPS = 4
EXPERTS_PER_GROUP = 4
N_EXPERTS = N_GROUPS * EXPERTS_PER_GROUP
EXPERT_HIDDEN = 256
LN_EPS = 1e-5
RMS_EPS = 1e-6

LANES = 128
SUBLANES = 8
VMEM_LIMIT_BYTES = 48 * 1024 * 1024
MOE_VMEM_LIMIT_BYTES = 56 * 1024 * 1024

HEAD_PAD = LANES
ROW_TILE = 512
MOE_ROW_TILE = 1024
ATTN_TQ = 512
Q_SCALE = MLA_SCALE * math.log2(math.e)
LRU_TILE = 512
S5_L = 8
S5_SUPER = LANES // S5_GROUP
S5_NSUPER = S5_GROUPS // S5_SUPER
S5_SUPER_IN = S5_L * LANES
S5_SUPER_STATE = S5_SUPER * S5_STATE
S5_STATE_W = S5_GROUPS * S5_STATE
S5_SCAN_LANES = 256
S5_ROW_TILE = 256
ROUTER_W = LANES


def _cparams(semantics, vmem_limit_bytes=VMEM_LIMIT_BYTES):
    return pltpu.CompilerParams(dimension_semantics=semantics,
                                vmem_limit_bytes=vmem_limit_bytes)


def _full_spec(a):
    nd = a.ndim
    return pl.BlockSpec(a.shape, lambda *_: (0,) * nd)


def _row_spec(tm, width):
    return pl.BlockSpec((tm, width), lambda i: (i, 0))


def _layer_norm(z, g, b):
    mu = jnp.mean(z, axis=-1, keepdims=True)
    zc = z - mu
    var = jnp.mean(zc * zc, axis=-1, keepdims=True)
    return zc * lax.rsqrt(var + LN_EPS) * g + b


def _rms_norm(z, g):
    ms = jnp.mean(z * z, axis=-1, keepdims=True)
    return z * lax.rsqrt(ms + RMS_EPS) * g


def _bdot(a, b):
    return jnp.dot(a.astype(BF16), b.astype(BF16), preferred_element_type=F32)


def _ab_in_kernel(x_ref, win_ref, qg_ref, kvg_ref, wuq_ref, wuk_ref, wuv_ref,
                  rc_ref, ra_ref, rb_ref,
                  q_out, k_out, v_out, c_out, kr_out, xb_out, yb_out, *, dm):
    proj = _bdot(x_ref[...], win_ref[...])
    o0 = Q_LORA
    o1 = o0 + KV_LORA
    o2 = o1 + LRU_WIDTH
    o3 = o2 + LRU_WIDTH
    q_lat = proj[:, 0:o0]
    kv_lat = proj[:, o0:o1]
    xb_out[...] = proj[:, o1:o2]
    yb_out[...] = proj[:, o2:o3]
    krp = proj[:, o3:o3 + HEAD_PAD]

    qn = _rms_norm(q_lat, qg_ref[...])
    cn = _rms_norm(kv_lat, kvg_ref[...])
    c_out[...] = cn
    q = _bdot(qn, wuq_ref[...])
    kk = _bdot(cn, wuk_ref[...])
    lane = lax.broadcasted_iota(jnp.int32, (1, MLA_HEADS * HEAD_PAD), 1)
    ones_col = (lane % HEAD_PAD == V_HEAD).astype(F32)
    v_out[...] = (_bdot(cn, wuv_ref[...]) + ones_col).astype(v_out.dtype)

    rc = rc_ref[...]
    ra = ra_ref[...]
    rb = rb_ref[...]
    half = QK_ROPE // 2

    def rope(z):
        return (z * rc + pltpu.roll(z, half, 1) * ra
                + pltpu.roll(z, HEAD_PAD - half, 1) * rb)

    kr = rope(krp)
    kr_out[...] = kr[:, QK_NOPE:QK_NOPE + QK_ROPE]
    for h in range(MLA_HEADS):
        sl = slice(h * HEAD_PAD, (h + 1) * HEAD_PAD)
        q_out[:, sl] = (rope(q[:, sl]) * Q_SCALE).astype(q_out.dtype)
        k_out[:, sl] = (kk[:, sl] + kr).astype(k_out.dtype)


def _ab_in(x, w, rope_tabs):
    n, dm = x.shape
    tm = min(ROW_TILE, n)
    rc, ra, rb = rope_tabs
    consts = [w['w_in'], w['q_g'], w['kv_g'], w['w_uq'], w['w_uk'], w['w_uv']]
    hp = MLA_HEADS * HEAD_PAD
    out_shape = [
        jax.ShapeDtypeStruct((n, hp), BF16),
        jax.ShapeDtypeStruct((n, hp), BF16),
        jax.ShapeDtypeStruct((n, hp), BF16),
        jax.ShapeDtypeStruct((n, KV_LORA), F32),
        jax.ShapeDtypeStruct((n, QK_ROPE), F32),
        jax.ShapeDtypeStruct((n, LRU_WIDTH), F32),
        jax.ShapeDtypeStruct((n, LRU_WIDTH), F32),
    ]
    return pl.pallas_call(
        functools.partial(_ab_in_kernel, dm=dm), name="ab_in",
        grid=(n // tm,),
        in_specs=([_row_spec(tm, dm)] + [_full_spec(c) for c in consts]
                  + [_row_spec(tm, HEAD_PAD)] * 3),
        out_specs=[_row_spec(tm, s.shape[1]) for s in out_shape],
        out_shape=out_shape,
        compiler_params=_cparams(("parallel",)),
    )(x, *consts, rc, ra, rb)


def _attn_prompt_kernel(q_ref, k_ref, v_ref, o_ref):
    i = pl.program_id(1)
    tq = tk = ATTN_TQ
    row_chunk = lax.broadcasted_iota(jnp.int32, (tq, tk), 0) // CHUNK
    col_chunk = lax.broadcasted_iota(jnp.int32, (tq, tk), 1) // CHUNK
    visible = col_chunk <= row_chunk
    qs = [q_ref[:, hh * HEAD_PAD:(hh + 1) * HEAD_PAD] for hh in range(2)]

    def scores(hh, j):
        start = pl.multiple_of(j * tk, tk)
        k = k_ref[pl.ds(start, tk), hh * HEAD_PAD:(hh + 1) * HEAD_PAD]
        v = v_ref[pl.ds(start, tk), hh * HEAD_PAD:(hh + 1) * HEAD_PAD]
        s = lax.dot_general(qs[hh], k, (((1,), (1,)), ((), ())), preferred_element_type=F32)
        return s, v

    state = []
    for hh in range(2):
        s, v = scores(hh, i)
        s = jnp.where(visible, s, NEG_INF)
        m = jnp.max(s, axis=-1, keepdims=True)
        p = jnp.exp2((s - m).astype(BF16))
        state += [m, jnp.dot(p, v, preferred_element_type=F32)]

    def body(j, state):
        new = []
        for hh in range(2):
            m, acc = state[2 * hh:2 * hh + 2]
            s, v = scores(hh, j)
            m_new = jnp.maximum(m, jnp.max(s, axis=-1, keepdims=True))
            alpha = jnp.exp2(m - m_new)
            p = jnp.exp2((s - m_new).astype(BF16))
            acc = alpha * acc + jnp.dot(p, v, preferred_element_type=F32)
            new += [m_new, acc]
        return tuple(new)

    state = lax.fori_loop(0, i, body, tuple(state))
    outs = []
    for hh in range(2):
        acc = state[2 * hh + 1]
        outs.append(acc[:, :V_HEAD] / acc[:, V_HEAD:V_HEAD + 1])
    o_ref[...] = jnp.concatenate(outs, axis=-1).astype(o_ref.dtype)


def _attn_prompt(q, k, v):
    t = q.shape[0]
    pairs = MLA_HEADS // 2
    return pl.pallas_call(
        _attn_prompt_kernel, name="attn_prompt",
        grid=(pairs, t // ATTN_TQ),
        in_specs=[
            pl.BlockSpec((ATTN_TQ, 2 * HEAD_PAD), lambda p, i: (i, p)),
            pl.BlockSpec((t, 2 * HEAD_PAD), lambda p, i: (0, p)),
            pl.BlockSpec((t, 2 * HEAD_PAD), lambda p, i: (0, p)),
        ],
        out_specs=pl.BlockSpec((ATTN_TQ, 2 * V_HEAD), lambda p, i: (i, p)),
        out_shape=jax.ShapeDtypeStruct((t, MLA_HEADS * V_HEAD), BF16),
        compiler_params=_cparams(("parallel", "parallel")),
    )(q, k, v)


def _attn_sample_kernel(q_ref, cn_ref, krn_ref, cp_ref, krp_ref, wukt_ref, wuv_ref, o_ref):
    cp = cp_ref[...].astype(BF16)
    krp = krp_ref[...].astype(BF16)
    cn = cn_ref[...].astype(BF16)
    krn = krn_ref[...].astype(BF16)
    ts = cn.shape[0]
    dn = (((1,), (1,)), ((), ()))
    outs = []
    for hp in range(MLA_HEADS // 2):
        qa, qr = [], []
        for hh in range(2):
            h = 2 * hp + hh
            qh = q_ref[:, h * HEAD_PAD:(h + 1) * HEAD_PAD]
            qa.append(jnp.dot(qh[:, :QK_NOPE], wukt_ref[h], preferred_element_type=F32))
            qr.append(qh[:, QK_NOPE:QK_NOPE + QK_ROPE])
        qa = jnp.concatenate(qa, axis=0).astype(BF16)
        qr = jnp.concatenate(qr, axis=0)
        s_past = (lax.dot_general(qa, cp, dn, preferred_element_type=F32)
                  + lax.dot_general(qr, krp, dn, preferred_element_type=F32))
        s_new = (lax.dot_general(qa, cn, dn, preferred_element_type=F32)
                 + lax.dot_general(qr, krn, dn, preferred_element_type=F32))
        m = jnp.maximum(jnp.max(s_past, axis=-1, keepdims=True),
                        jnp.max(s_new, axis=-1, keepdims=True))
        p_past = jnp.exp2(s_past - m)
        p_new = jnp.exp2(s_new - m)
        l = (jnp.sum(p_past, axis=-1, keepdims=True)
             + jnp.sum(p_new, axis=-1, keepdims=True))
        o_lat = (jnp.dot(p_past.astype(BF16), cp, preferred_element_type=F32)
                 + jnp.dot(p_new.astype(BF16), cn, preferred_element_type=F32)) / l
        for hh in range(2):
            h = 2 * hp + hh
            outs.append(_bdot(o_lat[hh * ts:(hh + 1) * ts], wuv_ref[h]))
    o_ref[...] = jnp.concatenate(outs, axis=-1).astype(o_ref.dtype)


def _attn_sample(q, c_new, kr_new, c_past, kr_past, wukt, wuv):
    bs, past, _ = c_past.shape
    ts = q.shape[0] // bs
    return pl.pallas_call(
        _attn_sample_kernel, name="attn_sample",
        grid=(bs,),
        in_specs=[
            pl.BlockSpec((ts, MLA_HEADS * HEAD_PAD), lambda b: (b, 0)),
            pl.BlockSpec((ts, KV_LORA), lambda b: (b, 0)),
            pl.BlockSpec((ts, QK_ROPE), lambda b: (b, 0)),
            pl.BlockSpec((None, past, KV_LORA), lambda b: (b, 0, 0)),
            pl.BlockSpec((None, past, QK_ROPE), lambda b: (b, 0, 0)),
            _full_spec(wukt), _full_spec(wuv),
        ],
        out_specs=pl.BlockSpec((ts, MLA_HEADS * V_HEAD), lambda b: (b, 0)),
        out_shape=jax.ShapeDtypeStruct((bs * ts, MLA_HEADS * V_HEAD), BF16),
        compiler_params=_cparams(("parallel",)),
    )(q, c_new, kr_new, c_past, kr_past, wukt, wuv)


def _lru_kernel(xb_ref, yb_ref, conv0_ref, h0_ref, cw_ref, cb_ref, wax_ref, bax_ref, ap_ref,
                rec_ref, conv_out_ref, h_out_ref, xcat, hc, *, tb):
    t = pl.program_id(1)
    nt = pl.num_programs(1)
    tail = CONV_WIDTH - 1
    base = SUBLANES

    @pl.when(t == 0)
    def _():
        xcat[base - tail:base, :] = conv0_ref[...]
        hc[...] = h0_ref[...]

    @pl.when(t > 0)
    def _():
        xcat[0:base, :] = xcat[tb:tb + base, :]

    xcat[base:base + tb, :] = xb_ref[...]
    xc = cb_ref[...]
    for tap in range(CONV_WIDTH):
        xc = xc + xcat[base - tail + tap:base - tail + tap + tb, :] * cw_ref[tap:tap + 1, :]

    gates = _bdot(xc, wax_ref[...]) + bax_ref[...]
    r = jax.nn.sigmoid(gates[:, :LRU_WIDTH])
    ig = jax.nn.sigmoid(gates[:, LRU_WIDTH:])
    log_a = -RG_C * r * ap_ref[...]
    a = jnp.exp(log_a)
    u = jnp.sqrt(jnp.tanh(-log_a) * (1.0 + a * a)) * (ig * xc)

    row = lax.broadcasted_iota(jnp.int32, (tb, LRU_WIDTH), 0)
    s = 1
    while s < tb:
        keep = row >= s
        u = jnp.where(keep, a * pltpu.roll(u, s, 0) + u, u)
        a = jnp.where(keep, a * pltpu.roll(a, s, 0), a)
        s *= 2
    h = a * hc[...] + u
    hc[...] = h[tb - 1:tb, :]
    rec_ref[...] = (h * jax.nn.gelu(yb_ref[...])).astype(rec_ref.dtype)

    @pl.when(t == nt - 1)
    def _():
        conv_out_ref[...] = xcat[base + tb - tail:base + tb, :]
        h_out_ref[...] = h[tb - 1:tb, :]


def _lru(xb, yb, conv0, h0, w):
    b, t, wd = xb.shape
    tb = min(LRU_TILE, t)
    tail = CONV_WIDTH - 1
    consts = [w['conv_w'], w['conv_b'], w['w_ax'], w['b_ax'], w['a_param']]
    seq = pl.BlockSpec((None, tb, wd), lambda bi, ti: (bi, ti, 0))
    return pl.pallas_call(
        functools.partial(_lru_kernel, tb=tb), name="rg_lru",
        grid=(b, t // tb),
        in_specs=[seq, seq,
                  pl.BlockSpec((None, tail, wd), lambda bi, ti: (bi, 0, 0)),
                  pl.BlockSpec((None, 1, wd), lambda bi, ti: (bi, 0, 0))]
                 + [pl.BlockSpec(c.shape, lambda bi, ti: (0, 0)) for c in consts],
        out_specs=[seq,
                   pl.BlockSpec((None, tail, wd), lambda bi, ti: (bi, 0, 0)),
                   pl.BlockSpec((None, 1, wd), lambda bi, ti: (bi, 0, 0))],
        out_shape=[jax.ShapeDtypeStruct((b, t, wd), BF16),
                   jax.ShapeDtypeStruct((b, tail, wd), F32),
                   jax.ShapeDtypeStruct((b, 1, wd), F32)],
        scratch_shapes=[pltpu.VMEM((tb + 2 * SUBLANES, wd), F32), pltpu.VMEM((1, wd), F32)],
        compiler_params=_cparams(("parallel", "arbitrary")),
    )(xb, yb, conv0, h0, *consts)


def _out_ln_kernel(a_ref, b_ref, x_ref, wa_ref, wb_ref, g_ref, beta_ref, o_ref, *, alpha):
    y = (jnp.dot(a_ref[...], wa_ref[...], preferred_element_type=F32)
         + jnp.dot(b_ref[...], wb_ref[...], preferred_element_type=F32))
    o_ref[...] = _layer_norm(alpha * x_ref[...] + y, g_ref[...], beta_ref[...])


def _out_ln(att, rec, x, wa, wb, g, beta, alpha):
    n, dm = x.shape
    tm = min(ROW_TILE, n)
    consts = [wa, wb, g, beta]
    return pl.pallas_call(
        functools.partial(_out_ln_kernel, alpha=alpha), name="out_ln",
        grid=(n // tm,),
        in_specs=[_row_spec(tm, att.shape[1]), _row_spec(tm, rec.shape[1]), _row_spec(tm, dm)]
                 + [_full_spec(c) for c in consts],
        out_specs=_row_spec(tm, dm),
        out_shape=jax.ShapeDtypeStruct((n, dm), F32),
        compiler_params=_cparams(("parallel",)),
    )(att, rec, x, *consts)


def _moe_ln_kernel(x_ref, wrh_ref, wrl_ref, wg_ref, wu_ref, wd_ref, g_ref, beta_ref, o_ref,
                   acc, gates, xb16, *, alpha):
    grp = pl.program_id(1)
    tm = x_ref.shape[0]
    lane = lax.broadcasted_iota(jnp.int32, (tm, ROUTER_W), 1).astype(F32)

    @pl.when(grp == 0)
    def _():
        x = x_ref[...]
        xh = x.astype(BF16)
        xb16[...] = xh
        xl = (x - xh.astype(F32)).astype(BF16)
        wh = wrh_ref[...]
        logits = (jnp.dot(xh, wh, preferred_element_type=F32)
                  + jnp.dot(xl, wh, preferred_element_type=F32)
                  + jnp.dot(xh, wrl_ref[...], preferred_element_type=F32))
        big = float(ROUTER_W)
        lg = jnp.where(lane < N_GROUPS, logits, -jnp.inf)
        mg = jnp.max(lg, axis=-1, keepdims=True)
        gi = jnp.min(jnp.where(lg == mg, lane, big), axis=-1, keepdims=True)
        p_top = 1.0 / jnp.sum(jnp.exp(lg - mg), axis=-1, keepdims=True)
        lo = N_GROUPS + gi * EXPERTS_PER_GROUP
        le = jnp.where((lane >= lo) & (lane < lo + EXPERTS_PER_GROUP), logits, -jnp.inf)
        v1 = jnp.max(le, axis=-1, keepdims=True)
        i1 = jnp.min(jnp.where(le == v1, lane, big), axis=-1, keepdims=True)
        le2 = jnp.where(lane == i1, -jnp.inf, le)
        v2 = jnp.max(le2, axis=-1, keepdims=True)
        i2 = jnp.min(jnp.where(le2 == v2, lane, big), axis=-1, keepdims=True)
        e2 = jnp.exp(v2 - v1)
        w1 = p_top / (1.0 + e2)
        w2 = p_top * e2 / (1.0 + e2)
        gates[...] = jnp.where(lane == i1, w1, 0.0) + jnp.where(lane == i2, w2, 0.0)
        acc[...] = jnp.zeros_like(acc)

    xb = xb16[...]
    gt = gates[...]
    first = (N_GROUPS + grp * EXPERTS_PER_GROUP).astype(F32)
    for el in range(EXPERTS_PER_GROUP):
        sl = slice(el * EXPERT_HIDDEN, (el + 1) * EXPERT_HIDDEN)
        ge = jnp.sum(jnp.where(lane == first + el, gt, 0.0), axis=-1, keepdims=True)
        h = jnp.dot(xb, wg_ref[:, sl], preferred_element_type=F32)
        u = jnp.dot(xb, wu_ref[:, sl], preferred_element_type=F32)
        act = (jax.nn.silu(h) * u * ge).astype(BF16)
        acc[...] += jnp.dot(act, wd_ref[sl, :], preferred_element_type=F32)

    @pl.when(grp == pl.num_programs(1) - 1)
    def _():
        o_ref[...] = _layer_norm(alpha * x_ref[...] + acc[...], g_ref[...], beta_ref[...])


def _moe_ln(x, w, g, beta, alpha):
    n, dm = x.shape
    tm = min(MOE_ROW_TILE, n)
    ng, _, gh = w['w_gate'].shape
    return pl.pallas_call(
        functools.partial(_moe_ln_kernel, alpha=alpha), name="moe_ln",
        grid=(n // tm, ng),
        in_specs=[
            pl.BlockSpec((tm, dm), lambda i, e: (i, 0)),
            pl.BlockSpec(w['w_router_hi'].shape, lambda i, e: (0, 0)),
            pl.BlockSpec(w['w_router_lo'].shape, lambda i, e: (0, 0)),
            pl.BlockSpec((None, dm, gh), lambda i, e: (e, 0, 0)),
            pl.BlockSpec((None, dm, gh), lambda i, e: (e, 0, 0)),
            pl.BlockSpec((None, gh, dm), lambda i, e: (e, 0, 0)),
            pl.BlockSpec(g.shape, lambda i, e: (0, 0)),
            pl.BlockSpec(beta.shape, lambda i, e: (0, 0)),
        ],
        out_specs=pl.BlockSpec((tm, dm), lambda i, e: (i, 0)),
        out_shape=jax.ShapeDtypeStruct((n, dm), F32),
        scratch_shapes=[pltpu.VMEM((tm, dm), F32), pltpu.VMEM((tm, ROUTER_W), F32),
                        pltpu.VMEM((tm, dm), BF16)],
        compiler_params=_cparams(("parallel", "arbitrary"), MOE_VMEM_LIMIT_BYTES),
    )(x, w['w_router_hi'], w['w_router_lo'], w['w_gate'], w['w_up'], w['w_down'], g, beta)


def _c_in_kernel(x_ref, w_ref, u_ref, u2_ref, us):
    u = _bdot(x_ref[...], w_ref[...])
    u_ref[...] = u
    rows = u2_ref.shape[0]
    for c in range(S5_WIDTH // LANES):
        us[c] = u[:, c * LANES:(c + 1) * LANES]
    for s in range(S5_L):
        for c in range(S5_WIDTH // LANES):
            lo = s * S5_WIDTH + c * LANES
            u2_ref[:, lo:lo + LANES] = us[c, pl.ds(s, rows, stride=S5_L), :].astype(u2_ref.dtype)


def _c_in(x, w_in):
    n, dm = x.shape
    tm = min(ROW_TILE, n)
    wu = w_in.shape[1]
    return pl.pallas_call(
        _c_in_kernel, name="c_in",
        grid=(n // tm,),
        in_specs=[_row_spec(tm, dm), _full_spec(w_in)],
        out_specs=[_row_spec(tm, wu), _row_spec(tm // S5_L, S5_L * wu)],
        out_shape=[jax.ShapeDtypeStruct((n, wu), F32),
                   jax.ShapeDtypeStruct((n // S5_L, S5_L * wu), BF16)],
        scratch_shapes=[pltpu.VMEM((wu // LANES, tm, LANES), F32)],
        compiler_params=_cparams(("parallel",)),
    )(x, w_in)


def _s5_super_rows(u_ref, q):
    tiles = [u_ref[:, s * S5_WIDTH + q * LANES:s * S5_WIDTH + (q + 1) * LANES]
             for s in range(S5_L)]
    return jnp.concatenate(tiles, axis=1).astype(BF16)


def _s5_e_kernel(u_ref, ere_ref, eim_ref, ere_out, eim_out):
    for q in range(S5_NSUPER):
        uq = _s5_super_rows(u_ref, q)
        sl = slice(q * S5_SUPER_STATE, (q + 1) * S5_SUPER_STATE)
        ere_out[:, sl] = jnp.dot(uq, ere_ref[q], preferred_element_type=F32)
        eim_out[:, sl] = jnp.dot(uq, eim_ref[q], preferred_element_type=F32)


def _const_spec(a):
    nd = a.ndim
    return pl.BlockSpec(a.shape, lambda *_: (0,) * nd, pipeline_mode=pl.Buffered(1))


def _s5_e(u2, tabs):
    rows, wu = u2.shape
    tr = min(S5_ROW_TILE, rows)
    out = jax.ShapeDtypeStruct((rows, S5_STATE_W), F32)
    return pl.pallas_call(
        _s5_e_kernel, name="s5_e",
        grid=(rows // tr,),
        in_specs=[_row_spec(tr, wu), _const_spec(tabs['e_re']), _const_spec(tabs['e_im'])],
        out_specs=[_row_spec(tr, S5_STATE_W)] * 2,
        out_shape=[out, out],
        compiler_params=_cparams(("parallel",)),
    )(u2, tabs['e_re'], tabs['e_im'])


def _s5_scan_kernel(ere_ref, eim_ref, h0r_ref, h0i_ref, stp_re_ref, stp_im_ref,
                    apw_re_ref, apw_im_ref, hpr_ref, hpi_ref, hfr_ref, hfi_ref, *, nc, last_row):
    w = ere_ref.shape[-1]
    row = lax.broadcasted_iota(jnp.int32, (SUBLANES, w), 0)
    apr = apw_re_ref[...]
    api = apw_im_ref[...]

    def tile(i, carry):
        cr, ci = carry
        start = pl.multiple_of(i * SUBLANES, SUBLANES)
        sr = ere_ref[pl.ds(start, SUBLANES), :]
        si = eim_ref[pl.ds(start, SUBLANES), :]
        for k in range(3):
            s = 1 << k
            keep = row >= s
            ar = stp_re_ref[k:k + 1, :]
            ai = stp_im_ref[k:k + 1, :]
            pr = pltpu.roll(sr, s, 0)
            pi = pltpu.roll(si, s, 0)
            sr, si = (jnp.where(keep, sr + ar * pr - ai * pi, sr),
                      jnp.where(keep, si + ar * pi + ai * pr, si))
        hr = sr + apr * cr - api * ci
        hi = si + apr * ci + api * cr
        first = row == 0
        hpr_ref[pl.ds(start, SUBLANES), :] = jnp.where(first, cr, pltpu.roll(hr, 1, 0))
        hpi_ref[pl.ds(start, SUBLANES), :] = jnp.where(first, ci, pltpu.roll(hi, 1, 0))
        return hr, hi

    def body(i, carry):
        hr, hi = tile(i, carry)
        return hr[SUBLANES - 1:SUBLANES, :], hi[SUBLANES - 1:SUBLANES, :]

    ntile = nc // SUBLANES
    carry = lax.fori_loop(0, ntile - 1, body, (h0r_ref[...], h0i_ref[...]))
    hr, hi = tile(ntile - 1, carry)
    hfr_ref[...] = hr[last_row:last_row + 1, :]
    hfi_ref[...] = hi[last_row:last_row + 1, :]


def _s5_scan(e_re, e_im, h0r, h0i, tabs, n_chunks):
    b, nc, w = e_re.shape
    last_row = (n_chunks - 1) % SUBLANES
    wb = S5_SCAN_LANES
    consts = [tabs['step_re'], tabs['step_im'], tabs['apow_re'], tabs['apow_im']]
    seq = pl.BlockSpec((None, nc, wb), lambda bi, li: (bi, 0, li))
    vec = pl.BlockSpec((None, 1, wb), lambda bi, li: (bi, 0, li))
    return pl.pallas_call(
        functools.partial(_s5_scan_kernel, nc=nc, last_row=last_row), name="s5_scan",
        grid=(b, w // wb),
        in_specs=[seq, seq, vec, vec]
                 + [pl.BlockSpec((c.shape[0], wb), lambda bi, li: (0, li)) for c in consts],
        out_specs=[seq, seq, vec, vec],
        out_shape=[jax.ShapeDtypeStruct((b, nc, w), F32)] * 2
                  + [jax.ShapeDtypeStruct((b, 1, w), F32)] * 2,
        compiler_params=_cparams(("parallel", "parallel")),
    )(e_re, e_im, h0r, h0i, *consts)


def _s5_y_kernel(u_ref, hpr_ref, hpi_ref, m_ref, fre_ref, fim_ref, y_ref):
    for q in range(S5_NSUPER):
        uq = _s5_super_rows(u_ref, q)
        sl = slice(q * S5_SUPER_STATE, (q + 1) * S5_SUPER_STATE)
        yq = (jnp.dot(uq, m_ref[q], preferred_element_type=F32)
              + _bdot(hpr_ref[:, sl], fre_ref[q])
              + _bdot(hpi_ref[:, sl], fim_ref[q]))
        for t in range(S5_L):
            y_ref[:, t * S5_WIDTH + q * LANES:t * S5_WIDTH + (q + 1) * LANES] = (
                yq[:, t * LANES:(t + 1) * LANES])


def _s5_y(u2, hp_re, hp_im, tabs):
    rows, wu = u2.shape
    tr = min(S5_ROW_TILE, rows)
    consts = [tabs['m'], tabs['f_re'], tabs['f_im']]
    return pl.pallas_call(
        _s5_y_kernel, name="s5_y",
        grid=(rows // tr,),
        in_specs=[_row_spec(tr, wu), _row_spec(tr, S5_STATE_W), _row_spec(tr, S5_STATE_W)]
                 + [_const_spec(c) for c in consts],
        out_specs=_row_spec(tr, wu),
        out_shape=jax.ShapeDtypeStruct((rows, wu), F32),
        compiler_params=_cparams(("parallel",)),
    )(u2, hp_re, hp_im, *consts)


def _c_out_kernel(y2_ref, u_ref, x_ref, d_ref, wglu_ref, wout_ref, g_ref, beta_ref, o_ref, ys,
                  *, alpha):
    rows = y2_ref.shape[0]
    ntile = S5_WIDTH // LANES
    for s in range(S5_L):
        for c in range(ntile):
            lo = s * S5_WIDTH + c * LANES
            ys[c, pl.ds(s, rows, stride=S5_L), :] = y2_ref[:, lo:lo + LANES]
    y = jnp.concatenate([ys[c] for c in range(ntile)], axis=1) + d_ref[...] * u_ref[...]
    vg = _bdot(jax.nn.gelu(y), wglu_ref[...])
    half = vg.shape[1] // 2
    z = vg[:, :half] * jax.nn.sigmoid(vg[:, half:])
    o = _bdot(z, wout_ref[...])
    o_ref[...] = _layer_norm(alpha * x_ref[...] + o, g_ref[...], beta_ref[...])


def _c_out(y2, u, x, w, g, beta, alpha):
    n, dm = x.shape
    tm = min(ROW_TILE, n)
    consts = [w['d'], w['w_glu'], w['w_out'], g, beta]
    return pl.pallas_call(
        functools.partial(_c_out_kernel, alpha=alpha), name="c_out",
        grid=(n // tm,),
        in_specs=[_row_spec(tm // S5_L, y2.shape[1]), _row_spec(tm, u.shape[1]),
                  _row_spec(tm, dm)] + [_full_spec(c) for c in consts],
        out_specs=_row_spec(tm, dm),
        out_shape=jax.ShapeDtypeStruct((n, dm), F32),
        scratch_shapes=[pltpu.VMEM((u.shape[1] // LANES, tm, LANES), F32)],
        compiler_params=_cparams(("parallel",)),
    )(y2, u, x, *consts)


def _prep_ab(w_in, q_g, kv_g, w_uq, w_ukv, conv_w, conv_b, wa, ba, wx, bx, a_param, w_out):
    dm = w_in.shape[0]
    s0 = Q_LORA
    s1 = s0 + KV_LORA
    s2 = s1 + QK_ROPE
    s3 = s2 + LRU_WIDTH
    pad_lo = jnp.zeros((dm, QK_NOPE), F32)
    pad_hi = jnp.zeros((dm, HEAD_PAD - QK_HEAD), F32)
    w_in_p = jnp.concatenate([w_in[:, :s1], w_in[:, s2:s3], w_in[:, s3:],
                              pad_lo, w_in[:, s1:s2], pad_hi], axis=1).astype(BF16)
    wq = w_uq.reshape(Q_LORA, MLA_HEADS, QK_HEAD)
    wq = jnp.pad(wq, ((0, 0), (0, 0), (0, HEAD_PAD - QK_HEAD)))
    wq = wq.reshape(Q_LORA, MLA_HEADS * HEAD_PAD).astype(BF16)
    wkv = w_ukv.reshape(KV_LORA, MLA_HEADS, QK_NOPE + V_HEAD)
    wk = jnp.pad(wkv[:, :, :QK_NOPE], ((0, 0), (0, 0), (0, HEAD_PAD - QK_NOPE)))
    wk = wk.reshape(KV_LORA, MLA_HEADS * HEAD_PAD).astype(BF16)
    wv = jnp.pad(wkv[:, :, QK_NOPE:], ((0, 0), (0, 0), (0, HEAD_PAD - V_HEAD)))
    wv = wv.reshape(KV_LORA, MLA_HEADS * HEAD_PAD).astype(BF16)
    wukt = jnp.transpose(wkv[:, :, :QK_NOPE], (1, 2, 0)).astype(BF16)
    wuv_h = jnp.transpose(wkv[:, :, QK_NOPE:], (1, 0, 2)).astype(BF16)
    eye = jnp.eye(LRU_BLOCKS, dtype=F32)

    def blockdiag(wb):
        return jnp.einsum('hij,hk->hikj', wb, eye).reshape(LRU_WIDTH, LRU_WIDTH)

    w_ax = jnp.concatenate([blockdiag(wa), blockdiag(wx)], axis=1).astype(BF16)
    b_ax = jnp.concatenate([ba, bx])[None, :]
    att_w = MLA_HEADS * V_HEAD
    return dict(w_in=w_in_p, q_g=q_g[None, :], kv_g=kv_g[None, :], w_uq=wq, w_uk=wk, w_uv=wv,
                wukt=wukt, wuv_h=wuv_h, conv_w=conv_w, conv_b=conv_b[None, :], w_ax=w_ax,
                b_ax=b_ax, a_param=jax.nn.softplus(-a_param)[None, :],
                w_out_att=w_out[:att_w].astype(BF16), w_out_rec=w_out[att_w:].astype(BF16))


def _rope_tables(pos):
    half = QK_ROPE // 2
    inv = ROPE_THETA ** (-jnp.arange(half, dtype=F32) / half)
    ang = pos.astype(F32)[:, None] * inv[None, :]
    cos, sin = jnp.cos(ang), jnp.sin(ang)
    n = pos.shape[0]
    z = lambda k: jnp.zeros((n, k), F32)
    rc = jnp.concatenate([jnp.ones((n, QK_NOPE), F32), cos, cos, z(HEAD_PAD - QK_HEAD)], axis=1)
    ra = jnp.concatenate([z(QK_NOPE + half), sin, z(HEAD_PAD - QK_HEAD)], axis=1)
    rb = jnp.concatenate([z(QK_NOPE), -sin, z(half + HEAD_PAD - QK_HEAD)], axis=1)
    return rc, ra, rb


def _prep_moe(w_group, w_expert, w_gate, w_up, w_down):
    dm = w_group.shape[0]
    pad = jnp.zeros((dm, ROUTER_W - N_GROUPS - N_EXPERTS), F32)
    w_router = jnp.concatenate([w_group, w_expert, pad], axis=1)
    w_router_hi = w_router.astype(BF16)
    w_router_lo = (w_router - w_router_hi.astype(F32)).astype(BF16)
    ne, _, eh = w_gate.shape
    ng = ne // EXPERTS_PER_GROUP

    def by_group(wt):
        wt = wt.astype(BF16).reshape(ng, EXPERTS_PER_GROUP, dm, eh)
        return jnp.transpose(wt, (0, 2, 1, 3)).reshape(ng, dm, EXPERTS_PER_GROUP * eh)

    return dict(w_router_hi=w_router_hi, w_router_lo=w_router_lo, w_gate=by_group(w_gate),
                w_up=by_group(w_up),
                w_down=w_down.astype(BF16).reshape(ng, EXPERTS_PER_GROUP * eh, dm))


def _cmul(ar, ai, br, bi):
    return ar * br - ai * bi, ar * bi + ai * br


def _prep_s5(lam_re, lam_im, log_step, b_re, b_im, c_re, c_im):
    hi = lax.Precision.HIGHEST
    L, G, P, C = S5_L, S5_GROUPS, S5_STATE, S5_GROUP
    lr, li = lam_re, lam_im
    dt = jnp.exp(log_step)[:, None]
    mag = jnp.exp(lr * dt)
    ar, ai = mag * jnp.cos(li * dt), mag * jnp.sin(li * dt)
    den = lr * lr + li * li
    nr = ar - 1.0
    cr = (nr * lr + ai * li) / den
    ci = (ai * lr - nr * li) / den
    bbr = cr[..., None] * b_re - ci[..., None] * b_im
    bbi = cr[..., None] * b_im + ci[..., None] * b_re
    prs, pis = [jnp.ones_like(ar)], [jnp.zeros_like(ar)]
    for _ in range(L):
        nr_, ni_ = _cmul(prs[-1], pis[-1], ar, ai)
        prs.append(nr_)
        pis.append(ni_)
    pr = jnp.stack(prs)
    pi = jnp.stack(pis)
    xr = pr[:L, :, :, None] * bbr - pi[:L, :, :, None] * bbi
    xi = pr[:L, :, :, None] * bbi + pi[:L, :, :, None] * bbr
    kk = (jnp.einsum('gcp,kgpd->kgcd', c_re, xr, precision=hi)
          - jnp.einsum('gcp,kgpd->kgcd', c_im, xi, precision=hi))
    s_idx = jnp.arange(L)[:, None]
    t_idx = jnp.arange(L)[None, :]
    lag = t_idx - s_idx
    km = jnp.where((lag >= 0)[:, :, None, None, None], kk[jnp.clip(lag, 0, L - 1)], 0.0)
    m = jnp.transpose(km, (2, 0, 4, 1, 3))
    e_re = jnp.transpose(xr[::-1], (1, 0, 3, 2))
    e_im = jnp.transpose(xi[::-1], (1, 0, 3, 2))
    p1r = jnp.transpose(pr[1:], (1, 2, 0))[:, :, :, None]
    p1i = jnp.transpose(pi[1:], (1, 2, 0))[:, :, :, None]
    cre_t = jnp.transpose(c_re, (0, 2, 1))[:, :, None, :]
    cim_t = jnp.transpose(c_im, (0, 2, 1))[:, :, None, :]
    f_re = cre_t * p1r - cim_t * p1i
    f_im = -cre_t * p1i - cim_t * p1r

    nq, sg = S5_NSUPER, S5_SUPER
    eye = jnp.eye(sg, dtype=F32)
    m = jnp.einsum('qgsdtc,gh->qsgdthc', m.reshape(nq, sg, L, C, L, C), eye)
    m = m.reshape(nq, L * sg * C, L * sg * C)

    def pack_e(e):
        e = jnp.einsum('qgsdp,gh->qsgdhp', e.reshape(nq, sg, L, C, P), eye)
        return e.reshape(nq, L * sg * C, sg * P).astype(BF16)

    def pack_f(f):
        f = jnp.einsum('qgptc,gh->qgpthc', f.reshape(nq, sg, P, L, C), eye)
        return f.reshape(nq, sg * P, L * sg * C).astype(BF16)

    alr, ali = pr[L].reshape(1, G * P), pi[L].reshape(1, G * P)
    qr, qi = [alr], [ali]
    for _ in range(SUBLANES - 1):
        nr_, ni_ = _cmul(qr[-1], qi[-1], alr, ali)
        qr.append(nr_)
        qi.append(ni_)
    apow_re = jnp.concatenate(qr, axis=0)
    apow_im = jnp.concatenate(qi, axis=0)
    step_re = jnp.concatenate([qr[0], qr[1], qr[3]], axis=0)
    step_im = jnp.concatenate([qi[0], qi[1], qi[3]], axis=0)
    return dict(m=m.astype(BF16), e_re=pack_e(e_re), e_im=pack_e(e_im), f_re=pack_f(f_re),
                f_im=pack_f(f_im), apow_re=apow_re, apow_im=apow_im,
                step_re=step_re, step_im=step_im)


def _mixer_ab(x, pos, past_ckv, past_krope, conv0, h0, w):
    b, t, dm = x.shape
    xf = x.reshape(b * t, dm)
    tabs = _rope_tables(pos)
    if b > 1:
        tabs = tuple(jnp.tile(tb, (b, 1)) for tb in tabs)
    q, k, v, c_new, kr_new, xb, yb = _ab_in(xf, w, tabs)
    if past_ckv is None:
        att = _attn_prompt(q, k, v)
    else:
        att = _attn_sample(q, c_new, kr_new, past_ckv, past_krope, w['wukt'], w['wuv_h'])
    rec, conv_new, h_new = _lru(xb.reshape(b, t, -1), yb.reshape(b, t, -1), conv0,
                                h0[:, None, :], w)
    return (att, rec.reshape(b * t, -1), c_new.reshape(b, t, -1), kr_new.reshape(b, t, -1),
            conv_new, h_new[:, 0, :])


def _mixer_c(x, h0_re, h0_im, w):
    b, t, dm = x.shape
    xf = x.reshape(b * t, dm)
    nc = t // S5_L
    assert t % S5_L == 0 and nc % SUBLANES == 0
    u, u2 = _c_in(xf, w['w_in'])
    e_re, e_im = _s5_e(u2, w['tabs'])
    hp_re, hp_im, hf_re, hf_im = _s5_scan(
        e_re.reshape(b, nc, S5_STATE_W), e_im.reshape(b, nc, S5_STATE_W),
        h0_re.reshape(b, 1, S5_STATE_W), h0_im.reshape(b, 1, S5_STATE_W), w['tabs'], nc)
    y2 = _s5_y(u2, hp_re.reshape(b * nc, S5_STATE_W), hp_im.reshape(b * nc, S5_STATE_W),
               w['tabs'])
    return (y2, u, hf_re.reshape(b, S5_GROUPS, S5_STATE), hf_im.reshape(b, S5_GROUPS, S5_STATE))


def kernel(x_prompt, x_sample, cache_mla_ckv, cache_mla_krope, state_lru_conv, state_lru_h,
           state_s5_re, state_s5_im, w_in_ab, q_norm_g, kv_norm_g, w_uq, w_ukv,
           lru_conv_w, lru_conv_b, lru_w_a, lru_b_a, lru_w_x, lru_b_x, lru_a_param, w_out_ab,
           w_in_c, s5_lam_re, s5_lam_im, s5_log_step, s5_b_re, s5_b_im, s5_c_re, s5_c_im,
           s5_d, s5_w_glu, w_out_c, ln_mix_g, ln_mix_b, ln_ffn_g, ln_ffn_b,
           moe_w_group, moe_w_expert, moe_w_gate, moe_w_up, moe_w_down):
    bp, tp, dm = x_prompt.shape
    bs, ts, _ = x_sample.shape
    past = cache_mla_ckv.shape[2]
    depth = ln_mix_g.shape[0]
    alpha = (2 * depth) ** 0.25
    pos_p = jnp.arange(tp)
    pos_s = past + jnp.arange(ts)
    assert (past + ts - 1) // CHUNK <= past // CHUNK
    hp, hs = x_prompt, x_sample
    outs = {k: [] for k in ('ckv_p', 'ckv_s', 'kr_p', 'kr_s', 'cv_p', 'cv_s', 'lh_p', 'lh_s',
                            's5r_p', 's5r_s', 's5i_p', 's5i_s')}
    for layer in range(depth):
        j = layer // 2
        g_mix, b_mix = ln_mix_g[layer][None, :], ln_mix_b[layer][None, :]
        g_ffn, b_ffn = ln_ffn_g[layer][None, :], ln_ffn_b[layer][None, :]
        if layer % 2 == 0:
            w = _prep_ab(w_in_ab[j], q_norm_g[j], kv_norm_g[j], w_uq[j], w_ukv[j], lru_conv_w[j],
                         lru_conv_b[j], lru_w_a[j], lru_b_a[j], lru_w_x[j], lru_b_x[j],
                         lru_a_param[j], w_out_ab[j])
            att_p, rec_p, c1, k1, v1, h1 = _mixer_ab(
                hp, pos_p, None, None, jnp.zeros((bp, CONV_WIDTH - 1, LRU_WIDTH), F32),
                jnp.zeros((bp, LRU_WIDTH), F32), w)
            att_s, rec_s, c2, k2, v2, h2 = _mixer_ab(
                hs, pos_s, cache_mla_ckv[j], cache_mla_krope[j], state_lru_conv[j],
                state_lru_h[j], w)
            outs['ckv_p'].append(c1); outs['ckv_s'].append(c2)
            outs['kr_p'].append(k1); outs['kr_s'].append(k2)
            outs['cv_p'].append(v1); outs['cv_s'].append(v2)
            outs['lh_p'].append(h1); outs['lh_s'].append(h2)
            hp = _out_ln(att_p, rec_p, hp.reshape(bp * tp, dm), w['w_out_att'], w['w_out_rec'],
                         g_mix, b_mix, alpha)
            hs = _out_ln(att_s, rec_s, hs.reshape(bs * ts, dm), w['w_out_att'], w['w_out_rec'],
                         g_mix, b_mix, alpha)
        else:
            w = dict(w_in=w_in_c[j].astype(BF16), d=s5_d[j][None, :],
                     w_glu=s5_w_glu[j].astype(BF16), w_out=w_out_c[j].astype(BF16),
                     tabs=_prep_s5(s5_lam_re[j], s5_lam_im[j], s5_log_step[j], s5_b_re[j],
                                   s5_b_im[j], s5_c_re[j], s5_c_im[j]))
            zero_state = jnp.zeros((bp, S5_GROUPS, S5_STATE), F32)
            y_p, u_p, r1, i1 = _mixer_c(hp, zero_state, zero_state, w)
            y_s, u_s, r2, i2 = _mixer_c(hs, state_s5_re[j], state_s5_im[j], w)
            outs['s5r_p'].append(r1); outs['s5r_s'].append(r2)
            outs['s5i_p'].append(i1); outs['s5i_s'].append(i2)
            hp = _c_out(y_p, u_p, hp.reshape(bp * tp, dm), w, g_mix, b_mix, alpha)
            hs = _c_out(y_s, u_s, hs.reshape(bs * ts, dm), w, g_mix, b_mix, alpha)
        wm = _prep_moe(moe_w_group[layer], moe_w_expert[layer], moe_w_gate[layer],
                       moe_w_up[layer], moe_w_down[layer])
        hp = _moe_ln(hp, wm, g_ffn, b_ffn, alpha).reshape(bp, tp, dm)
        hs = _moe_ln(hs, wm, g_ffn, b_ffn, alpha).reshape(bs, ts, dm)
    st = lambda k: jnp.stack(outs[k])
    return (hp, hs, st('ckv_p'), st('ckv_s'), st('kr_p'), st('kr_s'), st('cv_p'), st('cv_s'),
            st('lh_p'), st('lh_s'), st('s5r_p'), st('s5r_s'), st('s5i_p'), st('s5i_s'))
```

```python
import functools
import math

import jax
import jax.numpy as jnp
from jax import lax
from jax.experimental import pallas as pl
from jax.experimental.pallas import tpu as pltpu

F32 = jnp.float32
BF16 = jnp.bfloat16

CHUNK = 64
MLA_HEADS = 8
QK_NOPE = 64
QK_ROPE = 32
QK_HEAD = QK_NOPE + QK_ROPE
V_HEAD = 64
Q_LORA = 256
KV_LORA = 128
ROPE_THETA = 10000.0
MLA_SCALE = QK_HEAD ** -0.5
NEG_INF = -1e30
LRU_WIDTH = 512
LRU_BLOCKS = 8
LRU_BLOCK = LRU_WIDTH // LRU_BLOCKS
CONV_WIDTH = 4
RG_C = 8.0
S5_WIDTH = 512
S5_GROUP = 16
S5_GROUPS = S5_WIDTH // S5_GROUP
S5_STATE = 64
N_GROUPS = 4
EXPERTS_PER_GROUP = 4
N_EXPERTS = N_GROUPS * EXPERTS_PER_GROUP
EXPERT_HIDDEN = 256
LN_EPS = 1e-5
RMS_EPS = 1e-6

LANES = 128
SUBLANES = 8
VMEM_LIMIT_BYTES = 48 * 1024 * 1024
MOE_VMEM_LIMIT_BYTES = 56 * 1024 * 1024

HEAD_PAD = LANES
ROW_TILE = 512
MOE_ROW_TILE = 1024
ATTN_TQ = 512
ATTN_UNROLL = 4
ATTN_ROW_SLABS = 1
Q_SCALE = MLA_SCALE * math.log2(math.e)
LRU_TILE = 512
S5_L = 8
S5_SUPER = LANES // S5_GROUP
S5_NSUPER = S5_GROUPS // S5_SUPER
S5_SUPER_IN = S5_L * LANES
S5_SUPER_STATE = S5_SUPER * S5_STATE
S5_STATE_W = S5_GROUPS * S5_STATE
S5_SCAN_LANES = 256
S5_ROW_TILE = 256
ROUTER_W = LANES


def _cparams(semantics, vmem_limit_bytes=VMEM_LIMIT_BYTES):
    return pltpu.CompilerParams(dimension_semantics=semantics,
                                vmem_limit_bytes=vmem_limit_bytes)


def _full_spec(a):
    nd = a.ndim
    return pl.BlockSpec(a.shape, lambda *_: (0,) * nd)


def _row_spec(tm, width):
    return pl.BlockSpec((tm, width), lambda i: (i, 0))


def _layer_norm(z, g, b):
    mu = jnp.mean(z, axis=-1, keepdims=True)
    zc = z - mu
    var = jnp.mean(zc * zc, axis=-1, keepdims=True)
    return zc * lax.rsqrt(var + LN_EPS) * g + b


def _rms_norm(z, g):
    ms = jnp.mean(z * z, axis=-1, keepdims=True)
    return z * lax.rsqrt(ms + RMS_EPS) * g


def _bdot(a, b):
    return jnp.dot(a.astype(BF16), b.astype(BF16), preferred_element_type=F32)


def _ab_in_kernel(x_ref, win_ref, qg_ref, kvg_ref, wuq_ref, wuk_ref, wuv_ref,
                  rc_ref, ra_ref, rb_ref,
                  q_out, k_out, v_out, c_out, kr_out, xb_out, yb_out, *, dm):
    proj = _bdot(x_ref[...], win_ref[...])
    o0 = Q_LORA
    o1 = o0 + KV_LORA
    o2 = o1 + LRU_WIDTH
    o3 = o2 + LRU_WIDTH
    q_lat = proj[:, 0:o0]
    kv_lat = proj[:, o0:o1]
    xb_out[...] = proj[:, o1:o2]
    yb_out[...] = proj[:, o2:o3]
    krp = proj[:, o3:o3 + HEAD_PAD]

    qn = _rms_norm(q_lat, qg_ref[...])
    cn = _rms_norm(kv_lat, kvg_ref[...])
    c_out[...] = cn
    q = _bdot(qn, wuq_ref[...])
    kk = _bdot(cn, wuk_ref[...])
    lane = lax.broadcasted_iota(jnp.int32, (1, MLA_HEADS * HEAD_PAD), 1)
    ones_col = (lane % HEAD_PAD == V_HEAD).astype(F32)
    v_out[...] = (_bdot(cn, wuv_ref[...]) + ones_col).astype(v_out.dtype)

    rc = rc_ref[...]
    ra = ra_ref[...]
    rb = rb_ref[...]
    half = QK_ROPE // 2

    def rope(z):
        return (z * rc + pltpu.roll(z, half, 1) * ra
                + pltpu.roll(z, HEAD_PAD - half, 1) * rb)

    kr = rope(krp)
    kr_out[...] = kr[:, QK_NOPE:QK_NOPE + QK_ROPE]
    for h in range(MLA_HEADS):
        sl = slice(h * HEAD_PAD, (h + 1) * HEAD_PAD)
        q_out[:, sl] = (rope(q[:, sl]) * Q_SCALE).astype(q_out.dtype)
        k_out[:, sl] = (kk[:, sl] + kr).astype(k_out.dtype)


def _ab_in(x, w, rope_tabs):
    n, dm = x.shape
    tm = min(ROW_TILE, n)
    rc, ra, rb = rope_tabs
    consts = [w['w_in'], w['q_g'], w['kv_g'], w['w_uq'], w['w_uk'], w['w_uv']]
    hp = MLA_HEADS * HEAD_PAD
    out_shape = [
        jax.ShapeDtypeStruct((n, hp), BF16),
        jax.ShapeDtypeStruct((n, hp), BF16),
        jax.ShapeDtypeStruct((n, hp), BF16),
        jax.ShapeDtypeStruct((n, KV_LORA), F32),
        jax.ShapeDtypeStruct((n, QK_ROPE), F32),
        jax.ShapeDtypeStruct((n, LRU_WIDTH), F32),
        jax.ShapeDtypeStruct((n, LRU_WIDTH), F32),
    ]
    return pl.pallas_call(
        functools.partial(_ab_in_kernel, dm=dm), name="ab_in",
        grid=(n // tm,),
        in_specs=([_row_spec(tm, dm)] + [_full_spec(c) for c in consts]
                  + [_row_spec(tm, HEAD_PAD)] * 3),
        out_specs=[_row_spec(tm, s.shape[1]) for s in out_shape],
        out_shape=out_shape,
        compiler_params=_cparams(("parallel",)),
    )(x, *consts, rc, ra, rb)


def _attn_prompt_kernel(q_ref, k_ref, v_ref, o_ref):
    i = pl.program_id(1)
    tq = tk = ATTN_TQ
    row_chunk = lax.broadcasted_iota(jnp.int32, (tq, tk), 0) // CHUNK
    col_chunk = lax.broadcasted_iota(jnp.int32, (tq, tk), 1) // CHUNK
    visible = col_chunk <= row_chunk
    nslab = ATTN_ROW_SLABS
    rs = tq // nslab
    chains = [(hh, r) for hh in range(2) for r in range(nslab)]
    qs = {(hh, r): q_ref[r * rs:(r + 1) * rs, hh * HEAD_PAD:(hh + 1) * HEAD_PAD]
          for hh, r in chains}

    def update(chain, j, m, acc, mask):
        hh, r = chain
        start = pl.multiple_of(j * tk, tk)
        k = k_ref[pl.ds(start, tk), hh * HEAD_PAD:(hh + 1) * HEAD_PAD]
        v = v_ref[pl.ds(start, tk), hh * HEAD_PAD:(hh + 1) * HEAD_PAD]
        s = lax.dot_general(qs[chain], k, (((1,), (1,)), ((), ())), preferred_element_type=F32)
        if mask is not None:
            s = jnp.where(mask[r * rs:(r + 1) * rs], s, NEG_INF)
        m_new = jnp.maximum(m, jnp.max(s, axis=-1, keepdims=True))
        alpha = jnp.exp2(m - m_new)
        p = jnp.exp2((s - m_new).astype(BF16))
        return m_new, alpha * acc + jnp.dot(p, v, preferred_element_type=F32)

    def body(j, carry):
        new = []
        for c, chain in enumerate(chains):
            new += update(chain, j, carry[2 * c], carry[2 * c + 1], None)
        return tuple(new)

    unroll = ATTN_UNROLL

    def body_unrolled(jj, carry):
        for b in range(unroll):
            carry = body(unroll * jj + b, carry)
        return carry

    init = []
    for _ in chains:
        init += [jnp.full((rs, 1), NEG_INF, F32), jnp.zeros((rs, HEAD_PAD), F32)]
    carry = lax.fori_loop(0, i // unroll, body_unrolled, tuple(init))
    carry = lax.fori_loop(i - i % unroll, i, body, carry)
    outs = []
    for hh in range(2):
        slabs = []
        for r in range(nslab):
            c = chains.index((hh, r))
            _, acc = update((hh, r), i, carry[2 * c], carry[2 * c + 1], visible)
            slabs.append(acc[:, :V_HEAD] / acc[:, V_HEAD:V_HEAD + 1])
        outs.append(jnp.concatenate(slabs, axis=0))
    o_ref[...] = jnp.concatenate(outs, axis=-1).astype(o_ref.dtype)


def _attn_prompt(q, k, v):
    t = q.shape[0]
    pairs = MLA_HEADS // 2
    return pl.pallas_call(
        _attn_prompt_kernel, name="attn_prompt",
        grid=(pairs, t // ATTN_TQ),
        in_specs=[
            pl.BlockSpec((ATTN_TQ, 2 * HEAD_PAD), lambda p, i: (i, p)),
            pl.BlockSpec((t, 2 * HEAD_PAD), lambda p, i: (0, p)),
            pl.BlockSpec((t, 2 * HEAD_PAD), lambda p, i: (0, p)),
        ],
        out_specs=pl.BlockSpec((ATTN_TQ, 2 * V_HEAD), lambda p, i: (i, p)),
        out_shape=jax.ShapeDtypeStruct((t, MLA_HEADS * V_HEAD), BF16),
        compiler_params=_cparams(("parallel", "parallel")),
    )(q, k, v)


def _attn_sample_kernel(q_ref, cn_ref, krn_ref, cp_ref, krp_ref, wukt_ref, wuv_ref, o_ref):
    cp = cp_ref[...].astype(BF16)
    krp = krp_ref[...].astype(BF16)
    cn = cn_ref[...].astype(BF16)
    krn = krn_ref[...].astype(BF16)
    ts = cn.shape[0]
    dn = (((1,), (1,)), ((), ()))
    outs = []
    for hp in range(MLA_HEADS // 2):
        qa, qr = [], []
        for hh in range(2):
            h = 2 * hp + hh
            qh = q_ref[:, h * HEAD_PAD:(h + 1) * HEAD_PAD]
            qa.append(jnp.dot(qh[:, :QK_NOPE], wukt_ref[h], preferred_element_type=F32))
            qr.append(qh[:, QK_NOPE:QK_NOPE + QK_ROPE])
        qa = jnp.concatenate(qa, axis=0).astype(BF16)
        qr = jnp.concatenate(qr, axis=0)
        s_past = (lax.dot_general(qa, cp, dn, preferred_element_type=F32)
                  + lax.dot_general(qr, krp, dn, preferred_element_type=F32))
        s_new = (lax.dot_general(qa, cn, dn, preferred_element_type=F32)
                 + lax.dot_general(qr, krn, dn, preferred_element_type=F32))
        m = jnp.maximum(jnp.max(s_past, axis=-1, keepdims=True),
                        jnp.max(s_new, axis=-1, keepdims=True))
        p_past = jnp.exp2(s_past - m)
        p_new = jnp.exp2(s_new - m)
        l = (jnp.sum(p_past, axis=-1, keepdims=True)
             + jnp.sum(p_new, axis=-1, keepdims=True))
        o_lat = (jnp.dot(p_past.astype(BF16), cp, preferred_element_type=F32)
                 + jnp.dot(p_new.astype(BF16), cn, preferred_element_type=F32)) / l
        for hh in range(2):
            h = 2 * hp + hh
            outs.append(_bdot(o_lat[hh * ts:(hh + 1) * ts], wuv_ref[h]))
    o_ref[...] = jnp.concatenate(outs, axis=-1).astype(o_ref.dtype)


def _attn_sample(q, c_new, kr_new, c_past, kr_past, wukt, wuv):
    bs, past, _ = c_past.shape
    ts = q.shape[0] // bs
    return pl.pallas_call(
        _attn_sample_kernel, name="attn_sample",
        grid=(bs,),
        in_specs=[
            pl.BlockSpec((ts, MLA_HEADS * HEAD_PAD), lambda b: (b, 0)),
            pl.BlockSpec((ts, KV_LORA), lambda b: (b, 0)),
            pl.BlockSpec((ts, QK_ROPE), lambda b: (b, 0)),
            pl.BlockSpec((None, past, KV_LORA), lambda b: (b, 0, 0)),
            pl.BlockSpec((None, past, QK_ROPE), lambda b: (b, 0, 0)),
            _full_spec(wukt), _full_spec(wuv),
        ],
        out_specs=pl.BlockSpec((ts, MLA_HEADS * V_HEAD), lambda b: (b, 0)),
        out_shape=jax.ShapeDtypeStruct((bs * ts, MLA_HEADS * V_HEAD), BF16),
        compiler_params=_cparams(("parallel",)),
    )(q, c_new, kr_new, c_past, kr_past, wukt, wuv)


def _lru_kernel(xb_ref, yb_ref, conv0_ref, h0_ref, cw_ref, cb_ref, wax_ref, bax_ref, ap_ref,
                rec_ref, conv_out_ref, h_out_ref, xcat, hc, *, tb):
    t = pl.program_id(1)
    nt = pl.num_programs(1)
    tail = CONV_WIDTH - 1
    base = SUBLANES

    @pl.when(t == 0)
    def _():
        xcat[base - tail:base, :] = conv0_ref[...]
        hc[...] = h0_ref[...]

    @pl.when(t > 0)
    def _():
        xcat[0:base, :] = xcat[tb:tb + base, :]

    xcat[base:base + tb, :] = xb_ref[...]
    xc = cb_ref[...]
    for tap in range(CONV_WIDTH):
        xc = xc + xcat[base - tail + tap:base - tail + tap + tb, :] * cw_ref[tap:tap + 1, :]

    gates = _bdot(xc, wax_ref[...]) + bax_ref[...]
    r = jax.nn.sigmoid(gates[:, :LRU_WIDTH])
    ig = jax.nn.sigmoid(gates[:, LRU_WIDTH:])
    log_a = -RG_C * r * ap_ref[...]
    a = jnp.exp(log_a)
    u = jnp.sqrt(jnp.tanh(-log_a) * (1.0 + a * a)) * (ig * xc)

    row = lax.broadcasted_iota(jnp.int32, (tb, LRU_WIDTH), 0)
    s = 1
    while s < tb:
        keep = row >= s
        u = jnp.where(keep, a * pltpu.roll(u, s, 0) + u, u)
        a = jnp.where(keep, a * pltpu.roll(a, s, 0), a)
        s *= 2
    h = a * hc[...] + u
    hc[...] = h[tb - 1:tb, :]
    rec_ref[...] = (h * jax.nn.gelu(yb_ref[...])).astype(rec_ref.dtype)

    @pl.when(t == nt - 1)
    def _():
        conv_out_ref[...] = xcat[base + tb - tail:base + tb, :]
        h_out_ref[...] = h[tb - 1:tb, :]


def _lru(xb, yb, conv0, h0, w):
    b, t, wd = xb.shape
    tb = min(LRU_TILE, t)
    tail = CONV_WIDTH - 1
    consts = [w['conv_w'], w['conv_b'], w['w_ax'], w['b_ax'], w['a_param']]
    seq = pl.BlockSpec((None, tb, wd), lambda bi, ti: (bi, ti, 0))
    return pl.pallas_call(
        functools.partial(_lru_kernel, tb=tb), name="rg_lru",
        grid=(b, t // tb),
        in_specs=[seq, seq,
                  pl.BlockSpec((None, tail, wd), lambda bi, ti: (bi, 0, 0)),
                  pl.BlockSpec((None, 1, wd), lambda bi, ti: (bi, 0, 0))]
                 + [pl.BlockSpec(c.shape, lambda bi, ti: (0, 0)) for c in consts],
        out_specs=[seq,
                   pl.BlockSpec((None, tail, wd), lambda bi, ti: (bi, 0, 0)),
                   pl.BlockSpec((None, 1, wd), lambda bi, ti: (bi, 0, 0))],
        out_shape=[jax.ShapeDtypeStruct((b, t, wd), BF16),
                   jax.ShapeDtypeStruct((b, tail, wd), F32),
                   jax.ShapeDtypeStruct((b, 1, wd), F32)],
        scratch_shapes=[pltpu.VMEM((tb + 2 * SUBLANES, wd), F32), pltpu.VMEM((1, wd), F32)],
        compiler_params=_cparams(("parallel", "arbitrary")),
    )(xb, yb, conv0, h0, *consts)


def _out_ln_kernel(a_ref, b_ref, x_ref, wa_ref, wb_ref, g_ref, beta_ref, o_ref, *, alpha):
    y = (jnp.dot(a_ref[...], wa_ref[...], preferred_element_type=F32)
         + jnp.dot(b_ref[...], wb_ref[...], preferred_element_type=F32))
    o_ref[...] = _layer_norm(alpha * x_ref[...] + y, g_ref[...], beta_ref[...])


def _out_ln(att, rec, x, wa, wb, g, beta, alpha):
    n, dm = x.shape
    tm = min(ROW_TILE, n)
    consts = [wa, wb, g, beta]
    return pl.pallas_call(
        functools.partial(_out_ln_kernel, alpha=alpha), name="out_ln",
        grid=(n // tm,),
        in_specs=[_row_spec(tm, att.shape[1]), _row_spec(tm, rec.shape[1]), _row_spec(tm, dm)]
                 + [_full_spec(c) for c in consts],
        out_specs=_row_spec(tm, dm),
        out_shape=jax.ShapeDtypeStruct((n, dm), F32),
        compiler_params=_cparams(("parallel",)),
    )(att, rec, x, *consts)


def _moe_ln_kernel(x_ref, wrh_ref, wrl_ref, wg_ref, wu_ref, wd_ref, g_ref, beta_ref, o_ref,
                   acc, gates, xb16, *, alpha):
    grp = pl.program_id(1)
    tm = x_ref.shape[0]
    lane = lax.broadcasted_iota(jnp.int32, (tm, ROUTER_W), 1).astype(F32)

    @pl.when(grp == 0)
    def _():
        x = x_ref[...]
        xh = x.astype(BF16)
        xb16[...] = xh
        xl = (x - xh.astype(F32)).astype(BF16)
        wh = wrh_ref[...]
        logits = (jnp.dot(xh, wh, preferred_element_type=F32)
                  + jnp.dot(xl, wh, preferred_element_type=F32)
                  + jnp.dot(xh, wrl_ref[...], preferred_element_type=F32))
        big = float(ROUTER_W)
        lg = jnp.where(lane < N_GROUPS, logits, -jnp.inf)
        mg = jnp.max(lg, axis=-1, keepdims=True)
        gi = jnp.min(jnp.where(lg == mg, lane, big), axis=-1, keepdims=True)
        p_top = 1.0 / jnp.sum(jnp.exp(lg - mg), axis=-1, keepdims=True)
        lo = N_GROUPS + gi * EXPERTS_PER_GROUP
        le = jnp.where((lane >= lo) & (lane < lo + EXPERTS_PER_GROUP), logits, -jnp.inf)
        v1 = jnp.max(le, axis=-1, keepdims=True)
        i1 = jnp.min(jnp.where(le == v1, lane, big), axis=-1, keepdims=True)
        le2 = jnp.where(lane == i1, -jnp.inf, le)
        v2 = jnp.max(le2, axis=-1, keepdims=True)
        i2 = jnp.min(jnp.where(le2 == v2, lane, big), axis=-1, keepdims=True)
        e2 = jnp.exp(v2 - v1)
        w1 = p_top / (1.0 + e2)
        w2 = p_top * e2 / (1.0 + e2)
        gates[...] = jnp.where(lane == i1, w1, 0.0) + jnp.where(lane == i2, w2, 0.0)
        acc[...] = jnp.zeros_like(acc)

    xb = xb16[...]
    gt = gates[...]
    first = (N_GROUPS + grp * EXPERTS_PER_GROUP).astype(F32)
    for el in range(EXPERTS_PER_GROUP):
        ge = jnp.sum(jnp.where(lane == first + el, gt, 0.0), axis=-1, keepdims=True)
        h = jnp.dot(xb, wg_ref[el], preferred_element_type=F32)
        u = jnp.dot(xb, wu_ref[el], preferred_element_type=F32)
        act = (jax.nn.silu(h) * u * ge).astype(BF16)
        acc[...] += jnp.dot(act, wd_ref[el], preferred_element_type=F32)

    @pl.when(grp == pl.num_programs(1) - 1)
    def _():
        o_ref[...] = _layer_norm(alpha * x_ref[...] + acc[...], g_ref[...], beta_ref[...])


def _moe_ln(x, w, g, beta, alpha):
    n, dm = x.shape
    tm = min(MOE_ROW_TILE, n)
    ne, _, eh = w['w_gate'].shape
    epg = EXPERTS_PER_GROUP
    return pl.pallas_call(
        functools.partial(_moe_ln_kernel, alpha=alpha), name="moe_ln",
        grid=(n // tm, ne // epg),
        in_specs=[
            pl.BlockSpec((tm, dm), lambda i, e: (i, 0)),
            pl.BlockSpec(w['w_router_hi'].shape, lambda i, e: (0, 0)),
            pl.BlockSpec(w['w_router_lo'].shape, lambda i, e: (0, 0)),
            pl.BlockSpec((epg, dm, eh), lambda i, e: (e, 0, 0)),
            pl.BlockSpec((epg, dm, eh), lambda i, e: (e, 0, 0)),
            pl.BlockSpec((epg, eh, dm), lambda i, e: (e, 0, 0)),
            pl.BlockSpec(g.shape, lambda i, e: (0, 0)),
            pl.BlockSpec(beta.shape, lambda i, e: (0, 0)),
        ],
        out_specs=pl.BlockSpec((tm, dm), lambda i, e: (i, 0)),
        out_shape=jax.ShapeDtypeStruct((n, dm), F32),
        scratch_shapes=[pltpu.VMEM((tm, dm), F32), pltpu.VMEM((tm, ROUTER_W), F32),
                        pltpu.VMEM((tm, dm), BF16)],
        compiler_params=_cparams(("parallel", "arbitrary"), MOE_VMEM_LIMIT_BYTES),
    )(x, w['w_router_hi'], w['w_router_lo'], w['w_gate'], w['w_up'], w['w_down'], g, beta)


def _c_in_kernel(x_ref, w_ref, u_ref, u2_ref, us):
    u = _bdot(x_ref[...], w_ref[...])
    u_ref[...] = u
    rows = u2_ref.shape[0]
    for c in range(S5_WIDTH // LANES):
        us[c] = u[:, c * LANES:(c + 1) * LANES]
    for s in range(S5_L):
        for c in range(S5_WIDTH // LANES):
            lo = s * S5_WIDTH + c * LANES
            u2_ref[:, lo:lo + LANES] = us[c, pl.ds(s, rows, stride=S5_L), :].astype(u2_ref.dtype)


def _c_in(x, w_in):
    n, dm = x.shape
    tm = min(ROW_TILE, n)
    wu = w_in.shape[1]
    return pl.pallas_call(
        _c_in_kernel, name="c_in",
        grid=(n // tm,),
        in_specs=[_row_spec(tm, dm), _full_spec(w_in)],
        out_specs=[_row_spec(tm, wu), _row_spec(tm // S5_L, S5_L * wu)],
        out_shape=[jax.ShapeDtypeStruct((n, wu), F32),
                   jax.ShapeDtypeStruct((n // S5_L, S5_L * wu), BF16)],
        scratch_shapes=[pltpu.VMEM((wu // LANES, tm, LANES), F32)],
        compiler_params=_cparams(("parallel",)),
    )(x, w_in)


def _s5_super_rows(u_ref, q):
    tiles = [u_ref[:, s * S5_WIDTH + q * LANES:s * S5_WIDTH + (q + 1) * LANES]
             for s in range(S5_L)]
    return jnp.concatenate(tiles, axis=1).astype(BF16)


def _s5_e_kernel(u_ref, ere_ref, eim_ref, ere_out, eim_out):
    for q in range(S5_NSUPER):
        uq = _s5_super_rows(u_ref, q)
        sl = slice(q * S5_SUPER_STATE, (q + 1) * S5_SUPER_STATE)
        ere_out[:, sl] = jnp.dot(uq, ere_ref[q], preferred_element_type=F32)
        eim_out[:, sl] = jnp.dot(uq, eim_ref[q], preferred_element_type=F32)


def _const_spec(a):
    nd = a.ndim
    return pl.BlockSpec(a.shape, lambda *_: (0,) * nd, pipeline_mode=pl.Buffered(1))


def _s5_e(u2, tabs):
    rows, wu = u2.shape
    tr = min(S5_ROW_TILE, rows)
    out = jax.ShapeDtypeStruct((rows, S5_STATE_W), F32)
    return pl.pallas_call(
        _s5_e_kernel, name="s5_e",
        grid=(rows // tr,),
        in_specs=[_row_spec(tr, wu), _const_spec(tabs['e_re']), _const_spec(tabs['e_im'])],
        out_specs=[_row_spec(tr, S5_STATE_W)] * 2,
        out_shape=[out, out],
        compiler_params=_cparams(("parallel",)),
    )(u2, tabs['e_re'], tabs['e_im'])


def _s5_scan_kernel(ere_ref, eim_ref, h0r_ref, h0i_ref, stp_re_ref, stp_im_ref,
                    apw_re_ref, apw_im_ref, hpr_ref, hpi_ref, hfr_ref, hfi_ref, *, nc, last_row):
    w = ere_ref.shape[-1]
    row = lax.broadcasted_iota(jnp.int32, (SUBLANES, w), 0)
    apr = apw_re_ref[...]
    api = apw_im_ref[...]

    def tile(i, carry):
        cr, ci = carry
        start = pl.multiple_of(i * SUBLANES, SUBLANES)
        sr = ere_ref[pl.ds(start, SUBLANES), :]
        si = eim_ref[pl.ds(start, SUBLANES), :]
        for k in range(3):
            s = 1 << k
            keep = row >= s
            ar = stp_re_ref[k:k + 1, :]
            ai = stp_im_ref[k:k + 1, :]
            pr = pltpu.roll(sr, s, 0)
            pi = pltpu.roll(si, s, 0)
            sr, si = (jnp.where(keep, sr + ar * pr - ai * pi, sr),
                      jnp.where(keep, si + ar * pi + ai * pr, si))
        hr = sr + apr * cr - api * ci
        hi = si + apr * ci + api * cr
        first = row == 0
        hpr_ref[pl.ds(start, SUBLANES), :] = jnp.where(first, cr, pltpu.roll(hr, 1, 0))
        hpi_ref[pl.ds(start, SUBLANES), :] = jnp.where(first, ci, pltpu.roll(hi, 1, 0))
        return hr, hi

    def body(i, carry):
        hr, hi = tile(i, carry)
        return hr[SUBLANES - 1:SUBLANES, :], hi[SUBLANES - 1:SUBLANES, :]

    ntile = nc // SUBLANES
    carry = lax.fori_loop(0, ntile - 1, body, (h0r_ref[...], h0i_ref[...]))
    hr, hi = tile(ntile - 1, carry)
    hfr_ref[...] = hr[last_row:last_row + 1, :]
    hfi_ref[...] = hi[last_row:last_row + 1, :]


def _s5_scan(e_re, e_im, h0r, h0i, tabs, n_chunks):
    b, nc, w = e_re.shape
    last_row = (n_chunks - 1) % SUBLANES
    wb = S5_SCAN_LANES
    consts = [tabs['step_re'], tabs['step_im'], tabs['apow_re'], tabs['apow_im']]
    seq = pl.BlockSpec((None, nc, wb), lambda bi, li: (bi, 0, li))
    vec = pl.BlockSpec((None, 1, wb), lambda bi, li: (bi, 0, li))
    return pl.pallas_call(
        functools.partial(_s5_scan_kernel, nc=nc, last_row=last_row), name="s5_scan",
        grid=(b, w // wb),
        in_specs=[seq, seq, vec, vec]
                 + [pl.BlockSpec((c.shape[0], wb), lambda bi, li: (0, li)) for c in consts],
        out_specs=[seq, seq, vec, vec],
        out_shape=[jax.ShapeDtypeStruct((b, nc, w), F32)] * 2
                  + [jax.ShapeDtypeStruct((b, 1, w), F32)] * 2,
        compiler_params=_cparams(("parallel", "parallel")),
    )(e_re, e_im, h0r, h0i, *consts)


def _s5_y_kernel(u_ref, hpr_ref, hpi_ref, m_ref, fre_ref, fim_ref, y_ref):
    for q in range(S5_NSUPER):
        uq = _s5_super_rows(u_ref, q)
        sl = slice(q * S5_SUPER_STATE, (q + 1) * S5_SUPER_STATE)
        yq = (jnp.dot(uq, m_ref[q], preferred_element_type=F32)
              + _bdot(hpr_ref[:, sl], fre_ref[q])
              + _bdot(hpi_ref[:, sl], fim_ref[q]))
        for t in range(S5_L):
            y_ref[:, t * S5_WIDTH + q * LANES:t * S5_WIDTH + (q + 1) * LANES] = (
                yq[:, t * LANES:(t + 1) * LANES])


def _s5_y(u2, hp_re, hp_im, tabs):
    rows, wu = u2.shape
    tr = min(S5_ROW_TILE, rows)
    consts = [tabs['m'], tabs['f_re'], tabs['f_im']]
    return pl.pallas_call(
        _s5_y_kernel, name="s5_y",
        grid=(rows // tr,),
        in_specs=[_row_spec(tr, wu), _row_spec(tr, S5_STATE_W), _row_spec(tr, S5_STATE_W)]
                 + [_const_spec(c) for c in consts],
        out_specs=_row_spec(tr, wu),
        out_shape=jax.ShapeDtypeStruct((rows, wu), F32),
        compiler_params=_cparams(("parallel",)),
    )(u2, hp_re, hp_im, *consts)


def _c_out_kernel(y2_ref, u_ref, x_ref, d_ref, wglu_ref, wout_ref, g_ref, beta_ref, o_ref, ys,
                  *, alpha):
    rows = y2_ref.shape[0]
    ntile = S5_WIDTH // LANES
    for s in range(S5_L):
        for c in range(ntile):
            lo = s * S5_WIDTH + c * LANES
            ys[c, pl.ds(s, rows, stride=S5_L), :] = y2_ref[:, lo:lo + LANES]
    y = jnp.concatenate([ys[c] for c in range(ntile)], axis=1) + d_ref[...] * u_ref[...]
    vg = _bdot(jax.nn.gelu(y), wglu_ref[...])
    half = vg.shape[1] // 2
    z = vg[:, :half] * jax.nn.sigmoid(vg[:, half:])
    o = _bdot(z, wout_ref[...])
    o_ref[...] = _layer_norm(alpha * x_ref[...] + o, g_ref[...], beta_ref[...])


def _c_out(y2, u, x, w, g, beta, alpha):
    n, dm = x.shape
    tm = min(ROW_TILE, n)
    consts = [w['d'], w['w_glu'], w['w_out'], g, beta]
    return pl.pallas_call(
        functools.partial(_c_out_kernel, alpha=alpha), name="c_out",
        grid=(n // tm,),
        in_specs=[_row_spec(tm // S5_L, y2.shape[1]), _row_spec(tm, u.shape[1]),
                  _row_spec(tm, dm)] + [_full_spec(c) for c in consts],
        out_specs=_row_spec(tm, dm),
        out_shape=jax.ShapeDtypeStruct((n, dm), F32),
        scratch_shapes=[pltpu.VMEM((u.shape[1] // LANES, tm, LANES), F32)],
        compiler_params=_cparams(("parallel",)),
    )(y2, u, x, *consts)


def _prep_ab(w_in, q_g, kv_g, w_uq, w_ukv, conv_w, conv_b, wa, ba, wx, bx, a_param, w_out):
    dm = w_in.shape[0]
    s0 = Q_LORA
    s1 = s0 + KV_LORA
    s2 = s1 + QK_ROPE
    s3 = s2 + LRU_WIDTH
    pad_lo = jnp.zeros((dm, QK_NOPE), F32)
    pad_hi = jnp.zeros((dm, HEAD_PAD - QK_HEAD), F32)
    w_in_p = jnp.concatenate([w_in[:, :s1], w_in[:, s2:s3], w_in[:, s3:],
                              pad_lo, w_in[:, s1:s2], pad_hi], axis=1).astype(BF16)
    wq = w_uq.reshape(Q_LORA, MLA_HEADS, QK_HEAD)
    wq = jnp.pad(wq, ((0, 0), (0, 0), (0, HEAD_PAD - QK_HEAD)))
    wq = wq.reshape(Q_LORA, MLA_HEADS * HEAD_PAD).astype(BF16)
    wkv = w_ukv.reshape(KV_LORA, MLA_HEADS, QK_NOPE + V_HEAD)
    wk = jnp.pad(wkv[:, :, :QK_NOPE], ((0, 0), (0, 0), (0, HEAD_PAD - QK_NOPE)))
    wk = wk.reshape(KV_LORA, MLA_HEADS * HEAD_PAD).astype(BF16)
    wv = jnp.pad(wkv[:, :, QK_NOPE:], ((0, 0), (0, 0), (0, HEAD_PAD - V_HEAD)))
    wv = wv.reshape(KV_LORA, MLA_HEADS * HEAD_PAD).astype(BF16)
    wukt = jnp.transpose(wkv[:, :, :QK_NOPE], (1, 2, 0)).astype(BF16)
    wuv_h = jnp.transpose(wkv[:, :, QK_NOPE:], (1, 0, 2)).astype(BF16)
    eye = jnp.eye(LRU_BLOCKS, dtype=F32)

    def blockdiag(wb):
        return jnp.einsum('hij,hk->hikj', wb, eye).reshape(LRU_WIDTH, LRU_WIDTH)

    w_ax = jnp.concatenate([blockdiag(wa), blockdiag(wx)], axis=1).astype(BF16)
    b_ax = jnp.concatenate([ba, bx])[None, :]
    att_w = MLA_HEADS * V_HEAD
    return dict(w_in=w_in_p, q_g=q_g[None, :], kv_g=kv_g[None, :], w_uq=wq, w_uk=wk, w_uv=wv,
                wukt=wukt, wuv_h=wuv_h, conv_w=conv_w, conv_b=conv_b[None, :], w_ax=w_ax,
                b_ax=b_ax, a_param=jax.nn.softplus(-a_param)[None, :],
                w_out_att=w_out[:att_w].astype(BF16), w_out_rec=w_out[att_w:].astype(BF16))


def _rope_tables(pos):
    half = QK_ROPE // 2
    inv = ROPE_THETA ** (-jnp.arange(half, dtype=F32) / half)
    ang = pos.astype(F32)[:, None] * inv[None, :]
    cos, sin = jnp.cos(ang), jnp.sin(ang)
    n = pos.shape[0]
    z = lambda k: jnp.zeros((n, k), F32)
    rc = jnp.concatenate([jnp.ones((n, QK_NOPE), F32), cos, cos, z(HEAD_PAD - QK_HEAD)], axis=1)
    ra = jnp.concatenate([z(QK_NOPE + half), sin, z(HEAD_PAD - QK_HEAD)], axis=1)
    rb = jnp.concatenate([z(QK_NOPE), -sin, z(half + HEAD_PAD - QK_HEAD)], axis=1)
    return rc, ra, rb


def _prep_moe(w_group, w_expert, w_gate, w_up, w_down):
    dm = w_group.shape[0]
    pad = jnp.zeros((dm, ROUTER_W - N_GROUPS - N_EXPERTS), F32)
    w_router = jnp.concatenate([w_group, w_expert, pad], axis=1)
    w_router_hi = w_router.astype(BF16)
    w_router_lo = (w_router - w_router_hi.astype(F32)).astype(BF16)
    return dict(w_router_hi=w_router_hi, w_router_lo=w_router_lo, w_gate=w_gate.astype(BF16),
                w_up=w_up.astype(BF16), w_down=w_down.astype(BF16))


def _cmul(ar, ai, br, bi):
    return ar * br - ai * bi, ar * bi + ai * br


def _prep_s5(lam_re, lam_im, log_step, b_re, b_im, c_re, c_im):
    hi = lax.Precision.HIGHEST
    L, G, P, C = S5_L, S5_GROUPS, S5_STATE, S5_GROUP
    lr, li = lam_re, lam_im
    dt = jnp.exp(log_step)[:, None]
    mag = jnp.exp(lr * dt)
    ar, ai = mag * jnp.cos(li * dt), mag * jnp.sin(li * dt)
    den = lr * lr + li * li
    nr = ar - 1.0
    cr = (nr * lr + ai * li) / den
    ci = (ai * lr - nr * li) / den
    bbr = cr[..., None] * b_re - ci[..., None] * b_im
    bbi = cr[..., None] * b_im + ci[..., None] * b_re
    prs, pis = [jnp.ones_like(ar)], [jnp.zeros_like(ar)]
    for _ in range(L):
        nr_, ni_ = _cmul(prs[-1], pis[-1], ar, ai)
        prs.append(nr_)
        pis.append(ni_)
    pr = jnp.stack(prs)
    pi = jnp.stack(pis)
    xr = pr[:L, :, :, None] * bbr - pi[:L, :, :, None] * bbi
    xi = pr[:L, :, :, None] * bbi + pi[:L, :, :, None] * bbr
    kk = (jnp.einsum('gcp,kgpd->kgcd', c_re, xr, precision=hi)
          - jnp.einsum('gcp,kgpd->kgcd', c_im, xi, precision=hi))
    p1r = pr[1:, :, :, None]
    p1i = pi[1:, :, :, None]
    cre_t = jnp.transpose(c_re, (0, 2, 1))[None]
    cim_t = jnp.transpose(c_im, (0, 2, 1))[None]
    f_re = cre_t * p1r - cim_t * p1i
    f_im = -cre_t * p1i - cim_t * p1r

    nq, sg = S5_NSUPER, S5_SUPER
    eye = jnp.eye(sg, dtype=F32)
    dk = jnp.einsum('kqgcd,gh->kqgdhc', kk.reshape(L, nq, sg, C, C), eye)
    dk = dk.reshape(L, nq, LANES, LANES).astype(BF16)
    lag = jnp.arange(L)[None, :] - jnp.arange(L)[:, None]
    m = jnp.where((lag >= 0)[:, :, None, None, None], dk[jnp.clip(lag, 0, L - 1)], 0)
    m = jnp.transpose(m, (2, 0, 3, 1, 4)).reshape(nq, L * LANES, L * LANES)

    row_g = (jnp.arange(L * LANES) // C) % sg
    col_g = jnp.arange(sg * P) // P
    same_group = (row_g[:, None] == col_g[None, :])[None]

    def pack_rows(a):
        a = jnp.transpose(a.reshape(L, nq, sg, P, C), (1, 0, 4, 2, 3))
        a = a.reshape(nq, L, 1, C, sg * P).astype(BF16)
        a = jnp.broadcast_to(a, (nq, L, sg, C, sg * P)).reshape(nq, L * LANES, sg * P)
        return jnp.where(same_group, a, 0)

    def pack_e(x):
        return pack_rows(x[::-1])

    def pack_f(f):
        return jnp.swapaxes(pack_rows(f), 1, 2)

    alr, ali = pr[L].reshape(1, G * P), pi[L].reshape(1, G * P)
    qr, qi = [alr], [ali]
    for _ in range(SUBLANES - 1):
        nr_, ni_ = _cmul(qr[-1], qi[-1], alr, ali)
        qr.append(nr_)
        qi.append(ni_)
    apow_re = jnp.concatenate(qr, axis=0)
    apow_im = jnp.concatenate(qi, axis=0)
    step_re = jnp.concatenate([qr[0], qr[1], qr[3]], axis=0)
    step_im = jnp.concatenate([qi[0], qi[1], qi[3]], axis=0)
    return dict(m=m, e_re=pack_e(xr), e_im=pack_e(xi), f_re=pack_f(f_re),
                f_im=pack_f(f_im), apow_re=apow_re, apow_im=apow_im,
                step_re=step_re, step_im=step_im)


def _mixer_ab(x, pos, past_ckv, past_krope, conv0, h0, w):
    b, t, dm = x.shape
    xf = x.reshape(b * t, dm)
    tabs = _rope_tables(pos)
    if b > 1:
        tabs = tuple(jnp.tile(tb, (b, 1)) for tb in tabs)
    q, k, v, c_new, kr_new, xb, yb = _ab_in(xf, w, tabs)
    if past_ckv is None:
        att = _attn_prompt(q, k, v)
    else:
        att = _attn_sample(q, c_new, kr_new, past_ckv, past_krope, w['wukt'], w['wuv_h'])
    rec, conv_new, h_new = _lru(xb.reshape(b, t, -1), yb.reshape(b, t, -1), conv0,
                                h0[:, None, :], w)
    return (att, rec.reshape(b * t, -1), c_new.reshape(b, t, -1), kr_new.reshape(b, t, -1),
            conv_new, h_new[:, 0, :])


def _mixer_c(x, h0_re, h0_im, w):
    b, t, dm = x.shape
    xf = x.reshape(b * t, dm)
    nc = t // S5_L
    assert t % S5_L == 0 and nc % SUBLANES == 0
    u, u2 = _c_in(xf, w['w_in'])
    e_re, e_im = _s5_e(u2, w['tabs'])
    hp_re, hp_im, hf_re, hf_im = _s5_scan(
        e_re.reshape(b, nc, S5_STATE_W), e_im.reshape(b, nc, S5_STATE_W),
        h0_re.reshape(b, 1, S5_STATE_W), h0_im.reshape(b, 1, S5_STATE_W), w['tabs'], nc)
    y2 = _s5_y(u2, hp_re.reshape(b * nc, S5_STATE_W), hp_im.reshape(b * nc, S5_STATE_W),
               w['tabs'])
    return (y2, u, hf_re.reshape(b, S5_GROUPS, S5_STATE), hf_im.reshape(b, S5_GROUPS, S5_STATE))


def kernel(x_prompt, x_sample, cache_mla_ckv, cache_mla_krope, state_lru_conv, state_lru_h,
           state_s5_re, state_s5_im, w_in_ab, q_norm_g, kv_norm_g, w_uq, w_ukv,
           lru_conv_w, lru_conv_b, lru_w_a, lru_b_a, lru_w_x, lru_b_x, lru_a_param, w_out_ab,
           w_in_c, s5_lam_re, s5_lam_im, s5_log_step, s5_b_re, s5_b_im, s5_c_re, s5_c_im,
           s5_d, s5_w_glu, w_out_c, ln_mix_g, ln_mix_b, ln_ffn_g, ln_ffn_b,
           moe_w_group, moe_w_expert, moe_w_gate, moe_w_up, moe_w_down):
    bp, tp, dm = x_prompt.shape
    bs, ts, _ = x_sample.shape
    past = cache_mla_ckv.shape[2]
    depth = ln_mix_g.shape[0]
    alpha = (2 * depth) ** 0.25
    pos_p = jnp.arange(tp)
    pos_s = past + jnp.arange(ts)
    assert (past + ts - 1) // CHUNK <= past // CHUNK
    hp, hs = x_prompt, x_sample
    outs = {k: [] for k in ('ckv_p', 'ckv_s', 'kr_p', 'kr_s', 'cv_p', 'cv_s', 'lh_p', 'lh_s',
                            's5r_p', 's5r_s', 's5i_p', 's5i_s')}
    for layer in range(depth):
        j = layer // 2
        g_mix, b_mix = ln_mix_g[layer][None, :], ln_mix_b[layer][None, :]
        g_ffn, b_ffn = ln_ffn_g[layer][None, :], ln_ffn_b[layer][None, :]
        if layer % 2 == 0:
            w = _prep_ab(w_in_ab[j], q_norm_g[j], kv_norm_g[j], w_uq[j], w_ukv[j], lru_conv_w[j],
                         lru_conv_b[j], lru_w_a[j], lru_b_a[j], lru_w_x[j], lru_b_x[j],
                         lru_a_param[j], w_out_ab[j])
            att_p, rec_p, c1, k1, v1, h1 = _mixer_ab(
                hp, pos_p, None, None, jnp.zeros((bp, CONV_WIDTH - 1, LRU_WIDTH), F32),
                jnp.zeros((bp, LRU_WIDTH), F32), w)
            att_s, rec_s, c2, k2, v2, h2 = _mixer_ab(
                hs, pos_s, cache_mla_ckv[j], cache_mla_krope[j], state_lru_conv[j],
                state_lru_h[j], w)
            outs['ckv_p'].append(c1); outs['ckv_s'].append(c2)
            outs['kr_p'].append(k1); outs['kr_s'].append(k2)
            outs['cv_p'].append(v1); outs['cv_s'].append(v2)
            outs['lh_p'].append(h1); outs['lh_s'].append(h2)
            hp = _out_ln(att_p, rec_p, hp.reshape(bp * tp, dm), w['w_out_att'], w['w_out_rec'],
                         g_mix, b_mix, alpha)
            hs = _out_ln(att_s, rec_s, hs.reshape(bs * ts, dm), w['w_out_att'], w['w_out_rec'],
                         g_mix, b_mix, alpha)
        else:
            w = dict(w_in=w_in_c[j].astype(BF16), d=s5_d[j][None, :],
                     w_glu=s5_w_glu[j].astype(BF16), w_out=w_out_c[j].astype(BF16),
                     tabs=_prep_s5(s5_lam_re[j], s5_lam_im[j], s5_log_step[j], s5_b_re[j],
                                   s5_b_im[j], s5_c_re[j], s5_c_im[j]))
            zero_state = jnp.zeros((bp, S5_GROUPS, S5_STATE), F32)
            y_p, u_p, r1, i1 = _mixer_c(hp, zero_state, zero_state, w)
            y_s, u_s, r2, i2 = _mixer_c(hs, state_s5_re[j], state_s5_im[j], w)
            outs['s5r_p'].append(r1); outs['s5r_s'].append(r2)
            outs['s5i_p'].append(i1); outs['s5i_s'].append(i2)
            hp = _c_out(y_p, u_p, hp.reshape(bp * tp, dm), w, g_mix, b_mix, alpha)
            hs = _c_out(y_s, u_s, hs.reshape(bs * ts, dm), w, g_mix, b_mix, alpha)
        wm = _prep_moe(moe_w_group[layer], moe_w_expert[layer], moe_w_gate[layer],
                       moe_w_up[layer], moe_w_down[layer])
        hp = _moe_ln(hp, wm, g_ffn, b_ffn, alpha).reshape(bp, tp, dm)
        hs = _moe_ln(hs, wm, g_ffn, b_ffn, alpha).reshape(bs, ts, dm)
    st = lambda k: jnp.stack(outs[k])
    return (hp, hs, st('ckv_p'), st('ckv_s'), st('kr_p'), st('kr_s'), st('cv_p'), st('cv_s'),
            st('lh_p'), st('lh_s'), st('s5r_p'), st('s5r_s'), st('s5i_p'), st('s5i_s'))
```

```python
import functools
import math

import jax
import jax.numpy as jnp
from jax import lax
from jax.experimental import pallas as pl
from jax.experimental.pallas import tpu as pltpu

F32 = jnp.float32
BF16 = jnp.bfloat16

CHUNK = 64
MLA_HEADS = 8
QK_NOPE = 64
QK_ROPE = 32
QK_HEAD = QK_NOPE + QK_ROPE
V_HEAD = 64
Q_LORA = 256
KV_LORA = 128
ROPE_THETA = 10000.0
MLA_SCALE = QK_HEAD ** -0.5
NEG_INF = -1e30
LRU_WIDTH = 512
LRU_BLOCKS = 8
LRU_BLOCK = LRU_WIDTH // LRU_BLOCKS
CONV_WIDTH = 4
RG_C = 8.0
S5_WIDTH = 512
S5_GROUP = 16
S5_GROUPS = S5_WIDTH // S5_GROUP
S5_STATE = 64
N_GROUPS = 4
EXPERTS_PER_GROUP = 4
N_EXPERTS = N_GROUPS * EXPERTS_PER_GROUP
EXPERT_HIDDEN = 256
LN_EPS = 1e-5
RMS_EPS = 1e-6

LANES = 128
SUBLANES = 8
VMEM_LIMIT_BYTES = 48 * 1024 * 1024
MOE_VMEM_LIMIT_BYTES = 56 * 1024 * 1024

HEAD_PAD = LANES
ROW_TILE = 512
MOE_ROW_TILE = 1024
ATTN_TQ = 1024
ATTN_TK = 512
ATTN_UNROLL = 4
Q_SCALE = MLA_SCALE * math.log2(math.e)
LRU_TILE = 512
S5_L = 8
S5_SUPER = LANES // S5_GROUP
S5_NSUPER = S5_GROUPS // S5_SUPER
S5_SUPER_IN = S5_L * LANES
S5_SUPER_STATE = S5_SUPER * S5_STATE
S5_STATE_W = S5_GROUPS * S5_STATE
S5_SCAN_LANES = 256
S5_ROW_TILE = 256
ROUTER_W = LANES


def _cparams(semantics, vmem_limit_bytes=VMEM_LIMIT_BYTES):
    return pltpu.CompilerParams(dimension_semantics=semantics,
                                vmem_limit_bytes=vmem_limit_bytes)


def _full_spec(a):
    nd = a.ndim
    return pl.BlockSpec(a.shape, lambda *_: (0,) * nd)


def _row_spec(tm, width):
    return pl.BlockSpec((tm, width), lambda i: (i, 0))


def _layer_norm(z, g, b):
    mu = jnp.mean(z, axis=-1, keepdims=True)
    zc = z - mu
    var = jnp.mean(zc * zc, axis=-1, keepdims=True)
    return zc * lax.rsqrt(var + LN_EPS) * g + b


def _rms_norm(z, g):
    ms = jnp.mean(z * z, axis=-1, keepdims=True)
    return z * lax.rsqrt(ms + RMS_EPS) * g


def _bdot(a, b):
    return jnp.dot(a.astype(BF16), b.astype(BF16), preferred_element_type=F32)


def _ab_in_kernel(x_ref, win_ref, qg_ref, kvg_ref, wuq_ref, wuk_ref, wuv_ref,
                  rc_ref, ra_ref, rb_ref,
                  q_out, k_out, v_out, c_out, kr_out, xb_out, yb_out, *, dm):
    proj = _bdot(x_ref[...], win_ref[...])
    o0 = Q_LORA
    o1 = o0 + KV_LORA
    o2 = o1 + LRU_WIDTH
    o3 = o2 + LRU_WIDTH
    q_lat = proj[:, 0:o0]
    kv_lat = proj[:, o0:o1]
    xb_out[...] = proj[:, o1:o2]
    yb_out[...] = proj[:, o2:o3]
    krp = proj[:, o3:o3 + HEAD_PAD]

    qn = _rms_norm(q_lat, qg_ref[...])
    cn = _rms_norm(kv_lat, kvg_ref[...])
    c_out[...] = cn
    q = _bdot(qn, wuq_ref[...])
    kk = _bdot(cn, wuk_ref[...])
    lane = lax.broadcasted_iota(jnp.int32, (1, MLA_HEADS * HEAD_PAD), 1)
    ones_col = (lane % HEAD_PAD == V_HEAD).astype(F32)
    v_out[...] = (_bdot(cn, wuv_ref[...]) + ones_col).astype(v_out.dtype)

    rc = rc_ref[...]
    ra = ra_ref[...]
    rb = rb_ref[...]
    half = QK_ROPE // 2

    def rope(z):
        return (z * rc + pltpu.roll(z, half, 1) * ra
                + pltpu.roll(z, HEAD_PAD - half, 1) * rb)

    kr = rope(krp)
    kr_out[...] = kr[:, QK_NOPE:QK_NOPE + QK_ROPE]
    for h in range(MLA_HEADS):
        sl = slice(h * HEAD_PAD, (h + 1) * HEAD_PAD)
        q_out[:, sl] = (rope(q[:, sl]) * Q_SCALE).astype(q_out.dtype)
        k_out[:, sl] = (kk[:, sl] + kr).astype(k_out.dtype)


def _ab_in(x, w, rope_tabs):
    n, dm = x.shape
    tm = min(ROW_TILE, n)
    rc, ra, rb = rope_tabs
    consts = [w['w_in'], w['q_g'], w['kv_g'], w['w_uq'], w['w_uk'], w['w_uv']]
    hp = MLA_HEADS * HEAD_PAD
    out_shape = [
        jax.ShapeDtypeStruct((n, hp), BF16),
        jax.ShapeDtypeStruct((n, hp), BF16),
        jax.ShapeDtypeStruct((n, hp), BF16),
        jax.ShapeDtypeStruct((n, KV_LORA), F32),
        jax.ShapeDtypeStruct((n, QK_ROPE), F32),
        jax.ShapeDtypeStruct((n, LRU_WIDTH), F32),
        jax.ShapeDtypeStruct((n, LRU_WIDTH), F32),
    ]
    return pl.pallas_call(
        functools.partial(_ab_in_kernel, dm=dm), name="ab_in",
        grid=(n // tm,),
        in_specs=([_row_spec(tm, dm)] + [_full_spec(c) for c in consts]
                  + [_row_spec(tm, HEAD_PAD)] * 3),
        out_specs=[_row_spec(tm, s.shape[1]) for s in out_shape],
        out_shape=out_shape,
        compiler_params=_cparams(("parallel",)),
    )(x, *consts, rc, ra, rb)


def _attn_prompt_kernel(q_ref, k_ref, v_ref, o_ref):
    i = pl.program_id(1)
    tq, tk = ATTN_TQ, ATTN_TK
    ndiag = tq // tk
    nfull = i * ndiag
    row_chunk = lax.broadcasted_iota(jnp.int32, (tq, tk), 0) // CHUNK
    col_chunk = lax.broadcasted_iota(jnp.int32, (tq, tk), 1) // CHUNK
    qs = [q_ref[:, hh * HEAD_PAD:(hh + 1) * HEAD_PAD] for hh in range(2)]

    def update(hh, j, m, acc, mask):
        start = pl.multiple_of(j * tk, tk)
        k = k_ref[pl.ds(start, tk), hh * HEAD_PAD:(hh + 1) * HEAD_PAD]
        v = v_ref[pl.ds(start, tk), hh * HEAD_PAD:(hh + 1) * HEAD_PAD]
        s = lax.dot_general(qs[hh], k, (((1,), (1,)), ((), ())), preferred_element_type=F32)
        if mask is not None:
            s = jnp.where(mask, s, NEG_INF)
        m_new = jnp.maximum(m, jnp.max(s, axis=-1, keepdims=True))
        alpha = jnp.exp2(m - m_new)
        p = jnp.exp2((s - m_new).astype(BF16))
        return m_new, alpha * acc + jnp.dot(p, v, preferred_element_type=F32)

    def body(j, carry):
        new = []
        for hh in range(2):
            new += update(hh, j, carry[2 * hh], carry[2 * hh + 1], None)
        return tuple(new)

    unroll = ATTN_UNROLL

    def body_unrolled(jj, carry):
        for b in range(unroll):
            carry = body(unroll * jj + b, carry)
        return carry

    init = []
    for _ in range(2):
        init += [jnp.full((tq, 1), NEG_INF, F32), jnp.zeros((tq, HEAD_PAD), F32)]
    carry = lax.fori_loop(0, nfull // unroll, body_unrolled, tuple(init))
    carry = list(lax.fori_loop(nfull - nfull % unroll, nfull, body, carry))
    for d in range(ndiag):
        visible = col_chunk + d * (tk // CHUNK) <= row_chunk
        for hh in range(2):
            carry[2 * hh], carry[2 * hh + 1] = update(
                hh, nfull + d, carry[2 * hh], carry[2 * hh + 1], visible)
    outs = [carry[2 * hh + 1][:, :V_HEAD] / carry[2 * hh + 1][:, V_HEAD:V_HEAD + 1]
            for hh in range(2)]
    o_ref[...] = jnp.concatenate(outs, axis=-1).astype(o_ref.dtype)


def _attn_prompt(q, k, v):
    t = q.shape[0]
    pairs = MLA_HEADS // 2
    return pl.pallas_call(
        _attn_prompt_kernel, name="attn_prompt",
        grid=(pairs, t // ATTN_TQ),
        in_specs=[
            pl.BlockSpec((ATTN_TQ, 2 * HEAD_PAD), lambda p, i: (i, p)),
            pl.BlockSpec((t, 2 * HEAD_PAD), lambda p, i: (0, p)),
            pl.BlockSpec((t, 2 * HEAD_PAD), lambda p, i: (0, p)),
        ],
        out_specs=pl.BlockSpec((ATTN_TQ, 2 * V_HEAD), lambda p, i: (i, p)),
        out_shape=jax.ShapeDtypeStruct((t, MLA_HEADS * V_HEAD), BF16),
        compiler_params=_cparams(("parallel", "parallel")),
    )(q, k, v)


def _attn_sample_kernel(q_ref, cn_ref, krn_ref, cp_ref, krp_ref, wukt_ref, wuv_ref, o_ref):
    cp = cp_ref[...].astype(BF16)
    krp = krp_ref[...].astype(BF16)
    cn = cn_ref[...].astype(BF16)
    krn = krn_ref[...].astype(BF16)
    ts = cn.shape[0]
    dn = (((1,), (1,)), ((), ()))
    outs = []
    for hp in range(MLA_HEADS // 2):
        qa, qr = [], []
        for hh in range(2):
            h = 2 * hp + hh
            qh = q_ref[:, h * HEAD_PAD:(h + 1) * HEAD_PAD]
            qa.append(jnp.dot(qh[:, :QK_NOPE], wukt_ref[h], preferred_element_type=F32))
            qr.append(qh[:, QK_NOPE:QK_NOPE + QK_ROPE])
        qa = jnp.concatenate(qa, axis=0).astype(BF16)
        qr = jnp.concatenate(qr, axis=0)
        s_past = (lax.dot_general(qa, cp, dn, preferred_element_type=F32)
                  + lax.dot_general(qr, krp, dn, preferred_element_type=F32))
        s_new = (lax.dot_general(qa, cn, dn, preferred_element_type=F32)
                 + lax.dot_general(qr, krn, dn, preferred_element_type=F32))
        m = jnp.maximum(jnp.max(s_past, axis=-1, keepdims=True),
                        jnp.max(s_new, axis=-1, keepdims=True))
        p_past = jnp.exp2(s_past - m)
        p_new = jnp.exp2(s_new - m)
        l = (jnp.sum(p_past, axis=-1, keepdims=True)
             + jnp.sum(p_new, axis=-1, keepdims=True))
        o_lat = (jnp.dot(p_past.astype(BF16), cp, preferred_element_type=F32)
                 + jnp.dot(p_new.astype(BF16), cn, preferred_element_type=F32)) / l
        for hh in range(2):
            h = 2 * hp + hh
            outs.append(_bdot(o_lat[hh * ts:(hh + 1) * ts], wuv_ref[h]))
    o_ref[...] = jnp.concatenate(outs, axis=-1).astype(o_ref.dtype)


def _attn_sample(q, c_new, kr_new, c_past, kr_past, wukt, wuv):
    bs, past, _ = c_past.shape
    ts = q.shape[0] // bs
    return pl.pallas_call(
        _attn_sample_kernel, name="attn_sample",
        grid=(bs,),
        in_specs=[
            pl.BlockSpec((ts, MLA_HEADS * HEAD_PAD), lambda b: (b, 0)),
            pl.BlockSpec((ts, KV_LORA), lambda b: (b, 0)),
            pl.BlockSpec((ts, QK_ROPE), lambda b: (b, 0)),
            pl.BlockSpec((None, past, KV_LORA), lambda b: (b, 0, 0)),
            pl.BlockSpec((None, past, QK_ROPE), lambda b: (b, 0, 0)),
            _full_spec(wukt), _full_spec(wuv),
        ],
        out_specs=pl.BlockSpec((ts, MLA_HEADS * V_HEAD), lambda b: (b, 0)),
        out_shape=jax.ShapeDtypeStruct((bs * ts, MLA_HEADS * V_HEAD), BF16),
        compiler_params=_cparams(("parallel",)),
    )(q, c_new, kr_new, c_past, kr_past, wukt, wuv)


def _lru_kernel(xb_ref, yb_ref, conv0_ref, h0_ref, cw_ref, cb_ref, wax_ref, bax_ref, ap_ref,
                rec_ref, conv_out_ref, h_out_ref, xcat, hc, *, tb):
    t = pl.program_id(1)
    nt = pl.num_programs(1)
    tail = CONV_WIDTH - 1
    base = SUBLANES

    @pl.when(t == 0)
    def _():
        xcat[base - tail:base, :] = conv0_ref[...]
        hc[...] = h0_ref[...]

    @pl.when(t > 0)
    def _():
        xcat[0:base, :] = xcat[tb:tb + base, :]

    xcat[base:base + tb, :] = xb_ref[...]
    xc = cb_ref[...]
    for tap in range(CONV_WIDTH):
        xc = xc + xcat[base - tail + tap:base - tail + tap + tb, :] * cw_ref[tap:tap + 1, :]

    gates = _bdot(xc, wax_ref[...]) + bax_ref[...]
    r = jax.nn.sigmoid(gates[:, :LRU_WIDTH])
    ig = jax.nn.sigmoid(gates[:, LRU_WIDTH:])
    log_a = -RG_C * r * ap_ref[...]
    a = jnp.exp(log_a)
    u = jnp.sqrt(jnp.tanh(-log_a) * (1.0 + a * a)) * (ig * xc)

    row8 = lax.broadcasted_iota(jnp.int32, (tb, LRU_WIDTH), 0) % SUBLANES
    s = 1
    while s < SUBLANES:
        keep = row8 >= s
        u = jnp.where(keep, a * pltpu.roll(u, s, 0) + u, u)
        a = jnp.where(keep, a * pltpu.roll(a, s, 0), a)
        s *= 2
    hprev = hc[...]
    groups = []
    for g in range(tb // SUBLANES):
        sl = slice(g * SUBLANES, (g + 1) * SUBLANES)
        hg = a[sl] * hprev + u[sl]
        groups.append(hg)
        hprev = hg[SUBLANES - 1:SUBLANES, :]
    h = jnp.concatenate(groups, axis=0)
    hc[...] = hprev
    rec_ref[...] = (h * jax.nn.gelu(yb_ref[...])).astype(rec_ref.dtype)

    @pl.when(t == nt - 1)
    def _():
        conv_out_ref[...] = xcat[base + tb - tail:base + tb, :]
        h_out_ref[...] = h[tb - 1:tb, :]


def _lru(xb, yb, conv0, h0, w):
    b, t, wd = xb.shape
    tb = min(LRU_TILE, t)
    tail = CONV_WIDTH - 1
    consts = [w['conv_w'], w['conv_b'], w['w_ax'], w['b_ax'], w['a_param']]
    seq = pl.BlockSpec((None, tb, wd), lambda bi, ti: (bi, ti, 0))
    return pl.pallas_call(
        functools.partial(_lru_kernel, tb=tb), name="rg_lru",
        grid=(b, t // tb),
        in_specs=[seq, seq,
                  pl.BlockSpec((None, tail, wd), lambda bi, ti: (bi, 0, 0)),
                  pl.BlockSpec((None, 1, wd), lambda bi, ti: (bi, 0, 0))]
                 + [pl.BlockSpec(c.shape, lambda bi, ti: (0, 0)) for c in consts],
        out_specs=[seq,
                   pl.BlockSpec((None, tail, wd), lambda bi, ti: (bi, 0, 0)),
                   pl.BlockSpec((None, 1, wd), lambda bi, ti: (bi, 0, 0))],
        out_shape=[jax.ShapeDtypeStruct((b, t, wd), BF16),
                   jax.ShapeDtypeStruct((b, tail, wd), F32),
                   jax.ShapeDtypeStruct((b, 1, wd), F32)],
        scratch_shapes=[pltpu.VMEM((tb + 2 * SUBLANES, wd), F32), pltpu.VMEM((1, wd), F32)],
        compiler_params=_cparams(("parallel", "arbitrary")),
    )(xb, yb, conv0, h0, *consts)


def _out_ln_kernel(a_ref, b_ref, x_ref, wa_ref, wb_ref, g_ref, beta_ref, o_ref, *, alpha):
    y = (jnp.dot(a_ref[...], wa_ref[...], preferred_element_type=F32)
         + jnp.dot(b_ref[...], wb_ref[...], preferred_element_type=F32))
    o_ref[...] = _layer_norm(alpha * x_ref[...] + y, g_ref[...], beta_ref[...])


def _out_ln(att, rec, x, wa, wb, g, beta, alpha):
    n, dm = x.shape
    tm = min(ROW_TILE, n)
    consts = [wa, wb, g, beta]
    return pl.pallas_call(
        functools.partial(_out_ln_kernel, alpha=alpha), name="out_ln",
        grid=(n // tm,),
        in_specs=[_row_spec(tm, att.shape[1]), _row_spec(tm, rec.shape[1]), _row_spec(tm, dm)]
                 + [_full_spec(c) for c in consts],
        out_specs=_row_spec(tm, dm),
        out_shape=jax.ShapeDtypeStruct((n, dm), F32),
        compiler_params=_cparams(("parallel",)),
    )(att, rec, x, *consts)


def _moe_ln_kernel(x_ref, wrh_ref, wrl_ref, wg_ref, wu_ref, wd_ref, g_ref, beta_ref, o_ref,
                   acc, gates, xb16, *, alpha):
    grp = pl.program_id(1)
    tm = x_ref.shape[0]
    lane = lax.broadcasted_iota(jnp.int32, (tm, ROUTER_W), 1).astype(F32)

    @pl.when(grp == 0)
    def _():
        x = x_ref[...]
        xh = x.astype(BF16)
        xb16[...] = xh
        xl = (x - xh.astype(F32)).astype(BF16)
        wh = wrh_ref[...]
        logits = (jnp.dot(xh, wh, preferred_element_type=F32)
                  + jnp.dot(xl, wh, preferred_element_type=F32)
                  + jnp.dot(xh, wrl_ref[...], preferred_element_type=F32))
        big = float(ROUTER_W)
        lg = jnp.where(lane < N_GROUPS, logits, -jnp.inf)
        mg = jnp.max(lg, axis=-1, keepdims=True)
        gi = jnp.min(jnp.where(lg == mg, lane, big), axis=-1, keepdims=True)
        p_top = 1.0 / jnp.sum(jnp.exp(lg - mg), axis=-1, keepdims=True)
        lo = N_GROUPS + gi * EXPERTS_PER_GROUP
        le = jnp.where((lane >= lo) & (lane < lo + EXPERTS_PER_GROUP), logits, -jnp.inf)
        v1 = jnp.max(le, axis=-1, keepdims=True)
        i1 = jnp.min(jnp.where(le == v1, lane, big), axis=-1, keepdims=True)
        le2 = jnp.where(lane == i1, -jnp.inf, le)
        v2 = jnp.max(le2, axis=-1, keepdims=True)
        i2 = jnp.min(jnp.where(le2 == v2, lane, big), axis=-1, keepdims=True)
        e2 = jnp.exp(v2 - v1)
        w1 = p_top / (1.0 + e2)
        w2 = p_top * e2 / (1.0 + e2)
        gates[...] = jnp.where(lane == i1, w1, 0.0) + jnp.where(lane == i2, w2, 0.0)
        acc[...] = jnp.zeros_like(acc)

    xb = xb16[...]
    gt = gates[...]
    first = (N_GROUPS + grp * EXPERTS_PER_GROUP).astype(F32)
    for el in range(EXPERTS_PER_GROUP):
        ge = jnp.sum(jnp.where(lane == first + el, gt, 0.0), axis=-1, keepdims=True)
        h = jnp.dot(xb, wg_ref[el], preferred_element_type=F32)
        u = jnp.dot(xb, wu_ref[el], preferred_element_type=F32)
        act = (jax.nn.silu(h) * u * ge).astype(BF16)
        acc[...] += jnp.dot(act, wd_ref[el], preferred_element_type=F32)

    @pl.when(grp == pl.num_programs(1) - 1)
    def _():
        o_ref[...] = _layer_norm(alpha * x_ref[...] + acc[...], g_ref[...], beta_ref[...])


def _moe_ln(x, w, g, beta, alpha):
    n, dm = x.shape
    tm = min(MOE_ROW_TILE, n)
    ne, _, eh = w['w_gate'].shape
    epg = EXPERTS_PER_GROUP
    return pl.pallas_call(
        functools.partial(_moe_ln_kernel, alpha=alpha), name="moe_ln",
        grid=(n // tm, ne // epg),
        in_specs=[
            pl.BlockSpec((tm, dm), lambda i, e: (i, 0)),
            pl.BlockSpec(w['w_router_hi'].shape, lambda i, e: (0, 0)),
            pl.BlockSpec(w['w_router_lo'].shape, lambda i, e: (0, 0)),
            pl.BlockSpec((epg, dm, eh), lambda i, e: (e, 0, 0)),
            pl.BlockSpec((epg, dm, eh), lambda i, e: (e, 0, 0)),
            pl.BlockSpec((epg, eh, dm), lambda i, e: (e, 0, 0)),
            pl.BlockSpec(g.shape, lambda i, e: (0, 0)),
            pl.BlockSpec(beta.shape, lambda i, e: (0, 0)),
        ],
        out_specs=pl.BlockSpec((tm, dm), lambda i, e: (i, 0)),
        out_shape=jax.ShapeDtypeStruct((n, dm), F32),
        scratch_shapes=[pltpu.VMEM((tm, dm), F32), pltpu.VMEM((tm, ROUTER_W), F32),
                        pltpu.VMEM((tm, dm), BF16)],
        compiler_params=_cparams(("parallel", "arbitrary"), MOE_VMEM_LIMIT_BYTES),
    )(x, w['w_router_hi'], w['w_router_lo'], w['w_gate'], w['w_up'], w['w_down'], g, beta)


def _c_in_kernel(x_ref, w_ref, u_ref, u2_ref, us):
    u = _bdot(x_ref[...], w_ref[...])
    u_ref[...] = u
    rows = u2_ref.shape[0]
    for c in range(S5_WIDTH // LANES):
        us[c] = u[:, c * LANES:(c + 1) * LANES]
    for s in range(S5_L):
        for c in range(S5_WIDTH // LANES):
            lo = s * S5_WIDTH + c * LANES
            u2_ref[:, lo:lo + LANES] = us[c, pl.ds(s, rows, stride=S5_L), :].astype(u2_ref.dtype)


def _c_in(x, w_in):
    n, dm = x.shape
    tm = min(ROW_TILE, n)
    wu = w_in.shape[1]
    return pl.pallas_call(
        _c_in_kernel, name="c_in",
        grid=(n // tm,),
        in_specs=[_row_spec(tm, dm), _full_spec(w_in)],
        out_specs=[_row_spec(tm, wu), _row_spec(tm // S5_L, S5_L * wu)],
        out_shape=[jax.ShapeDtypeStruct((n, wu), F32),
                   jax.ShapeDtypeStruct((n // S5_L, S5_L * wu), BF16)],
        scratch_shapes=[pltpu.VMEM((wu // LANES, tm, LANES), F32)],
        compiler_params=_cparams(("parallel",)),
    )(x, w_in)


def _s5_super_rows(u_ref, q):
    tiles = [u_ref[:, s * S5_WIDTH + q * LANES:s * S5_WIDTH + (q + 1) * LANES]
             for s in range(S5_L)]
    return jnp.concatenate(tiles, axis=1).astype(BF16)


def _s5_e_kernel(u_ref, ere_ref, eim_ref, ere_out, eim_out):
    for q in range(S5_NSUPER):
        uq = _s5_super_rows(u_ref, q)
        sl = slice(q * S5_SUPER_STATE, (q + 1) * S5_SUPER_STATE)
        ere_out[:, sl] = jnp.dot(uq, ere_ref[q], preferred_element_type=F32)
        eim_out[:, sl] = jnp.dot(uq, eim_ref[q], preferred_element_type=F32)


def _const_spec(a):
    nd = a.ndim
    return pl.BlockSpec(a.shape, lambda *_: (0,) * nd, pipeline_mode=pl.Buffered(1))


def _s5_e(u2, tabs):
    rows, wu = u2.shape
    tr = min(S5_ROW_TILE, rows)
    out = jax.ShapeDtypeStruct((rows, S5_STATE_W), F32)
    return pl.pallas_call(
        _s5_e_kernel, name="s5_e",
        grid=(rows // tr,),
        in_specs=[_row_spec(tr, wu), _const_spec(tabs['e_re']), _const_spec(tabs['e_im'])],
        out_specs=[_row_spec(tr, S5_STATE_W)] * 2,
        out_shape=[out, out],
        compiler_params=_cparams(("parallel",)),
    )(u2, tabs['e_re'], tabs['e_im'])


def _s5_scan_kernel(ere_ref, eim_ref, h0r_ref, h0i_ref, stp_re_ref, stp_im_ref,
                    apw_re_ref, apw_im_ref, hpr_ref, hpi_ref, hfr_ref, hfi_ref, *, nc, last_row):
    w = ere_ref.shape[-1]
    row = lax.broadcasted_iota(jnp.int32, (SUBLANES, w), 0)
    apr = apw_re_ref[...]
    api = apw_im_ref[...]

    def tile(i, carry):
        cr, ci = carry
        start = pl.multiple_of(i * SUBLANES, SUBLANES)
        sr = ere_ref[pl.ds(start, SUBLANES), :]
        si = eim_ref[pl.ds(start, SUBLANES), :]
        for k in range(3):
            s = 1 << k
            keep = row >= s
            ar = stp_re_ref[k:k + 1, :]
            ai = stp_im_ref[k:k + 1, :]
            pr = pltpu.roll(sr, s, 0)
            pi = pltpu.roll(si, s, 0)
            sr, si = (jnp.where(keep, sr + ar * pr - ai * pi, sr),
                      jnp.where(keep, si + ar * pi + ai * pr, si))
        hr = sr + apr * cr - api * ci
        hi = si + apr * ci + api * cr
        first = row == 0
        hpr_ref[pl.ds(start, SUBLANES), :] = jnp.where(first, cr, pltpu.roll(hr, 1, 0))
        hpi_ref[pl.ds(start, SUBLANES), :] = jnp.where(first, ci, pltpu.roll(hi, 1, 0))
        return hr, hi

    def body(i, carry):
        hr, hi = tile(i, carry)
        return hr[SUBLANES - 1:SUBLANES, :], hi[SUBLANES - 1:SUBLANES, :]

    ntile = nc // SUBLANES
    carry = lax.fori_loop(0, ntile - 1, body, (h0r_ref[...], h0i_ref[...]))
    hr, hi = tile(ntile - 1, carry)
    hfr_ref[...] = hr[last_row:last_row + 1, :]
    hfi_ref[...] = hi[last_row:last_row + 1, :]


def _s5_scan(e_re, e_im, h0r, h0i, tabs, n_chunks):
    b, nc, w = e_re.shape
    last_row = (n_chunks - 1) % SUBLANES
    wb = S5_SCAN_LANES
    consts = [tabs['step_re'], tabs['step_im'], tabs['apow_re'], tabs['apow_im']]
    seq = pl.BlockSpec((None, nc, wb), lambda bi, li: (bi, 0, li))
    vec = pl.BlockSpec((None, 1, wb), lambda bi, li: (bi, 0, li))
    return pl.pallas_call(
        functools.partial(_s5_scan_kernel, nc=nc, last_row=last_row), name="s5_scan",
        grid=(b, w // wb),
        in_specs=[seq, seq, vec, vec]
                 + [pl.BlockSpec((c.shape[0], wb), lambda bi, li: (0, li)) for c in consts],
        out_specs=[seq, seq, vec, vec],
        out_shape=[jax.ShapeDtypeStruct((b, nc, w), F32)] * 2
                  + [jax.ShapeDtypeStruct((b, 1, w), F32)] * 2,
        compiler_params=_cparams(("parallel", "parallel")),
    )(e_re, e_im, h0r, h0i, *consts)


def _s5_y_kernel(u_ref, hpr_ref, hpi_ref, m_ref, fre_ref, fim_ref, y_ref):
    for q in range(S5_NSUPER):
        uq = _s5_super_rows(u_ref, q)
        sl = slice(q * S5_SUPER_STATE, (q + 1) * S5_SUPER_STATE)
        yq = (jnp.dot(uq, m_ref[q], preferred_element_type=F32)
              + _bdot(hpr_ref[:, sl], fre_ref[q])
              + _bdot(hpi_ref[:, sl], fim_ref[q]))
        for t in range(S5_L):
            y_ref[:, t * S5_WIDTH + q * LANES:t * S5_WIDTH + (q + 1) * LANES] = (
                yq[:, t * LANES:(t + 1) * LANES])


def _s5_y(u2, hp_re, hp_im, tabs):
    rows, wu = u2.shape
    tr = min(S5_ROW_TILE, rows)
    consts = [tabs['m'], tabs['f_re'], tabs['f_im']]
    return pl.pallas_call(
        _s5_y_kernel, name="s5_y",
        grid=(rows // tr,),
        in_specs=[_row_spec(tr, wu), _row_spec(tr, S5_STATE_W), _row_spec(tr, S5_STATE_W)]
                 + [_const_spec(c) for c in consts],
        out_specs=_row_spec(tr, wu),
        out_shape=jax.ShapeDtypeStruct((rows, wu), F32),
        compiler_params=_cparams(("parallel",)),
    )(u2, hp_re, hp_im, *consts)


def _c_out_kernel(y2_ref, u_ref, x_ref, d_ref, wglu_ref, wout_ref, g_ref, beta_ref, o_ref, ys,
                  *, alpha):
    rows = y2_ref.shape[0]
    ntile = S5_WIDTH // LANES
    for s in range(S5_L):
        for c in range(ntile):
            lo = s * S5_WIDTH + c * LANES
            ys[c, pl.ds(s, rows, stride=S5_L), :] = y2_ref[:, lo:lo + LANES]
    y = jnp.concatenate([ys[c] for c in range(ntile)], axis=1) + d_ref[...] * u_ref[...]
    vg = _bdot(jax.nn.gelu(y), wglu_ref[...])
    half = vg.shape[1] // 2
    z = vg[:, :half] * jax.nn.sigmoid(vg[:, half:])
    o = _bdot(z, wout_ref[...])
    o_ref[...] = _layer_norm(alpha * x_ref[...] + o, g_ref[...], beta_ref[...])


def _c_out(y2, u, x, w, g, beta, alpha):
    n, dm = x.shape
    tm = min(ROW_TILE, n)
    consts = [w['d'], w['w_glu'], w['w_out'], g, beta]
    return pl.pallas_call(
        functools.partial(_c_out_kernel, alpha=alpha), name="c_out",
        grid=(n // tm,),
        in_specs=[_row_spec(tm // S5_L, y2.shape[1]), _row_spec(tm, u.shape[1]),
                  _row_spec(tm, dm)] + [_full_spec(c) for c in consts],
        out_specs=_row_spec(tm, dm),
        out_shape=jax.ShapeDtypeStruct((n, dm), F32),
        scratch_shapes=[pltpu.VMEM((u.shape[1] // LANES, tm, LANES), F32)],
        compiler_params=_cparams(("parallel",)),
    )(y2, u, x, *consts)


def _prep_ab(w_in, q_g, kv_g, w_uq, w_ukv, conv_w, conv_b, wa, ba, wx, bx, a_param, w_out):
    dm = w_in.shape[0]
    s0 = Q_LORA
    s1 = s0 + KV_LORA
    s2 = s1 + QK_ROPE
    s3 = s2 + LRU_WIDTH
    pad_lo = jnp.zeros((dm, QK_NOPE), F32)
    pad_hi = jnp.zeros((dm, HEAD_PAD - QK_HEAD), F32)
    w_in_p = jnp.concatenate([w_in[:, :s1], w_in[:, s2:s3], w_in[:, s3:],
                              pad_lo, w_in[:, s1:s2], pad_hi], axis=1).astype(BF16)
    wq = w_uq.reshape(Q_LORA, MLA_HEADS, QK_HEAD)
    wq = jnp.pad(wq, ((0, 0), (0, 0), (0, HEAD_PAD - QK_HEAD)))
    wq = wq.reshape(Q_LORA, MLA_HEADS * HEAD_PAD).astype(BF16)
    wkv = w_ukv.reshape(KV_LORA, MLA_HEADS, QK_NOPE + V_HEAD)
    wk = jnp.pad(wkv[:, :, :QK_NOPE], ((0, 0), (0, 0), (0, HEAD_PAD - QK_NOPE)))
    wk = wk.reshape(KV_LORA, MLA_HEADS * HEAD_PAD).astype(BF16)
    wv = jnp.pad(wkv[:, :, QK_NOPE:], ((0, 0), (0, 0), (0, HEAD_PAD - V_HEAD)))
    wv = wv.reshape(KV_LORA, MLA_HEADS * HEAD_PAD).astype(BF16)
    wukt = jnp.transpose(wkv[:, :, :QK_NOPE], (1, 2, 0)).astype(BF16)
    wuv_h = jnp.transpose(wkv[:, :, QK_NOPE:], (1, 0, 2)).astype(BF16)
    eye = jnp.eye(LRU_BLOCKS, dtype=F32)

    def blockdiag(wb):
        return jnp.einsum('hij,hk->hikj', wb, eye).reshape(LRU_WIDTH, LRU_WIDTH)

    w_ax = jnp.concatenate([blockdiag(wa), blockdiag(wx)], axis=1).astype(BF16)
    b_ax = jnp.concatenate([ba, bx])[None, :]
    att_w = MLA_HEADS * V_HEAD
    return dict(w_in=w_in_p, q_g=q_g[None, :], kv_g=kv_g[None, :], w_uq=wq, w_uk=wk, w_uv=wv,
                wukt=wukt, wuv_h=wuv_h, conv_w=conv_w, conv_b=conv_b[None, :], w_ax=w_ax,
                b_ax=b_ax, a_param=jax.nn.softplus(-a_param)[None, :],
                w_out_att=w_out[:att_w].astype(BF16), w_out_rec=w_out[att_w:].astype(BF16))


def _rope_tables(pos):
    half = QK_ROPE // 2
    inv = ROPE_THETA ** (-jnp.arange(half, dtype=F32) / half)
    ang = pos.astype(F32)[:, None] * inv[None, :]
    cos, sin = jnp.cos(ang), jnp.sin(ang)
    n = pos.shape[0]
    z = lambda k: jnp.zeros((n, k), F32)
    rc = jnp.concatenate([jnp.ones((n, QK_NOPE), F32), cos, cos, z(HEAD_PAD - QK_HEAD)], axis=1)
    ra = jnp.concatenate([z(QK_NOPE + half), sin, z(HEAD_PAD - QK_HEAD)], axis=1)
    rb = jnp.concatenate([z(QK_NOPE), -sin, z(half + HEAD_PAD - QK_HEAD)], axis=1)
    return rc, ra, rb


def _prep_moe(w_group, w_expert, w_gate, w_up, w_down):
    dm = w_group.shape[0]
    pad = jnp.zeros((dm, ROUTER_W - N_GROUPS - N_EXPERTS), F32)
    w_router = jnp.concatenate([w_group, w_expert, pad], axis=1)
    w_router_hi = w_router.astype(BF16)
    w_router_lo = (w_router - w_router_hi.astype(F32)).astype(BF16)
    return dict(w_router_hi=w_router_hi, w_router_lo=w_router_lo, w_gate=w_gate.astype(BF16),
                w_up=w_up.astype(BF16), w_down=w_down.astype(BF16))


def _cmul(ar, ai, br, bi):
    return ar * br - ai * bi, ar * bi + ai * br


def _prep_s5(lam_re, lam_im, log_step, b_re, b_im, c_re, c_im):
    hi = lax.Precision.HIGHEST
    L, G, P, C = S5_L, S5_GROUPS, S5_STATE, S5_GROUP
    lr, li = lam_re, lam_im
    dt = jnp.exp(log_step)[:, None]
    mag = jnp.exp(lr * dt)
    ar, ai = mag * jnp.cos(li * dt), mag * jnp.sin(li * dt)
    den = lr * lr + li * li
    nr = ar - 1.0
    cr = (nr * lr + ai * li) / den
    ci = (ai * lr - nr * li) / den
    bbr = cr[..., None] * b_re - ci[..., None] * b_im
    bbi = cr[..., None] * b_im + ci[..., None] * b_re
    prs, pis = [jnp.ones_like(ar)], [jnp.zeros_like(ar)]
    for _ in range(L):
        nr_, ni_ = _cmul(prs[-1], pis[-1], ar, ai)
        prs.append(nr_)
        pis.append(ni_)
    pr = jnp.stack(prs)
    pi = jnp.stack(pis)
    xr = pr[:L, :, :, None] * bbr - pi[:L, :, :, None] * bbi
    xi = pr[:L, :, :, None] * bbi + pi[:L, :, :, None] * bbr
    kk = (jnp.einsum('gcp,kgpd->kgcd', c_re, xr, precision=hi)
          - jnp.einsum('gcp,kgpd->kgcd', c_im, xi, precision=hi))
    p1r = pr[1:, :, :, None]
    p1i = pi[1:, :, :, None]
    cre_t = jnp.transpose(c_re, (0, 2, 1))[None]
    cim_t = jnp.transpose(c_im, (0, 2, 1))[None]
    f_re = cre_t * p1r - cim_t * p1i
    f_im = -cre_t * p1i - cim_t * p1r

    nq, sg = S5_NSUPER, S5_SUPER
    eye = jnp.eye(sg, dtype=F32)
    dk = jnp.einsum('kqgcd,gh->kqgdhc', kk.reshape(L, nq, sg, C, C), eye)
    dk = dk.reshape(L, nq, LANES, LANES).astype(BF16)
    lag = jnp.arange(L)[None, :] - jnp.arange(L)[:, None]
    m = jnp.where((lag >= 0)[:, :, None, None, None], dk[jnp.clip(lag, 0, L - 1)], 0)
    m = jnp.transpose(m, (2, 0, 3, 1, 4)).reshape(nq, L * LANES, L * LANES)

    row_g = (jnp.arange(L * LANES) // C) % sg
    col_g = jnp.arange(sg * P) // P
    same_group = (row_g[:, None] == col_g[None, :])[None]

    def pack_rows(a):
        a = jnp.transpose(a.reshape(L, nq, sg, P, C), (1, 0, 4, 2, 3))
        a = a.reshape(nq, L, 1, C, sg * P).astype(BF16)
        a = jnp.broadcast_to(a, (nq, L, sg, C, sg * P)).reshape(nq, L * LANES, sg * P)
        return jnp.where(same_group, a, 0)

    def pack_e(x):
        return pack_rows(x[::-1])

    def pack_f(f):
        return jnp.swapaxes(pack_rows(f), 1, 2)

    alr, ali = pr[L].reshape(1, G * P), pi[L].reshape(1, G * P)
    qr, qi = [alr], [ali]
    for _ in range(SUBLANES - 1):
        nr_, ni_ = _cmul(qr[-1], qi[-1], alr, ali)
        qr.append(nr_)
        qi.append(ni_)
    apow_re = jnp.concatenate(qr, axis=0)
    apow_im = jnp.concatenate(qi, axis=0)
    step_re = jnp.concatenate([qr[0], qr[1], qr[3]], axis=0)
    step_im = jnp.concatenate([qi[0], qi[1], qi[3]], axis=0)
    return dict(m=m, e_re=pack_e(xr), e_im=pack_e(xi), f_re=pack_f(f_re),
                f_im=pack_f(f_im), apow_re=apow_re, apow_im=apow_im,
                step_re=step_re, step_im=step_im)


def _mixer_ab(x, pos, past_ckv, past_krope, conv0, h0, w):
    b, t, dm = x.shape
    xf = x.reshape(b * t, dm)
    tabs = _rope_tables(pos)
    if b > 1:
        tabs = tuple(jnp.tile(tb, (b, 1)) for tb in tabs)
    q, k, v, c_new, kr_new, xb, yb = _ab_in(xf, w, tabs)
    if past_ckv is None:
        att = _attn_prompt(q, k, v)
    else:
        att = _attn_sample(q, c_new, kr_new, past_ckv, past_krope, w['wukt'], w['wuv_h'])
    rec, conv_new, h_new = _lru(xb.reshape(b, t, -1), yb.reshape(b, t, -1), conv0,
                                h0[:, None, :], w)
    return (att, rec.reshape(b * t, -1), c_new.reshape(b, t, -1), kr_new.reshape(b, t, -1),
            conv_new, h_new[:, 0, :])


def _mixer_c(x, h0_re, h0_im, w):
    b, t, dm = x.shape
    xf = x.reshape(b * t, dm)
    nc = t // S5_L
    assert t % S5_L == 0 and nc % SUBLANES == 0
    u, u2 = _c_in(xf, w['w_in'])
    e_re, e_im = _s5_e(u2, w['tabs'])
    hp_re, hp_im, hf_re, hf_im = _s5_scan(
        e_re.reshape(b, nc, S5_STATE_W), e_im.reshape(b, nc, S5_STATE_W),
        h0_re.reshape(b, 1, S5_STATE_W), h0_im.reshape(b, 1, S5_STATE_W), w['tabs'], nc)
    y2 = _s5_y(u2, hp_re.reshape(b * nc, S5_STATE_W), hp_im.reshape(b * nc, S5_STATE_W),
               w['tabs'])
    return (y2, u, hf_re.reshape(b, S5_GROUPS, S5_STATE), hf_im.reshape(b, S5_GROUPS, S5_STATE))


def kernel(x_prompt, x_sample, cache_mla_ckv, cache_mla_krope, state_lru_conv, state_lru_h,
           state_s5_re, state_s5_im, w_in_ab, q_norm_g, kv_norm_g, w_uq, w_ukv,
           lru_conv_w, lru_conv_b, lru_w_a, lru_b_a, lru_w_x, lru_b_x, lru_a_param, w_out_ab,
           w_in_c, s5_lam_re, s5_lam_im, s5_log_step, s5_b_re, s5_b_im, s5_c_re, s5_c_im,
           s5_d, s5_w_glu, w_out_c, ln_mix_g, ln_mix_b, ln_ffn_g, ln_ffn_b,
           moe_w_group, moe_w_expert, moe_w_gate, moe_w_up, moe_w_down):
    bp, tp, dm = x_prompt.shape
    bs, ts, _ = x_sample.shape
    past = cache_mla_ckv.shape[2]
    depth = ln_mix_g.shape[0]
    alpha = (2 * depth) ** 0.25
    pos_p = jnp.arange(tp)
    pos_s = past + jnp.arange(ts)
    assert (past + ts - 1) // CHUNK <= past // CHUNK
    hp, hs = x_prompt, x_sample
    outs = {k: [] for k in ('ckv_p', 'ckv_s', 'kr_p', 'kr_s', 'cv_p', 'cv_s', 'lh_p', 'lh_s',
                            's5r_p', 's5r_s', 's5i_p', 's5i_s')}
    for layer in range(depth):
        j = layer // 2
        g_mix, b_mix = ln_mix_g[layer][None, :], ln_mix_b[layer][None, :]
        g_ffn, b_ffn = ln_ffn_g[layer][None, :], ln_ffn_b[layer][None, :]
        if layer % 2 == 0:
            w = _prep_ab(w_in_ab[j], q_norm_g[j], kv_norm_g[j], w_uq[j], w_ukv[j], lru_conv_w[j],
                         lru_conv_b[j], lru_w_a[j], lru_b_a[j], lru_w_x[j], lru_b_x[j],
                         lru_a_param[j], w_out_ab[j])
            att_p, rec_p, c1, k1, v1, h1 = _mixer_ab(
                hp, pos_p, None, None, jnp.zeros((bp, CONV_WIDTH - 1, LRU_WIDTH), F32),
                jnp.zeros((bp, LRU_WIDTH), F32), w)
            att_s, rec_s, c2, k2, v2, h2 = _mixer_ab(
                hs, pos_s, cache_mla_ckv[j], cache_mla_krope[j], state_lru_conv[j],
                state_lru_h[j], w)
            outs['ckv_p'].append(c1); outs['ckv_s'].append(c2)
            outs['kr_p'].append(k1); outs['kr_s'].append(k2)
            outs['cv_p'].append(v1); outs['cv_s'].append(v2)
            outs['lh_p'].append(h1); outs['lh_s'].append(h2)
            hp = _out_ln(att_p, rec_p, hp.reshape(bp * tp, dm), w['w_out_att'], w['w_out_rec'],
                         g_mix, b_mix, alpha)
            hs = _out_ln(att_s, rec_s, hs.reshape(bs * ts, dm), w['w_out_att'], w['w_out_rec'],
                         g_mix, b_mix, alpha)
        else:
            w = dict(w_in=w_in_c[j].astype(BF16), d=s5_d[j][None, :],
                     w_glu=s5_w_glu[j].astype(BF16), w_out=w_out_c[j].astype(BF16),
                     tabs=_prep_s5(s5_lam_re[j], s5_lam_im[j], s5_log_step[j], s5_b_re[j],
                                   s5_b_im[j], s5_c_re[j], s5_c_im[j]))
            zero_state = jnp.zeros((bp, S5_GROUPS, S5_STATE), F32)
            y_p, u_p, r1, i1 = _mixer_c(hp, zero_state, zero_state, w)
            y_s, u_s, r2, i2 = _mixer_c(hs, state_s5_re[j], state_s5_im[j], w)
            outs['s5r_p'].append(r1); outs['s5r_s'].append(r2)
            outs['s5i_p'].append(i1); outs['s5i_s'].append(i2)
            hp = _c_out(y_p, u_p, hp.reshape(bp * tp, dm), w, g_mix, b_mix, alpha)
            hs = _c_out(y_s, u_s, hs.reshape(bs * ts, dm), w, g_mix, b_mix, alpha)
        wm = _prep_moe(moe_w_group[layer], moe_w_expert[layer], moe_w_gate[layer],
                       moe_w_up[layer], moe_w_down[layer])
        hp = _moe_ln(hp, wm, g_ffn, b_ffn, alpha).reshape(bp, tp, dm)
        hs = _moe_ln(hs, wm, g_ffn, b_ffn, alpha).reshape(bs, ts, dm)
    st = lambda k: jnp.stack(outs[k])
    return (hp, hs, st('ckv_p'), st('ckv_s'), st('kr_p'), st('kr_s'), st('cv_p'), st('cv_s'),
            st('lh_p'), st('lh_s'), st('s5r_p'), st('s5r_s'), st('s5i_p'), st('s5i_s'))
```

```python
import functools
import math

import jax
import jax.numpy as jnp
from jax import lax
from jax.experimental import pallas as pl
from jax.experimental.pallas import tpu as pltpu

F32 = jnp.float32
BF16 = jnp.bfloat16

CHUNK = 64
MLA_HEADS = 8
QK_NOPE = 64
QK_ROPE = 32
QK_HEAD = QK_NOPE + QK_ROPE
V_HEAD = 64
Q_LORA = 256
KV_LORA = 128
ROPE_THETA = 10000.0
MLA_SCALE = QK_HEAD ** -0.5
NEG_INF = -1e30
LRU_WIDTH = 512
LRU_BLOCKS = 8
LRU_BLOCK = LRU_WIDTH // LRU_BLOCKS
CONV_WIDTH = 4
RG_C = 8.0
S5_WIDTH = 512
S5_GROUP = 16
S5_GROUPS = S5_WIDTH // S5_GROUP
S5_STATE = 64
N_GROUPS = 4
EXPERTS_PER_GROUP = 4
N_EXPERTS = N_GROUPS * EXPERTS_PER_GROUP
EXPERT_HIDDEN = 256
LN_EPS = 1e-5
RMS_EPS = 1e-6

LANES = 128
SUBLANES = 8
VMEM_LIMIT_BYTES = 48 * 1024 * 1024
MOE_VMEM_LIMIT_BYTES = 56 * 1024 * 1024

HEAD_PAD = LANES
ROW_TILE = 512
MOE_ROW_TILE = 1024
ATTN_TQ = 1024
ATTN_TK = 512
ATTN_UNROLL = 4
Q_SCALE = MLA_SCALE * math.log2(math.e)
LRU_TILE = 512
S5_L = 8
S5_SUPER = LANES // S5_GROUP
S5_NSUPER = S5_GROUPS // S5_SUPER
S5_SUPER_IN = S5_L * LANES
S5_SUPER_STATE = S5_SUPER * S5_STATE
S5_STATE_W = S5_GROUPS * S5_STATE
S5_SCAN_LANES = 256
S5_ROW_TILE = 256
ROUTER_W = LANES


def _cparams(semantics, vmem_limit_bytes=VMEM_LIMIT_BYTES):
    return pltpu.CompilerParams(dimension_semantics=semantics,
                                vmem_limit_bytes=vmem_limit_bytes)


def _full_spec(a):
    nd = a.ndim
    return pl.BlockSpec(a.shape, lambda *_: (0,) * nd)


def _row_spec(tm, width):
    return pl.BlockSpec((tm, width), lambda i: (i, 0))


def _layer_norm(z, g, b):
    mu = jnp.mean(z, axis=-1, keepdims=True)
    zc = z - mu
    var = jnp.mean(zc * zc, axis=-1, keepdims=True)
    return zc * lax.rsqrt(var + LN_EPS) * g + b


def _rms_norm(z, g):
    ms = jnp.mean(z * z, axis=-1, keepdims=True)
    return z * lax.rsqrt(ms + RMS_EPS) * g


def _bdot(a, b):
    return jnp.dot(a.astype(BF16), b.astype(BF16), preferred_element_type=F32)


def _ab_in_kernel(x_ref, win_ref, qg_ref, kvg_ref, wuq_ref, wuk_ref, wuv_ref,
                  cosb_ref, sinb_ref, cost_ref, sint_ref,
                  q_out, k_out, v_out, c_out, kr_out, xb_out, yb_out, *, dm):
    proj = _bdot(x_ref[...], win_ref[...])
    o0 = Q_LORA
    o1 = o0 + KV_LORA
    o2 = o1 + LRU_WIDTH
    o3 = o2 + LRU_WIDTH
    q_lat = proj[:, 0:o0]
    kv_lat = proj[:, o0:o1]
    xb_out[...] = proj[:, o1:o2]
    yb_out[...] = proj[:, o2:o3]
    krp = proj[:, o3:o3 + HEAD_PAD]

    qn = _rms_norm(q_lat, qg_ref[...])
    cn = _rms_norm(kv_lat, kvg_ref[...])
    c_out[...] = cn
    q = _bdot(qn, wuq_ref[...])
    kk = _bdot(cn, wuk_ref[...])
    lane = lax.broadcasted_iota(jnp.int32, (1, MLA_HEADS * HEAD_PAD), 1)
    ones_col = (lane % HEAD_PAD == V_HEAD).astype(F32)
    v_out[...] = (_bdot(cn, wuv_ref[...]) + ones_col).astype(v_out.dtype)

    cb, sb = cosb_ref[...], sinb_ref[...]
    ct, st = cost_ref[...], sint_ref[...]
    cos = cb * ct - sb * st
    sin = sb * ct + cb * st
    half = QK_ROPE // 2
    lane1 = lax.broadcasted_iota(jnp.int32, (1, HEAD_PAD), 1)
    rc = jnp.where(lane1 < QK_HEAD, cos, 0.0)
    ra = jnp.where((lane1 >= QK_NOPE + half) & (lane1 < QK_HEAD), sin, 0.0)
    rb = jnp.where((lane1 >= QK_NOPE) & (lane1 < QK_NOPE + half), -sin, 0.0)

    def rope(z):
        return (z * rc + pltpu.roll(z, half, 1) * ra
                + pltpu.roll(z, HEAD_PAD - half, 1) * rb)

    kr = rope(krp)
    kr_out[...] = kr[:, QK_NOPE:QK_NOPE + QK_ROPE]
    for h in range(MLA_HEADS):
        sl = slice(h * HEAD_PAD, (h + 1) * HEAD_PAD)
        q_out[:, sl] = (rope(q[:, sl]) * Q_SCALE).astype(q_out.dtype)
        k_out[:, sl] = (kk[:, sl] + kr).astype(k_out.dtype)


def _ab_in(x, w, rope_tabs):
    n, dm = x.shape
    tm = min(ROW_TILE, n)
    cosb, sinb, cost, sint = rope_tabs
    consts = [w['w_in'], w['q_g'], w['kv_g'], w['w_uq'], w['w_uk'], w['w_uv'], cosb, sinb]
    tile_spec = pl.BlockSpec((None, 1, HEAD_PAD), lambda i: (i, 0, 0))
    hp = MLA_HEADS * HEAD_PAD
    out_shape = [
        jax.ShapeDtypeStruct((n, hp), BF16),
        jax.ShapeDtypeStruct((n, hp), BF16),
        jax.ShapeDtypeStruct((n, hp), BF16),
        jax.ShapeDtypeStruct((n, KV_LORA), F32),
        jax.ShapeDtypeStruct((n, QK_ROPE), F32),
        jax.ShapeDtypeStruct((n, LRU_WIDTH), F32),
        jax.ShapeDtypeStruct((n, LRU_WIDTH), F32),
    ]
    return pl.pallas_call(
        functools.partial(_ab_in_kernel, dm=dm), name="ab_in",
        grid=(n // tm,),
        in_specs=([_row_spec(tm, dm)] + [_full_spec(c) for c in consts]
                  + [tile_spec, tile_spec]),
        out_specs=[_row_spec(tm, s.shape[1]) for s in out_shape],
        out_shape=out_shape,
        compiler_params=_cparams(("parallel",)),
    )(x, *consts, cost, sint)


def _attn_prompt_kernel(q_ref, k_ref, v_ref, o_ref):
    i = pl.program_id(1)
    tq, tk = ATTN_TQ, ATTN_TK
    ndiag = tq // tk
    nfull = i * ndiag
    row_chunk = lax.broadcasted_iota(jnp.int32, (tq, tk), 0) // CHUNK
    col_chunk = lax.broadcasted_iota(jnp.int32, (tq, tk), 1) // CHUNK
    qs = [q_ref[:, hh * HEAD_PAD:(hh + 1) * HEAD_PAD] for hh in range(2)]

    def update(hh, j, m, acc, mask, row0=0):
        start = pl.multiple_of(j * tk, tk)
        k = k_ref[pl.ds(start, tk), hh * HEAD_PAD:(hh + 1) * HEAD_PAD]
        v = v_ref[pl.ds(start, tk), hh * HEAD_PAD:(hh + 1) * HEAD_PAD]
        s = lax.dot_general(qs[hh][row0:], k, (((1,), (1,)), ((), ())),
                            preferred_element_type=F32)
        if mask is not None:
            s = jnp.where(mask[row0:], s, NEG_INF)
        m_new = jnp.maximum(m, jnp.max(s, axis=-1, keepdims=True))
        alpha = jnp.exp2(m - m_new)
        p = jnp.exp2((s - m_new).astype(BF16))
        return m_new, alpha * acc + jnp.dot(p, v, preferred_element_type=F32)

    def body(j, carry):
        new = []
        for hh in range(2):
            new += update(hh, j, carry[2 * hh], carry[2 * hh + 1], None)
        return tuple(new)

    unroll = ATTN_UNROLL

    def body_unrolled(jj, carry):
        for b in range(unroll):
            carry = body(unroll * jj + b, carry)
        return carry

    init = []
    for _ in range(2):
        init += [jnp.full((tq, 1), NEG_INF, F32), jnp.zeros((tq, HEAD_PAD), F32)]
    carry = lax.fori_loop(0, nfull // unroll, body_unrolled, tuple(init))
    carry = list(lax.fori_loop(nfull - nfull % unroll, nfull, body, carry))
    for d in range(ndiag):
        visible = col_chunk + d * (tk // CHUNK) <= row_chunk
        r0 = d * tk
        for hh in range(2):
            m, acc = carry[2 * hh], carry[2 * hh + 1]
            m_new, acc_new = update(hh, nfull + d, m[r0:], acc[r0:], visible, r0)
            if r0:
                m_new = jnp.concatenate([m[:r0], m_new], axis=0)
                acc_new = jnp.concatenate([acc[:r0], acc_new], axis=0)
            carry[2 * hh], carry[2 * hh + 1] = m_new, acc_new
    outs = [carry[2 * hh + 1][:, :V_HEAD] / carry[2 * hh + 1][:, V_HEAD:V_HEAD + 1]
            for hh in range(2)]
    o_ref[...] = jnp.concatenate(outs, axis=-1).astype(o_ref.dtype)


def _attn_prompt(q, k, v):
    t = q.shape[0]
    pairs = MLA_HEADS // 2
    return pl.pallas_call(
        _attn_prompt_kernel, name="attn_prompt",
        grid=(pairs, t // ATTN_TQ),
        in_specs=[
            pl.BlockSpec((ATTN_TQ, 2 * HEAD_PAD), lambda p, i: (i, p)),
            pl.BlockSpec((t, 2 * HEAD_PAD), lambda p, i: (0, p)),
            pl.BlockSpec((t, 2 * HEAD_PAD), lambda p, i: (0, p)),
        ],
        out_specs=pl.BlockSpec((ATTN_TQ, 2 * V_HEAD), lambda p, i: (i, p)),
        out_shape=jax.ShapeDtypeStruct((t, MLA_HEADS * V_HEAD), BF16),
        compiler_params=_cparams(("parallel", "parallel")),
    )(q, k, v)


def _attn_sample_kernel(q_ref, cn_ref, krn_ref, cp_ref, krp_ref, wukt_ref, wuv_ref, o_ref):
    cp = cp_ref[...].astype(BF16)
    krp = krp_ref[...].astype(BF16)
    cn = cn_ref[...].astype(BF16)
    krn = krn_ref[...].astype(BF16)
    ts = cn.shape[0]
    dn = (((1,), (1,)), ((), ()))
    outs = []
    for hp in range(MLA_HEADS // 2):
        qa, qr = [], []
        for hh in range(2):
            h = 2 * hp + hh
            qh = q_ref[:, h * HEAD_PAD:(h + 1) * HEAD_PAD]
            qa.append(jnp.dot(qh[:, :QK_NOPE], wukt_ref[h], preferred_element_type=F32))
            qr.append(qh[:, QK_NOPE:QK_NOPE + QK_ROPE])
        qa = jnp.concatenate(qa, axis=0).astype(BF16)
        qr = jnp.concatenate(qr, axis=0)
        s_past = (lax.dot_general(qa, cp, dn, preferred_element_type=F32)
                  + lax.dot_general(qr, krp, dn, preferred_element_type=F32))
        s_new = (lax.dot_general(qa, cn, dn, preferred_element_type=F32)
                 + lax.dot_general(qr, krn, dn, preferred_element_type=F32))
        m = jnp.maximum(jnp.max(s_past, axis=-1, keepdims=True),
                        jnp.max(s_new, axis=-1, keepdims=True))
        p_past = jnp.exp2(s_past - m)
        p_new = jnp.exp2(s_new - m)
        l = (jnp.sum(p_past, axis=-1, keepdims=True)
             + jnp.sum(p_new, axis=-1, keepdims=True))
        o_lat = (jnp.dot(p_past.astype(BF16), cp, preferred_element_type=F32)
                 + jnp.dot(p_new.astype(BF16), cn, preferred_element_type=F32)) / l
        for hh in range(2):
            h = 2 * hp + hh
            outs.append(_bdot(o_lat[hh * ts:(hh + 1) * ts], wuv_ref[h]))
    o_ref[...] = jnp.concatenate(outs, axis=-1).astype(o_ref.dtype)


def _attn_sample(q, c_new, kr_new, c_past, kr_past, wukt, wuv):
    bs, past, _ = c_past.shape
    ts = q.shape[0] // bs
    return pl.pallas_call(
        _attn_sample_kernel, name="attn_sample",
        grid=(bs,),
        in_specs=[
            pl.BlockSpec((ts, MLA_HEADS * HEAD_PAD), lambda b: (b, 0)),
            pl.BlockSpec((ts, KV_LORA), lambda b: (b, 0)),
            pl.BlockSpec((ts, QK_ROPE), lambda b: (b, 0)),
            pl.BlockSpec((None, past, KV_LORA), lambda b: (b, 0, 0)),
            pl.BlockSpec((None, past, QK_ROPE), lambda b: (b, 0, 0)),
            _full_spec(wukt), _full_spec(wuv),
        ],
        out_specs=pl.BlockSpec((ts, MLA_HEADS * V_HEAD), lambda b: (b, 0)),
        out_shape=jax.ShapeDtypeStruct((bs * ts, MLA_HEADS * V_HEAD), BF16),
        compiler_params=_cparams(("parallel",)),
    )(q, c_new, kr_new, c_past, kr_past, wukt, wuv)


def _lru_kernel(xb_ref, yb_ref, conv0_ref, h0_ref, cw_ref, cb_ref, wax_ref, bax_ref, ap_ref,
                rec_ref, conv_out_ref, h_out_ref, xcat, hc, *, tb):
    t = pl.program_id(1)
    nt = pl.num_programs(1)
    tail = CONV_WIDTH - 1
    base = SUBLANES

    @pl.when(t == 0)
    def _():
        xcat[base - tail:base, :] = conv0_ref[...]
        hc[...] = h0_ref[...]

    @pl.when(t > 0)
    def _():
        xcat[0:base, :] = xcat[tb:tb + base, :]

    xcat[base:base + tb, :] = xb_ref[...]
    xc = cb_ref[...]
    for tap in range(CONV_WIDTH):
        xc = xc + xcat[base - tail + tap:base - tail + tap + tb, :] * cw_ref[tap:tap + 1, :]

    gates = _bdot(xc, wax_ref[...]) + bax_ref[...]
    r = jax.nn.sigmoid(gates[:, :LRU_WIDTH])
    ig = jax.nn.sigmoid(gates[:, LRU_WIDTH:])
    log_a = -RG_C * r * ap_ref[...]
    a = jnp.exp(log_a)
    u = jnp.sqrt(jnp.tanh(-log_a) * (1.0 + a * a)) * (ig * xc)

    row8 = lax.broadcasted_iota(jnp.int32, (tb, LRU_WIDTH), 0) % SUBLANES
    s = 1
    while s < SUBLANES:
        keep = row8 >= s
        u = jnp.where(keep, a * pltpu.roll(u, s, 0) + u, u)
        a = jnp.where(keep, a * pltpu.roll(a, s, 0), a)
        s *= 2
    hprev = hc[...]
    groups = []
    for g in range(tb // SUBLANES):
        sl = slice(g * SUBLANES, (g + 1) * SUBLANES)
        hg = a[sl] * hprev + u[sl]
        groups.append(hg)
        hprev = hg[SUBLANES - 1:SUBLANES, :]
    h = jnp.concatenate(groups, axis=0)
    hc[...] = hprev
    rec_ref[...] = (h * jax.nn.gelu(yb_ref[...])).astype(rec_ref.dtype)

    @pl.when(t == nt - 1)
    def _():
        conv_out_ref[...] = xcat[base + tb - tail:base + tb, :]
        h_out_ref[...] = h[tb - 1:tb, :]


def _lru(xb, yb, conv0, h0, w):
    b, t, wd = xb.shape
    tb = min(LRU_TILE, t)
    tail = CONV_WIDTH - 1
    consts = [w['conv_w'], w['conv_b'], w['w_ax'], w['b_ax'], w['a_param']]
    seq = pl.BlockSpec((None, tb, wd), lambda bi, ti: (bi, ti, 0))
    return pl.pallas_call(
        functools.partial(_lru_kernel, tb=tb), name="rg_lru",
        grid=(b, t // tb),
        in_specs=[seq, seq,
                  pl.BlockSpec((None, tail, wd), lambda bi, ti: (bi, 0, 0)),
                  pl.BlockSpec((None, 1, wd), lambda bi, ti: (bi, 0, 0))]
                 + [pl.BlockSpec(c.shape, lambda bi, ti: (0, 0)) for c in consts],
        out_specs=[seq,
                   pl.BlockSpec((None, tail, wd), lambda bi, ti: (bi, 0, 0)),
                   pl.BlockSpec((None, 1, wd), lambda bi, ti: (bi, 0, 0))],
        out_shape=[jax.ShapeDtypeStruct((b, t, wd), BF16),
                   jax.ShapeDtypeStruct((b, tail, wd), F32),
                   jax.ShapeDtypeStruct((b, 1, wd), F32)],
        scratch_shapes=[pltpu.VMEM((tb + 2 * SUBLANES, wd), F32), pltpu.VMEM((1, wd), F32)],
        compiler_params=_cparams(("parallel", "arbitrary")),
    )(xb, yb, conv0, h0, *consts)


def _out_ln_kernel(a_ref, b_ref, x_ref, wa_ref, wb_ref, g_ref, beta_ref, o_ref, *, alpha):
    y = (jnp.dot(a_ref[...], wa_ref[...], preferred_element_type=F32)
         + jnp.dot(b_ref[...], wb_ref[...], preferred_element_type=F32))
    o_ref[...] = _layer_norm(alpha * x_ref[...] + y, g_ref[...], beta_ref[...])


def _out_ln(att, rec, x, wa, wb, g, beta, alpha):
    n, dm = x.shape
    tm = min(ROW_TILE, n)
    consts = [wa, wb, g, beta]
    return pl.pallas_call(
        functools.partial(_out_ln_kernel, alpha=alpha), name="out_ln",
        grid=(n // tm,),
        in_specs=[_row_spec(tm, att.shape[1]), _row_spec(tm, rec.shape[1]), _row_spec(tm, dm)]
                 + [_full_spec(c) for c in consts],
        out_specs=_row_spec(tm, dm),
        out_shape=jax.ShapeDtypeStruct((n, dm), F32),
        compiler_params=_cparams(("parallel",)),
    )(att, rec, x, *consts)


def _moe_ln_kernel(x_ref, wrh_ref, wrl_ref, wg_ref, wu_ref, wd_ref, g_ref, beta_ref, o_ref,
                   acc, gates, xb16, *, alpha):
    grp = pl.program_id(1)
    tm = x_ref.shape[0]
    lane = lax.broadcasted_iota(jnp.int32, (tm, ROUTER_W), 1).astype(F32)

    @pl.when(grp == 0)
    def _():
        x = x_ref[...]
        xh = x.astype(BF16)
        xb16[...] = xh
        xl = (x - xh.astype(F32)).astype(BF16)
        wh = wrh_ref[...]
        logits = (jnp.dot(xh, wh, preferred_element_type=F32)
                  + jnp.dot(xl, wh, preferred_element_type=F32)
                  + jnp.dot(xh, wrl_ref[...], preferred_element_type=F32))
        big = float(ROUTER_W)
        lg = jnp.where(lane < N_GROUPS, logits, -jnp.inf)
        mg = jnp.max(lg, axis=-1, keepdims=True)
        gi = jnp.min(jnp.where(lg == mg, lane, big), axis=-1, keepdims=True)
        p_top = 1.0 / jnp.sum(jnp.exp(lg - mg), axis=-1, keepdims=True)
        lo = N_GROUPS + gi * EXPERTS_PER_GROUP
        le = jnp.where((lane >= lo) & (lane < lo + EXPERTS_PER_GROUP), logits, -jnp.inf)
        v1 = jnp.max(le, axis=-1, keepdims=True)
        i1 = jnp.min(jnp.where(le == v1, lane, big), axis=-1, keepdims=True)
        le2 = jnp.where(lane == i1, -jnp.inf, le)
        v2 = jnp.max(le2, axis=-1, keepdims=True)
        i2 = jnp.min(jnp.where(le2 == v2, lane, big), axis=-1, keepdims=True)
        e2 = jnp.exp(v2 - v1)
        w1 = p_top / (1.0 + e2)
        w2 = p_top * e2 / (1.0 + e2)
        gates[...] = jnp.where(lane == i1, w1, 0.0) + jnp.where(lane == i2, w2, 0.0)
        acc[...] = jnp.zeros_like(acc)

    xb = xb16[...]
    gt = gates[...]
    first = (N_GROUPS + grp * EXPERTS_PER_GROUP).astype(F32)
    for el in range(EXPERTS_PER_GROUP):
        ge = jnp.sum(jnp.where(lane == first + el, gt, 0.0), axis=-1, keepdims=True)
        h = jnp.dot(xb, wg_ref[el], preferred_element_type=F32)
        u = jnp.dot(xb, wu_ref[el], preferred_element_type=F32)
        act = (jax.nn.silu(h) * u * ge).astype(BF16)
        acc[...] += jnp.dot(act, wd_ref[el], preferred_element_type=F32)

    @pl.when(grp == pl.num_programs(1) - 1)
    def _():
        o_ref[...] = _layer_norm(alpha * x_ref[...] + acc[...], g_ref[...], beta_ref[...])


def _moe_ln(x, w, layer, g, beta, alpha):
    n, dm = x.shape
    tm = min(MOE_ROW_TILE, n)
    _, ne, _, eh = w['w_gate'].shape
    epg = EXPERTS_PER_GROUP
    return pl.pallas_call(
        functools.partial(_moe_ln_kernel, alpha=alpha), name="moe_ln",
        grid=(n // tm, ne // epg),
        in_specs=[
            pl.BlockSpec((tm, dm), lambda i, e: (i, 0)),
            pl.BlockSpec(w['w_router_hi'].shape, lambda i, e: (0, 0)),
            pl.BlockSpec(w['w_router_lo'].shape, lambda i, e: (0, 0)),
            pl.BlockSpec((None, epg, dm, eh), lambda i, e: (layer, e, 0, 0)),
            pl.BlockSpec((None, epg, dm, eh), lambda i, e: (layer, e, 0, 0)),
            pl.BlockSpec((None, epg, eh, dm), lambda i, e: (layer, e, 0, 0)),
            pl.BlockSpec(g.shape, lambda i, e: (0, 0)),
            pl.BlockSpec(beta.shape, lambda i, e: (0, 0)),
        ],
        out_specs=pl.BlockSpec((tm, dm), lambda i, e: (i, 0)),
        out_shape=jax.ShapeDtypeStruct((n, dm), F32),
        scratch_shapes=[pltpu.VMEM((tm, dm), F32), pltpu.VMEM((tm, ROUTER_W), F32),
                        pltpu.VMEM((tm, dm), BF16)],
        compiler_params=_cparams(("parallel", "arbitrary"), MOE_VMEM_LIMIT_BYTES),
    )(x, w['w_router_hi'], w['w_router_lo'], w['w_gate'], w['w_up'], w['w_down'], g, beta)


def _c_in_kernel(x_ref, w_ref, u_ref, u2_ref, us):
    u = _bdot(x_ref[...], w_ref[...])
    u_ref[...] = u
    rows = u2_ref.shape[0]
    for c in range(S5_WIDTH // LANES):
        us[c] = u[:, c * LANES:(c + 1) * LANES]
    for s in range(S5_L):
        for c in range(S5_WIDTH // LANES):
            lo = s * S5_WIDTH + c * LANES
            u2_ref[:, lo:lo + LANES] = us[c, pl.ds(s, rows, stride=S5_L), :].astype(u2_ref.dtype)


def _c_in(x, w_in):
    n, dm = x.shape
    tm = min(ROW_TILE, n)
    wu = w_in.shape[1]
    return pl.pallas_call(
        _c_in_kernel, name="c_in",
        grid=(n // tm,),
        in_specs=[_row_spec(tm, dm), _full_spec(w_in)],
        out_specs=[_row_spec(tm, wu), _row_spec(tm // S5_L, S5_L * wu)],
        out_shape=[jax.ShapeDtypeStruct((n, wu), F32),
                   jax.ShapeDtypeStruct((n // S5_L, S5_L * wu), BF16)],
        scratch_shapes=[pltpu.VMEM((wu // LANES, tm, LANES), F32)],
        compiler_params=_cparams(("parallel",)),
    )(x, w_in)


def _s5_super_rows(u_ref, q):
    tiles = [u_ref[:, s * S5_WIDTH + q * LANES:s * S5_WIDTH + (q + 1) * LANES]
             for s in range(S5_L)]
    return jnp.concatenate(tiles, axis=1).astype(BF16)


def _s5_e_kernel(u_ref, ere_ref, eim_ref, ere_out, eim_out):
    for q in range(S5_NSUPER):
        uq = _s5_super_rows(u_ref, q)
        sl = slice(q * S5_SUPER_STATE, (q + 1) * S5_SUPER_STATE)
        ere_out[:, sl] = jnp.dot(uq, ere_ref[q], preferred_element_type=F32)
        eim_out[:, sl] = jnp.dot(uq, eim_ref[q], preferred_element_type=F32)


def _const_spec(a):
    nd = a.ndim
    return pl.BlockSpec(a.shape, lambda *_: (0,) * nd, pipeline_mode=pl.Buffered(1))


def _s5_e(u2, tabs):
    rows, wu = u2.shape
    tr = min(S5_ROW_TILE, rows)
    out = jax.ShapeDtypeStruct((rows, S5_STATE_W), F32)
    return pl.pallas_call(
        _s5_e_kernel, name="s5_e",
        grid=(rows // tr,),
        in_specs=[_row_spec(tr, wu), _const_spec(tabs['e_re']), _const_spec(tabs['e_im'])],
        out_specs=[_row_spec(tr, S5_STATE_W)] * 2,
        out_shape=[out, out],
        compiler_params=_cparams(("parallel",)),
    )(u2, tabs['e_re'], tabs['e_im'])


def _s5_scan_kernel(ere_ref, eim_ref, h0r_ref, h0i_ref, stp_re_ref, stp_im_ref,
                    apw_re_ref, apw_im_ref, hpr_ref, hpi_ref, hfr_ref, hfi_ref, *, nc, last_row):
    w = ere_ref.shape[-1]
    row = lax.broadcasted_iota(jnp.int32, (SUBLANES, w), 0)
    apr = apw_re_ref[...]
    api = apw_im_ref[...]

    def tile(i, carry):
        cr, ci = carry
        start = pl.multiple_of(i * SUBLANES, SUBLANES)
        sr = ere_ref[pl.ds(start, SUBLANES), :]
        si = eim_ref[pl.ds(start, SUBLANES), :]
        for k in range(3):
            s = 1 << k
            keep = row >= s
            ar = stp_re_ref[k:k + 1, :]
            ai = stp_im_ref[k:k + 1, :]
            pr = pltpu.roll(sr, s, 0)
            pi = pltpu.roll(si, s, 0)
            sr, si = (jnp.where(keep, sr + ar * pr - ai * pi, sr),
                      jnp.where(keep, si + ar * pi + ai * pr, si))
        hr = sr + apr * cr - api * ci
        hi = si + apr * ci + api * cr
        first = row == 0
        hpr_ref[pl.ds(start, SUBLANES), :] = jnp.where(first, cr, pltpu.roll(hr, 1, 0))
        hpi_ref[pl.ds(start, SUBLANES), :] = jnp.where(first, ci, pltpu.roll(hi, 1, 0))
        return hr, hi

    def body(i, carry):
        hr, hi = tile(i, carry)
        return hr[SUBLANES - 1:SUBLANES, :], hi[SUBLANES - 1:SUBLANES, :]

    ntile = nc // SUBLANES
    carry = lax.fori_loop(0, ntile - 1, body, (h0r_ref[...], h0i_ref[...]))
    hr, hi = tile(ntile - 1, carry)
    hfr_ref[...] = hr[last_row:last_row + 1, :]
    hfi_ref[...] = hi[last_row:last_row + 1, :]


def _s5_scan(e_re, e_im, h0r, h0i, tabs, n_chunks):
    b, nc, w = e_re.shape
    last_row = (n_chunks - 1) % SUBLANES
    wb = S5_SCAN_LANES
    consts = [tabs['step_re'], tabs['step_im'], tabs['apow_re'], tabs['apow_im']]
    seq = pl.BlockSpec((None, nc, wb), lambda bi, li: (bi, 0, li))
    vec = pl.BlockSpec((None, 1, wb), lambda bi, li: (bi, 0, li))
    return pl.pallas_call(
        functools.partial(_s5_scan_kernel, nc=nc, last_row=last_row), name="s5_scan",
        grid=(b, w // wb),
        in_specs=[seq, seq, vec, vec]
                 + [pl.BlockSpec((c.shape[0], wb), lambda bi, li: (0, li)) for c in consts],
        out_specs=[seq, seq, vec, vec],
        out_shape=[jax.ShapeDtypeStruct((b, nc, w), F32)] * 2
                  + [jax.ShapeDtypeStruct((b, 1, w), F32)] * 2,
        compiler_params=_cparams(("parallel", "parallel")),
    )(e_re, e_im, h0r, h0i, *consts)


def _s5_y_kernel(u_ref, hpr_ref, hpi_ref, m_ref, fre_ref, fim_ref, y_ref):
    for q in range(S5_NSUPER):
        uq = _s5_super_rows(u_ref, q)
        sl = slice(q * S5_SUPER_STATE, (q + 1) * S5_SUPER_STATE)
        yq = (jnp.dot(uq, m_ref[q], preferred_element_type=F32)
              + _bdot(hpr_ref[:, sl], fre_ref[q])
              + _bdot(hpi_ref[:, sl], fim_ref[q]))
        for t in range(S5_L):
            y_ref[:, t * S5_WIDTH + q * LANES:t * S5_WIDTH + (q + 1) * LANES] = (
                yq[:, t * LANES:(t + 1) * LANES])


def _s5_y(u2, hp_re, hp_im, tabs):
    rows, wu = u2.shape
    tr = min(S5_ROW_TILE, rows)
    consts = [tabs['m'], tabs['f_re'], tabs['f_im']]
    return pl.pallas_call(
        _s5_y_kernel, name="s5_y",
        grid=(rows // tr,),
        in_specs=[_row_spec(tr, wu), _row_spec(tr, S5_STATE_W), _row_spec(tr, S5_STATE_W)]
                 + [_const_spec(c) for c in consts],
        out_specs=_row_spec(tr, wu),
        out_shape=jax.ShapeDtypeStruct((rows, wu), F32),
        compiler_params=_cparams(("parallel",)),
    )(u2, hp_re, hp_im, *consts)


def _c_out_kernel(y2_ref, u_ref, x_ref, d_ref, wglu_ref, wout_ref, g_ref, beta_ref, o_ref, ys,
                  *, alpha):
    rows = y2_ref.shape[0]
    ntile = S5_WIDTH // LANES
    for s in range(S5_L):
        for c in range(ntile):
            lo = s * S5_WIDTH + c * LANES
            ys[c, pl.ds(s, rows, stride=S5_L), :] = y2_ref[:, lo:lo + LANES]
    y = jnp.concatenate([ys[c] for c in range(ntile)], axis=1) + d_ref[...] * u_ref[...]
    vg = _bdot(jax.nn.gelu(y), wglu_ref[...])
    half = vg.shape[1] // 2
    z = vg[:, :half] * jax.nn.sigmoid(vg[:, half:])
    o = _bdot(z, wout_ref[...])
    o_ref[...] = _layer_norm(alpha * x_ref[...] + o, g_ref[...], beta_ref[...])


def _c_out(y2, u, x, w, g, beta, alpha):
    n, dm = x.shape
    tm = min(ROW_TILE, n)
    consts = [w['d'], w['w_glu'], w['w_out'], g, beta]
    return pl.pallas_call(
        functools.partial(_c_out_kernel, alpha=alpha), name="c_out",
        grid=(n // tm,),
        in_specs=[_row_spec(tm // S5_L, y2.shape[1]), _row_spec(tm, u.shape[1]),
                  _row_spec(tm, dm)] + [_full_spec(c) for c in consts],
        out_specs=_row_spec(tm, dm),
        out_shape=jax.ShapeDtypeStruct((n, dm), F32),
        scratch_shapes=[pltpu.VMEM((u.shape[1] // LANES, tm, LANES), F32)],
        compiler_params=_cparams(("parallel",)),
    )(y2, u, x, *consts)


def _prep_ab(w_in, q_g, kv_g, w_uq, w_ukv, conv_w, conv_b, wa, ba, wx, bx, a_param, w_out):
    dm = w_in.shape[0]
    s0 = Q_LORA
    s1 = s0 + KV_LORA
    s2 = s1 + QK_ROPE
    s3 = s2 + LRU_WIDTH
    pad_lo = jnp.zeros((dm, QK_NOPE), F32)
    pad_hi = jnp.zeros((dm, HEAD_PAD - QK_HEAD), F32)
    w_in_p = jnp.concatenate([w_in[:, :s1], w_in[:, s2:s3], w_in[:, s3:],
                              pad_lo, w_in[:, s1:s2], pad_hi], axis=1).astype(BF16)
    wq = w_uq.reshape(Q_LORA, MLA_HEADS, QK_HEAD)
    wq = jnp.pad(wq, ((0, 0), (0, 0), (0, HEAD_PAD - QK_HEAD)))
    wq = wq.reshape(Q_LORA, MLA_HEADS * HEAD_PAD).astype(BF16)
    wkv = w_ukv.reshape(KV_LORA, MLA_HEADS, QK_NOPE + V_HEAD)
    wk = jnp.pad(wkv[:, :, :QK_NOPE], ((0, 0), (0, 0), (0, HEAD_PAD - QK_NOPE)))
    wk = wk.reshape(KV_LORA, MLA_HEADS * HEAD_PAD).astype(BF16)
    wv = jnp.pad(wkv[:, :, QK_NOPE:], ((0, 0), (0, 0), (0, HEAD_PAD - V_HEAD)))
    wv = wv.reshape(KV_LORA, MLA_HEADS * HEAD_PAD).astype(BF16)
    wukt = jnp.transpose(wkv[:, :, :QK_NOPE], (1, 2, 0)).astype(BF16)
    wuv_h = jnp.transpose(wkv[:, :, QK_NOPE:], (1, 0, 2)).astype(BF16)
    eye = jnp.eye(LRU_BLOCKS, dtype=F32)

    def blockdiag(wb):
        return jnp.einsum('hij,hk->hikj', wb, eye).reshape(LRU_WIDTH, LRU_WIDTH)

    w_ax = jnp.concatenate([blockdiag(wa), blockdiag(wx)], axis=1).astype(BF16)
    b_ax = jnp.concatenate([ba, bx])[None, :]
    att_w = MLA_HEADS * V_HEAD
    return dict(w_in=w_in_p, q_g=q_g[None, :], kv_g=kv_g[None, :], w_uq=wq, w_uk=wk, w_uv=wv,
                wukt=wukt, wuv_h=wuv_h, conv_w=conv_w, conv_b=conv_b[None, :], w_ax=w_ax,
                b_ax=b_ax, a_param=jax.nn.softplus(-a_param)[None, :],
                w_out_att=w_out[:att_w].astype(BF16), w_out_rec=w_out[att_w:].astype(BF16))


def _rope_tables(row_pos, tile_pos):
    half = QK_ROPE // 2
    inv = ROPE_THETA ** (-jnp.arange(half, dtype=F32) / half)
    inv_lane = jnp.concatenate([jnp.zeros((QK_NOPE,), F32), inv, inv,
                                jnp.zeros((HEAD_PAD - QK_HEAD,), F32)])
    ang_row = row_pos.astype(F32)[:, None] * inv_lane[None, :]
    ang_tile = tile_pos.astype(F32)[:, None, None] * inv_lane
    return jnp.cos(ang_row), jnp.sin(ang_row), jnp.cos(ang_tile), jnp.sin(ang_tile)


def _prep_router(w_group, w_expert):
    dm = w_group.shape[0]
    pad = jnp.zeros((dm, ROUTER_W - N_GROUPS - N_EXPERTS), F32)
    w_router = jnp.concatenate([w_group, w_expert, pad], axis=1)
    w_router_hi = w_router.astype(BF16)
    w_router_lo = (w_router - w_router_hi.astype(F32)).astype(BF16)
    return dict(w_router_hi=w_router_hi, w_router_lo=w_router_lo)


def _cmul(ar, ai, br, bi):
    return ar * br - ai * bi, ar * bi + ai * br


def _prep_s5(lam_re, lam_im, log_step, b_re, b_im, c_re, c_im):
    hi = lax.Precision.HIGHEST
    L, G, P, C = S5_L, S5_GROUPS, S5_STATE, S5_GROUP
    lr, li = lam_re, lam_im
    dt = jnp.exp(log_step)[:, None]
    mag = jnp.exp(lr * dt)
    ar, ai = mag * jnp.cos(li * dt), mag * jnp.sin(li * dt)
    den = lr * lr + li * li
    nr = ar - 1.0
    cr = (nr * lr + ai * li) / den
    ci = (ai * lr - nr * li) / den
    bbr = cr[..., None] * b_re - ci[..., None] * b_im
    bbi = cr[..., None] * b_im + ci[..., None] * b_re
    prs, pis = [jnp.ones_like(ar)], [jnp.zeros_like(ar)]
    for _ in range(L):
        nr_, ni_ = _cmul(prs[-1], pis[-1], ar, ai)
        prs.append(nr_)
        pis.append(ni_)
    pr = jnp.stack(prs)
    pi = jnp.stack(pis)
    xr = pr[:L, :, :, None] * bbr - pi[:L, :, :, None] * bbi
    xi = pr[:L, :, :, None] * bbi + pi[:L, :, :, None] * bbr
    kk = (jnp.einsum('gcp,kgpd->kgcd', c_re, xr, precision=hi)
          - jnp.einsum('gcp,kgpd->kgcd', c_im, xi, precision=hi))
    p1r = pr[1:, :, :, None]
    p1i = pi[1:, :, :, None]
    cre_t = jnp.transpose(c_re, (0, 2, 1))[None]
    cim_t = jnp.transpose(c_im, (0, 2, 1))[None]
    f_re = cre_t * p1r - cim_t * p1i
    f_im = -cre_t * p1i - cim_t * p1r

    nq, sg = S5_NSUPER, S5_SUPER
    eye = jnp.eye(sg, dtype=F32)
    dk = jnp.einsum('kqgcd,gh->kqgdhc', kk.reshape(L, nq, sg, C, C), eye)
    dk = dk.reshape(L, nq, LANES, LANES).astype(BF16)
    lag = jnp.arange(L)[None, :] - jnp.arange(L)[:, None]
    m = jnp.where((lag >= 0)[:, :, None, None, None], dk[jnp.clip(lag, 0, L - 1)], 0)
    m = jnp.transpose(m, (2, 0, 3, 1, 4)).reshape(nq, L * LANES, L * LANES)

    row_g = (jnp.arange(L * LANES) // C) % sg
    col_g = jnp.arange(sg * P) // P
    same_group = (row_g[:, None] == col_g[None, :])[None]

    def pack_rows(a):
        a = jnp.transpose(a.reshape(L, nq, sg, P, C), (1, 0, 4, 2, 3))
        a = a.reshape(nq, L, 1, C, sg * P).astype(BF16)
        a = jnp.broadcast_to(a, (nq, L, sg, C, sg * P)).reshape(nq, L * LANES, sg * P)
        return jnp.where(same_group, a, 0)

    def pack_e(x):
        return pack_rows(x[::-1])

    def pack_f(f):
        return jnp.swapaxes(pack_rows(f), 1, 2)

    alr, ali = pr[L].reshape(1, G * P), pi[L].reshape(1, G * P)
    qr, qi = [alr], [ali]
    for _ in range(SUBLANES - 1):
        nr_, ni_ = _cmul(qr[-1], qi[-1], alr, ali)
        qr.append(nr_)
        qi.append(ni_)
    apow_re = jnp.concatenate(qr, axis=0)
    apow_im = jnp.concatenate(qi, axis=0)
    step_re = jnp.concatenate([qr[0], qr[1], qr[3]], axis=0)
    step_im = jnp.concatenate([qi[0], qi[1], qi[3]], axis=0)
    return dict(m=m, e_re=pack_e(xr), e_im=pack_e(xi), f_re=pack_f(f_re),
                f_im=pack_f(f_im), apow_re=apow_re, apow_im=apow_im,
                step_re=step_re, step_im=step_im)


def _mixer_ab(x, pos0, past_ckv, past_krope, conv0, h0, w):
    b, t, dm = x.shape
    xf = x.reshape(b * t, dm)
    tm = min(ROW_TILE, b * t)
    assert tm % t == 0 or t % tm == 0
    row_pos = jnp.arange(tm) % t
    tile_pos = pos0 + (jnp.arange(b * t // tm) * tm) % t
    q, k, v, c_new, kr_new, xb, yb = _ab_in(xf, w, _rope_tables(row_pos, tile_pos))
    if past_ckv is None:
        att = _attn_prompt(q, k, v)
    else:
        att = _attn_sample(q, c_new, kr_new, past_ckv, past_krope, w['wukt'], w['wuv_h'])
    rec, conv_new, h_new = _lru(xb.reshape(b, t, -1), yb.reshape(b, t, -1), conv0,
                                h0[:, None, :], w)
    return (att, rec.reshape(b * t, -1), c_new.reshape(b, t, -1), kr_new.reshape(b, t, -1),
            conv_new, h_new[:, 0, :])


def _mixer_c(x, h0_re, h0_im, w):
    b, t, dm = x.shape
    xf = x.reshape(b * t, dm)
    nc = t // S5_L
    assert t % S5_L == 0 and nc % SUBLANES == 0
    u, u2 = _c_in(xf, w['w_in'])
    e_re, e_im = _s5_e(u2, w['tabs'])
    hp_re, hp_im, hf_re, hf_im = _s5_scan(
        e_re.reshape(b, nc, S5_STATE_W), e_im.reshape(b, nc, S5_STATE_W),
        h0_re.reshape(b, 1, S5_STATE_W), h0_im.reshape(b, 1, S5_STATE_W), w['tabs'], nc)
    y2 = _s5_y(u2, hp_re.reshape(b * nc, S5_STATE_W), hp_im.reshape(b * nc, S5_STATE_W),
               w['tabs'])
    return (y2, u, hf_re.reshape(b, S5_GROUPS, S5_STATE), hf_im.reshape(b, S5_GROUPS, S5_STATE))


def kernel(x_prompt, x_sample, cache_mla_ckv, cache_mla_krope, state_lru_conv, state_lru_h,
           state_s5_re, state_s5_im, w_in_ab, q_norm_g, kv_norm_g, w_uq, w_ukv,
           lru_conv_w, lru_conv_b, lru_w_a, lru_b_a, lru_w_x, lru_b_x, lru_a_param, w_out_ab,
           w_in_c, s5_lam_re, s5_lam_im, s5_log_step, s5_b_re, s5_b_im, s5_c_re, s5_c_im,
           s5_d, s5_w_glu, w_out_c, ln_mix_g, ln_mix_b, ln_ffn_g, ln_ffn_b,
           moe_w_group, moe_w_expert, moe_w_gate, moe_w_up, moe_w_down):
    bp, tp, dm = x_prompt.shape
    bs, ts, _ = x_sample.shape
    past = cache_mla_ckv.shape[2]
    depth = ln_mix_g.shape[0]
    alpha = (2 * depth) ** 0.25
    assert (past + ts - 1) // CHUNK <= past // CHUNK
    hp, hs = x_prompt, x_sample
    moe_experts = dict(w_gate=moe_w_gate.astype(BF16), w_up=moe_w_up.astype(BF16),
                       w_down=moe_w_down.astype(BF16))
    outs = {k: [] for k in ('ckv_p', 'ckv_s', 'kr_p', 'kr_s', 'cv_p', 'cv_s', 'lh_p', 'lh_s',
                            's5r_p', 's5r_s', 's5i_p', 's5i_s')}
    for layer in range(depth):
        j = layer // 2
        g_mix, b_mix = ln_mix_g[layer][None, :], ln_mix_b[layer][None, :]
        g_ffn, b_ffn = ln_ffn_g[layer][None, :], ln_ffn_b[layer][None, :]
        if layer % 2 == 0:
            w = _prep_ab(w_in_ab[j], q_norm_g[j], kv_norm_g[j], w_uq[j], w_ukv[j], lru_conv_w[j],
                         lru_conv_b[j], lru_w_a[j], lru_b_a[j], lru_w_x[j], lru_b_x[j],
                         lru_a_param[j], w_out_ab[j])
            att_p, rec_p, c1, k1, v1, h1 = _mixer_ab(
                hp, 0, None, None, jnp.zeros((bp, CONV_WIDTH - 1, LRU_WIDTH), F32),
                jnp.zeros((bp, LRU_WIDTH), F32), w)
            att_s, rec_s, c2, k2, v2, h2 = _mixer_ab(
                hs, past, cache_mla_ckv[j], cache_mla_krope[j], state_lru_conv[j],
                state_lru_h[j], w)
            outs['ckv_p'].append(c1); outs['ckv_s'].append(c2)
            outs['kr_p'].append(k1); outs['kr_s'].append(k2)
            outs['cv_p'].append(v1); outs['cv_s'].append(v2)
            outs['lh_p'].append(h1); outs['lh_s'].append(h2)
            hp = _out_ln(att_p, rec_p, hp.reshape(bp * tp, dm), w['w_out_att'], w['w_out_rec'],
                         g_mix, b_mix, alpha)
            hs = _out_ln(att_s, rec_s, hs.reshape(bs * ts, dm), w['w_out_att'], w['w_out_rec'],
                         g_mix, b_mix, alpha)
        else:
            w = dict(w_in=w_in_c[j].astype(BF16), d=s5_d[j][None, :],
                     w_glu=s5_w_glu[j].astype(BF16), w_out=w_out_c[j].astype(BF16),
                     tabs=_prep_s5(s5_lam_re[j], s5_lam_im[j], s5_log_step[j], s5_b_re[j],
                                   s5_b_im[j], s5_c_re[j], s5_c_im[j]))
            zero_state = jnp.zeros((bp, S5_GROUPS, S5_STATE), F32)
            y_p, u_p, r1, i1 = _mixer_c(hp, zero_state, zero_state, w)
            y_s, u_s, r2, i2 = _mixer_c(hs, state_s5_re[j], state_s5_im[j], w)
            outs['s5r_p'].append(r1); outs['s5r_s'].append(r2)
            outs['s5i_p'].append(i1); outs['s5i_s'].append(i2)
            hp = _c_out(y_p, u_p, hp.reshape(bp * tp, dm), w, g_mix, b_mix, alpha)
            hs = _c_out(y_s, u_s, hs.reshape(bs * ts, dm), w, g_mix, b_mix, alpha)
        wm = dict(moe_experts, **_prep_router(moe_w_group[layer], moe_w_expert[layer]))
        hp = _moe_ln(hp, wm, layer, g_ffn, b_ffn, alpha).reshape(bp, tp, dm)
        hs = _moe_ln(hs, wm, layer, g_ffn, b_ffn, alpha).reshape(bs, ts, dm)
    st = lambda k: jnp.stack(outs[k])
    return (hp, hs, st('ckv_p'), st('ckv_s'), st('kr_p'), st('kr_s'), st('cv_p'), st('cv_s'),
            st('lh_p'), st('lh_s'), st('s5r_p'), st('s5r_s'), st('s5i_p'), st('s5i_s'))
```

```python
import functools
import math

import jax
import jax.numpy as jnp
from jax import lax
from jax.experimental import pallas as pl
from jax.experimental.pallas import tpu as pltpu

F32 = jnp.float32
BF16 = jnp.bfloat16

CHUNK = 64
MLA_HEADS = 8
QK_NOPE = 64
QK_ROPE = 32
QK_HEAD = QK_NOPE + QK_ROPE
V_HEAD = 64
Q_LORA = 256
KV_LORA = 128
ROPE_THETA = 10000.0
MLA_SCALE = QK_HEAD ** -0.5
NEG_INF = -1e30
LRU_WIDTH = 512
LRU_BLOCKS = 8
LRU_BLOCK = LRU_WIDTH // LRU_BLOCKS
CONV_WIDTH = 4
RG_C = 8.0
S5_WIDTH = 512
S5_GROUP = 16
S5_GROUPS = S5_WIDTH // S5_GROUP
S5_STATE = 64
N_GROUPS = 4
EXPERTS_PER_GROUP = 4
N_EXPERTS = N_GROUPS * EXPERTS_PER_GROUP
EXPERT_HIDDEN = 256
LN_EPS = 1e-5
RMS_EPS = 1e-6

LANES = 128
SUBLANES = 8
VMEM_LIMIT_BYTES = 48 * 1024 * 1024
MOE_VMEM_LIMIT_BYTES = 56 * 1024 * 1024

HEAD_PAD = LANES
ROW_TILE = 512
MOE_ROW_TILE = 1024
ATTN_TQ = 1024
ATTN_TK = 512
ATTN_UNROLL = 4
Q_SCALE = MLA_SCALE * math.log2(math.e)
LRU_TILE = 512
S5_L = 8
S5_SUPER = LANES // S5_GROUP
S5_NSUPER = S5_GROUPS // S5_SUPER
S5_SUPER_IN = S5_L * LANES
S5_SUPER_STATE = S5_SUPER * S5_STATE
S5_STATE_W = S5_GROUPS * S5_STATE
S5_SCAN_LANES = 256
S5_ROW_TILE = 256
ROUTER_W = LANES


def _cparams(semantics, vmem_limit_bytes=VMEM_LIMIT_BYTES):
    return pltpu.CompilerParams(dimension_semantics=semantics,
                                vmem_limit_bytes=vmem_limit_bytes)


def _full_spec(a):
    nd = a.ndim
    return pl.BlockSpec(a.shape, lambda *_: (0,) * nd)


def _row_spec(tm, width):
    return pl.BlockSpec((tm, width), lambda i: (i, 0))


def _layer_norm(z, g, b):
    mu = jnp.mean(z, axis=-1, keepdims=True)
    zc = z - mu
    var = jnp.mean(zc * zc, axis=-1, keepdims=True)
    return zc * lax.rsqrt(var + LN_EPS) * g + b


def _rms_norm(z, g):
    ms = jnp.mean(z * z, axis=-1, keepdims=True)
    return z * lax.rsqrt(ms + RMS_EPS) * g


def _bdot(a, b):
    return jnp.dot(a.astype(BF16), b.astype(BF16), preferred_element_type=F32)


def _ab_in_kernel(x_ref, win_ref, qg_ref, kvg_ref, wuq_ref, wuk_ref, wuv_ref,
                  cosb_ref, sinb_ref, cost_ref, sint_ref,
                  q_out, k_out, v_out, c_out, kr_out, xb_out, yb_out, *, dm):
    proj = _bdot(x_ref[...], win_ref[...])
    o0 = Q_LORA
    o1 = o0 + KV_LORA
    o2 = o1 + LRU_WIDTH
    o3 = o2 + LRU_WIDTH
    q_lat = proj[:, 0:o0]
    kv_lat = proj[:, o0:o1]
    xb_out[...] = proj[:, o1:o2]
    yb_out[...] = proj[:, o2:o3]
    krp = proj[:, o3:o3 + HEAD_PAD]

    qn = _rms_norm(q_lat, qg_ref[...])
    cn = _rms_norm(kv_lat, kvg_ref[...])
    c_out[...] = cn
    q = _bdot(qn, wuq_ref[...])
    kk = _bdot(cn, wuk_ref[...])
    lane = lax.broadcasted_iota(jnp.int32, (1, MLA_HEADS * HEAD_PAD), 1)
    ones_col = (lane % HEAD_PAD == V_HEAD).astype(F32)
    v_out[...] = (_bdot(cn, wuv_ref[...]) + ones_col).astype(v_out.dtype)

    cb, sb = cosb_ref[...], sinb_ref[...]
    ct, st = cost_ref[...], sint_ref[...]
    cos = cb * ct - sb * st
    sin = sb * ct + cb * st
    half = QK_ROPE // 2
    lane1 = lax.broadcasted_iota(jnp.int32, (1, HEAD_PAD), 1)
    rc = jnp.where(lane1 < QK_HEAD, cos, 0.0)
    ra = jnp.where((lane1 >= QK_NOPE + half) & (lane1 < QK_HEAD), sin, 0.0)
    rb = jnp.where((lane1 >= QK_NOPE) & (lane1 < QK_NOPE + half), -sin, 0.0)

    def rope(z):
        return (z * rc + pltpu.roll(z, half, 1) * ra
                + pltpu.roll(z, HEAD_PAD - half, 1) * rb)

    kr = rope(krp)
    kr_out[...] = kr[:, QK_NOPE:QK_NOPE + QK_ROPE]
    for h in range(MLA_HEADS):
        sl = slice(h * HEAD_PAD, (h + 1) * HEAD_PAD)
        q_out[:, sl] = (rope(q[:, sl]) * Q_SCALE).astype(q_out.dtype)
        k_out[:, sl] = (kk[:, sl] + kr).astype(k_out.dtype)


def _ab_in(x, w, rope_tabs):
    n, dm = x.shape
    tm = min(ROW_TILE, n)
    cosb, sinb, cost, sint = rope_tabs
    consts = [w['w_in'], w['q_g'], w['kv_g'], w['w_uq'], w['w_uk'], w['w_uv'], cosb, sinb]
    tile_spec = pl.BlockSpec((None, 1, HEAD_PAD), lambda i: (i, 0, 0))
    hp = MLA_HEADS * HEAD_PAD
    out_shape = [
        jax.ShapeDtypeStruct((n, hp), BF16),
        jax.ShapeDtypeStruct((n, hp), BF16),
        jax.ShapeDtypeStruct((n, hp), BF16),
        jax.ShapeDtypeStruct((n, KV_LORA), F32),
        jax.ShapeDtypeStruct((n, QK_ROPE), F32),
        jax.ShapeDtypeStruct((n, LRU_WIDTH), F32),
        jax.ShapeDtypeStruct((n, LRU_WIDTH), F32),
    ]
    return pl.pallas_call(
        functools.partial(_ab_in_kernel, dm=dm), name="ab_in",
        grid=(n // tm,),
        in_specs=([_row_spec(tm, dm)] + [_full_spec(c) for c in consts]
                  + [tile_spec, tile_spec]),
        out_specs=[_row_spec(tm, s.shape[1]) for s in out_shape],
        out_shape=out_shape,
        compiler_params=_cparams(("parallel",)),
    )(x, *consts, cost, sint)


def _attn_prompt_kernel(q_ref, k_ref, v_ref, o_ref):
    i = pl.program_id(1)
    tq, tk = ATTN_TQ, ATTN_TK
    ndiag = tq // tk
    nfull = i * ndiag
    row_chunk = lax.broadcasted_iota(jnp.int32, (tq, tk), 0) // CHUNK
    col_chunk = lax.broadcasted_iota(jnp.int32, (tq, tk), 1) // CHUNK
    qs = [q_ref[:, hh * HEAD_PAD:(hh + 1) * HEAD_PAD] for hh in range(2)]

    def update(hh, j, m, acc, mask, row0=0):
        start = pl.multiple_of(j * tk, tk)
        k = k_ref[pl.ds(start, tk), hh * HEAD_PAD:(hh + 1) * HEAD_PAD]
        v = v_ref[pl.ds(start, tk), hh * HEAD_PAD:(hh + 1) * HEAD_PAD]
        s = lax.dot_general(qs[hh][row0:], k, (((1,), (1,)), ((), ())),
                            preferred_element_type=F32)
        if mask is not None:
            s = jnp.where(mask[row0:], s, NEG_INF)
        m_new = jnp.maximum(m, jnp.max(s, axis=-1, keepdims=True))
        alpha = jnp.exp2(m - m_new)
        p = jnp.exp2((s - m_new).astype(BF16))
        return m_new, alpha * acc + jnp.dot(p, v, preferred_element_type=F32)

    def body(j, carry):
        new = []
        for hh in range(2):
            new += update(hh, j, carry[2 * hh], carry[2 * hh + 1], None)
        return tuple(new)

    unroll = ATTN_UNROLL

    def body_unrolled(jj, carry):
        for b in range(unroll):
            carry = body(unroll * jj + b, carry)
        return carry

    init = []
    for _ in range(2):
        init += [jnp.full((tq, 1), NEG_INF, F32), jnp.zeros((tq, HEAD_PAD), F32)]
    carry = lax.fori_loop(0, nfull // unroll, body_unrolled, tuple(init))
    rem0 = nfull - nfull % unroll

    def body_rem(jj, carry):
        for b in range(ndiag):
            carry = body(rem0 + ndiag * jj + b, carry)
        return carry

    carry = list(lax.fori_loop(0, (nfull % unroll) // ndiag, body_rem, carry))
    for d in range(ndiag):
        visible = col_chunk + d * (tk // CHUNK) <= row_chunk
        r0 = d * tk
        for hh in range(2):
            m, acc = carry[2 * hh], carry[2 * hh + 1]
            m_new, acc_new = update(hh, nfull + d, m[r0:], acc[r0:], visible, r0)
            if r0:
                m_new = jnp.concatenate([m[:r0], m_new], axis=0)
                acc_new = jnp.concatenate([acc[:r0], acc_new], axis=0)
            carry[2 * hh], carry[2 * hh + 1] = m_new, acc_new
    outs = [carry[2 * hh + 1][:, :V_HEAD] / carry[2 * hh + 1][:, V_HEAD:V_HEAD + 1]
            for hh in range(2)]
    o_ref[...] = jnp.concatenate(outs, axis=-1).astype(o_ref.dtype)


def _attn_prompt(q, k, v):
    t = q.shape[0]
    pairs = MLA_HEADS // 2
    return pl.pallas_call(
        _attn_prompt_kernel, name="attn_prompt",
        grid=(pairs, t // ATTN_TQ),
        in_specs=[
            pl.BlockSpec((ATTN_TQ, 2 * HEAD_PAD), lambda p, i: (i, p)),
            pl.BlockSpec((t, 2 * HEAD_PAD), lambda p, i: (0, p)),
            pl.BlockSpec((t, 2 * HEAD_PAD), lambda p, i: (0, p)),
        ],
        out_specs=pl.BlockSpec((ATTN_TQ, 2 * V_HEAD), lambda p, i: (i, p)),
        out_shape=jax.ShapeDtypeStruct((t, MLA_HEADS * V_HEAD), BF16),
        compiler_params=_cparams(("parallel", "parallel")),
    )(q, k, v)


def _attn_sample_kernel(q_ref, cn_ref, krn_ref, cp_ref, krp_ref, wukt_ref, wuv_ref, o_ref):
    cp = cp_ref[...].astype(BF16)
    krp = krp_ref[...].astype(BF16)
    cn = cn_ref[...].astype(BF16)
    krn = krn_ref[...].astype(BF16)
    ts = cn.shape[0]
    dn = (((1,), (1,)), ((), ()))
    outs = []
    for hp in range(MLA_HEADS // 2):
        qa, qr = [], []
        for hh in range(2):
            h = 2 * hp + hh
            qh = q_ref[:, h * HEAD_PAD:(h + 1) * HEAD_PAD]
            qa.append(jnp.dot(qh[:, :QK_NOPE], wukt_ref[h], preferred_element_type=F32))
            qr.append(qh[:, QK_NOPE:QK_NOPE + QK_ROPE])
        qa = jnp.concatenate(qa, axis=0).astype(BF16)
        qr = jnp.concatenate(qr, axis=0)
        s_past = (lax.dot_general(qa, cp, dn, preferred_element_type=F32)
                  + lax.dot_general(qr, krp, dn, preferred_element_type=F32))
        s_new = (lax.dot_general(qa, cn, dn, preferred_element_type=F32)
                 + lax.dot_general(qr, krn, dn, preferred_element_type=F32))
        m = jnp.maximum(jnp.max(s_past, axis=-1, keepdims=True),
                        jnp.max(s_new, axis=-1, keepdims=True))
        p_past = jnp.exp2(s_past - m)
        p_new = jnp.exp2(s_new - m)
        l = (jnp.sum(p_past, axis=-1, keepdims=True)
             + jnp.sum(p_new, axis=-1, keepdims=True))
        o_lat = (jnp.dot(p_past.astype(BF16), cp, preferred_element_type=F32)
                 + jnp.dot(p_new.astype(BF16), cn, preferred_element_type=F32)) / l
        for hh in range(2):
            h = 2 * hp + hh
            outs.append(_bdot(o_lat[hh * ts:(hh + 1) * ts], wuv_ref[h]))
    o_ref[...] = jnp.concatenate(outs, axis=-1).astype(o_ref.dtype)


def _attn_sample(q, c_new, kr_new, c_past, kr_past, wukt, wuv):
    bs, past, _ = c_past.shape
    ts = q.shape[0] // bs
    return pl.pallas_call(
        _attn_sample_kernel, name="attn_sample",
        grid=(bs,),
        in_specs=[
            pl.BlockSpec((ts, MLA_HEADS * HEAD_PAD), lambda b: (b, 0)),
            pl.BlockSpec((ts, KV_LORA), lambda b: (b, 0)),
            pl.BlockSpec((ts, QK_ROPE), lambda b: (b, 0)),
            pl.BlockSpec((None, past, KV_LORA), lambda b: (b, 0, 0)),
            pl.BlockSpec((None, past, QK_ROPE), lambda b: (b, 0, 0)),
            _full_spec(wukt), _full_spec(wuv),
        ],
        out_specs=pl.BlockSpec((ts, MLA_HEADS * V_HEAD), lambda b: (b, 0)),
        out_shape=jax.ShapeDtypeStruct((bs * ts, MLA_HEADS * V_HEAD), BF16),
        compiler_params=_cparams(("parallel",)),
    )(q, c_new, kr_new, c_past, kr_past, wukt, wuv)


def _lru_kernel(xb_ref, yb_ref, conv0_ref, h0_ref, cw_ref, cb_ref, wax_ref, bax_ref, ap_ref,
                rec_ref, conv_out_ref, h_out_ref, xcat, hc, *, tb):
    t = pl.program_id(1)
    nt = pl.num_programs(1)
    tail = CONV_WIDTH - 1
    base = SUBLANES

    @pl.when(t == 0)
    def _():
        xcat[base - tail:base, :] = conv0_ref[...]
        hc[...] = h0_ref[...]

    @pl.when(t > 0)
    def _():
        xcat[0:base, :] = xcat[tb:tb + base, :]

    xcat[base:base + tb, :] = xb_ref[...]
    xc = cb_ref[...]
    for tap in range(CONV_WIDTH):
        xc = xc + xcat[base - tail + tap:base - tail + tap + tb, :] * cw_ref[tap:tap + 1, :]

    gates = _bdot(xc, wax_ref[...]) + bax_ref[...]
    r = jax.nn.sigmoid(gates[:, :LRU_WIDTH])
    ig = jax.nn.sigmoid(gates[:, LRU_WIDTH:])
    log_a = -RG_C * r * ap_ref[...]
    a = jnp.exp(log_a)
    u = jnp.sqrt(jnp.tanh(-log_a) * (1.0 + a * a)) * (ig * xc)

    row8 = lax.broadcasted_iota(jnp.int32, (tb, LRU_WIDTH), 0) % SUBLANES
    s = 1
    while s < SUBLANES:
        keep = row8 >= s
        u = jnp.where(keep, a * pltpu.roll(u, s, 0) + u, u)
        a = jnp.where(keep, a * pltpu.roll(a, s, 0), a)
        s *= 2
    hprev = hc[...]
    groups = []
    for g in range(tb // SUBLANES):
        sl = slice(g * SUBLANES, (g + 1) * SUBLANES)
        hg = a[sl] * hprev + u[sl]
        groups.append(hg)
        hprev = hg[SUBLANES - 1:SUBLANES, :]
    h = jnp.concatenate(groups, axis=0)
    hc[...] = hprev
    rec_ref[...] = (h * jax.nn.gelu(yb_ref[...])).astype(rec_ref.dtype)

    @pl.when(t == nt - 1)
    def _():
        conv_out_ref[...] = xcat[base + tb - tail:base + tb, :]
        h_out_ref[...] = h[tb - 1:tb, :]


def _lru(xb, yb, conv0, h0, w):
    b, t, wd = xb.shape
    tb = min(LRU_TILE, t)
    tail = CONV_WIDTH - 1
    consts = [w['conv_w'], w['conv_b'], w['w_ax'], w['b_ax'], w['a_param']]
    seq = pl.BlockSpec((None, tb, wd), lambda bi, ti: (bi, ti, 0))
    return pl.pallas_call(
        functools.partial(_lru_kernel, tb=tb), name="rg_lru",
        grid=(b, t // tb),
        in_specs=[seq, seq,
                  pl.BlockSpec((None, tail, wd), lambda bi, ti: (bi, 0, 0)),
                  pl.BlockSpec((None, 1, wd), lambda bi, ti: (bi, 0, 0))]
                 + [pl.BlockSpec(c.shape, lambda bi, ti: (0, 0)) for c in consts],
        out_specs=[seq,
                   pl.BlockSpec((None, tail, wd), lambda bi, ti: (bi, 0, 0)),
                   pl.BlockSpec((None, 1, wd), lambda bi, ti: (bi, 0, 0))],
        out_shape=[jax.ShapeDtypeStruct((b, t, wd), BF16),
                   jax.ShapeDtypeStruct((b, tail, wd), F32),
                   jax.ShapeDtypeStruct((b, 1, wd), F32)],
        scratch_shapes=[pltpu.VMEM((tb + 2 * SUBLANES, wd), F32), pltpu.VMEM((1, wd), F32)],
        compiler_params=_cparams(("parallel", "arbitrary")),
    )(xb, yb, conv0, h0, *consts)


def _out_ln_kernel(a_ref, b_ref, x_ref, wa_ref, wb_ref, g_ref, beta_ref, o_ref, *, alpha):
    y = (jnp.dot(a_ref[...], wa_ref[...], preferred_element_type=F32)
         + jnp.dot(b_ref[...], wb_ref[...], preferred_element_type=F32))
    o_ref[...] = _layer_norm(alpha * x_ref[...] + y, g_ref[...], beta_ref[...])


def _out_ln(att, rec, x, wa, wb, g, beta, alpha):
    n, dm = x.shape
    tm = min(ROW_TILE, n)
    consts = [wa, wb, g, beta]
    return pl.pallas_call(
        functools.partial(_out_ln_kernel, alpha=alpha), name="out_ln",
        grid=(n // tm,),
        in_specs=[_row_spec(tm, att.shape[1]), _row_spec(tm, rec.shape[1]), _row_spec(tm, dm)]
                 + [_full_spec(c) for c in consts],
        out_specs=_row_spec(tm, dm),
        out_shape=jax.ShapeDtypeStruct((n, dm), F32),
        compiler_params=_cparams(("parallel",)),
    )(att, rec, x, *consts)


def _moe_ln_kernel(x_ref, wr_ref, wg_ref, wu_ref, wd_ref, g_ref, beta_ref, o_ref,
                   acc, gates, xb16, *, alpha):
    grp = pl.program_id(1)
    tm = x_ref.shape[0]
    lane = lax.broadcasted_iota(jnp.int32, (tm, ROUTER_W), 1).astype(F32)

    @pl.when(grp == 0)
    def _():
        x = x_ref[...]
        xh = x.astype(BF16)
        xb16[...] = xh
        xl = (x - xh.astype(F32)).astype(BF16)
        hh_hl = jnp.dot(xh, wr_ref[...], preferred_element_type=F32)
        logits = (hh_hl[:, :ROUTER_W] + hh_hl[:, ROUTER_W:]
                  + jnp.dot(xl, wr_ref[:, :ROUTER_W], preferred_element_type=F32))
        big = float(ROUTER_W)
        lg = jnp.where(lane < N_GROUPS, logits, -jnp.inf)
        mg = jnp.max(lg, axis=-1, keepdims=True)
        gi = jnp.min(jnp.where(lg == mg, lane, big), axis=-1, keepdims=True)
        p_top = 1.0 / jnp.sum(jnp.exp(lg - mg), axis=-1, keepdims=True)
        lo = N_GROUPS + gi * EXPERTS_PER_GROUP
        le = jnp.where((lane >= lo) & (lane < lo + EXPERTS_PER_GROUP), logits, -jnp.inf)
        v1 = jnp.max(le, axis=-1, keepdims=True)
        i1 = jnp.min(jnp.where(le == v1, lane, big), axis=-1, keepdims=True)
        le2 = jnp.where(lane == i1, -jnp.inf, le)
        v2 = jnp.max(le2, axis=-1, keepdims=True)
        i2 = jnp.min(jnp.where(le2 == v2, lane, big), axis=-1, keepdims=True)
        e2 = jnp.exp(v2 - v1)
        w1 = p_top / (1.0 + e2)
        w2 = p_top * e2 / (1.0 + e2)
        gates[...] = jnp.where(lane == i1, w1, 0.0) + jnp.where(lane == i2, w2, 0.0)
        acc[...] = jnp.zeros_like(acc)

    xb = xb16[...]
    gt = gates[...]
    first = (N_GROUPS + grp * EXPERTS_PER_GROUP).astype(F32)
    for el in range(EXPERTS_PER_GROUP):
        ge = jnp.sum(jnp.where(lane == first + el, gt, 0.0), axis=-1, keepdims=True)
        h = jnp.dot(xb, wg_ref[el], preferred_element_type=F32)
        u = jnp.dot(xb, wu_ref[el], preferred_element_type=F32)
        act = (jax.nn.silu(h) * u * ge).astype(BF16)
        acc[...] += jnp.dot(act, wd_ref[el], preferred_element_type=F32)

    @pl.when(grp == pl.num_programs(1) - 1)
    def _():
        o_ref[...] = _layer_norm(alpha * x_ref[...] + acc[...], g_ref[...], beta_ref[...])


def _moe_ln(x, w, layer, g, beta, alpha):
    n, dm = x.shape
    tm = min(MOE_ROW_TILE, n)
    _, ne, _, eh = w['w_gate'].shape
    epg = EXPERTS_PER_GROUP
    return pl.pallas_call(
        functools.partial(_moe_ln_kernel, alpha=alpha), name="moe_ln",
        grid=(n // tm, ne // epg),
        in_specs=[
            pl.BlockSpec((tm, dm), lambda i, e: (i, 0)),
            pl.BlockSpec(w['w_router'].shape, lambda i, e: (0, 0)),
            pl.BlockSpec((None, epg, dm, eh), lambda i, e: (layer, e, 0, 0)),
            pl.BlockSpec((None, epg, dm, eh), lambda i, e: (layer, e, 0, 0)),
            pl.BlockSpec((None, epg, eh, dm), lambda i, e: (layer, e, 0, 0)),
            pl.BlockSpec(g.shape, lambda i, e: (0, 0)),
            pl.BlockSpec(beta.shape, lambda i, e: (0, 0)),
        ],
        out_specs=pl.BlockSpec((tm, dm), lambda i, e: (i, 0)),
        out_shape=jax.ShapeDtypeStruct((n, dm), F32),
        scratch_shapes=[pltpu.VMEM((tm, dm), F32), pltpu.VMEM((tm, ROUTER_W), F32),
                        pltpu.VMEM((tm, dm), BF16)],
        compiler_params=_cparams(("parallel", "arbitrary"), MOE_VMEM_LIMIT_BYTES),
    )(x, w['w_router'], w['w_gate'], w['w_up'], w['w_down'], g, beta)


def _c_in_kernel(x_ref, w_ref, u_ref, u2_ref, us):
    u = _bdot(x_ref[...], w_ref[...])
    u_ref[...] = u
    rows = u2_ref.shape[0]
    for c in range(S5_WIDTH // LANES):
        us[c] = u[:, c * LANES:(c + 1) * LANES]
    for s in range(S5_L):
        for c in range(S5_WIDTH // LANES):
            lo = s * S5_WIDTH + c * LANES
            u2_ref[:, lo:lo + LANES] = us[c, pl.ds(s, rows, stride=S5_L), :].astype(u2_ref.dtype)


def _c_in(x, w_in):
    n, dm = x.shape
    tm = min(ROW_TILE, n)
    wu = w_in.shape[1]
    return pl.pallas_call(
        _c_in_kernel, name="c_in",
        grid=(n // tm,),
        in_specs=[_row_spec(tm, dm), _full_spec(w_in)],
        out_specs=[_row_spec(tm, wu), _row_spec(tm // S5_L, S5_L * wu)],
        out_shape=[jax.ShapeDtypeStruct((n, wu), F32),
                   jax.ShapeDtypeStruct((n // S5_L, S5_L * wu), BF16)],
        scratch_shapes=[pltpu.VMEM((wu // LANES, tm, LANES), F32)],
        compiler_params=_cparams(("parallel",)),
    )(x, w_in)


def _s5_super_rows(u_ref, q):
    tiles = [u_ref[:, s * S5_WIDTH + q * LANES:s * S5_WIDTH + (q + 1) * LANES]
             for s in range(S5_L)]
    return jnp.concatenate(tiles, axis=1).astype(BF16)


def _s5_e_kernel(u_ref, ere_ref, eim_ref, ere_out, eim_out):
    for q in range(S5_NSUPER):
        uq = _s5_super_rows(u_ref, q)
        sl = slice(q * S5_SUPER_STATE, (q + 1) * S5_SUPER_STATE)
        ere_out[:, sl] = jnp.dot(uq, ere_ref[q], preferred_element_type=F32)
        eim_out[:, sl] = jnp.dot(uq, eim_ref[q], preferred_element_type=F32)


def _const_spec(a):
    nd = a.ndim
    return pl.BlockSpec(a.shape, lambda *_: (0,) * nd, pipeline_mode=pl.Buffered(1))


def _s5_e(u2, tabs):
    rows, wu = u2.shape
    tr = min(S5_ROW_TILE, rows)
    out = jax.ShapeDtypeStruct((rows, S5_STATE_W), F32)
    return pl.pallas_call(
        _s5_e_kernel, name="s5_e",
        grid=(rows // tr,),
        in_specs=[_row_spec(tr, wu), _const_spec(tabs['e_re']), _const_spec(tabs['e_im'])],
        out_specs=[_row_spec(tr, S5_STATE_W)] * 2,
        out_shape=[out, out],
        compiler_params=_cparams(("parallel",)),
    )(u2, tabs['e_re'], tabs['e_im'])


def _s5_scan_kernel(ere_ref, eim_ref, h0r_ref, h0i_ref, stp_re_ref, stp_im_ref,
                    apw_re_ref, apw_im_ref, hpr_ref, hpi_ref, hfr_ref, hfi_ref, *, nc, last_row):
    w = ere_ref.shape[-1]
    row = lax.broadcasted_iota(jnp.int32, (SUBLANES, w), 0)
    apr = apw_re_ref[...]
    api = apw_im_ref[...]

    def tile(i, carry):
        cr, ci = carry
        start = pl.multiple_of(i * SUBLANES, SUBLANES)
        sr = ere_ref[pl.ds(start, SUBLANES), :]
        si = eim_ref[pl.ds(start, SUBLANES), :]
        for k in range(3):
            s = 1 << k
            keep = row >= s
            ar = stp_re_ref[k:k + 1, :]
            ai = stp_im_ref[k:k + 1, :]
            pr = pltpu.roll(sr, s, 0)
            pi = pltpu.roll(si, s, 0)
            sr, si = (jnp.where(keep, sr + ar * pr - ai * pi, sr),
                      jnp.where(keep, si + ar * pi + ai * pr, si))
        hr = sr + apr * cr - api * ci
        hi = si + apr * ci + api * cr
        first = row == 0
        hpr_ref[pl.ds(start, SUBLANES), :] = jnp.where(first, cr, pltpu.roll(hr, 1, 0))
        hpi_ref[pl.ds(start, SUBLANES), :] = jnp.where(first, ci, pltpu.roll(hi, 1, 0))
        return hr, hi

    def body(i, carry):
        hr, hi = tile(i, carry)
        return hr[SUBLANES - 1:SUBLANES, :], hi[SUBLANES - 1:SUBLANES, :]

    ntile = nc // SUBLANES
    carry = lax.fori_loop(0, ntile - 1, body, (h0r_ref[...], h0i_ref[...]))
    hr, hi = tile(ntile - 1, carry)
    hfr_ref[...] = hr[last_row:last_row + 1, :]
    hfi_ref[...] = hi[last_row:last_row + 1, :]


def _s5_scan(e_re, e_im, h0r, h0i, tabs, n_chunks):
    b, nc, w = e_re.shape
    last_row = (n_chunks - 1) % SUBLANES
    wb = S5_SCAN_LANES
    consts = [tabs['step_re'], tabs['step_im'], tabs['apow_re'], tabs['apow_im']]
    seq = pl.BlockSpec((None, nc, wb), lambda bi, li: (bi, 0, li))
    vec = pl.BlockSpec((None, 1, wb), lambda bi, li: (bi, 0, li))
    return pl.pallas_call(
        functools.partial(_s5_scan_kernel, nc=nc, last_row=last_row), name="s5_scan",
        grid=(b, w // wb),
        in_specs=[seq, seq, vec, vec]
                 + [pl.BlockSpec((c.shape[0], wb), lambda bi, li: (0, li)) for c in consts],
        out_specs=[seq, seq, vec, vec],
        out_shape=[jax.ShapeDtypeStruct((b, nc, w), F32)] * 2
                  + [jax.ShapeDtypeStruct((b, 1, w), F32)] * 2,
        compiler_params=_cparams(("parallel", "parallel")),
    )(e_re, e_im, h0r, h0i, *consts)


def _s5_y_kernel(u_ref, hpr_ref, hpi_ref, m_ref, fre_ref, fim_ref, y_ref):
    for q in range(S5_NSUPER):
        uq = _s5_super_rows(u_ref, q)
        sl = slice(q * S5_SUPER_STATE, (q + 1) * S5_SUPER_STATE)
        yq = (jnp.dot(uq, m_ref[q], preferred_element_type=F32)
              + _bdot(hpr_ref[:, sl], fre_ref[q])
              + _bdot(hpi_ref[:, sl], fim_ref[q]))
        for t in range(S5_L):
            y_ref[:, t * S5_WIDTH + q * LANES:t * S5_WIDTH + (q + 1) * LANES] = (
                yq[:, t * LANES:(t + 1) * LANES])


def _s5_y(u2, hp_re, hp_im, tabs):
    rows, wu = u2.shape
    tr = min(S5_ROW_TILE, rows)
    consts = [tabs['m'], tabs['f_re'], tabs['f_im']]
    return pl.pallas_call(
        _s5_y_kernel, name="s5_y",
        grid=(rows // tr,),
        in_specs=[_row_spec(tr, wu), _row_spec(tr, S5_STATE_W), _row_spec(tr, S5_STATE_W)]
                 + [_const_spec(c) for c in consts],
        out_specs=_row_spec(tr, wu),
        out_shape=jax.ShapeDtypeStruct((rows, wu), F32),
        compiler_params=_cparams(("parallel",)),
    )(u2, hp_re, hp_im, *consts)


def _c_out_kernel(y2_ref, u_ref, x_ref, d_ref, wglu_ref, wout_ref, g_ref, beta_ref, o_ref, ys,
                  *, alpha):
    rows = y2_ref.shape[0]
    ntile = S5_WIDTH // LANES
    for s in range(S5_L):
        for c in range(ntile):
            lo = s * S5_WIDTH + c * LANES
            ys[c, pl.ds(s, rows, stride=S5_L), :] = y2_ref[:, lo:lo + LANES]
    y = jnp.concatenate([ys[c] for c in range(ntile)], axis=1) + d_ref[...] * u_ref[...]
    vg = _bdot(jax.nn.gelu(y), wglu_ref[...])
    half = vg.shape[1] // 2
    z = vg[:, :half] * jax.nn.sigmoid(vg[:, half:])
    o = _bdot(z, wout_ref[...])
    o_ref[...] = _layer_norm(alpha * x_ref[...] + o, g_ref[...], beta_ref[...])


def _c_out(y2, u, x, w, g, beta, alpha):
    n, dm = x.shape
    tm = min(ROW_TILE, n)
    consts = [w['d'], w['w_glu'], w['w_out'], g, beta]
    return pl.pallas_call(
        functools.partial(_c_out_kernel, alpha=alpha), name="c_out",
        grid=(n // tm,),
        in_specs=[_row_spec(tm // S5_L, y2.shape[1]), _row_spec(tm, u.shape[1]),
                  _row_spec(tm, dm)] + [_full_spec(c) for c in consts],
        out_specs=_row_spec(tm, dm),
        out_shape=jax.ShapeDtypeStruct((n, dm), F32),
        scratch_shapes=[pltpu.VMEM((u.shape[1] // LANES, tm, LANES), F32)],
        compiler_params=_cparams(("parallel",)),
    )(y2, u, x, *consts)


def _prep_ab(w_in, q_g, kv_g, w_uq, w_ukv, conv_w, conv_b, wa, ba, wx, bx, a_param, w_out):
    dm = w_in.shape[0]
    s0 = Q_LORA
    s1 = s0 + KV_LORA
    s2 = s1 + QK_ROPE
    s3 = s2 + LRU_WIDTH
    pad_lo = jnp.zeros((dm, QK_NOPE), F32)
    pad_hi = jnp.zeros((dm, HEAD_PAD - QK_HEAD), F32)
    w_in_p = jnp.concatenate([w_in[:, :s1], w_in[:, s2:s3], w_in[:, s3:],
                              pad_lo, w_in[:, s1:s2], pad_hi], axis=1).astype(BF16)
    wq = w_uq.reshape(Q_LORA, MLA_HEADS, QK_HEAD)
    wq = jnp.pad(wq, ((0, 0), (0, 0), (0, HEAD_PAD - QK_HEAD)))
    wq = wq.reshape(Q_LORA, MLA_HEADS * HEAD_PAD).astype(BF16)
    wkv = w_ukv.reshape(KV_LORA, MLA_HEADS, QK_NOPE + V_HEAD)
    wk = jnp.pad(wkv[:, :, :QK_NOPE], ((0, 0), (0, 0), (0, HEAD_PAD - QK_NOPE)))
    wk = wk.reshape(KV_LORA, MLA_HEADS * HEAD_PAD).astype(BF16)
    wv = jnp.pad(wkv[:, :, QK_NOPE:], ((0, 0), (0, 0), (0, HEAD_PAD - V_HEAD)))
    wv = wv.reshape(KV_LORA, MLA_HEADS * HEAD_PAD).astype(BF16)
    wukt = jnp.transpose(wkv[:, :, :QK_NOPE], (1, 2, 0)).astype(BF16)
    wuv_h = jnp.transpose(wkv[:, :, QK_NOPE:], (1, 0, 2)).astype(BF16)
    eye = jnp.eye(LRU_BLOCKS, dtype=F32)

    def blockdiag(wb):
        return jnp.einsum('hij,hk->hikj', wb, eye).reshape(LRU_WIDTH, LRU_WIDTH)

    w_ax = jnp.concatenate([blockdiag(wa), blockdiag(wx)], axis=1).astype(BF16)
    b_ax = jnp.concatenate([ba, bx])[None, :]
    att_w = MLA_HEADS * V_HEAD
    return dict(w_in=w_in_p, q_g=q_g[None, :], kv_g=kv_g[None, :], w_uq=wq, w_uk=wk, w_uv=wv,
                wukt=wukt, wuv_h=wuv_h, conv_w=conv_w, conv_b=conv_b[None, :], w_ax=w_ax,
                b_ax=b_ax, a_param=jax.nn.softplus(-a_param)[None, :],
                w_out_att=w_out[:att_w].astype(BF16), w_out_rec=w_out[att_w:].astype(BF16))


def _rope_tables(row_pos, tile_pos):
    half = QK_ROPE // 2
    inv = ROPE_THETA ** (-jnp.arange(half, dtype=F32) / half)
    inv_lane = jnp.concatenate([jnp.zeros((QK_NOPE,), F32), inv, inv,
                                jnp.zeros((HEAD_PAD - QK_HEAD,), F32)])
    ang_row = row_pos.astype(F32)[:, None] * inv_lane[None, :]
    ang_tile = tile_pos.astype(F32)[:, None, None] * inv_lane
    return jnp.cos(ang_row), jnp.sin(ang_row), jnp.cos(ang_tile), jnp.sin(ang_tile)


def _prep_router(w_group, w_expert):
    dm = w_group.shape[0]
    pad = jnp.zeros((dm, ROUTER_W - N_GROUPS - N_EXPERTS), F32)
    w_router = jnp.concatenate([w_group, w_expert, pad], axis=1)
    w_router_hi = w_router.astype(BF16)
    w_router_lo = (w_router - w_router_hi.astype(F32)).astype(BF16)
    return dict(w_router=jnp.concatenate([w_router_hi, w_router_lo], axis=1))


def _cmul(ar, ai, br, bi):
    return ar * br - ai * bi, ar * bi + ai * br


def _prep_s5(lam_re, lam_im, log_step, b_re, b_im, c_re, c_im):
    hi = lax.Precision.HIGHEST
    L, G, P, C = S5_L, S5_GROUPS, S5_STATE, S5_GROUP
    lr, li = lam_re, lam_im
    dt = jnp.exp(log_step)[:, None]
    mag = jnp.exp(lr * dt)
    ar, ai = mag * jnp.cos(li * dt), mag * jnp.sin(li * dt)
    den = lr * lr + li * li
    nr = ar - 1.0
    cr = (nr * lr + ai * li) / den
    ci = (ai * lr - nr * li) / den
    bbr = cr[..., None] * b_re - ci[..., None] * b_im
    bbi = cr[..., None] * b_im + ci[..., None] * b_re
    prs, pis = [jnp.ones_like(ar)], [jnp.zeros_like(ar)]
    for _ in range(L):
        nr_, ni_ = _cmul(prs[-1], pis[-1], ar, ai)
        prs.append(nr_)
        pis.append(ni_)
    pr = jnp.stack(prs)
    pi = jnp.stack(pis)
    xr = pr[:L, :, :, None] * bbr - pi[:L, :, :, None] * bbi
    xi = pr[:L, :, :, None] * bbi + pi[:L, :, :, None] * bbr
    kk = (jnp.einsum('gcp,kgpd->kgcd', c_re, xr, precision=hi)
          - jnp.einsum('gcp,kgpd->kgcd', c_im, xi, precision=hi))
    p1r = pr[1:, :, :, None]
    p1i = pi[1:, :, :, None]
    cre_t = jnp.transpose(c_re, (0, 2, 1))[None]
    cim_t = jnp.transpose(c_im, (0, 2, 1))[None]
    f_re = cre_t * p1r - cim_t * p1i
    f_im = -cre_t * p1i - cim_t * p1r

    nq, sg = S5_NSUPER, S5_SUPER
    eye = jnp.eye(sg, dtype=F32)
    dk = jnp.einsum('kqgcd,gh->kqgdhc', kk.reshape(L, nq, sg, C, C), eye)
    dk = dk.reshape(L, nq, LANES, LANES).astype(BF16)
    lag = jnp.arange(L)[None, :] - jnp.arange(L)[:, None]
    m = jnp.where((lag >= 0)[:, :, None, None, None], dk[jnp.clip(lag, 0, L - 1)], 0)
    m = jnp.transpose(m, (2, 0, 3, 1, 4)).reshape(nq, L * LANES, L * LANES)

    row_g = (jnp.arange(L * LANES) // C) % sg
    col_g = jnp.arange(sg * P) // P
    same_group = (row_g[:, None] == col_g[None, :])[None]

    def pack_rows(a):
        a = jnp.transpose(a.reshape(L, nq, sg, P, C), (1, 0, 4, 2, 3))
        a = a.reshape(nq, L, 1, C, sg * P).astype(BF16)
        a = jnp.broadcast_to(a, (nq, L, sg, C, sg * P)).reshape(nq, L * LANES, sg * P)
        return jnp.where(same_group, a, 0)

    def pack_e(x):
        return pack_rows(x[::-1])

    def pack_f(f):
        return jnp.swapaxes(pack_rows(f), 1, 2)

    alr, ali = pr[L].reshape(1, G * P), pi[L].reshape(1, G * P)
    qr, qi = [alr], [ali]
    for _ in range(SUBLANES - 1):
        nr_, ni_ = _cmul(qr[-1], qi[-1], alr, ali)
        qr.append(nr_)
        qi.append(ni_)
    apow_re = jnp.concatenate(qr, axis=0)
    apow_im = jnp.concatenate(qi, axis=0)
    step_re = jnp.concatenate([qr[0], qr[1], qr[3]], axis=0)
    step_im = jnp.concatenate([qi[0], qi[1], qi[3]], axis=0)
    return dict(m=m, e_re=pack_e(xr), e_im=pack_e(xi), f_re=pack_f(f_re),
                f_im=pack_f(f_im), apow_re=apow_re, apow_im=apow_im,
                step_re=step_re, step_im=step_im)


def _mixer_ab(x, pos0, past_ckv, past_krope, conv0, h0, w):
    b, t, dm = x.shape
    xf = x.reshape(b * t, dm)
    tm = min(ROW_TILE, b * t)
    assert tm % t == 0 or t % tm == 0
    row_pos = jnp.arange(tm) % t
    tile_pos = pos0 + (jnp.arange(b * t // tm) * tm) % t
    q, k, v, c_new, kr_new, xb, yb = _ab_in(xf, w, _rope_tables(row_pos, tile_pos))
    if past_ckv is None:
        att = _attn_prompt(q, k, v)
    else:
        att = _attn_sample(q, c_new, kr_new, past_ckv, past_krope, w['wukt'], w['wuv_h'])
    rec, conv_new, h_new = _lru(xb.reshape(b, t, -1), yb.reshape(b, t, -1), conv0,
                                h0[:, None, :], w)
    return (att, rec.reshape(b * t, -1), c_new.reshape(b, t, -1), kr_new.reshape(b, t, -1),
            conv_new, h_new[:, 0, :])


def _mixer_c(x, h0_re, h0_im, w):
    b, t, dm = x.shape
    xf = x.reshape(b * t, dm)
    nc = t // S5_L
    assert t % S5_L == 0 and nc % SUBLANES == 0
    u, u2 = _c_in(xf, w['w_in'])
    e_re, e_im = _s5_e(u2, w['tabs'])
    hp_re, hp_im, hf_re, hf_im = _s5_scan(
        e_re.reshape(b, nc, S5_STATE_W), e_im.reshape(b, nc, S5_STATE_W),
        h0_re.reshape(b, 1, S5_STATE_W), h0_im.reshape(b, 1, S5_STATE_W), w['tabs'], nc)
    y2 = _s5_y(u2, hp_re.reshape(b * nc, S5_STATE_W), hp_im.reshape(b * nc, S5_STATE_W),
               w['tabs'])
    return (y2, u, hf_re.reshape(b, S5_GROUPS, S5_STATE), hf_im.reshape(b, S5_GROUPS, S5_STATE))


def kernel(x_prompt, x_sample, cache_mla_ckv, cache_mla_krope, state_lru_conv, state_lru_h,
           state_s5_re, state_s5_im, w_in_ab, q_norm_g, kv_norm_g, w_uq, w_ukv,
           lru_conv_w, lru_conv_b, lru_w_a, lru_b_a, lru_w_x, lru_b_x, lru_a_param, w_out_ab,
           w_in_c, s5_lam_re, s5_lam_im, s5_log_step, s5_b_re, s5_b_im, s5_c_re, s5_c_im,
           s5_d, s5_w_glu, w_out_c, ln_mix_g, ln_mix_b, ln_ffn_g, ln_ffn_b,
           moe_w_group, moe_w_expert, moe_w_gate, moe_w_up, moe_w_down):
    bp, tp, dm = x_prompt.shape
    bs, ts, _ = x_sample.shape
    past = cache_mla_ckv.shape[2]
    depth = ln_mix_g.shape[0]
    alpha = (2 * depth) ** 0.25
    assert (past + ts - 1) // CHUNK <= past // CHUNK
    hp, hs = x_prompt, x_sample
    moe_experts = dict(w_gate=moe_w_gate.astype(BF16), w_up=moe_w_up.astype(BF16),
                       w_down=moe_w_down.astype(BF16))
    outs = {k: [] for k in ('ckv_p', 'ckv_s', 'kr_p', 'kr_s', 'cv_p', 'cv_s', 'lh_p', 'lh_s',
                            's5r_p', 's5r_s', 's5i_p', 's5i_s')}
    for layer in range(depth):
        j = layer // 2
        g_mix, b_mix = ln_mix_g[layer][None, :], ln_mix_b[layer][None, :]
        g_ffn, b_ffn = ln_ffn_g[layer][None, :], ln_ffn_b[layer][None, :]
        if layer % 2 == 0:
            w = _prep_ab(w_in_ab[j], q_norm_g[j], kv_norm_g[j], w_uq[j], w_ukv[j], lru_conv_w[j],
                         lru_conv_b[j], lru_w_a[j], lru_b_a[j], lru_w_x[j], lru_b_x[j],
                         lru_a_param[j], w_out_ab[j])
            att_p, rec_p, c1, k1, v1, h1 = _mixer_ab(
                hp, 0, None, None, jnp.zeros((bp, CONV_WIDTH - 1, LRU_WIDTH), F32),
                jnp.zeros((bp, LRU_WIDTH), F32), w)
            att_s, rec_s, c2, k2, v2, h2 = _mixer_ab(
                hs, past, cache_mla_ckv[j], cache_mla_krope[j], state_lru_conv[j],
                state_lru_h[j], w)
            outs['ckv_p'].append(c1); outs['ckv_s'].append(c2)
            outs['kr_p'].append(k1); outs['kr_s'].append(k2)
            outs['cv_p'].append(v1); outs['cv_s'].append(v2)
            outs['lh_p'].append(h1); outs['lh_s'].append(h2)
            hp = _out_ln(att_p, rec_p, hp.reshape(bp * tp, dm), w['w_out_att'], w['w_out_rec'],
                         g_mix, b_mix, alpha)
            hs = _out_ln(att_s, rec_s, hs.reshape(bs * ts, dm), w['w_out_att'], w['w_out_rec'],
                         g_mix, b_mix, alpha)
        else:
            w = dict(w_in=w_in_c[j].astype(BF16), d=s5_d[j][None, :],
                     w_glu=s5_w_glu[j].astype(BF16), w_out=w_out_c[j].astype(BF16),
                     tabs=_prep_s5(s5_lam_re[j], s5_lam_im[j], s5_log_step[j], s5_b_re[j],
                                   s5_b_im[j], s5_c_re[j], s5_c_im[j]))
            zero_state = jnp.zeros((bp, S5_GROUPS, S5_STATE), F32)
            y_p, u_p, r1, i1 = _mixer_c(hp, zero_state, zero_state, w)
            y_s, u_s, r2, i2 = _mixer_c(hs, state_s5_re[j], state_s5_im[j], w)
            outs['s5r_p'].append(r1); outs['s5r_s'].append(r2)
            outs['s5i_p'].append(i1); outs['s5i_s'].append(i2)
            hp = _c_out(y_p, u_p, hp.reshape(bp * tp, dm), w, g_mix, b_mix, alpha)
            hs = _c_out(y_s, u_s, hs.reshape(bs * ts, dm), w, g_mix, b_mix, alpha)
        wm = dict(moe_experts, **_prep_router(moe_w_group[layer], moe_w_expert[layer]))
        hp = _moe_ln(hp, wm, layer, g_ffn, b_ffn, alpha).reshape(bp, tp, dm)
        hs = _moe_ln(hs, wm, layer, g_ffn, b_ffn, alpha).reshape(bs, ts, dm)
    st = lambda k: jnp.stack(outs[k])
    return (hp, hs, st('ckv_p'), st('ckv_s'), st('kr_p'), st('kr_s'), st('cv_p'), st('cv_s'),
            st('lh_p'), st('lh_s'), st('s5r_p'), st('s5r_s'), st('s5i_p'), st('s5i_s'))
```

```python
import functools
import math

import jax
import jax.numpy as jnp
from jax import lax
from jax.experimental import pallas as pl
from jax.experimental.pallas import tpu as pltpu
from jax.experimental.pallas import tpu_sc as plsc

F32 = jnp.float32
BF16 = jnp.bfloat16

CHUNK = 64
MLA_HEADS = 8
QK_NOPE = 64
QK_ROPE = 32
QK_HEAD = QK_NOPE + QK_ROPE
V_HEAD = 64
Q_LORA = 256
KV_LORA = 128
ROPE_THETA = 10000.0
MLA_SCALE = QK_HEAD ** -0.5
NEG_INF = -1e30
LRU_WIDTH = 512
LRU_BLOCKS = 8
LRU_BLOCK = LRU_WIDTH // LRU_BLOCKS
CONV_WIDTH = 4
RG_C = 8.0
S5_WIDTH = 512
S5_GROUP = 16
S5_GROUPS = S5_WIDTH // S5_GROUP
S5_STATE = 64
N_GROUPS = 4
EXPERTS_PER_GROUP = 4
N_EXPERTS = N_GROUPS * EXPERTS_PER_GROUP
EXPERT_HIDDEN = 256
LN_EPS = 1e-5
RMS_EPS = 1e-6

LANES = 128
SUBLANES = 8
VMEM_LIMIT_BYTES = 48 * 1024 * 1024
MOE_VMEM_LIMIT_BYTES = 56 * 1024 * 1024

HEAD_PAD = LANES
ROW_TILE = 512
MOE_ROW_TILE = 1024
MOE_EXPERT_TILE = 512
ATTN_TQ = 1024
ATTN_TK = 512
ATTN_UNROLL = 4
Q_SCALE = MLA_SCALE * math.log2(math.e)
LRU_TILE = 512
S5_L = 8
S5_SUPER = LANES // S5_GROUP
S5_NSUPER = S5_GROUPS // S5_SUPER
S5_SUPER_IN = S5_L * LANES
S5_SUPER_STATE = S5_SUPER * S5_STATE
S5_STATE_W = S5_GROUPS * S5_STATE
S5_SCAN_LANES = 256
S5_ROW_TILE = 256
ROUTER_W = LANES


def _cparams(semantics, vmem_limit_bytes=VMEM_LIMIT_BYTES):
    return pltpu.CompilerParams(dimension_semantics=semantics,
                                vmem_limit_bytes=vmem_limit_bytes)


def _full_spec(a):
    nd = a.ndim
    return pl.BlockSpec(a.shape, lambda *_: (0,) * nd)


def _row_spec(tm, width):
    return pl.BlockSpec((tm, width), lambda i: (i, 0))


def _layer_norm(z, g, b):
    mu = jnp.mean(z, axis=-1, keepdims=True)
    zc = z - mu
    var = jnp.mean(zc * zc, axis=-1, keepdims=True)
    return zc * lax.rsqrt(var + LN_EPS) * g + b


def _rms_norm(z, g):
    ms = jnp.mean(z * z, axis=-1, keepdims=True)
    return z * lax.rsqrt(ms + RMS_EPS) * g


def _bdot(a, b):
    return jnp.dot(a.astype(BF16), b.astype(BF16), preferred_element_type=F32)


def _ab_in_kernel(x_ref, win_ref, qg_ref, kvg_ref, wuq_ref, wuk_ref, wuv_ref,
                  cosb_ref, sinb_ref, cost_ref, sint_ref,
                  q_out, k_out, v_out, c_out, kr_out, xb_out, yb_out, *, dm):
    proj = _bdot(x_ref[...], win_ref[...])
    o0 = Q_LORA
    o1 = o0 + KV_LORA
    o2 = o1 + LRU_WIDTH
    o3 = o2 + LRU_WIDTH
    q_lat = proj[:, 0:o0]
    kv_lat = proj[:, o0:o1]
    xb_out[...] = proj[:, o1:o2]
    yb_out[...] = proj[:, o2:o3]
    krp = proj[:, o3:o3 + HEAD_PAD]

    qn = _rms_norm(q_lat, qg_ref[...])
    cn = _rms_norm(kv_lat, kvg_ref[...])
    c_out[...] = cn
    q = _bdot(qn, wuq_ref[...])
    kk = _bdot(cn, wuk_ref[...])
    lane = lax.broadcasted_iota(jnp.int32, (1, MLA_HEADS * HEAD_PAD), 1)
    ones_col = (lane % HEAD_PAD == V_HEAD).astype(F32)
    v_out[...] = (_bdot(cn, wuv_ref[...]) + ones_col).astype(v_out.dtype)

    cb, sb = cosb_ref[...], sinb_ref[...]
    ct, st = cost_ref[...], sint_ref[...]
    cos = cb * ct - sb * st
    sin = sb * ct + cb * st
    half = QK_ROPE // 2
    lane1 = lax.broadcasted_iota(jnp.int32, (1, HEAD_PAD), 1)
    rc = jnp.where(lane1 < QK_HEAD, cos, 0.0)
    ra = jnp.where((lane1 >= QK_NOPE + half) & (lane1 < QK_HEAD), sin, 0.0)
    rb = jnp.where((lane1 >= QK_NOPE) & (lane1 < QK_NOPE + half), -sin, 0.0)

    def rope(z):
        return (z * rc + pltpu.roll(z, half, 1) * ra
                + pltpu.roll(z, HEAD_PAD - half, 1) * rb)

    kr = rope(krp)
    kr_out[...] = kr[:, QK_NOPE:QK_NOPE + QK_ROPE]
    for h in range(MLA_HEADS):
        sl = slice(h * HEAD_PAD, (h + 1) * HEAD_PAD)
        q_out[:, sl] = (rope(q[:, sl]) * Q_SCALE).astype(q_out.dtype)
        k_out[:, sl] = (kk[:, sl] + kr).astype(k_out.dtype)


def _ab_in(x, w, rope_tabs):
    n, dm = x.shape
    tm = min(ROW_TILE, n)
    cosb, sinb, cost, sint = rope_tabs
    consts = [w['w_in'], w['q_g'], w['kv_g'], w['w_uq'], w['w_uk'], w['w_uv'], cosb, sinb]
    tile_spec = pl.BlockSpec((None, 1, HEAD_PAD), lambda i: (i, 0, 0))
    hp = MLA_HEADS * HEAD_PAD
    out_shape = [
        jax.ShapeDtypeStruct((n, hp), BF16),
        jax.ShapeDtypeStruct((n, hp), BF16),
        jax.ShapeDtypeStruct((n, hp), BF16),
        jax.ShapeDtypeStruct((n, KV_LORA), F32),
        jax.ShapeDtypeStruct((n, QK_ROPE), F32),
        jax.ShapeDtypeStruct((n, LRU_WIDTH), F32),
        jax.ShapeDtypeStruct((n, LRU_WIDTH), F32),
    ]
    return pl.pallas_call(
        functools.partial(_ab_in_kernel, dm=dm), name="ab_in",
        grid=(n // tm,),
        in_specs=([_row_spec(tm, dm)] + [_full_spec(c) for c in consts]
                  + [tile_spec, tile_spec]),
        out_specs=[_row_spec(tm, s.shape[1]) for s in out_shape],
        out_shape=out_shape,
        compiler_params=_cparams(("parallel",)),
    )(x, *consts, cost, sint)


def _attn_prompt_kernel(q_ref, k_ref, v_ref, o_ref):
    i = pl.program_id(1)
    tq, tk = ATTN_TQ, ATTN_TK
    ndiag = tq // tk
    nfull = i * ndiag
    row_chunk = lax.broadcasted_iota(jnp.int32, (tq, tk), 0) // CHUNK
    col_chunk = lax.broadcasted_iota(jnp.int32, (tq, tk), 1) // CHUNK
    qs = [q_ref[:, hh * HEAD_PAD:(hh + 1) * HEAD_PAD] for hh in range(2)]

    def update(hh, j, m, acc, mask, row0=0):
        start = pl.multiple_of(j * tk, tk)
        k = k_ref[pl.ds(start, tk), hh * HEAD_PAD:(hh + 1) * HEAD_PAD]
        v = v_ref[pl.ds(start, tk), hh * HEAD_PAD:(hh + 1) * HEAD_PAD]
        s = lax.dot_general(qs[hh][row0:], k, (((1,), (1,)), ((), ())),
                            preferred_element_type=F32)
        if mask is not None:
            s = jnp.where(mask[row0:], s, NEG_INF)
        m_new = jnp.maximum(m, jnp.max(s, axis=-1, keepdims=True))
        alpha = jnp.exp2(m - m_new)
        p = jnp.exp2((s - m_new).astype(BF16))
        return m_new, alpha * acc + jnp.dot(p, v, preferred_element_type=F32)

    def body(j, carry):
        new = []
        for hh in range(2):
            new += update(hh, j, carry[2 * hh], carry[2 * hh + 1], None)
        return tuple(new)

    unroll = ATTN_UNROLL

    def body_unrolled(jj, carry):
        for b in range(unroll):
            carry = body(unroll * jj + b, carry)
        return carry

    init = []
    for _ in range(2):
        init += [jnp.full((tq, 1), NEG_INF, F32), jnp.zeros((tq, HEAD_PAD), F32)]
    carry = lax.fori_loop(0, nfull // unroll, body_unrolled, tuple(init))
    rem0 = nfull - nfull % unroll

    def body_rem(jj, carry):
        for b in range(ndiag):
            carry = body(rem0 + ndiag * jj + b, carry)
        return carry

    carry = list(lax.fori_loop(0, (nfull % unroll) // ndiag, body_rem, carry))
    for d in range(ndiag):
        visible = col_chunk + d * (tk // CHUNK) <= row_chunk
        r0 = d * tk
        for hh in range(2):
            m, acc = carry[2 * hh], carry[2 * hh + 1]
            m_new, acc_new = update(hh, nfull + d, m[r0:], acc[r0:], visible, r0)
            if r0:
                m_new = jnp.concatenate([m[:r0], m_new], axis=0)
                acc_new = jnp.concatenate([acc[:r0], acc_new], axis=0)
            carry[2 * hh], carry[2 * hh + 1] = m_new, acc_new
    outs = [carry[2 * hh + 1][:, :V_HEAD] / carry[2 * hh + 1][:, V_HEAD:V_HEAD + 1]
            for hh in range(2)]
    o_ref[...] = jnp.concatenate(outs, axis=-1).astype(o_ref.dtype)


def _attn_prompt(q, k, v):
    t = q.shape[0]
    pairs = MLA_HEADS // 2
    return pl.pallas_call(
        _attn_prompt_kernel, name="attn_prompt",
        grid=(pairs, t // ATTN_TQ),
        in_specs=[
            pl.BlockSpec((ATTN_TQ, 2 * HEAD_PAD), lambda p, i: (i, p)),
            pl.BlockSpec((t, 2 * HEAD_PAD), lambda p, i: (0, p)),
            pl.BlockSpec((t, 2 * HEAD_PAD), lambda p, i: (0, p)),
        ],
        out_specs=pl.BlockSpec((ATTN_TQ, 2 * V_HEAD), lambda p, i: (i, p)),
        out_shape=jax.ShapeDtypeStruct((t, MLA_HEADS * V_HEAD), BF16),
        compiler_params=_cparams(("parallel", "parallel")),
    )(q, k, v)


def _attn_sample_kernel(q_ref, cn_ref, krn_ref, cp_ref, krp_ref, wukt_ref, wuv_ref, o_ref):
    cp = cp_ref[...].astype(BF16)
    krp = krp_ref[...].astype(BF16)
    cn = cn_ref[...].astype(BF16)
    krn = krn_ref[...].astype(BF16)
    ts = cn.shape[0]
    dn = (((1,), (1,)), ((), ()))
    outs = []
    for hp in range(MLA_HEADS // 2):
        qa, qr = [], []
        for hh in range(2):
            h = 2 * hp + hh
            qh = q_ref[:, h * HEAD_PAD:(h + 1) * HEAD_PAD]
            qa.append(jnp.dot(qh[:, :QK_NOPE], wukt_ref[h], preferred_element_type=F32))
            qr.append(qh[:, QK_NOPE:QK_NOPE + QK_ROPE])
        qa = jnp.concatenate(qa, axis=0).astype(BF16)
        qr = jnp.concatenate(qr, axis=0)
        s_past = (lax.dot_general(qa, cp, dn, preferred_element_type=F32)
                  + lax.dot_general(qr, krp, dn, preferred_element_type=F32))
        s_new = (lax.dot_general(qa, cn, dn, preferred_element_type=F32)
                 + lax.dot_general(qr, krn, dn, preferred_element_type=F32))
        m = jnp.maximum(jnp.max(s_past, axis=-1, keepdims=True),
                        jnp.max(s_new, axis=-1, keepdims=True))
        p_past = jnp.exp2(s_past - m)
        p_new = jnp.exp2(s_new - m)
        l = (jnp.sum(p_past, axis=-1, keepdims=True)
             + jnp.sum(p_new, axis=-1, keepdims=True))
        o_lat = (jnp.dot(p_past.astype(BF16), cp, preferred_element_type=F32)
                 + jnp.dot(p_new.astype(BF16), cn, preferred_element_type=F32)) / l
        for hh in range(2):
            h = 2 * hp + hh
            outs.append(_bdot(o_lat[hh * ts:(hh + 1) * ts], wuv_ref[h]))
    o_ref[...] = jnp.concatenate(outs, axis=-1).astype(o_ref.dtype)


def _attn_sample(q, c_new, kr_new, c_past, kr_past, wukt, wuv):
    bs, past, _ = c_past.shape
    ts = q.shape[0] // bs
    return pl.pallas_call(
        _attn_sample_kernel, name="attn_sample",
        grid=(bs,),
        in_specs=[
            pl.BlockSpec((ts, MLA_HEADS * HEAD_PAD), lambda b: (b, 0)),
            pl.BlockSpec((ts, KV_LORA), lambda b: (b, 0)),
            pl.BlockSpec((ts, QK_ROPE), lambda b: (b, 0)),
            pl.BlockSpec((None, past, KV_LORA), lambda b: (b, 0, 0)),
            pl.BlockSpec((None, past, QK_ROPE), lambda b: (b, 0, 0)),
            _full_spec(wukt), _full_spec(wuv),
        ],
        out_specs=pl.BlockSpec((ts, MLA_HEADS * V_HEAD), lambda b: (b, 0)),
        out_shape=jax.ShapeDtypeStruct((bs * ts, MLA_HEADS * V_HEAD), BF16),
        compiler_params=_cparams(("parallel",)),
    )(q, c_new, kr_new, c_past, kr_past, wukt, wuv)


def _lru_kernel(xb_ref, yb_ref, conv0_ref, h0_ref, cw_ref, cb_ref, wax_ref, bax_ref, ap_ref,
                rec_ref, conv_out_ref, h_out_ref, xcat, hc, *, tb):
    t = pl.program_id(1)
    nt = pl.num_programs(1)
    tail = CONV_WIDTH - 1
    base = SUBLANES

    @pl.when(t == 0)
    def _():
        xcat[base - tail:base, :] = conv0_ref[...]
        hc[...] = h0_ref[...]

    @pl.when(t > 0)
    def _():
        xcat[0:base, :] = xcat[tb:tb + base, :]

    xcat[base:base + tb, :] = xb_ref[...]
    xc = cb_ref[...]
    for tap in range(CONV_WIDTH):
        xc = xc + xcat[base - tail + tap:base - tail + tap + tb, :] * cw_ref[tap:tap + 1, :]

    gates = _bdot(xc, wax_ref[...]) + bax_ref[...]
    r = jax.nn.sigmoid(gates[:, :LRU_WIDTH])
    ig = jax.nn.sigmoid(gates[:, LRU_WIDTH:])
    log_a = -RG_C * r * ap_ref[...]
    a = jnp.exp(log_a)
    u = jnp.sqrt(jnp.tanh(-log_a) * (1.0 + a * a)) * (ig * xc)

    row8 = lax.broadcasted_iota(jnp.int32, (tb, LRU_WIDTH), 0) % SUBLANES
    s = 1
    while s < SUBLANES:
        keep = row8 >= s
        u = jnp.where(keep, a * pltpu.roll(u, s, 0) + u, u)
        a = jnp.where(keep, a * pltpu.roll(a, s, 0), a)
        s *= 2
    hprev = hc[...]
    groups = []
    for g in range(tb // SUBLANES):
        sl = slice(g * SUBLANES, (g + 1) * SUBLANES)
        hg = a[sl] * hprev + u[sl]
        groups.append(hg)
        hprev = hg[SUBLANES - 1:SUBLANES, :]
    h = jnp.concatenate(groups, axis=0)
    hc[...] = hprev
    rec_ref[...] = (h * jax.nn.gelu(yb_ref[...])).astype(rec_ref.dtype)

    @pl.when(t == nt - 1)
    def _():
        conv_out_ref[...] = xcat[base + tb - tail:base + tb, :]
        h_out_ref[...] = h[tb - 1:tb, :]


def _lru(xb, yb, conv0, h0, w):
    b, t, wd = xb.shape
    tb = min(LRU_TILE, t)
    tail = CONV_WIDTH - 1
    consts = [w['conv_w'], w['conv_b'], w['w_ax'], w['b_ax'], w['a_param']]
    seq = pl.BlockSpec((None, tb, wd), lambda bi, ti: (bi, ti, 0))
    return pl.pallas_call(
        functools.partial(_lru_kernel, tb=tb), name="rg_lru",
        grid=(b, t // tb),
        in_specs=[seq, seq,
                  pl.BlockSpec((None, tail, wd), lambda bi, ti: (bi, 0, 0)),
                  pl.BlockSpec((None, 1, wd), lambda bi, ti: (bi, 0, 0))]
                 + [pl.BlockSpec(c.shape, lambda bi, ti: (0, 0)) for c in consts],
        out_specs=[seq,
                   pl.BlockSpec((None, tail, wd), lambda bi, ti: (bi, 0, 0)),
                   pl.BlockSpec((None, 1, wd), lambda bi, ti: (bi, 0, 0))],
        out_shape=[jax.ShapeDtypeStruct((b, t, wd), BF16),
                   jax.ShapeDtypeStruct((b, tail, wd), F32),
                   jax.ShapeDtypeStruct((b, 1, wd), F32)],
        scratch_shapes=[pltpu.VMEM((tb + 2 * SUBLANES, wd), F32), pltpu.VMEM((1, wd), F32)],
        compiler_params=_cparams(("parallel", "arbitrary")),
    )(xb, yb, conv0, h0, *consts)


def _out_ln_kernel(a_ref, b_ref, x_ref, wa_ref, wb_ref, g_ref, beta_ref, o_ref, *, alpha):
    y = (jnp.dot(a_ref[...], wa_ref[...], preferred_element_type=F32)
         + jnp.dot(b_ref[...], wb_ref[...], preferred_element_type=F32))
    o_ref[...] = _layer_norm(alpha * x_ref[...] + y, g_ref[...], beta_ref[...])


def _out_ln(att, rec, x, wa, wb, g, beta, alpha):
    n, dm = x.shape
    tm = min(ROW_TILE, n)
    consts = [wa, wb, g, beta]
    return pl.pallas_call(
        functools.partial(_out_ln_kernel, alpha=alpha), name="out_ln",
        grid=(n // tm,),
        in_specs=[_row_spec(tm, att.shape[1]), _row_spec(tm, rec.shape[1]), _row_spec(tm, dm)]
                 + [_full_spec(c) for c in consts],
        out_specs=_row_spec(tm, dm),
        out_shape=jax.ShapeDtypeStruct((n, dm), F32),
        compiler_params=_cparams(("parallel",)),
    )(att, rec, x, *consts)


def _route(x, xh, wr_ref, lane):
    xl = (x - xh.astype(F32)).astype(BF16)
    hh_hl = jnp.dot(xh, wr_ref[...], preferred_element_type=F32)
    logits = (hh_hl[:, :ROUTER_W] + hh_hl[:, ROUTER_W:]
              + jnp.dot(xl, wr_ref[:, :ROUTER_W], preferred_element_type=F32))
    big = float(ROUTER_W)
    lg = jnp.where(lane < N_GROUPS, logits, -jnp.inf)
    mg = jnp.max(lg, axis=-1, keepdims=True)
    gi = jnp.min(jnp.where(lg == mg, lane, big), axis=-1, keepdims=True)
    p_top = 1.0 / jnp.sum(jnp.exp(lg - mg), axis=-1, keepdims=True)
    lo = N_GROUPS + gi * EXPERTS_PER_GROUP
    le = jnp.where((lane >= lo) & (lane < lo + EXPERTS_PER_GROUP), logits, -jnp.inf)
    v1 = jnp.max(le, axis=-1, keepdims=True)
    i1 = jnp.min(jnp.where(le == v1, lane, big), axis=-1, keepdims=True)
    le2 = jnp.where(lane == i1, -jnp.inf, le)
    v2 = jnp.max(le2, axis=-1, keepdims=True)
    i2 = jnp.min(jnp.where(le2 == v2, lane, big), axis=-1, keepdims=True)
    e2 = jnp.exp(v2 - v1)
    return gi, i1, i2, p_top / (1.0 + e2), p_top * e2 / (1.0 + e2)


def _moe_ln_kernel(x_ref, wr_ref, wg_ref, wu_ref, wd_ref, g_ref, beta_ref, o_ref,
                   acc, gates, xb16, *, alpha):
    grp = pl.program_id(1)
    tm = x_ref.shape[0]
    lane = lax.broadcasted_iota(jnp.int32, (tm, ROUTER_W), 1).astype(F32)

    @pl.when(grp == 0)
    def _():
        x = x_ref[...]
        xh = x.astype(BF16)
        xb16[...] = xh
        _, i1, i2, w1, w2 = _route(x, xh, wr_ref, lane)
        gates[...] = jnp.where(lane == i1, w1, 0.0) + jnp.where(lane == i2, w2, 0.0)
        acc[...] = jnp.zeros_like(acc)

    xb = xb16[...]
    gt = gates[...]
    first = (N_GROUPS + grp * EXPERTS_PER_GROUP).astype(F32)
    for el in range(EXPERTS_PER_GROUP):
        ge = jnp.sum(jnp.where(lane == first + el, gt, 0.0), axis=-1, keepdims=True)
        h = jnp.dot(xb, wg_ref[el], preferred_element_type=F32)
        u = jnp.dot(xb, wu_ref[el], preferred_element_type=F32)
        act = (jax.nn.silu(h) * u * ge).astype(BF16)
        acc[...] += jnp.dot(act, wd_ref[el], preferred_element_type=F32)

    @pl.when(grp == pl.num_programs(1) - 1)
    def _():
        o_ref[...] = _layer_norm(alpha * x_ref[...] + acc[...], g_ref[...], beta_ref[...])


def _moe_ln(x, w, layer, g, beta, alpha):
    n, dm = x.shape
    tm = min(MOE_ROW_TILE, n)
    _, ne, _, eh = w['w_gate'].shape
    epg = EXPERTS_PER_GROUP
    return pl.pallas_call(
        functools.partial(_moe_ln_kernel, alpha=alpha), name="moe_ln",
        grid=(n // tm, ne // epg),
        in_specs=[
            pl.BlockSpec((tm, dm), lambda i, e: (i, 0)),
            pl.BlockSpec(w['w_router'].shape, lambda i, e: (0, 0)),
            pl.BlockSpec((None, epg, dm, eh), lambda i, e: (layer, e, 0, 0)),
            pl.BlockSpec((None, epg, dm, eh), lambda i, e: (layer, e, 0, 0)),
            pl.BlockSpec((None, epg, eh, dm), lambda i, e: (layer, e, 0, 0)),
            pl.BlockSpec(g.shape, lambda i, e: (0, 0)),
            pl.BlockSpec(beta.shape, lambda i, e: (0, 0)),
        ],
        out_specs=pl.BlockSpec((tm, dm), lambda i, e: (i, 0)),
        out_shape=jax.ShapeDtypeStruct((n, dm), F32),
        scratch_shapes=[pltpu.VMEM((tm, dm), F32), pltpu.VMEM((tm, ROUTER_W), F32),
                        pltpu.VMEM((tm, dm), BF16)],
        compiler_params=_cparams(("parallel", "arbitrary"), MOE_VMEM_LIMIT_BYTES),
    )(x, w['w_router'], w['w_gate'], w['w_up'], w['w_down'], g, beta)


META_W = LANES
META_GROUP = EXPERTS_PER_GROUP
META_RANK = EXPERTS_PER_GROUP + 1


def _moe_route_kernel(x_ref, wr_ref, tri_ref, xg_ref, cnt_ref, run):
    tm, dm = x_ref.shape
    lane = lax.broadcasted_iota(jnp.int32, (tm, ROUTER_W), 1).astype(F32)

    @pl.when(pl.program_id(0) == 0)
    def _():
        run[...] = jnp.zeros_like(run)

    x = x_ref[...]
    gi, i1, i2, w1, w2 = _route(x, x.astype(BF16), wr_ref, lane)
    lo = N_GROUPS + gi * EXPERTS_PER_GROUP
    in_group = lane == gi
    incl = jnp.dot(tri_ref[...], in_group.astype(BF16), preferred_element_type=F32)
    rank = jnp.sum(jnp.where(in_group, incl + run[...], 0.0), axis=-1, keepdims=True) - 1.0
    run[...] += incl[tm - 1:tm, :]
    cnt_ref[...] = run[...]
    meta = (jnp.where(lane == i1 - lo, w1, 0.0) + jnp.where(lane == i2 - lo, w2, 0.0)
            + jnp.where(lane == META_GROUP, gi, 0.0) + jnp.where(lane == META_RANK, rank, 0.0))
    xg_ref[:, :dm] = x
    xg_ref[:, dm:] = meta


def _moe_route(x, w_router):
    n, dm = x.shape
    tm = ROW_TILE
    tri = jnp.tril(jnp.ones((tm, tm), BF16))
    return pl.pallas_call(
        _moe_route_kernel, name="moe_route",
        grid=(n // tm,),
        in_specs=[_row_spec(tm, dm), _full_spec(w_router), _full_spec(tri)],
        out_specs=[_row_spec(tm, dm + META_W), pl.BlockSpec((1, ROUTER_W), lambda i: (0, 0))],
        out_shape=[jax.ShapeDtypeStruct((n, dm + META_W), F32),
                   jax.ShapeDtypeStruct((1, ROUTER_W), F32)],
        scratch_shapes=[pltpu.VMEM((1, ROUTER_W), F32)],
        compiler_params=_cparams(("arbitrary",)),
    )(x, w_router, tri)


def _moe_experts_kernel(tg_ref, nu_ref, xs_ref, wg_ref, wu_ref, wd_ref, y_ref):
    del tg_ref
    dm = y_ref.shape[1]

    @pl.when(pl.program_id(0) < nu_ref[0])
    def _():
        xb = xs_ref[:, :dm].astype(BF16)
        meta = xs_ref[:, dm:]
        acc = None
        for el in range(EXPERTS_PER_GROUP):
            h = jnp.dot(xb, wg_ref[el], preferred_element_type=F32)
            u = jnp.dot(xb, wu_ref[el], preferred_element_type=F32)
            act = (jax.nn.silu(h) * u * meta[:, el:el + 1]).astype(BF16)
            d = jnp.dot(act, wd_ref[el], preferred_element_type=F32)
            acc = d if acc is None else acc + d
        y_ref[...] = acc


def _moe_experts(xs, tile_group, n_used, w, layer):
    p, wx = xs.shape
    dm = wx - META_W
    te = MOE_EXPERT_TILE
    _, ne, _, eh = w['w_gate'].shape
    epg = EXPERTS_PER_GROUP

    def tile(i, tg, nu):
        return jnp.minimum(i, nu[0] - 1)

    wspec = lambda shape: pl.BlockSpec(
        (None, epg) + shape, lambda i, tg, nu: (layer, tg[tile(i, tg, nu)], 0, 0))
    grid_spec = pltpu.PrefetchScalarGridSpec(
        num_scalar_prefetch=2,
        grid=(p // te,),
        in_specs=[pl.BlockSpec((te, wx), lambda i, tg, nu: (tile(i, tg, nu), 0)),
                  wspec((dm, eh)), wspec((dm, eh)), wspec((eh, dm))],
        out_specs=pl.BlockSpec((te, dm), lambda i, tg, nu: (tile(i, tg, nu), 0)),
    )
    return pl.pallas_call(
        _moe_experts_kernel, name="moe_experts",
        grid_spec=grid_spec,
        out_shape=jax.ShapeDtypeStruct((p, dm), F32),
        compiler_params=_cparams(("arbitrary",)),
    )(tile_group, n_used, xs, w['w_gate'], w['w_up'], w['w_down'])


def _res_ln_kernel(xg_ref, y_ref, g_ref, beta_ref, o_ref, *, alpha):
    o_ref[...] = _layer_norm(alpha * xg_ref[...] + y_ref[...], g_ref[...], beta_ref[...])


def _res_ln(xg, y, g, beta, alpha):
    n, dm = y.shape
    tm = ROW_TILE
    return pl.pallas_call(
        functools.partial(_res_ln_kernel, alpha=alpha), name="res_ln",
        grid=(n // tm,),
        in_specs=[_row_spec(tm, dm), _row_spec(tm, dm), _full_spec(g), _full_spec(beta)],
        out_specs=_row_spec(tm, dm),
        out_shape=jax.ShapeDtypeStruct((n, dm), F32),
        compiler_params=_cparams(("parallel",)),
    )(xg, y, g, beta)


def _moe(x, w, layer, g, beta, alpha):
    n = x.shape[0]
    granule = 2 * SC_GATHER_ROWS * SC_CORES * SC_SUBCORES
    if n % granule == 0 and (N_GROUPS * MOE_EXPERT_TILE) % granule == 0:
        return _moe_routed_ln(x, w, layer, g, beta, alpha)
    return _moe_ln(x, w, layer, g, beta, alpha)


def _moe_routed_ln(x, w, layer, g, beta, alpha):
    n, dm = x.shape
    te = MOE_EXPERT_TILE
    p = n + N_GROUPS * te
    xg, cnt = _moe_route(x, w['w_router'])
    gid = xg[:, dm + META_GROUP].astype(jnp.int32)
    rank = xg[:, dm + META_RANK].astype(jnp.int32)
    counts = cnt[0, :N_GROUPS].astype(jnp.int32)
    padded = (counts + te - 1) // te * te
    ends = jnp.cumsum(padded)
    pos = (ends - padded)[gid] + rank
    src = jnp.zeros((p,), jnp.int32).at[pos].set(jnp.arange(n, dtype=jnp.int32))
    tile_start = jnp.arange(p // te, dtype=jnp.int32) * te
    tile_group = jnp.minimum(jnp.sum(tile_start[:, None] >= ends[None, :], axis=1),
                             N_GROUPS - 1).astype(jnp.int32)
    n_used = (ends[-1:] // te).astype(jnp.int32)
    xs = _sc_gather_rows(xg, src)
    ys = _moe_experts(xs, tile_group, n_used, w, layer)
    y = _sc_gather_rows(ys, pos)
    return _res_ln(xg, y, g, beta, alpha)


SC_CORES = 2
SC_SUBCORES = 16
SC_GATHER_ROWS = 32


def _sc_gather_rows(table, idx):
    b = idx.shape[0]
    _, d = table.shape
    nw = SC_CORES * SC_SUBCORES
    ch = SC_GATHER_ROWS
    per_w = b // nw
    assert b % (nw * 2 * ch) == 0
    npair = per_w // (2 * ch)
    mesh = plsc.VectorSubcoreMesh(core_axis_name="c", subcore_axis_name="s")

    @functools.partial(
        pl.kernel, mesh=mesh, out_type=jax.ShapeDtypeStruct((b, d), table.dtype),
        scratch_types=[pltpu.VMEM((2, ch), jnp.int32), pltpu.VMEM((2, ch, d), table.dtype),
                       pltpu.SemaphoreType.DMA, pltpu.SemaphoreType.DMA],
        name="sc_gather_rows")
    def gather(table_hbm, idx_hbm, out_hbm, idx_v, rows_v, sem0, sem1):
        sems = (sem0, sem1)
        wid = lax.axis_index("s") * SC_CORES + lax.axis_index("c")
        base = wid * per_w

        def start(chunk, slot):
            off = pl.multiple_of(base + chunk * ch, SUBLANES)
            pltpu.sync_copy(idx_hbm.at[pl.ds(off, ch)], idx_v.at[slot])
            pltpu.async_copy(table_hbm.at[idx_v.at[slot]], rows_v.at[slot], sems[slot])

        def finish(chunk, slot):
            off = pl.multiple_of(base + chunk * ch, SUBLANES)
            pltpu.make_async_copy(table_hbm.at[idx_v.at[slot]], rows_v.at[slot],
                                  sems[slot]).wait()
            pltpu.sync_copy(rows_v.at[slot], out_hbm.at[pl.ds(off, ch)])

        start(0, 0)

        @pl.loop(0, npair)
        def _(p):
            c0 = 2 * p
            start(c0 + 1, 1)
            finish(c0, 0)

            @pl.when(p + 1 < npair)
            def _():
                start(c0 + 2, 0)
            finish(c0 + 1, 1)

    return gather(table, idx)


def _c_in_kernel(x_ref, w_ref, u_ref, u2_ref, us):
    u = _bdot(x_ref[...], w_ref[...])
    u_ref[...] = u
    rows = u2_ref.shape[0]
    for c in range(S5_WIDTH // LANES):
        us[c] = u[:, c * LANES:(c + 1) * LANES]
    for s in range(S5_L):
        for c in range(S5_WIDTH // LANES):
            lo = s * S5_WIDTH + c * LANES
            u2_ref[:, lo:lo + LANES] = us[c, pl.ds(s, rows, stride=S5_L), :].astype(u2_ref.dtype)


def _c_in(x, w_in):
    n, dm = x.shape
    tm = min(ROW_TILE, n)
    wu = w_in.shape[1]
    return pl.pallas_call(
        _c_in_kernel, name="c_in",
        grid=(n // tm,),
        in_specs=[_row_spec(tm, dm), _full_spec(w_in)],
        out_specs=[_row_spec(tm, wu), _row_spec(tm // S5_L, S5_L * wu)],
        out_shape=[jax.ShapeDtypeStruct((n, wu), F32),
                   jax.ShapeDtypeStruct((n // S5_L, S5_L * wu), BF16)],
        scratch_shapes=[pltpu.VMEM((wu // LANES, tm, LANES), F32)],
        compiler_params=_cparams(("parallel",)),
    )(x, w_in)


def _s5_super_rows(u_ref, q):
    tiles = [u_ref[:, s * S5_WIDTH + q * LANES:s * S5_WIDTH + (q + 1) * LANES]
             for s in range(S5_L)]
    return jnp.concatenate(tiles, axis=1).astype(BF16)


def _s5_e_kernel(u_ref, ere_ref, eim_ref, ere_out, eim_out):
    for q in range(S5_NSUPER):
        uq = _s5_super_rows(u_ref, q)
        sl = slice(q * S5_SUPER_STATE, (q + 1) * S5_SUPER_STATE)
        ere_out[:, sl] = jnp.dot(uq, ere_ref[q], preferred_element_type=F32)
        eim_out[:, sl] = jnp.dot(uq, eim_ref[q], preferred_element_type=F32)


def _const_spec(a):
    nd = a.ndim
    return pl.BlockSpec(a.shape, lambda *_: (0,) * nd, pipeline_mode=pl.Buffered(1))


def _s5_e(u2, tabs):
    rows, wu = u2.shape
    tr = min(S5_ROW_TILE, rows)
    out = jax.ShapeDtypeStruct((rows, S5_STATE_W), F32)
    return pl.pallas_call(
        _s5_e_kernel, name="s5_e",
        grid=(rows // tr,),
        in_specs=[_row_spec(tr, wu), _const_spec(tabs['e_re']), _const_spec(tabs['e_im'])],
        out_specs=[_row_spec(tr, S5_STATE_W)] * 2,
        out_shape=[out, out],
        compiler_params=_cparams(("parallel",)),
    )(u2, tabs['e_re'], tabs['e_im'])


def _s5_scan_kernel(ere_ref, eim_ref, h0r_ref, h0i_ref, stp_re_ref, stp_im_ref,
                    apw_re_ref, apw_im_ref, hpr_ref, hpi_ref, hfr_ref, hfi_ref, *, nc, last_row):
    w = ere_ref.shape[-1]
    row = lax.broadcasted_iota(jnp.int32, (SUBLANES, w), 0)
    apr = apw_re_ref[...]
    api = apw_im_ref[...]

    def tile(i, carry):
        cr, ci = carry
        start = pl.multiple_of(i * SUBLANES, SUBLANES)
        sr = ere_ref[pl.ds(start, SUBLANES), :]
        si = eim_ref[pl.ds(start, SUBLANES), :]
        for k in range(3):
            s = 1 << k
            keep = row >= s
            ar = stp_re_ref[k:k + 1, :]
            ai = stp_im_ref[k:k + 1, :]
            pr = pltpu.roll(sr, s, 0)
            pi = pltpu.roll(si, s, 0)
            sr, si = (jnp.where(keep, sr + ar * pr - ai * pi, sr),
                      jnp.where(keep, si + ar * pi + ai * pr, si))
        hr = sr + apr * cr - api * ci
        hi = si + apr * ci + api * cr
        first = row == 0
        hpr_ref[pl.ds(start, SUBLANES), :] = jnp.where(first, cr, pltpu.roll(hr, 1, 0))
        hpi_ref[pl.ds(start, SUBLANES), :] = jnp.where(first, ci, pltpu.roll(hi, 1, 0))
        return hr, hi

    def body(i, carry):
        hr, hi = tile(i, carry)
        return hr[SUBLANES - 1:SUBLANES, :], hi[SUBLANES - 1:SUBLANES, :]

    ntile = nc // SUBLANES
    carry = lax.fori_loop(0, ntile - 1, body, (h0r_ref[...], h0i_ref[...]))
    hr, hi = tile(ntile - 1, carry)
    hfr_ref[...] = hr[last_row:last_row + 1, :]
    hfi_ref[...] = hi[last_row:last_row + 1, :]


def _s5_scan(e_re, e_im, h0r, h0i, tabs, n_chunks):
    b, nc, w = e_re.shape
    last_row = (n_chunks - 1) % SUBLANES
    wb = S5_SCAN_LANES
    consts = [tabs['step_re'], tabs['step_im'], tabs['apow_re'], tabs['apow_im']]
    seq = pl.BlockSpec((None, nc, wb), lambda bi, li: (bi, 0, li))
    vec = pl.BlockSpec((None, 1, wb), lambda bi, li: (bi, 0, li))
    return pl.pallas_call(
        functools.partial(_s5_scan_kernel, nc=nc, last_row=last_row), name="s5_scan",
        grid=(b, w // wb),
        in_specs=[seq, seq, vec, vec]
                 + [pl.BlockSpec((c.shape[0], wb), lambda bi, li: (0, li)) for c in consts],
        out_specs=[seq, seq, vec, vec],
        out_shape=[jax.ShapeDtypeStruct((b, nc, w), F32)] * 2
                  + [jax.ShapeDtypeStruct((b, 1, w), F32)] * 2,
        compiler_params=_cparams(("parallel", "parallel")),
    )(e_re, e_im, h0r, h0i, *consts)


def _s5_y_kernel(u_ref, hpr_ref, hpi_ref, m_ref, fre_ref, fim_ref, y_ref):
    for q in range(S5_NSUPER):
        uq = _s5_super_rows(u_ref, q)
        sl = slice(q * S5_SUPER_STATE, (q + 1) * S5_SUPER_STATE)
        yq = (jnp.dot(uq, m_ref[q], preferred_element_type=F32)
              + _bdot(hpr_ref[:, sl], fre_ref[q])
              + _bdot(hpi_ref[:, sl], fim_ref[q]))
        for t in range(S5_L):
            y_ref[:, t * S5_WIDTH + q * LANES:t * S5_WIDTH + (q + 1) * LANES] = (
                yq[:, t * LANES:(t + 1) * LANES])


def _s5_y(u2, hp_re, hp_im, tabs):
    rows, wu = u2.shape
    tr = min(S5_ROW_TILE, rows)
    consts = [tabs['m'], tabs['f_re'], tabs['f_im']]
    return pl.pallas_call(
        _s5_y_kernel, name="s5_y",
        grid=(rows // tr,),
        in_specs=[_row_spec(tr, wu), _row_spec(tr, S5_STATE_W), _row_spec(tr, S5_STATE_W)]
                 + [_const_spec(c) for c in consts],
        out_specs=_row_spec(tr, wu),
        out_shape=jax.ShapeDtypeStruct((rows, wu), F32),
        compiler_params=_cparams(("parallel",)),
    )(u2, hp_re, hp_im, *consts)


def _c_out_kernel(y2_ref, u_ref, x_ref, d_ref, wglu_ref, wout_ref, g_ref, beta_ref, o_ref, ys,
                  *, alpha):
    rows = y2_ref.shape[0]
    ntile = S5_WIDTH // LANES
    for s in range(S5_L):
        for c in range(ntile):
            lo = s * S5_WIDTH + c * LANES
            ys[c, pl.ds(s, rows, stride=S5_L), :] = y2_ref[:, lo:lo + LANES]
    y = jnp.concatenate([ys[c] for c in range(ntile)], axis=1) + d_ref[...] * u_ref[...]
    vg = _bdot(jax.nn.gelu(y), wglu_ref[...])
    half = vg.shape[1] // 2
    z = vg[:, :half] * jax.nn.sigmoid(vg[:, half:])
    o = _bdot(z, wout_ref[...])
    o_ref[...] = _layer_norm(alpha * x_ref[...] + o, g_ref[...], beta_ref[...])


def _c_out(y2, u, x, w, g, beta, alpha):
    n, dm = x.shape
    tm = min(ROW_TILE, n)
    consts = [w['d'], w['w_glu'], w['w_out'], g, beta]
    return pl.pallas_call(
        functools.partial(_c_out_kernel, alpha=alpha), name="c_out",
        grid=(n // tm,),
        in_specs=[_row_spec(tm // S5_L, y2.shape[1]), _row_spec(tm, u.shape[1]),
                  _row_spec(tm, dm)] + [_full_spec(c) for c in consts],
        out_specs=_row_spec(tm, dm),
        out_shape=jax.ShapeDtypeStruct((n, dm), F32),
        scratch_shapes=[pltpu.VMEM((u.shape[1] // LANES, tm, LANES), F32)],
        compiler_params=_cparams(("parallel",)),
    )(y2, u, x, *consts)


def _prep_ab(w_in, q_g, kv_g, w_uq, w_ukv, conv_w, conv_b, wa, ba, wx, bx, a_param, w_out):
    dm = w_in.shape[0]
    s0 = Q_LORA
    s1 = s0 + KV_LORA
    s2 = s1 + QK_ROPE
    s3 = s2 + LRU_WIDTH
    pad_lo = jnp.zeros((dm, QK_NOPE), F32)
    pad_hi = jnp.zeros((dm, HEAD_PAD - QK_HEAD), F32)
    w_in_p = jnp.concatenate([w_in[:, :s1], w_in[:, s2:s3], w_in[:, s3:],
                              pad_lo, w_in[:, s1:s2], pad_hi], axis=1).astype(BF16)
    wq = w_uq.reshape(Q_LORA, MLA_HEADS, QK_HEAD)
    wq = jnp.pad(wq, ((0, 0), (0, 0), (0, HEAD_PAD - QK_HEAD)))
    wq = wq.reshape(Q_LORA, MLA_HEADS * HEAD_PAD).astype(BF16)
    wkv = w_ukv.reshape(KV_LORA, MLA_HEADS, QK_NOPE + V_HEAD)
    wk = jnp.pad(wkv[:, :, :QK_NOPE], ((0, 0), (0, 0), (0, HEAD_PAD - QK_NOPE)))
    wk = wk.reshape(KV_LORA, MLA_HEADS * HEAD_PAD).astype(BF16)
    wv = jnp.pad(wkv[:, :, QK_NOPE:], ((0, 0), (0, 0), (0, HEAD_PAD - V_HEAD)))
    wv = wv.reshape(KV_LORA, MLA_HEADS * HEAD_PAD).astype(BF16)
    wukt = jnp.transpose(wkv[:, :, :QK_NOPE], (1, 2, 0)).astype(BF16)
    wuv_h = jnp.transpose(wkv[:, :, QK_NOPE:], (1, 0, 2)).astype(BF16)
    eye = jnp.eye(LRU_BLOCKS, dtype=F32)

    def blockdiag(wb):
        return jnp.einsum('hij,hk->hikj', wb, eye).reshape(LRU_WIDTH, LRU_WIDTH)

    w_ax = jnp.concatenate([blockdiag(wa), blockdiag(wx)], axis=1).astype(BF16)
    b_ax = jnp.concatenate([ba, bx])[None, :]
    att_w = MLA_HEADS * V_HEAD
    return dict(w_in=w_in_p, q_g=q_g[None, :], kv_g=kv_g[None, :], w_uq=wq, w_uk=wk, w_uv=wv,
                wukt=wukt, wuv_h=wuv_h, conv_w=conv_w, conv_b=conv_b[None, :], w_ax=w_ax,
                b_ax=b_ax, a_param=jax.nn.softplus(-a_param)[None, :],
                w_out_att=w_out[:att_w].astype(BF16), w_out_rec=w_out[att_w:].astype(BF16))


def _rope_tables(row_pos, tile_pos):
    half = QK_ROPE // 2
    inv = ROPE_THETA ** (-jnp.arange(half, dtype=F32) / half)
    inv_lane = jnp.concatenate([jnp.zeros((QK_NOPE,), F32), inv, inv,
                                jnp.zeros((HEAD_PAD - QK_HEAD,), F32)])
    ang_row = row_pos.astype(F32)[:, None] * inv_lane[None, :]
    ang_tile = tile_pos.astype(F32)[:, None, None] * inv_lane
    return jnp.cos(ang_row), jnp.sin(ang_row), jnp.cos(ang_tile), jnp.sin(ang_tile)


def _prep_router(w_group, w_expert):
    dm = w_group.shape[0]
    pad = jnp.zeros((dm, ROUTER_W - N_GROUPS - N_EXPERTS), F32)
    w_router = jnp.concatenate([w_group, w_expert, pad], axis=1)
    w_router_hi = w_router.astype(BF16)
    w_router_lo = (w_router - w_router_hi.astype(F32)).astype(BF16)
    return dict(w_router=jnp.concatenate([w_router_hi, w_router_lo], axis=1))


def _cmul(ar, ai, br, bi):
    return ar * br - ai * bi, ar * bi + ai * br


def _prep_s5(lam_re, lam_im, log_step, b_re, b_im, c_re, c_im):
    hi = lax.Precision.HIGHEST
    L, G, P, C = S5_L, S5_GROUPS, S5_STATE, S5_GROUP
    lr, li = lam_re, lam_im
    dt = jnp.exp(log_step)[:, None]
    mag = jnp.exp(lr * dt)
    ar, ai = mag * jnp.cos(li * dt), mag * jnp.sin(li * dt)
    den = lr * lr + li * li
    nr = ar - 1.0
    cr = (nr * lr + ai * li) / den
    ci = (ai * lr - nr * li) / den
    bbr = cr[..., None] * b_re - ci[..., None] * b_im
    bbi = cr[..., None] * b_im + ci[..., None] * b_re
    prs, pis = [jnp.ones_like(ar)], [jnp.zeros_like(ar)]
    for _ in range(L):
        nr_, ni_ = _cmul(prs[-1], pis[-1], ar, ai)
        prs.append(nr_)
        pis.append(ni_)
    pr = jnp.stack(prs)
    pi = jnp.stack(pis)
    xr = pr[:L, :, :, None] * bbr - pi[:L, :, :, None] * bbi
    xi = pr[:L, :, :, None] * bbi + pi[:L, :, :, None] * bbr
    kk = (jnp.einsum('gcp,kgpd->kgcd', c_re, xr, precision=hi)
          - jnp.einsum('gcp,kgpd->kgcd', c_im, xi, precision=hi))
    p1r = pr[1:, :, :, None]
    p1i = pi[1:, :, :, None]
    cre_t = jnp.transpose(c_re, (0, 2, 1))[None]
    cim_t = jnp.transpose(c_im, (0, 2, 1))[None]
    f_re = cre_t * p1r - cim_t * p1i
    f_im = -cre_t * p1i - cim_t * p1r

    nq, sg = S5_NSUPER, S5_SUPER
    eye = jnp.eye(sg, dtype=F32)
    dk = jnp.einsum('kqgcd,gh->kqgdhc', kk.reshape(L, nq, sg, C, C), eye)
    dk = dk.reshape(L, nq, LANES, LANES).astype(BF16)
    lag = jnp.arange(L)[None, :] - jnp.arange(L)[:, None]
    m = jnp.where((lag >= 0)[:, :, None, None, None], dk[jnp.clip(lag, 0, L - 1)], 0)
    m = jnp.transpose(m, (2, 0, 3, 1, 4)).reshape(nq, L * LANES, L * LANES)

    row_g = (jnp.arange(L * LANES) // C) % sg
    col_g = jnp.arange(sg * P) // P
    same_group = (row_g[:, None] == col_g[None, :])[None]

    def pack_rows(a):
        a = jnp.transpose(a.reshape(L, nq, sg, P, C), (1, 0, 4, 2, 3))
        a = a.reshape(nq, L, 1, C, sg * P).astype(BF16)
        a = jnp.broadcast_to(a, (nq, L, sg, C, sg * P)).reshape(nq, L * LANES, sg * P)
        return jnp.where(same_group, a, 0)

    def pack_e(x):
        return pack_rows(x[::-1])

    def pack_f(f):
        return jnp.swapaxes(pack_rows(f), 1, 2)

    alr, ali = pr[L].reshape(1, G * P), pi[L].reshape(1, G * P)
    qr, qi = [alr], [ali]
    for _ in range(SUBLANES - 1):
        nr_, ni_ = _cmul(qr[-1], qi[-1], alr, ali)
        qr.append(nr_)
        qi.append(ni_)
    apow_re = jnp.concatenate(qr, axis=0)
    apow_im = jnp.concatenate(qi, axis=0)
    step_re = jnp.concatenate([qr[0], qr[1], qr[3]], axis=0)
    step_im = jnp.concatenate([qi[0], qi[1], qi[3]], axis=0)
    return dict(m=m, e_re=pack_e(xr), e_im=pack_e(xi), f_re=pack_f(f_re),
                f_im=pack_f(f_im), apow_re=apow_re, apow_im=apow_im,
                step_re=step_re, step_im=step_im)


def _mixer_ab(x, pos0, past_ckv, past_krope, conv0, h0, w):
    b, t, dm = x.shape
    xf = x.reshape(b * t, dm)
    tm = min(ROW_TILE, b * t)
    assert tm % t == 0 or t % tm == 0
    row_pos = jnp.arange(tm) % t
    tile_pos = pos0 + (jnp.arange(b * t // tm) * tm) % t
    q, k, v, c_new, kr_new, xb, yb = _ab_in(xf, w, _rope_tables(row_pos, tile_pos))
    if past_ckv is None:
        att = _attn_prompt(q, k, v)
    else:
        att = _attn_sample(q, c_new, kr_new, past_ckv, past_krope, w['wukt'], w['wuv_h'])
    rec, conv_new, h_new = _lru(xb.reshape(b, t, -1), yb.reshape(b, t, -1), conv0,
                                h0[:, None, :], w)
    return (att, rec.reshape(b * t, -1), c_new.reshape(b, t, -1), kr_new.reshape(b, t, -1),
            conv_new, h_new[:, 0, :])


def _mixer_c(x, h0_re, h0_im, w):
    b, t, dm = x.shape
    xf = x.reshape(b * t, dm)
    nc = t // S5_L
    assert t % S5_L == 0 and nc % SUBLANES == 0
    u, u2 = _c_in(xf, w['w_in'])
    e_re, e_im = _s5_e(u2, w['tabs'])
    hp_re, hp_im, hf_re, hf_im = _s5_scan(
        e_re.reshape(b, nc, S5_STATE_W), e_im.reshape(b, nc, S5_STATE_W),
        h0_re.reshape(b, 1, S5_STATE_W), h0_im.reshape(b, 1, S5_STATE_W), w['tabs'], nc)
    y2 = _s5_y(u2, hp_re.reshape(b * nc, S5_STATE_W), hp_im.reshape(b * nc, S5_STATE_W),
               w['tabs'])
    return (y2, u, hf_re.reshape(b, S5_GROUPS, S5_STATE), hf_im.reshape(b, S5_GROUPS, S5_STATE))


def kernel(x_prompt, x_sample, cache_mla_ckv, cache_mla_krope, state_lru_conv, state_lru_h,
           state_s5_re, state_s5_im, w_in_ab, q_norm_g, kv_norm_g, w_uq, w_ukv,
           lru_conv_w, lru_conv_b, lru_w_a, lru_b_a, lru_w_x, lru_b_x, lru_a_param, w_out_ab,
           w_in_c, s5_lam_re, s5_lam_im, s5_log_step, s5_b_re, s5_b_im, s5_c_re, s5_c_im,
           s5_d, s5_w_glu, w_out_c, ln_mix_g, ln_mix_b, ln_ffn_g, ln_ffn_b,
           moe_w_group, moe_w_expert, moe_w_gate, moe_w_up, moe_w_down):
    bp, tp, dm = x_prompt.shape
    bs, ts, _ = x_sample.shape
    past = cache_mla_ckv.shape[2]
    depth = ln_mix_g.shape[0]
    alpha = (2 * depth) ** 0.25
    assert (past + ts - 1) // CHUNK <= past // CHUNK
    hp, hs = x_prompt, x_sample
    moe_experts = dict(w_gate=moe_w_gate.astype(BF16), w_up=moe_w_up.astype(BF16),
                       w_down=moe_w_down.astype(BF16))
    outs = {k: [] for k in ('ckv_p', 'ckv_s', 'kr_p', 'kr_s', 'cv_p', 'cv_s', 'lh_p', 'lh_s',
                            's5r_p', 's5r_s', 's5i_p', 's5i_s')}
    for layer in range(depth):
        j = layer // 2
        g_mix, b_mix = ln_mix_g[layer][None, :], ln_mix_b[layer][None, :]
        g_ffn, b_ffn = ln_ffn_g[layer][None, :], ln_ffn_b[layer][None, :]
        if layer % 2 == 0:
            w = _prep_ab(w_in_ab[j], q_norm_g[j], kv_norm_g[j], w_uq[j], w_ukv[j], lru_conv_w[j],
                         lru_conv_b[j], lru_w_a[j], lru_b_a[j], lru_w_x[j], lru_b_x[j],
                         lru_a_param[j], w_out_ab[j])
            att_p, rec_p, c1, k1, v1, h1 = _mixer_ab(
                hp, 0, None, None, jnp.zeros((bp, CONV_WIDTH - 1, LRU_WIDTH), F32),
                jnp.zeros((bp, LRU_WIDTH), F32), w)
            att_s, rec_s, c2, k2, v2, h2 = _mixer_ab(
                hs, past, cache_mla_ckv[j], cache_mla_krope[j], state_lru_conv[j],
                state_lru_h[j], w)
            outs['ckv_p'].append(c1); outs['ckv_s'].append(c2)
            outs['kr_p'].append(k1); outs['kr_s'].append(k2)
            outs['cv_p'].append(v1); outs['cv_s'].append(v2)
            outs['lh_p'].append(h1); outs['lh_s'].append(h2)
            hp = _out_ln(att_p, rec_p, hp.reshape(bp * tp, dm), w['w_out_att'], w['w_out_rec'],
                         g_mix, b_mix, alpha)
            hs = _out_ln(att_s, rec_s, hs.reshape(bs * ts, dm), w['w_out_att'], w['w_out_rec'],
                         g_mix, b_mix, alpha)
        else:
            w = dict(w_in=w_in_c[j].astype(BF16), d=s5_d[j][None, :],
                     w_glu=s5_w_glu[j].astype(BF16), w_out=w_out_c[j].astype(BF16),
                     tabs=_prep_s5(s5_lam_re[j], s5_lam_im[j], s5_log_step[j], s5_b_re[j],
                                   s5_b_im[j], s5_c_re[j], s5_c_im[j]))
            zero_state = jnp.zeros((bp, S5_GROUPS, S5_STATE), F32)
            y_p, u_p, r1, i1 = _mixer_c(hp, zero_state, zero_state, w)
            y_s, u_s, r2, i2 = _mixer_c(hs, state_s5_re[j], state_s5_im[j], w)
            outs['s5r_p'].append(r1); outs['s5r_s'].append(r2)
            outs['s5i_p'].append(i1); outs['s5i_s'].append(i2)
            hp = _c_out(y_p, u_p, hp.reshape(bp * tp, dm), w, g_mix, b_mix, alpha)
            hs = _c_out(y_s, u_s, hs.reshape(bs * ts, dm), w, g_mix, b_mix, alpha)
        wm = dict(moe_experts, **_prep_router(moe_w_group[layer], moe_w_expert[layer]))
        hp = _moe(hp, wm, layer, g_ffn, b_ffn, alpha).reshape(bp, tp, dm)
        hs = _moe(hs, wm, layer, g_ffn, b_ffn, alpha).reshape(bs, ts, dm)
    st = lambda k: jnp.stack(outs[k])
    return (hp, hs, st('ckv_p'), st('ckv_s'), st('kr_p'), st('kr_s'), st('cv_p'), st('cv_s'),
            st('lh_p'), st('lh_s'), st('s5r_p'), st('s5r_s'), st('s5i_p'), st('s5i_s'))
```

```python
import functools
import math

import jax
import jax.numpy as jnp
from jax import lax
from jax.experimental import pallas as pl
from jax.experimental.pallas import tpu as pltpu
from jax.experimental.pallas import tpu_sc as plsc

F32 = jnp.float32
BF16 = jnp.bfloat16

CHUNK = 64
MLA_HEADS = 8
QK_NOPE = 64
QK_ROPE = 32
QK_HEAD = QK_NOPE + QK_ROPE
V_HEAD = 64
Q_LORA = 256
KV_LORA = 128
ROPE_THETA = 10000.0
MLA_SCALE = QK_HEAD ** -0.5
NEG_INF = -1e30
LRU_WIDTH = 512
LRU_BLOCKS = 8
LRU_BLOCK = LRU_WIDTH // LRU_BLOCKS
CONV_WIDTH = 4
RG_C = 8.0
S5_WIDTH = 512
S5_GROUP = 16
S5_GROUPS = S5_WIDTH // S5_GROUP
S5_STATE = 64
N_GROUPS = 4
EXPERTS_PER_GROUP = 4
N_EXPERTS = N_GROUPS * EXPERTS_PER_GROUP
EXPERT_HIDDEN = 256
LN_EPS = 1e-5
RMS_EPS = 1e-6

LANES = 128
SUBLANES = 8
VMEM_LIMIT_BYTES = 48 * 1024 * 1024
MOE_VMEM_LIMIT_BYTES = 56 * 1024 * 1024

HEAD_PAD = LANES
ROW_TILE = 512
MOE_ROW_TILE = 1024
MOE_EXPERT_TILE = 512
ATTN_TQ = 1024
ATTN_TK = 512
ATTN_UNROLL = 4
Q_SCALE = MLA_SCALE * math.log2(math.e)
LRU_TILE = 512
S5_L = 8
S5_SUPER = LANES // S5_GROUP
S5_NSUPER = S5_GROUPS // S5_SUPER
S5_SUPER_IN = S5_L * LANES
S5_SUPER_STATE = S5_SUPER * S5_STATE
S5_STATE_W = S5_GROUPS * S5_STATE
S5_SCAN_LANES = 256
S5_ROW_TILE = 256
ROUTER_W = LANES


def _cparams(semantics, vmem_limit_bytes=VMEM_LIMIT_BYTES):
    return pltpu.CompilerParams(dimension_semantics=semantics,
                                vmem_limit_bytes=vmem_limit_bytes)


def _full_spec(a):
    nd = a.ndim
    return pl.BlockSpec(a.shape, lambda *_: (0,) * nd)


def _row_spec(tm, width):
    return pl.BlockSpec((tm, width), lambda i: (i, 0))


def _layer_norm(z, g, b):
    mu = jnp.mean(z, axis=-1, keepdims=True)
    zc = z - mu
    var = jnp.mean(zc * zc, axis=-1, keepdims=True)
    return zc * lax.rsqrt(var + LN_EPS) * g + b


def _rms_norm(z, g):
    ms = jnp.mean(z * z, axis=-1, keepdims=True)
    return z * lax.rsqrt(ms + RMS_EPS) * g


def _bdot(a, b):
    return jnp.dot(a.astype(BF16), b.astype(BF16), preferred_element_type=F32)


def _ab_in_kernel(x_ref, win_ref, qg_ref, kvg_ref, wuq_ref, wuk_ref, wuv_ref,
                  cosb_ref, sinb_ref, cost_ref, sint_ref,
                  q_out, k_out, v_out, c_out, kr_out, xb_out, yb_out, *, dm):
    proj = _bdot(x_ref[...], win_ref[...])
    o0 = Q_LORA
    o1 = o0 + KV_LORA
    o2 = o1 + LRU_WIDTH
    o3 = o2 + LRU_WIDTH
    q_lat = proj[:, 0:o0]
    kv_lat = proj[:, o0:o1]
    xb_out[...] = proj[:, o1:o2]
    yb_out[...] = proj[:, o2:o3]
    krp = proj[:, o3:o3 + HEAD_PAD]

    qn = _rms_norm(q_lat, qg_ref[...])
    cn = _rms_norm(kv_lat, kvg_ref[...])
    c_out[...] = cn
    q = _bdot(qn, wuq_ref[...])
    kk = _bdot(cn, wuk_ref[...])
    lane = lax.broadcasted_iota(jnp.int32, (1, MLA_HEADS * HEAD_PAD), 1)
    ones_col = (lane % HEAD_PAD == V_HEAD).astype(F32)
    v_out[...] = (_bdot(cn, wuv_ref[...]) + ones_col).astype(v_out.dtype)

    cb, sb = cosb_ref[...], sinb_ref[...]
    ct, st = cost_ref[...], sint_ref[...]
    cos = cb * ct - sb * st
    sin = sb * ct + cb * st
    half = QK_ROPE // 2
    lane1 = lax.broadcasted_iota(jnp.int32, (1, HEAD_PAD), 1)
    rc = jnp.where(lane1 < QK_HEAD, cos, 0.0)
    ra = jnp.where((lane1 >= QK_NOPE + half) & (lane1 < QK_HEAD), sin, 0.0)
    rb = jnp.where((lane1 >= QK_NOPE) & (lane1 < QK_NOPE + half), -sin, 0.0)

    def rope(z):
        return (z * rc + pltpu.roll(z, half, 1) * ra
                + pltpu.roll(z, HEAD_PAD - half, 1) * rb)

    kr = rope(krp)
    kr_out[...] = kr[:, QK_NOPE:QK_NOPE + QK_ROPE]
    for h in range(MLA_HEADS):
        sl = slice(h * HEAD_PAD, (h + 1) * HEAD_PAD)
        q_out[:, sl] = (rope(q[:, sl]) * Q_SCALE).astype(q_out.dtype)
        k_out[:, sl] = (kk[:, sl] + kr).astype(k_out.dtype)


def _ab_in(x, w, rope_tabs):
    n, dm = x.shape
    tm = min(ROW_TILE, n)
    cosb, sinb, cost, sint = rope_tabs
    consts = [w['w_in'], w['q_g'], w['kv_g'], w['w_uq'], w['w_uk'], w['w_uv'], cosb, sinb]
    tile_spec = pl.BlockSpec((None, 1, HEAD_PAD), lambda i: (i, 0, 0))
    hp = MLA_HEADS * HEAD_PAD
    out_shape = [
        jax.ShapeDtypeStruct((n, hp), BF16),
        jax.ShapeDtypeStruct((n, hp), BF16),
        jax.ShapeDtypeStruct((n, hp), BF16),
        jax.ShapeDtypeStruct((n, KV_LORA), F32),
        jax.ShapeDtypeStruct((n, QK_ROPE), F32),
        jax.ShapeDtypeStruct((n, LRU_WIDTH), F32),
        jax.ShapeDtypeStruct((n, LRU_WIDTH), F32),
    ]
    return pl.pallas_call(
        functools.partial(_ab_in_kernel, dm=dm), name="ab_in",
        grid=(n // tm,),
        in_specs=([_row_spec(tm, dm)] + [_full_spec(c) for c in consts]
                  + [tile_spec, tile_spec]),
        out_specs=[_row_spec(tm, s.shape[1]) for s in out_shape],
        out_shape=out_shape,
        compiler_params=_cparams(("parallel",)),
    )(x, *consts, cost, sint)


def _attn_prompt_kernel(q_ref, k_ref, v_ref, o_ref):
    i = pl.program_id(1)
    tq, tk = ATTN_TQ, ATTN_TK
    ndiag = tq // tk
    nfull = i * ndiag
    row_chunk = lax.broadcasted_iota(jnp.int32, (tq, tk), 0) // CHUNK
    col_chunk = lax.broadcasted_iota(jnp.int32, (tq, tk), 1) // CHUNK
    qs = [q_ref[:, hh * HEAD_PAD:(hh + 1) * HEAD_PAD] for hh in range(2)]

    def update(hh, j, m, acc, mask, row0=0):
        start = pl.multiple_of(j * tk, tk)
        k = k_ref[pl.ds(start, tk), hh * HEAD_PAD:(hh + 1) * HEAD_PAD]
        v = v_ref[pl.ds(start, tk), hh * HEAD_PAD:(hh + 1) * HEAD_PAD]
        s = lax.dot_general(qs[hh][row0:], k, (((1,), (1,)), ((), ())),
                            preferred_element_type=F32)
        if mask is not None:
            s = jnp.where(mask[row0:], s, NEG_INF)
        m_new = jnp.maximum(m, jnp.max(s, axis=-1, keepdims=True))
        alpha = jnp.exp2(m - m_new)
        p = jnp.exp2((s - m_new).astype(BF16))
        return m_new, alpha * acc + jnp.dot(p, v, preferred_element_type=F32)

    def body(j, carry):
        new = []
        for hh in range(2):
            new += update(hh, j, carry[2 * hh], carry[2 * hh + 1], None)
        return tuple(new)

    unroll = ATTN_UNROLL

    def body_unrolled(jj, carry):
        for b in range(unroll):
            carry = body(unroll * jj + b, carry)
        return carry

    init = []
    for _ in range(2):
        init += [jnp.full((tq, 1), NEG_INF, F32), jnp.zeros((tq, HEAD_PAD), F32)]
    carry = lax.fori_loop(0, nfull // unroll, body_unrolled, tuple(init))
    rem0 = nfull - nfull % unroll

    def body_rem(jj, carry):
        for b in range(ndiag):
            carry = body(rem0 + ndiag * jj + b, carry)
        return carry

    carry = list(lax.fori_loop(0, (nfull % unroll) // ndiag, body_rem, carry))
    for d in range(ndiag):
        visible = col_chunk + d * (tk // CHUNK) <= row_chunk
        r0 = d * tk
        for hh in range(2):
            m, acc = carry[2 * hh], carry[2 * hh + 1]
            m_new, acc_new = update(hh, nfull + d, m[r0:], acc[r0:], visible, r0)
            if r0:
                m_new = jnp.concatenate([m[:r0], m_new], axis=0)
                acc_new = jnp.concatenate([acc[:r0], acc_new], axis=0)
            carry[2 * hh], carry[2 * hh + 1] = m_new, acc_new
    outs = [carry[2 * hh + 1][:, :V_HEAD] / carry[2 * hh + 1][:, V_HEAD:V_HEAD + 1]
            for hh in range(2)]
    o_ref[...] = jnp.concatenate(outs, axis=-1).astype(o_ref.dtype)


def _attn_prompt(q, k, v):
    t = q.shape[0]
    pairs = MLA_HEADS // 2
    return pl.pallas_call(
        _attn_prompt_kernel, name="attn_prompt",
        grid=(pairs, t // ATTN_TQ),
        in_specs=[
            pl.BlockSpec((ATTN_TQ, 2 * HEAD_PAD), lambda p, i: (i, p)),
            pl.BlockSpec((t, 2 * HEAD_PAD), lambda p, i: (0, p)),
            pl.BlockSpec((t, 2 * HEAD_PAD), lambda p, i: (0, p)),
        ],
        out_specs=pl.BlockSpec((ATTN_TQ, 2 * V_HEAD), lambda p, i: (i, p)),
        out_shape=jax.ShapeDtypeStruct((t, MLA_HEADS * V_HEAD), BF16),
        compiler_params=_cparams(("parallel", "parallel")),
    )(q, k, v)


def _attn_sample_kernel(q_ref, cn_ref, krn_ref, cp_ref, krp_ref, wukt_ref, wuv_ref, o_ref):
    cp = cp_ref[...].astype(BF16)
    krp = krp_ref[...].astype(BF16)
    cn = cn_ref[...].astype(BF16)
    krn = krn_ref[...].astype(BF16)
    ts = cn.shape[0]
    dn = (((1,), (1,)), ((), ()))
    outs = []
    for hp in range(MLA_HEADS // 2):
        qa, qr = [], []
        for hh in range(2):
            h = 2 * hp + hh
            qh = q_ref[:, h * HEAD_PAD:(h + 1) * HEAD_PAD]
            qa.append(jnp.dot(qh[:, :QK_NOPE], wukt_ref[h], preferred_element_type=F32))
            qr.append(qh[:, QK_NOPE:QK_NOPE + QK_ROPE])
        qa = jnp.concatenate(qa, axis=0).astype(BF16)
        qr = jnp.concatenate(qr, axis=0)
        s_past = (lax.dot_general(qa, cp, dn, preferred_element_type=F32)
                  + lax.dot_general(qr, krp, dn, preferred_element_type=F32))
        s_new = (lax.dot_general(qa, cn, dn, preferred_element_type=F32)
                 + lax.dot_general(qr, krn, dn, preferred_element_type=F32))
        m = jnp.maximum(jnp.max(s_past, axis=-1, keepdims=True),
                        jnp.max(s_new, axis=-1, keepdims=True))
        p_past = jnp.exp2(s_past - m)
        p_new = jnp.exp2(s_new - m)
        l = (jnp.sum(p_past, axis=-1, keepdims=True)
             + jnp.sum(p_new, axis=-1, keepdims=True))
        o_lat = (jnp.dot(p_past.astype(BF16), cp, preferred_element_type=F32)
                 + jnp.dot(p_new.astype(BF16), cn, preferred_element_type=F32)) / l
        for hh in range(2):
            h = 2 * hp + hh
            outs.append(_bdot(o_lat[hh * ts:(hh + 1) * ts], wuv_ref[h]))
    o_ref[...] = jnp.concatenate(outs, axis=-1).astype(o_ref.dtype)


def _attn_sample(q, c_new, kr_new, c_past, kr_past, wukt, wuv):
    bs, past, _ = c_past.shape
    ts = q.shape[0] // bs
    return pl.pallas_call(
        _attn_sample_kernel, name="attn_sample",
        grid=(bs,),
        in_specs=[
            pl.BlockSpec((ts, MLA_HEADS * HEAD_PAD), lambda b: (b, 0)),
            pl.BlockSpec((ts, KV_LORA), lambda b: (b, 0)),
            pl.BlockSpec((ts, QK_ROPE), lambda b: (b, 0)),
            pl.BlockSpec((None, past, KV_LORA), lambda b: (b, 0, 0)),
            pl.BlockSpec((None, past, QK_ROPE), lambda b: (b, 0, 0)),
            _full_spec(wukt), _full_spec(wuv),
        ],
        out_specs=pl.BlockSpec((ts, MLA_HEADS * V_HEAD), lambda b: (b, 0)),
        out_shape=jax.ShapeDtypeStruct((bs * ts, MLA_HEADS * V_HEAD), BF16),
        compiler_params=_cparams(("parallel",)),
    )(q, c_new, kr_new, c_past, kr_past, wukt, wuv)


def _lru_kernel(xb_ref, yb_ref, conv0_ref, h0_ref, cw_ref, cb_ref, wax_ref, bax_ref, ap_ref,
                rec_ref, conv_out_ref, h_out_ref, xcat, hc, *, tb):
    t = pl.program_id(1)
    nt = pl.num_programs(1)
    tail = CONV_WIDTH - 1
    base = SUBLANES

    @pl.when(t == 0)
    def _():
        xcat[base - tail:base, :] = conv0_ref[...]
        hc[...] = h0_ref[...]

    @pl.when(t > 0)
    def _():
        xcat[0:base, :] = xcat[tb:tb + base, :]

    xcat[base:base + tb, :] = xb_ref[...]
    xc = cb_ref[...]
    for tap in range(CONV_WIDTH):
        xc = xc + xcat[base - tail + tap:base - tail + tap + tb, :] * cw_ref[tap:tap + 1, :]

    gates = _bdot(xc, wax_ref[...]) + bax_ref[...]
    r = jax.nn.sigmoid(gates[:, :LRU_WIDTH])
    ig = jax.nn.sigmoid(gates[:, LRU_WIDTH:])
    log_a = -RG_C * r * ap_ref[...]
    a = jnp.exp(log_a)
    u = jnp.sqrt(jnp.tanh(-log_a) * (1.0 + a * a)) * (ig * xc)

    row8 = lax.broadcasted_iota(jnp.int32, (tb, LRU_WIDTH), 0) % SUBLANES
    s = 1
    while s < SUBLANES:
        keep = row8 >= s
        u = jnp.where(keep, a * pltpu.roll(u, s, 0) + u, u)
        a = jnp.where(keep, a * pltpu.roll(a, s, 0), a)
        s *= 2
    hprev = hc[...]
    groups = []
    for g in range(tb // SUBLANES):
        sl = slice(g * SUBLANES, (g + 1) * SUBLANES)
        hg = a[sl] * hprev + u[sl]
        groups.append(hg)
        hprev = hg[SUBLANES - 1:SUBLANES, :]
    h = jnp.concatenate(groups, axis=0)
    hc[...] = hprev
    rec_ref[...] = (h * jax.nn.gelu(yb_ref[...])).astype(rec_ref.dtype)

    @pl.when(t == nt - 1)
    def _():
        conv_out_ref[...] = xcat[base + tb - tail:base + tb, :]
        h_out_ref[...] = h[tb - 1:tb, :]


def _lru(xb, yb, conv0, h0, w):
    b, t, wd = xb.shape
    tb = min(LRU_TILE, t)
    tail = CONV_WIDTH - 1
    consts = [w['conv_w'], w['conv_b'], w['w_ax'], w['b_ax'], w['a_param']]
    seq = pl.BlockSpec((None, tb, wd), lambda bi, ti: (bi, ti, 0))
    return pl.pallas_call(
        functools.partial(_lru_kernel, tb=tb), name="rg_lru",
        grid=(b, t // tb),
        in_specs=[seq, seq,
                  pl.BlockSpec((None, tail, wd), lambda bi, ti: (bi, 0, 0)),
                  pl.BlockSpec((None, 1, wd), lambda bi, ti: (bi, 0, 0))]
                 + [pl.BlockSpec(c.shape, lambda bi, ti: (0, 0)) for c in consts],
        out_specs=[seq,
                   pl.BlockSpec((None, tail, wd), lambda bi, ti: (bi, 0, 0)),
                   pl.BlockSpec((None, 1, wd), lambda bi, ti: (bi, 0, 0))],
        out_shape=[jax.ShapeDtypeStruct((b, t, wd), BF16),
                   jax.ShapeDtypeStruct((b, tail, wd), F32),
                   jax.ShapeDtypeStruct((b, 1, wd), F32)],
        scratch_shapes=[pltpu.VMEM((tb + 2 * SUBLANES, wd), F32), pltpu.VMEM((1, wd), F32)],
        compiler_params=_cparams(("parallel", "arbitrary")),
    )(xb, yb, conv0, h0, *consts)


def _out_ln_kernel(a_ref, b_ref, x_ref, wa_ref, wb_ref, g_ref, beta_ref, o_ref, *, alpha):
    y = (jnp.dot(a_ref[...], wa_ref[...], preferred_element_type=F32)
         + jnp.dot(b_ref[...], wb_ref[...], preferred_element_type=F32))
    o_ref[...] = _layer_norm(alpha * x_ref[...] + y, g_ref[...], beta_ref[...])


def _out_ln(att, rec, x, wa, wb, g, beta, alpha):
    n, dm = x.shape
    tm = min(ROW_TILE, n)
    consts = [wa, wb, g, beta]
    return pl.pallas_call(
        functools.partial(_out_ln_kernel, alpha=alpha), name="out_ln",
        grid=(n // tm,),
        in_specs=[_row_spec(tm, att.shape[1]), _row_spec(tm, rec.shape[1]), _row_spec(tm, dm)]
                 + [_full_spec(c) for c in consts],
        out_specs=_row_spec(tm, dm),
        out_shape=jax.ShapeDtypeStruct((n, dm), F32),
        compiler_params=_cparams(("parallel",)),
    )(att, rec, x, *consts)


def _route(x, xh, wr_ref, lane):
    xl = (x - xh.astype(F32)).astype(BF16)
    hh_hl = jnp.dot(xh, wr_ref[...], preferred_element_type=F32)
    logits = (hh_hl[:, :ROUTER_W] + hh_hl[:, ROUTER_W:]
              + jnp.dot(xl, wr_ref[:, :ROUTER_W], preferred_element_type=F32))
    big = float(ROUTER_W)
    lg = jnp.where(lane < N_GROUPS, logits, -jnp.inf)
    mg = jnp.max(lg, axis=-1, keepdims=True)
    gi = jnp.min(jnp.where(lg == mg, lane, big), axis=-1, keepdims=True)
    p_top = 1.0 / jnp.sum(jnp.exp(lg - mg), axis=-1, keepdims=True)
    lo = N_GROUPS + gi * EXPERTS_PER_GROUP
    le = jnp.where((lane >= lo) & (lane < lo + EXPERTS_PER_GROUP), logits, -jnp.inf)
    v1 = jnp.max(le, axis=-1, keepdims=True)
    i1 = jnp.min(jnp.where(le == v1, lane, big), axis=-1, keepdims=True)
    le2 = jnp.where(lane == i1, -jnp.inf, le)
    v2 = jnp.max(le2, axis=-1, keepdims=True)
    i2 = jnp.min(jnp.where(le2 == v2, lane, big), axis=-1, keepdims=True)
    e2 = jnp.exp(v2 - v1)
    return gi, i1, i2, p_top / (1.0 + e2), p_top * e2 / (1.0 + e2)


def _moe_ln_kernel(x_ref, wr_ref, wg_ref, wu_ref, wd_ref, g_ref, beta_ref, o_ref,
                   acc, gates, xb16, *, alpha):
    grp = pl.program_id(1)
    tm = x_ref.shape[0]
    lane = lax.broadcasted_iota(jnp.int32, (tm, ROUTER_W), 1).astype(F32)

    @pl.when(grp == 0)
    def _():
        x = x_ref[...]
        xh = x.astype(BF16)
        xb16[...] = xh
        _, i1, i2, w1, w2 = _route(x, xh, wr_ref, lane)
        gates[...] = jnp.where(lane == i1, w1, 0.0) + jnp.where(lane == i2, w2, 0.0)
        acc[...] = jnp.zeros_like(acc)

    xb = xb16[...]
    gt = gates[...]
    first = (N_GROUPS + grp * EXPERTS_PER_GROUP).astype(F32)
    for el in range(EXPERTS_PER_GROUP):
        ge = jnp.sum(jnp.where(lane == first + el, gt, 0.0), axis=-1, keepdims=True)
        h = jnp.dot(xb, wg_ref[el], preferred_element_type=F32)
        u = jnp.dot(xb, wu_ref[el], preferred_element_type=F32)
        act = (jax.nn.silu(h) * u * ge).astype(BF16)
        acc[...] += jnp.dot(act, wd_ref[el], preferred_element_type=F32)

    @pl.when(grp == pl.num_programs(1) - 1)
    def _():
        o_ref[...] = _layer_norm(alpha * x_ref[...] + acc[...], g_ref[...], beta_ref[...])


def _moe_ln(x, w, layer, g, beta, alpha):
    n, dm = x.shape
    tm = min(MOE_ROW_TILE, n)
    _, ne, _, eh = w['w_gate'].shape
    epg = EXPERTS_PER_GROUP
    return pl.pallas_call(
        functools.partial(_moe_ln_kernel, alpha=alpha), name="moe_ln",
        grid=(n // tm, ne // epg),
        in_specs=[
            pl.BlockSpec((tm, dm), lambda i, e: (i, 0)),
            pl.BlockSpec(w['w_router'].shape, lambda i, e: (0, 0)),
            pl.BlockSpec((None, epg, dm, eh), lambda i, e: (layer, e, 0, 0)),
            pl.BlockSpec((None, epg, dm, eh), lambda i, e: (layer, e, 0, 0)),
            pl.BlockSpec((None, epg, eh, dm), lambda i, e: (layer, e, 0, 0)),
            pl.BlockSpec(g.shape, lambda i, e: (0, 0)),
            pl.BlockSpec(beta.shape, lambda i, e: (0, 0)),
        ],
        out_specs=pl.BlockSpec((tm, dm), lambda i, e: (i, 0)),
        out_shape=jax.ShapeDtypeStruct((n, dm), F32),
        scratch_shapes=[pltpu.VMEM((tm, dm), F32), pltpu.VMEM((tm, ROUTER_W), F32),
                        pltpu.VMEM((tm, dm), BF16)],
        compiler_params=_cparams(("parallel", "arbitrary"), MOE_VMEM_LIMIT_BYTES),
    )(x, w['w_router'], w['w_gate'], w['w_up'], w['w_down'], g, beta)


META_W = LANES
META_GROUP = EXPERTS_PER_GROUP
META_RANK = EXPERTS_PER_GROUP + 1


def _moe_route_kernel(x_ref, wr_ref, tri_ref, xg_ref, meta_ref, cnt_ref, run, *, n_tiles):
    tm, dm = x_ref.shape
    i = pl.program_id(0)
    lane = lax.broadcasted_iota(jnp.int32, (tm, ROUTER_W), 1).astype(F32)

    @pl.when(i == 0)
    def _():
        run[...] = jnp.zeros_like(run)

    @pl.when(i < n_tiles)
    def _():
        x = x_ref[...]
        gi, i1, i2, w1, w2 = _route(x, x.astype(BF16), wr_ref, lane)
        lo = N_GROUPS + gi * EXPERTS_PER_GROUP
        in_group = lane == gi
        incl = jnp.dot(tri_ref[...], in_group.astype(BF16), preferred_element_type=F32)
        rank = jnp.sum(jnp.where(in_group, incl + run[...], 0.0), axis=-1,
                       keepdims=True) - 1.0
        run[...] += incl[tm - 1:tm, :]
        cnt_ref[...] = run[...]
        meta = (jnp.where(lane == i1 - lo, w1, 0.0) + jnp.where(lane == i2 - lo, w2, 0.0)
                + jnp.where(lane == META_GROUP, gi, 0.0)
                + jnp.where(lane == META_RANK, rank, 0.0))
        xg_ref[:, :dm] = x
        xg_ref[:, dm:] = meta
        meta_ref[...] = meta

    @pl.when(i >= n_tiles)
    def _():
        xg_ref[...] = jnp.zeros_like(xg_ref)


def _moe_route(x, w_router, p):
    n, dm = x.shape
    tm = ROW_TILE
    n_tiles = n // tm
    tri = jnp.tril(jnp.ones((tm, tm), BF16))
    last = lambda i: (jnp.minimum(i, n_tiles - 1), 0)
    return pl.pallas_call(
        functools.partial(_moe_route_kernel, n_tiles=n_tiles), name="moe_route",
        grid=(p // tm,),
        in_specs=[pl.BlockSpec((tm, dm), last), _full_spec(w_router), _full_spec(tri)],
        out_specs=[_row_spec(tm, dm + META_W), pl.BlockSpec((tm, META_W), last),
                   pl.BlockSpec((1, ROUTER_W), lambda i: (0, 0))],
        out_shape=[jax.ShapeDtypeStruct((p, dm + META_W), F32),
                   jax.ShapeDtypeStruct((n, META_W), F32),
                   jax.ShapeDtypeStruct((1, ROUTER_W), F32)],
        scratch_shapes=[pltpu.VMEM((1, ROUTER_W), F32)],
        compiler_params=_cparams(("arbitrary",)),
    )(x, w_router, tri)


def _moe_experts_kernel(tg_ref, nu_ref, xs_ref, wg_ref, wu_ref, wd_ref, y_ref):
    del tg_ref
    dm = y_ref.shape[1]

    @pl.when(pl.program_id(0) < nu_ref[0])
    def _():
        xb = xs_ref[:, :dm].astype(BF16)
        meta = xs_ref[:, dm:]
        acc = None
        for el in range(EXPERTS_PER_GROUP):
            h = jnp.dot(xb, wg_ref[el], preferred_element_type=F32)
            u = jnp.dot(xb, wu_ref[el], preferred_element_type=F32)
            act = (jax.nn.silu(h) * u * meta[:, el:el + 1]).astype(BF16)
            d = jnp.dot(act, wd_ref[el], preferred_element_type=F32)
            acc = d if acc is None else acc + d
        y_ref[...] = acc


def _moe_experts(xs, tile_group, n_used, w, layer):
    p, wx = xs.shape
    dm = wx - META_W
    te = MOE_EXPERT_TILE
    _, ne, _, eh = w['w_gate'].shape
    epg = EXPERTS_PER_GROUP

    def tile(i, tg, nu):
        return jnp.minimum(i, nu[0] - 1)

    wspec = lambda shape: pl.BlockSpec(
        (None, epg) + shape, lambda i, tg, nu: (layer, tg[tile(i, tg, nu)], 0, 0))
    grid_spec = pltpu.PrefetchScalarGridSpec(
        num_scalar_prefetch=2,
        grid=(p // te,),
        in_specs=[pl.BlockSpec((te, wx), lambda i, tg, nu: (tile(i, tg, nu), 0)),
                  wspec((dm, eh)), wspec((dm, eh)), wspec((eh, dm))],
        out_specs=pl.BlockSpec((te, dm), lambda i, tg, nu: (tile(i, tg, nu), 0)),
    )
    return pl.pallas_call(
        _moe_experts_kernel, name="moe_experts",
        grid_spec=grid_spec,
        out_shape=jax.ShapeDtypeStruct((p, dm), F32),
        compiler_params=_cparams(("arbitrary",)),
    )(tile_group, n_used, xs, w['w_gate'], w['w_up'], w['w_down'])


def _res_ln_kernel(xg_ref, y_ref, g_ref, beta_ref, o_ref, *, alpha):
    o_ref[...] = _layer_norm(alpha * xg_ref[...] + y_ref[...], g_ref[...], beta_ref[...])


def _res_ln(xg, y, g, beta, alpha):
    n, dm = y.shape
    tm = ROW_TILE
    return pl.pallas_call(
        functools.partial(_res_ln_kernel, alpha=alpha), name="res_ln",
        grid=(n // tm,),
        in_specs=[_row_spec(tm, dm), _row_spec(tm, dm), _full_spec(g), _full_spec(beta)],
        out_specs=_row_spec(tm, dm),
        out_shape=jax.ShapeDtypeStruct((n, dm), F32),
        compiler_params=_cparams(("parallel",)),
    )(xg, y, g, beta)


def _moe(x, w, layer, g, beta, alpha):
    n = x.shape[0]
    granule = 2 * SC_GATHER_ROWS * SC_CORES * SC_SUBCORES
    if n % granule == 0 and (N_GROUPS * MOE_EXPERT_TILE) % granule == 0:
        return _moe_routed_ln(x, w, layer, g, beta, alpha)
    return _moe_ln(x, w, layer, g, beta, alpha)


def _moe_routed_ln(x, w, layer, g, beta, alpha):
    n, dm = x.shape
    te = MOE_EXPERT_TILE
    npad = N_GROUPS * te
    p = n + npad
    xg, meta, cnt = _moe_route(x, w['w_router'], p)
    gid = meta[:, META_GROUP].astype(jnp.int32)
    rank = meta[:, META_RANK].astype(jnp.int32)
    counts = cnt[0, :N_GROUPS].astype(jnp.int32)
    padded = (counts + te - 1) // te * te
    ends = jnp.cumsum(padded)
    starts = ends - padded

    def lookup(table, idx):
        hit = idx[:, None] == jnp.arange(table.shape[0], dtype=jnp.int32)[None, :]
        return jnp.sum(jnp.where(hit, table[None, :], 0), axis=1)

    pos = lookup(starts, gid) + rank
    gap_len = jnp.concatenate([padded - counts, p - ends[-1:]])
    gap_start = jnp.concatenate([starts + counts, ends[-1:]])
    gap_end = jnp.cumsum(gap_len)
    k = jnp.arange(npad, dtype=jnp.int32)
    seg = jnp.sum(k[:, None] >= gap_end[None, :], axis=1)
    filler_pos = lookup(gap_start, seg) + k - lookup(gap_end - gap_len, seg)
    pos_all = jnp.concatenate([pos, filler_pos]).astype(jnp.int32)
    tile_start = jnp.arange(p // te, dtype=jnp.int32) * te
    tile_group = jnp.minimum(jnp.sum(tile_start[:, None] >= ends[None, :], axis=1),
                             N_GROUPS - 1).astype(jnp.int32)
    n_used = (ends[-1:] // te).astype(jnp.int32)
    xs = _sc_scatter_rows(xg, pos_all)
    ys = _moe_experts(xs, tile_group, n_used, w, layer)
    y = _sc_gather_rows(ys, pos)
    return _res_ln(xg, y, g, beta, alpha)


SC_CORES = 2
SC_SUBCORES = 16
SC_GATHER_ROWS = 32


def _sc_gather_rows(table, idx):
    b = idx.shape[0]
    _, d = table.shape
    nw = SC_CORES * SC_SUBCORES
    ch = SC_GATHER_ROWS
    per_w = b // nw
    assert b % (nw * 2 * ch) == 0
    npair = per_w // (2 * ch)
    mesh = plsc.VectorSubcoreMesh(core_axis_name="c", subcore_axis_name="s")

    @functools.partial(
        pl.kernel, mesh=mesh, out_type=jax.ShapeDtypeStruct((b, d), table.dtype),
        scratch_types=[pltpu.VMEM((2, ch), jnp.int32), pltpu.VMEM((2, ch, d), table.dtype),
                       pltpu.SemaphoreType.DMA, pltpu.SemaphoreType.DMA],
        name="sc_gather_rows")
    def gather(table_hbm, idx_hbm, out_hbm, idx_v, rows_v, sem0, sem1):
        sems = (sem0, sem1)
        wid = lax.axis_index("s") * SC_CORES + lax.axis_index("c")
        base = wid * per_w

        def start(chunk, slot):
            off = pl.multiple_of(base + chunk * ch, SUBLANES)
            pltpu.sync_copy(idx_hbm.at[pl.ds(off, ch)], idx_v.at[slot])
            pltpu.async_copy(table_hbm.at[idx_v.at[slot]], rows_v.at[slot], sems[slot])

        def finish(chunk, slot):
            off = pl.multiple_of(base + chunk * ch, SUBLANES)
            pltpu.make_async_copy(table_hbm.at[idx_v.at[slot]], rows_v.at[slot],
                                  sems[slot]).wait()
            pltpu.sync_copy(rows_v.at[slot], out_hbm.at[pl.ds(off, ch)])

        start(0, 0)

        @pl.loop(0, npair)
        def _(p):
            c0 = 2 * p
            start(c0 + 1, 1)
            finish(c0, 0)

            @pl.when(p + 1 < npair)
            def _():
                start(c0 + 2, 0)
            finish(c0 + 1, 1)

    return gather(table, idx)


def _sc_scatter_rows(rows, idx):
    b, d = rows.shape
    nw = SC_CORES * SC_SUBCORES
    ch = SC_GATHER_ROWS
    per_w = b // nw
    assert idx.shape == (b,) and b % (nw * 2 * ch) == 0
    npair = per_w // (2 * ch)
    mesh = plsc.VectorSubcoreMesh(core_axis_name="c", subcore_axis_name="s")

    @functools.partial(
        pl.kernel, mesh=mesh, out_type=jax.ShapeDtypeStruct((b, d), rows.dtype),
        scratch_types=[pltpu.VMEM((2, ch), jnp.int32), pltpu.VMEM((2, ch, d), rows.dtype),
                       pltpu.SemaphoreType.DMA, pltpu.SemaphoreType.DMA],
        name="sc_scatter_rows")
    def scatter(rows_hbm, idx_hbm, out_hbm, idx_v, rows_v, sem0, sem1):
        sems = (sem0, sem1)
        wid = lax.axis_index("s") * SC_CORES + lax.axis_index("c")
        base = wid * per_w

        def write(slot):
            return pltpu.make_async_copy(rows_v.at[slot], out_hbm.at[idx_v.at[slot]],
                                         sems[slot])

        @pl.loop(0, npair)
        def _(p):
            for slot in range(2):
                @pl.when(p > 0)
                def _():
                    write(slot).wait()
                off = pl.multiple_of(base + (2 * p + slot) * ch, SUBLANES)
                pltpu.sync_copy(idx_hbm.at[pl.ds(off, ch)], idx_v.at[slot])
                pltpu.sync_copy(rows_hbm.at[pl.ds(off, ch)], rows_v.at[slot])
                write(slot).start()

        write(0).wait()
        write(1).wait()

    return scatter(rows, idx)


def _c_in_kernel(x_ref, w_ref, u_ref, u2_ref, us):
    u = _bdot(x_ref[...], w_ref[...])
    u_ref[...] = u
    rows = u2_ref.shape[0]
    for c in range(S5_WIDTH // LANES):
        us[c] = u[:, c * LANES:(c + 1) * LANES]
    for s in range(S5_L):
        for c in range(S5_WIDTH // LANES):
            lo = s * S5_WIDTH + c * LANES
            u2_ref[:, lo:lo + LANES] = us[c, pl.ds(s, rows, stride=S5_L), :].astype(u2_ref.dtype)


def _c_in(x, w_in):
    n, dm = x.shape
    tm = min(ROW_TILE, n)
    wu = w_in.shape[1]
    return pl.pallas_call(
        _c_in_kernel, name="c_in",
        grid=(n // tm,),
        in_specs=[_row_spec(tm, dm), _full_spec(w_in)],
        out_specs=[_row_spec(tm, wu), _row_spec(tm // S5_L, S5_L * wu)],
        out_shape=[jax.ShapeDtypeStruct((n, wu), F32),
                   jax.ShapeDtypeStruct((n // S5_L, S5_L * wu), BF16)],
        scratch_shapes=[pltpu.VMEM((wu // LANES, tm, LANES), F32)],
        compiler_params=_cparams(("parallel",)),
    )(x, w_in)


def _s5_super_rows(u_ref, q):
    tiles = [u_ref[:, s * S5_WIDTH + q * LANES:s * S5_WIDTH + (q + 1) * LANES]
             for s in range(S5_L)]
    return jnp.concatenate(tiles, axis=1).astype(BF16)


def _s5_e_kernel(u_ref, ere_ref, eim_ref, ere_out, eim_out):
    for q in range(S5_NSUPER):
        uq = _s5_super_rows(u_ref, q)
        sl = slice(q * S5_SUPER_STATE, (q + 1) * S5_SUPER_STATE)
        ere_out[:, sl] = jnp.dot(uq, ere_ref[q], preferred_element_type=F32)
        eim_out[:, sl] = jnp.dot(uq, eim_ref[q], preferred_element_type=F32)


def _const_spec(a):
    nd = a.ndim
    return pl.BlockSpec(a.shape, lambda *_: (0,) * nd, pipeline_mode=pl.Buffered(1))


def _s5_e(u2, tabs):
    rows, wu = u2.shape
    tr = min(S5_ROW_TILE, rows)
    out = jax.ShapeDtypeStruct((rows, S5_STATE_W), F32)
    return pl.pallas_call(
        _s5_e_kernel, name="s5_e",
        grid=(rows // tr,),
        in_specs=[_row_spec(tr, wu), _const_spec(tabs['e_re']), _const_spec(tabs['e_im'])],
        out_specs=[_row_spec(tr, S5_STATE_W)] * 2,
        out_shape=[out, out],
        compiler_params=_cparams(("parallel",)),
    )(u2, tabs['e_re'], tabs['e_im'])


def _s5_scan_kernel(ere_ref, eim_ref, h0r_ref, h0i_ref, stp_re_ref, stp_im_ref,
                    apw_re_ref, apw_im_ref, hpr_ref, hpi_ref, hfr_ref, hfi_ref, *, nc, last_row):
    w = ere_ref.shape[-1]
    row = lax.broadcasted_iota(jnp.int32, (SUBLANES, w), 0)
    apr = apw_re_ref[...]
    api = apw_im_ref[...]

    def tile(i, carry):
        cr, ci = carry
        start = pl.multiple_of(i * SUBLANES, SUBLANES)
        sr = ere_ref[pl.ds(start, SUBLANES), :]
        si = eim_ref[pl.ds(start, SUBLANES), :]
        for k in range(3):
            s = 1 << k
            keep = row >= s
            ar = stp_re_ref[k:k + 1, :]
            ai = stp_im_ref[k:k + 1, :]
            pr = pltpu.roll(sr, s, 0)
            pi = pltpu.roll(si, s, 0)
            sr, si = (jnp.where(keep, sr + ar * pr - ai * pi, sr),
                      jnp.where(keep, si + ar * pi + ai * pr, si))
        hr = sr + apr * cr - api * ci
        hi = si + apr * ci + api * cr
        first = row == 0
        hpr_ref[pl.ds(start, SUBLANES), :] = jnp.where(first, cr, pltpu.roll(hr, 1, 0))
        hpi_ref[pl.ds(start, SUBLANES), :] = jnp.where(first, ci, pltpu.roll(hi, 1, 0))
        return hr, hi

    def body(i, carry):
        hr, hi = tile(i, carry)
        return hr[SUBLANES - 1:SUBLANES, :], hi[SUBLANES - 1:SUBLANES, :]

    ntile = nc // SUBLANES
    carry = lax.fori_loop(0, ntile - 1, body, (h0r_ref[...], h0i_ref[...]))
    hr, hi = tile(ntile - 1, carry)
    hfr_ref[...] = hr[last_row:last_row + 1, :]
    hfi_ref[...] = hi[last_row:last_row + 1, :]


def _s5_scan(e_re, e_im, h0r, h0i, tabs, n_chunks):
    b, nc, w = e_re.shape
    last_row = (n_chunks - 1) % SUBLANES
    wb = S5_SCAN_LANES
    consts = [tabs['step_re'], tabs['step_im'], tabs['apow_re'], tabs['apow_im']]
    seq = pl.BlockSpec((None, nc, wb), lambda bi, li: (bi, 0, li))
    vec = pl.BlockSpec((None, 1, wb), lambda bi, li: (bi, 0, li))
    return pl.pallas_call(
        functools.partial(_s5_scan_kernel, nc=nc, last_row=last_row), name="s5_scan",
        grid=(b, w // wb),
        in_specs=[seq, seq, vec, vec]
                 + [pl.BlockSpec((c.shape[0], wb), lambda bi, li: (0, li)) for c in consts],
        out_specs=[seq, seq, vec, vec],
        out_shape=[jax.ShapeDtypeStruct((b, nc, w), F32)] * 2
                  + [jax.ShapeDtypeStruct((b, 1, w), F32)] * 2,
        compiler_params=_cparams(("parallel", "parallel")),
    )(e_re, e_im, h0r, h0i, *consts)


def _s5_y_kernel(u_ref, hpr_ref, hpi_ref, m_ref, fre_ref, fim_ref, y_ref):
    for q in range(S5_NSUPER):
        uq = _s5_super_rows(u_ref, q)
        sl = slice(q * S5_SUPER_STATE, (q + 1) * S5_SUPER_STATE)
        yq = (jnp.dot(uq, m_ref[q], preferred_element_type=F32)
              + _bdot(hpr_ref[:, sl], fre_ref[q])
              + _bdot(hpi_ref[:, sl], fim_ref[q]))
        for t in range(S5_L):
            y_ref[:, t * S5_WIDTH + q * LANES:t * S5_WIDTH + (q + 1) * LANES] = (
                yq[:, t * LANES:(t + 1) * LANES])


def _s5_y(u2, hp_re, hp_im, tabs):
    rows, wu = u2.shape
    tr = min(S5_ROW_TILE, rows)
    consts = [tabs['m'], tabs['f_re'], tabs['f_im']]
    return pl.pallas_call(
        _s5_y_kernel, name="s5_y",
        grid=(rows // tr,),
        in_specs=[_row_spec(tr, wu), _row_spec(tr, S5_STATE_W), _row_spec(tr, S5_STATE_W)]
                 + [_const_spec(c) for c in consts],
        out_specs=_row_spec(tr, wu),
        out_shape=jax.ShapeDtypeStruct((rows, wu), F32),
        compiler_params=_cparams(("parallel",)),
    )(u2, hp_re, hp_im, *consts)


def _c_out_kernel(y2_ref, u_ref, x_ref, d_ref, wglu_ref, wout_ref, g_ref, beta_ref, o_ref, ys,
                  *, alpha):
    rows = y2_ref.shape[0]
    ntile = S5_WIDTH // LANES
    for s in range(S5_L):
        for c in range(ntile):
            lo = s * S5_WIDTH + c * LANES
            ys[c, pl.ds(s, rows, stride=S5_L), :] = y2_ref[:, lo:lo + LANES]
    y = jnp.concatenate([ys[c] for c in range(ntile)], axis=1) + d_ref[...] * u_ref[...]
    vg = _bdot(jax.nn.gelu(y), wglu_ref[...])
    half = vg.shape[1] // 2
    z = vg[:, :half] * jax.nn.sigmoid(vg[:, half:])
    o = _bdot(z, wout_ref[...])
    o_ref[...] = _layer_norm(alpha * x_ref[...] + o, g_ref[...], beta_ref[...])


def _c_out(y2, u, x, w, g, beta, alpha):
    n, dm = x.shape
    tm = min(ROW_TILE, n)
    consts = [w['d'], w['w_glu'], w['w_out'], g, beta]
    return pl.pallas_call(
        functools.partial(_c_out_kernel, alpha=alpha), name="c_out",
        grid=(n // tm,),
        in_specs=[_row_spec(tm // S5_L, y2.shape[1]), _row_spec(tm, u.shape[1]),
                  _row_spec(tm, dm)] + [_full_spec(c) for c in consts],
        out_specs=_row_spec(tm, dm),
        out_shape=jax.ShapeDtypeStruct((n, dm), F32),
        scratch_shapes=[pltpu.VMEM((u.shape[1] // LANES, tm, LANES), F32)],
        compiler_params=_cparams(("parallel",)),
    )(y2, u, x, *consts)


def _prep_ab(w_in, q_g, kv_g, w_uq, w_ukv, conv_w, conv_b, wa, ba, wx, bx, a_param, w_out):
    dm = w_in.shape[0]
    s0 = Q_LORA
    s1 = s0 + KV_LORA
    s2 = s1 + QK_ROPE
    s3 = s2 + LRU_WIDTH
    pad_lo = jnp.zeros((dm, QK_NOPE), F32)
    pad_hi = jnp.zeros((dm, HEAD_PAD - QK_HEAD), F32)
    w_in_p = jnp.concatenate([w_in[:, :s1], w_in[:, s2:s3], w_in[:, s3:],
                              pad_lo, w_in[:, s1:s2], pad_hi], axis=1).astype(BF16)
    wq = w_uq.reshape(Q_LORA, MLA_HEADS, QK_HEAD)
    wq = jnp.pad(wq, ((0, 0), (0, 0), (0, HEAD_PAD - QK_HEAD)))
    wq = wq.reshape(Q_LORA, MLA_HEADS * HEAD_PAD).astype(BF16)
    wkv = w_ukv.reshape(KV_LORA, MLA_HEADS, QK_NOPE + V_HEAD)
    wk = jnp.pad(wkv[:, :, :QK_NOPE], ((0, 0), (0, 0), (0, HEAD_PAD - QK_NOPE)))
    wk = wk.reshape(KV_LORA, MLA_HEADS * HEAD_PAD).astype(BF16)
    wv = jnp.pad(wkv[:, :, QK_NOPE:], ((0, 0), (0, 0), (0, HEAD_PAD - V_HEAD)))
    wv = wv.reshape(KV_LORA, MLA_HEADS * HEAD_PAD).astype(BF16)
    wukt = jnp.transpose(wkv[:, :, :QK_NOPE], (1, 2, 0)).astype(BF16)
    wuv_h = jnp.transpose(wkv[:, :, QK_NOPE:], (1, 0, 2)).astype(BF16)
    eye = jnp.eye(LRU_BLOCKS, dtype=F32)

    def blockdiag(wb):
        return jnp.einsum('hij,hk->hikj', wb, eye).reshape(LRU_WIDTH, LRU_WIDTH)

    w_ax = jnp.concatenate([blockdiag(wa), blockdiag(wx)], axis=1).astype(BF16)
    b_ax = jnp.concatenate([ba, bx])[None, :]
    att_w = MLA_HEADS * V_HEAD
    return dict(w_in=w_in_p, q_g=q_g[None, :], kv_g=kv_g[None, :], w_uq=wq, w_uk=wk, w_uv=wv,
                wukt=wukt, wuv_h=wuv_h, conv_w=conv_w, conv_b=conv_b[None, :], w_ax=w_ax,
                b_ax=b_ax, a_param=jax.nn.softplus(-a_param)[None, :],
                w_out_att=w_out[:att_w].astype(BF16), w_out_rec=w_out[att_w:].astype(BF16))


def _rope_tables(row_pos, tile_pos):
    half = QK_ROPE // 2
    inv = ROPE_THETA ** (-jnp.arange(half, dtype=F32) / half)
    inv_lane = jnp.concatenate([jnp.zeros((QK_NOPE,), F32), inv, inv,
                                jnp.zeros((HEAD_PAD - QK_HEAD,), F32)])
    ang_row = row_pos.astype(F32)[:, None] * inv_lane[None, :]
    ang_tile = tile_pos.astype(F32)[:, None, None] * inv_lane
    return jnp.cos(ang_row), jnp.sin(ang_row), jnp.cos(ang_tile), jnp.sin(ang_tile)


def _prep_router(w_group, w_expert):
    dm = w_group.shape[0]
    pad = jnp.zeros((dm, ROUTER_W - N_GROUPS - N_EXPERTS), F32)
    w_router = jnp.concatenate([w_group, w_expert, pad], axis=1)
    w_router_hi = w_router.astype(BF16)
    w_router_lo = (w_router - w_router_hi.astype(F32)).astype(BF16)
    return dict(w_router=jnp.concatenate([w_router_hi, w_router_lo], axis=1))


def _cmul(ar, ai, br, bi):
    return ar * br - ai * bi, ar * bi + ai * br


def _prep_s5(lam_re, lam_im, log_step, b_re, b_im, c_re, c_im):
    hi = lax.Precision.HIGHEST
    L, G, P, C = S5_L, S5_GROUPS, S5_STATE, S5_GROUP
    lr, li = lam_re, lam_im
    dt = jnp.exp(log_step)[:, None]
    mag = jnp.exp(lr * dt)
    ar, ai = mag * jnp.cos(li * dt), mag * jnp.sin(li * dt)
    den = lr * lr + li * li
    nr = ar - 1.0
    cr = (nr * lr + ai * li) / den
    ci = (ai * lr - nr * li) / den
    bbr = cr[..., None] * b_re - ci[..., None] * b_im
    bbi = cr[..., None] * b_im + ci[..., None] * b_re
    prs, pis = [jnp.ones_like(ar)], [jnp.zeros_like(ar)]
    for _ in range(L):
        nr_, ni_ = _cmul(prs[-1], pis[-1], ar, ai)
        prs.append(nr_)
        pis.append(ni_)
    pr = jnp.stack(prs)
    pi = jnp.stack(pis)
    xr = pr[:L, :, :, None] * bbr - pi[:L, :, :, None] * bbi
    xi = pr[:L, :, :, None] * bbi + pi[:L, :, :, None] * bbr
    kk = (jnp.einsum('gcp,kgpd->kgcd', c_re, xr, precision=hi)
          - jnp.einsum('gcp,kgpd->kgcd', c_im, xi, precision=hi))
    p1r = pr[1:, :, :, None]
    p1i = pi[1:, :, :, None]
    cre_t = jnp.transpose(c_re, (0, 2, 1))[None]
    cim_t = jnp.transpose(c_im, (0, 2, 1))[None]
    f_re = cre_t * p1r - cim_t * p1i
    f_im = -cre_t * p1i - cim_t * p1r

    nq, sg = S5_NSUPER, S5_SUPER
    eye = jnp.eye(sg, dtype=F32)
    dk = jnp.einsum('kqgcd,gh->kqgdhc', kk.reshape(L, nq, sg, C, C), eye)
    dk = dk.reshape(L, nq, LANES, LANES).astype(BF16)
    lag = jnp.arange(L)[None, :] - jnp.arange(L)[:, None]
    m = jnp.where((lag >= 0)[:, :, None, None, None], dk[jnp.clip(lag, 0, L - 1)], 0)
    m = jnp.transpose(m, (2, 0, 3, 1, 4)).reshape(nq, L * LANES, L * LANES)

    row_g = (jnp.arange(L * LANES) // C) % sg
    col_g = jnp.arange(sg * P) // P
    same_group = (row_g[:, None] == col_g[None, :])[None]

    def pack_rows(a):
        a = jnp.transpose(a.reshape(L, nq, sg, P, C), (1, 0, 4, 2, 3))
        a = a.reshape(nq, L, 1, C, sg * P).astype(BF16)
        a = jnp.broadcast_to(a, (nq, L, sg, C, sg * P)).reshape(nq, L * LANES, sg * P)
        return jnp.where(same_group, a, 0)

    def pack_e(x):
        return pack_rows(x[::-1])

    def pack_f(f):
        return jnp.swapaxes(pack_rows(f), 1, 2)

    alr, ali = pr[L].reshape(1, G * P), pi[L].reshape(1, G * P)
    qr, qi = [alr], [ali]
    for _ in range(SUBLANES - 1):
        nr_, ni_ = _cmul(qr[-1], qi[-1], alr, ali)
        qr.append(nr_)
        qi.append(ni_)
    apow_re = jnp.concatenate(qr, axis=0)
    apow_im = jnp.concatenate(qi, axis=0)
    step_re = jnp.concatenate([qr[0], qr[1], qr[3]], axis=0)
    step_im = jnp.concatenate([qi[0], qi[1], qi[3]], axis=0)
    return dict(m=m, e_re=pack_e(xr), e_im=pack_e(xi), f_re=pack_f(f_re),
                f_im=pack_f(f_im), apow_re=apow_re, apow_im=apow_im,
                step_re=step_re, step_im=step_im)


def _mixer_ab(x, pos0, past_ckv, past_krope, conv0, h0, w):
    b, t, dm = x.shape
    xf = x.reshape(b * t, dm)
    tm = min(ROW_TILE, b * t)
    assert tm % t == 0 or t % tm == 0
    row_pos = jnp.arange(tm) % t
    tile_pos = pos0 + (jnp.arange(b * t // tm) * tm) % t
    q, k, v, c_new, kr_new, xb, yb = _ab_in(xf, w, _rope_tables(row_pos, tile_pos))
    if past_ckv is None:
        att = _attn_prompt(q, k, v)
    else:
        att = _attn_sample(q, c_new, kr_new, past_ckv, past_krope, w['wukt'], w['wuv_h'])
    rec, conv_new, h_new = _lru(xb.reshape(b, t, -1), yb.reshape(b, t, -1), conv0,
                                h0[:, None, :], w)
    return (att, rec.reshape(b * t, -1), c_new.reshape(b, t, -1), kr_new.reshape(b, t, -1),
            conv_new, h_new[:, 0, :])


def _mixer_c(x, h0_re, h0_im, w):
    b, t, dm = x.shape
    xf = x.reshape(b * t, dm)
    nc = t // S5_L
    assert t % S5_L == 0 and nc % SUBLANES == 0
    u, u2 = _c_in(xf, w['w_in'])
    e_re, e_im = _s5_e(u2, w['tabs'])
    hp_re, hp_im, hf_re, hf_im = _s5_scan(
        e_re.reshape(b, nc, S5_STATE_W), e_im.reshape(b, nc, S5_STATE_W),
        h0_re.reshape(b, 1, S5_STATE_W), h0_im.reshape(b, 1, S5_STATE_W), w['tabs'], nc)
    y2 = _s5_y(u2, hp_re.reshape(b * nc, S5_STATE_W), hp_im.reshape(b * nc, S5_STATE_W),
               w['tabs'])
    return (y2, u, hf_re.reshape(b, S5_GROUPS, S5_STATE), hf_im.reshape(b, S5_GROUPS, S5_STATE))


def kernel(x_prompt, x_sample, cache_mla_ckv, cache_mla_krope, state_lru_conv, state_lru_h,
           state_s5_re, state_s5_im, w_in_ab, q_norm_g, kv_norm_g, w_uq, w_ukv,
           lru_conv_w, lru_conv_b, lru_w_a, lru_b_a, lru_w_x, lru_b_x, lru_a_param, w_out_ab,
           w_in_c, s5_lam_re, s5_lam_im, s5_log_step, s5_b_re, s5_b_im, s5_c_re, s5_c_im,
           s5_d, s5_w_glu, w_out_c, ln_mix_g, ln_mix_b, ln_ffn_g, ln_ffn_b,
           moe_w_group, moe_w_expert, moe_w_gate, moe_w_up, moe_w_down):
    bp, tp, dm = x_prompt.shape
    bs, ts, _ = x_sample.shape
    past = cache_mla_ckv.shape[2]
    depth = ln_mix_g.shape[0]
    alpha = (2 * depth) ** 0.25
    assert (past + ts - 1) // CHUNK <= past // CHUNK
    hp, hs = x_prompt, x_sample
    moe_experts = dict(w_gate=moe_w_gate.astype(BF16), w_up=moe_w_up.astype(BF16),
                       w_down=moe_w_down.astype(BF16))
    outs = {k: [] for k in ('ckv_p', 'ckv_s', 'kr_p', 'kr_s', 'cv_p', 'cv_s', 'lh_p', 'lh_s',
                            's5r_p', 's5r_s', 's5i_p', 's5i_s')}
    for layer in range(depth):
        j = layer // 2
        g_mix, b_mix = ln_mix_g[layer][None, :], ln_mix_b[layer][None, :]
        g_ffn, b_ffn = ln_ffn_g[layer][None, :], ln_ffn_b[layer][None, :]
        if layer % 2 == 0:
            w = _prep_ab(w_in_ab[j], q_norm_g[j], kv_norm_g[j], w_uq[j], w_ukv[j], lru_conv_w[j],
                         lru_conv_b[j], lru_w_a[j], lru_b_a[j], lru_w_x[j], lru_b_x[j],
                         lru_a_param[j], w_out_ab[j])
            att_p, rec_p, c1, k1, v1, h1 = _mixer_ab(
                hp, 0, None, None, jnp.zeros((bp, CONV_WIDTH - 1, LRU_WIDTH), F32),
                jnp.zeros((bp, LRU_WIDTH), F32), w)
            att_s, rec_s, c2, k2, v2, h2 = _mixer_ab(
                hs, past, cache_mla_ckv[j], cache_mla_krope[j], state_lru_conv[j],
                state_lru_h[j], w)
            outs['ckv_p'].append(c1); outs['ckv_s'].append(c2)
            outs['kr_p'].append(k1); outs['kr_s'].append(k2)
            outs['cv_p'].append(v1); outs['cv_s'].append(v2)
            outs['lh_p'].append(h1); outs['lh_s'].append(h2)
            hp = _out_ln(att_p, rec_p, hp.reshape(bp * tp, dm), w['w_out_att'], w['w_out_rec'],
                         g_mix, b_mix, alpha)
            hs = _out_ln(att_s, rec_s, hs.reshape(bs * ts, dm), w['w_out_att'], w['w_out_rec'],
                         g_mix, b_mix, alpha)
        else:
            w = dict(w_in=w_in_c[j].astype(BF16), d=s5_d[j][None, :],
                     w_glu=s5_w_glu[j].astype(BF16), w_out=w_out_c[j].astype(BF16),
                     tabs=_prep_s5(s5_lam_re[j], s5_lam_im[j], s5_log_step[j], s5_b_re[j],
                                   s5_b_im[j], s5_c_re[j], s5_c_im[j]))
            zero_state = jnp.zeros((bp, S5_GROUPS, S5_STATE), F32)
            y_p, u_p, r1, i1 = _mixer_c(hp, zero_state, zero_state, w)
            y_s, u_s, r2, i2 = _mixer_c(hs, state_s5_re[j], state_s5_im[j], w)
            outs['s5r_p'].append(r1); outs['s5r_s'].append(r2)
            outs['s5i_p'].append(i1); outs['s5i_s'].append(i2)
            hp = _c_out(y_p, u_p, hp.reshape(bp * tp, dm), w, g_mix, b_mix, alpha)
            hs = _c_out(y_s, u_s, hs.reshape(bs * ts, dm), w, g_mix, b_mix, alpha)
        wm = dict(moe_experts, **_prep_router(moe_w_group[layer], moe_w_expert[layer]))
        hp = _moe(hp, wm, layer, g_ffn, b_ffn, alpha).reshape(bp, tp, dm)
        hs = _moe(hs, wm, layer, g_ffn, b_ffn, alpha).reshape(bs, ts, dm)
    st = lambda k: jnp.stack(outs[k])
    return (hp, hs, st('ckv_p'), st('ckv_s'), st('kr_p'), st('kr_s'), st('cv_p'), st('cv_s'),
            st('lh_p'), st('lh_s'), st('s5r_p'), st('s5r_s'), st('s5i_p'), st('s5i_s'))
```

```python
import functools
import math

import jax
import jax.numpy as jnp
from jax import lax
from jax.experimental import pallas as pl
from jax.experimental.pallas import tpu as pltpu
from jax.experimental.pallas import tpu_sc as plsc

F32 = jnp.float32
BF16 = jnp.bfloat16

CHUNK = 64
MLA_HEADS = 8
QK_NOPE = 64
QK_ROPE = 32
QK_HEAD = QK_NOPE + QK_ROPE
V_HEAD = 64
Q_LORA = 256
KV_LORA = 128
ROPE_THETA = 10000.0
MLA_SCALE = QK_HEAD ** -0.5
NEG_INF = -1e30
LRU_WIDTH = 512
LRU_BLOCKS = 8
LRU_BLOCK = LRU_WIDTH // LRU_BLOCKS
CONV_WIDTH = 4
RG_C = 8.0
S5_WIDTH = 512
S5_GROUP = 16
S5_GROUPS = S5_WIDTH // S5_GROUP
S5_STATE = 64
N_GROUPS = 4
EXPERTS_PER_GROUP = 4
N_EXPERTS = N_GROUPS * EXPERTS_PER_GROUP
EXPERT_HIDDEN = 256
LN_EPS = 1e-5
RMS_EPS = 1e-6

LANES = 128
SUBLANES = 8
VMEM_LIMIT_BYTES = 48 * 1024 * 1024
MOE_VMEM_LIMIT_BYTES = 56 * 1024 * 1024

HEAD_PAD = LANES
ROW_TILE = 512
MOE_ROW_TILE = 1024
MOE_EXPERT_TILE = 512
ATTN_TQ = 1024
ATTN_TK = 512
ATTN_UNROLL = 4
Q_SCALE = MLA_SCALE * math.log2(math.e)
LRU_TILE = 512
S5_L = 8
S5_SUPER = LANES // S5_GROUP
S5_NSUPER = S5_GROUPS // S5_SUPER
S5_SUPER_IN = S5_L * LANES
S5_SUPER_STATE = S5_SUPER * S5_STATE
S5_STATE_W = S5_GROUPS * S5_STATE
S5_SCAN_LANES = 256
S5_ROW_TILE = 256
ROUTER_W = LANES


def _cparams(semantics, vmem_limit_bytes=VMEM_LIMIT_BYTES):
    return pltpu.CompilerParams(dimension_semantics=semantics,
                                vmem_limit_bytes=vmem_limit_bytes)


def _full_spec(a):
    nd = a.ndim
    return pl.BlockSpec(a.shape, lambda *_: (0,) * nd)


def _row_spec(tm, width):
    return pl.BlockSpec((tm, width), lambda i: (i, 0))


def _layer_norm(z, g, b):
    mu = jnp.mean(z, axis=-1, keepdims=True)
    zc = z - mu
    var = jnp.mean(zc * zc, axis=-1, keepdims=True)
    return zc * lax.rsqrt(var + LN_EPS) * g + b


def _rms_norm(z, g):
    ms = jnp.mean(z * z, axis=-1, keepdims=True)
    return z * lax.rsqrt(ms + RMS_EPS) * g


def _bdot(a, b):
    return jnp.dot(a.astype(BF16), b.astype(BF16), preferred_element_type=F32)


def _ab_in_kernel(x_ref, win_ref, qg_ref, kvg_ref, wuq_ref, wuk_ref, wuv_ref,
                  cosb_ref, sinb_ref, cost_ref, sint_ref,
                  q_out, k_out, v_out, c_out, kr_out, xb_out, yb_out, *, dm):
    proj = _bdot(x_ref[...], win_ref[...])
    o0 = Q_LORA
    o1 = o0 + KV_LORA
    o2 = o1 + LRU_WIDTH
    o3 = o2 + LRU_WIDTH
    q_lat = proj[:, 0:o0]
    kv_lat = proj[:, o0:o1]
    xb_out[...] = proj[:, o1:o2]
    yb_out[...] = proj[:, o2:o3]
    krp = proj[:, o3:o3 + HEAD_PAD]

    qn = _rms_norm(q_lat, qg_ref[...])
    cn = _rms_norm(kv_lat, kvg_ref[...])
    c_out[...] = cn
    q = _bdot(qn, wuq_ref[...])
    kk = _bdot(cn, wuk_ref[...])
    lane = lax.broadcasted_iota(jnp.int32, (1, MLA_HEADS * HEAD_PAD), 1)
    ones_col = (lane % HEAD_PAD == V_HEAD).astype(F32)
    v_out[...] = (_bdot(cn, wuv_ref[...]) + ones_col).astype(v_out.dtype)

    cb, sb = cosb_ref[...], sinb_ref[...]
    ct, st = cost_ref[...], sint_ref[...]
    cos = cb * ct - sb * st
    sin = sb * ct + cb * st
    half = QK_ROPE // 2
    lane1 = lax.broadcasted_iota(jnp.int32, (1, HEAD_PAD), 1)
    rc = jnp.where(lane1 < QK_HEAD, cos, 0.0)
    ra = jnp.where((lane1 >= QK_NOPE + half) & (lane1 < QK_HEAD), sin, 0.0)
    rb = jnp.where((lane1 >= QK_NOPE) & (lane1 < QK_NOPE + half), -sin, 0.0)

    def rope(z):
        return (z * rc + pltpu.roll(z, half, 1) * ra
                + pltpu.roll(z, HEAD_PAD - half, 1) * rb)

    kr = rope(krp)
    kr_out[...] = kr[:, QK_NOPE:QK_NOPE + QK_ROPE]
    for h in range(MLA_HEADS):
        sl = slice(h * HEAD_PAD, (h + 1) * HEAD_PAD)
        q_out[:, sl] = (rope(q[:, sl]) * Q_SCALE).astype(q_out.dtype)
        k_out[:, sl] = (kk[:, sl] + kr).astype(k_out.dtype)


def _ab_in(x, w, rope_tabs):
    n, dm = x.shape
    tm = min(ROW_TILE, n)
    cosb, sinb, cost, sint = rope_tabs
    consts = [w['w_in'], w['q_g'], w['kv_g'], w['w_uq'], w['w_uk'], w['w_uv'], cosb, sinb]
    tile_spec = pl.BlockSpec((None, 1, HEAD_PAD), lambda i: (i, 0, 0))
    hp = MLA_HEADS * HEAD_PAD
    out_shape = [
        jax.ShapeDtypeStruct((n, hp), BF16),
        jax.ShapeDtypeStruct((n, hp), BF16),
        jax.ShapeDtypeStruct((n, hp), BF16),
        jax.ShapeDtypeStruct((n, KV_LORA), F32),
        jax.ShapeDtypeStruct((n, QK_ROPE), F32),
        jax.ShapeDtypeStruct((n, LRU_WIDTH), F32),
        jax.ShapeDtypeStruct((n, LRU_WIDTH), F32),
    ]
    return pl.pallas_call(
        functools.partial(_ab_in_kernel, dm=dm), name="ab_in",
        grid=(n // tm,),
        in_specs=([_row_spec(tm, dm)] + [_full_spec(c) for c in consts]
                  + [tile_spec, tile_spec]),
        out_specs=[_row_spec(tm, s.shape[1]) for s in out_shape],
        out_shape=out_shape,
        compiler_params=_cparams(("parallel",)),
    )(x, *consts, cost, sint)


def _attn_prompt_kernel(q_ref, k_ref, v_ref, o_ref):
    i = pl.program_id(1)
    tq, tk = ATTN_TQ, ATTN_TK
    ndiag = tq // tk
    nfull = i * ndiag
    row_chunk = lax.broadcasted_iota(jnp.int32, (tq, tk), 0) // CHUNK
    col_chunk = lax.broadcasted_iota(jnp.int32, (tq, tk), 1) // CHUNK
    qs = [q_ref[:, hh * HEAD_PAD:(hh + 1) * HEAD_PAD] for hh in range(2)]

    def update(hh, j, m, acc, mask, row0=0):
        start = pl.multiple_of(j * tk, tk)
        k = k_ref[pl.ds(start, tk), hh * HEAD_PAD:(hh + 1) * HEAD_PAD]
        v = v_ref[pl.ds(start, tk), hh * HEAD_PAD:(hh + 1) * HEAD_PAD]
        s = lax.dot_general(qs[hh][row0:], k, (((1,), (1,)), ((), ())),
                            preferred_element_type=F32)
        if mask is not None:
            s = jnp.where(mask[row0:], s, NEG_INF)
        m_new = jnp.maximum(m, jnp.max(s, axis=-1, keepdims=True))
        alpha = jnp.exp2(m - m_new)
        p = jnp.exp2((s - m_new).astype(BF16))
        return m_new, alpha * acc + jnp.dot(p, v, preferred_element_type=F32)

    def body(j, carry):
        new = []
        for hh in range(2):
            new += update(hh, j, carry[2 * hh], carry[2 * hh + 1], None)
        return tuple(new)

    unroll = ATTN_UNROLL

    def body_unrolled(jj, carry):
        for b in range(unroll):
            carry = body(unroll * jj + b, carry)
        return carry

    init = []
    for _ in range(2):
        init += [jnp.full((tq, 1), NEG_INF, F32), jnp.zeros((tq, HEAD_PAD), F32)]
    carry = lax.fori_loop(0, nfull // unroll, body_unrolled, tuple(init))
    rem0 = nfull - nfull % unroll

    def body_rem(jj, carry):
        for b in range(ndiag):
            carry = body(rem0 + ndiag * jj + b, carry)
        return carry

    carry = list(lax.fori_loop(0, (nfull % unroll) // ndiag, body_rem, carry))
    for d in range(ndiag):
        visible = col_chunk + d * (tk // CHUNK) <= row_chunk
        r0 = d * tk
        for hh in range(2):
            m, acc = carry[2 * hh], carry[2 * hh + 1]
            m_new, acc_new = update(hh, nfull + d, m[r0:], acc[r0:], visible, r0)
            if r0:
                m_new = jnp.concatenate([m[:r0], m_new], axis=0)
                acc_new = jnp.concatenate([acc[:r0], acc_new], axis=0)
            carry[2 * hh], carry[2 * hh + 1] = m_new, acc_new
    outs = [carry[2 * hh + 1][:, :V_HEAD] / carry[2 * hh + 1][:, V_HEAD:V_HEAD + 1]
            for hh in range(2)]
    o_ref[...] = jnp.concatenate(outs, axis=-1).astype(o_ref.dtype)


def _attn_prompt(q, k, v):
    t = q.shape[0]
    pairs = MLA_HEADS // 2
    return pl.pallas_call(
        _attn_prompt_kernel, name="attn_prompt",
        grid=(pairs, t // ATTN_TQ),
        in_specs=[
            pl.BlockSpec((ATTN_TQ, 2 * HEAD_PAD), lambda p, i: (i, p)),
            pl.BlockSpec((t, 2 * HEAD_PAD), lambda p, i: (0, p)),
            pl.BlockSpec((t, 2 * HEAD_PAD), lambda p, i: (0, p)),
        ],
        out_specs=pl.BlockSpec((ATTN_TQ, 2 * V_HEAD), lambda p, i: (i, p)),
        out_shape=jax.ShapeDtypeStruct((t, MLA_HEADS * V_HEAD), BF16),
        compiler_params=_cparams(("parallel", "parallel")),
    )(q, k, v)


def _attn_sample_kernel(q_ref, cn_ref, krn_ref, cp_ref, krp_ref, wukt_ref, wuv_ref, o_ref):
    cp = cp_ref[...].astype(BF16)
    krp = krp_ref[...].astype(BF16)
    cn = cn_ref[...].astype(BF16)
    krn = krn_ref[...].astype(BF16)
    ts = cn.shape[0]
    dn = (((1,), (1,)), ((), ()))
    outs = []
    for hp in range(MLA_HEADS // 2):
        qa, qr = [], []
        for hh in range(2):
            h = 2 * hp + hh
            qh = q_ref[:, h * HEAD_PAD:(h + 1) * HEAD_PAD]
            qa.append(jnp.dot(qh[:, :QK_NOPE], wukt_ref[h], preferred_element_type=F32))
            qr.append(qh[:, QK_NOPE:QK_NOPE + QK_ROPE])
        qa = jnp.concatenate(qa, axis=0).astype(BF16)
        qr = jnp.concatenate(qr, axis=0)
        s_past = (lax.dot_general(qa, cp, dn, preferred_element_type=F32)
                  + lax.dot_general(qr, krp, dn, preferred_element_type=F32))
        s_new = (lax.dot_general(qa, cn, dn, preferred_element_type=F32)
                 + lax.dot_general(qr, krn, dn, preferred_element_type=F32))
        m = jnp.maximum(jnp.max(s_past, axis=-1, keepdims=True),
                        jnp.max(s_new, axis=-1, keepdims=True))
        p_past = jnp.exp2(s_past - m)
        p_new = jnp.exp2(s_new - m)
        l = (jnp.sum(p_past, axis=-1, keepdims=True)
             + jnp.sum(p_new, axis=-1, keepdims=True))
        o_lat = (jnp.dot(p_past.astype(BF16), cp, preferred_element_type=F32)
                 + jnp.dot(p_new.astype(BF16), cn, preferred_element_type=F32)) / l
        for hh in range(2):
            h = 2 * hp + hh
            outs.append(_bdot(o_lat[hh * ts:(hh + 1) * ts], wuv_ref[h]))
    o_ref[...] = jnp.concatenate(outs, axis=-1).astype(o_ref.dtype)


def _attn_sample(q, c_new, kr_new, c_past, kr_past, wukt, wuv):
    bs, past, _ = c_past.shape
    ts = q.shape[0] // bs
    return pl.pallas_call(
        _attn_sample_kernel, name="attn_sample",
        grid=(bs,),
        in_specs=[
            pl.BlockSpec((ts, MLA_HEADS * HEAD_PAD), lambda b: (b, 0)),
            pl.BlockSpec((ts, KV_LORA), lambda b: (b, 0)),
            pl.BlockSpec((ts, QK_ROPE), lambda b: (b, 0)),
            pl.BlockSpec((None, past, KV_LORA), lambda b: (b, 0, 0)),
            pl.BlockSpec((None, past, QK_ROPE), lambda b: (b, 0, 0)),
            _full_spec(wukt), _full_spec(wuv),
        ],
        out_specs=pl.BlockSpec((ts, MLA_HEADS * V_HEAD), lambda b: (b, 0)),
        out_shape=jax.ShapeDtypeStruct((bs * ts, MLA_HEADS * V_HEAD), BF16),
        compiler_params=_cparams(("parallel",)),
    )(q, c_new, kr_new, c_past, kr_past, wukt, wuv)


def _lru_kernel(xb_ref, yb_ref, conv0_ref, h0_ref, cw_ref, cb_ref, wax_ref, bax_ref, ap_ref,
                rec_ref, conv_out_ref, h_out_ref, xcat, hc, *, tb):
    t = pl.program_id(1)
    nt = pl.num_programs(1)
    tail = CONV_WIDTH - 1
    base = SUBLANES

    @pl.when(t == 0)
    def _():
        xcat[base - tail:base, :] = conv0_ref[...]
        hc[...] = h0_ref[...]

    @pl.when(t > 0)
    def _():
        xcat[0:base, :] = xcat[tb:tb + base, :]

    xcat[base:base + tb, :] = xb_ref[...]
    xc = cb_ref[...]
    for tap in range(CONV_WIDTH):
        xc = xc + xcat[base - tail + tap:base - tail + tap + tb, :] * cw_ref[tap:tap + 1, :]

    gates = _bdot(xc, wax_ref[...]) + bax_ref[...]
    r = jax.nn.sigmoid(gates[:, :LRU_WIDTH])
    ig = jax.nn.sigmoid(gates[:, LRU_WIDTH:])
    log_a = -RG_C * r * ap_ref[...]
    a = jnp.exp(log_a)
    u = jnp.sqrt(jnp.tanh(-log_a) * (1.0 + a * a)) * (ig * xc)

    row8 = lax.broadcasted_iota(jnp.int32, (tb, LRU_WIDTH), 0) % SUBLANES
    s = 1
    while s < SUBLANES:
        keep = row8 >= s
        u = jnp.where(keep, a * pltpu.roll(u, s, 0) + u, u)
        a = jnp.where(keep, a * pltpu.roll(a, s, 0), a)
        s *= 2
    hprev = hc[...]
    groups = []
    for g in range(tb // SUBLANES):
        sl = slice(g * SUBLANES, (g + 1) * SUBLANES)
        hg = a[sl] * hprev + u[sl]
        groups.append(hg)
        hprev = hg[SUBLANES - 1:SUBLANES, :]
    h = jnp.concatenate(groups, axis=0)
    hc[...] = hprev
    rec_ref[...] = (h * jax.nn.gelu(yb_ref[...])).astype(rec_ref.dtype)

    @pl.when(t == nt - 1)
    def _():
        conv_out_ref[...] = xcat[base + tb - tail:base + tb, :]
        h_out_ref[...] = h[tb - 1:tb, :]


def _lru(xb, yb, conv0, h0, w):
    b, t, wd = xb.shape
    tb = min(LRU_TILE, t)
    tail = CONV_WIDTH - 1
    consts = [w['conv_w'], w['conv_b'], w['w_ax'], w['b_ax'], w['a_param']]
    seq = pl.BlockSpec((None, tb, wd), lambda bi, ti: (bi, ti, 0))
    return pl.pallas_call(
        functools.partial(_lru_kernel, tb=tb), name="rg_lru",
        grid=(b, t // tb),
        in_specs=[seq, seq,
                  pl.BlockSpec((None, tail, wd), lambda bi, ti: (bi, 0, 0)),
                  pl.BlockSpec((None, 1, wd), lambda bi, ti: (bi, 0, 0))]
                 + [pl.BlockSpec(c.shape, lambda bi, ti: (0, 0)) for c in consts],
        out_specs=[seq,
                   pl.BlockSpec((None, tail, wd), lambda bi, ti: (bi, 0, 0)),
                   pl.BlockSpec((None, 1, wd), lambda bi, ti: (bi, 0, 0))],
        out_shape=[jax.ShapeDtypeStruct((b, t, wd), BF16),
                   jax.ShapeDtypeStruct((b, tail, wd), F32),
                   jax.ShapeDtypeStruct((b, 1, wd), F32)],
        scratch_shapes=[pltpu.VMEM((tb + 2 * SUBLANES, wd), F32), pltpu.VMEM((1, wd), F32)],
        compiler_params=_cparams(("parallel", "arbitrary")),
    )(xb, yb, conv0, h0, *consts)


def _out_ln_value(a_ref, b_ref, x_ref, wa_ref, wb_ref, g_ref, beta_ref, *, alpha):
    y = (jnp.dot(a_ref[...], wa_ref[...], preferred_element_type=F32)
         + jnp.dot(b_ref[...], wb_ref[...], preferred_element_type=F32))
    return _layer_norm(alpha * x_ref[...] + y, g_ref[...], beta_ref[...])


def _out_ln_stage(att, rec, x, wa, wb, g, beta, alpha):
    n, dm = x.shape
    tm = min(ROW_TILE, n)
    return (functools.partial(_out_ln_value, alpha=alpha), [(att, tm), (rec, tm), (x, tm)],
            [wa, wb, g, beta], [], n, dm)


def _stage_kernel(*refs, n_in, value_fn):
    refs[n_in][...] = value_fn(*refs[:n_in], *refs[n_in + 1:])


def _stage_call(stage, name):
    value_fn, row_inputs, consts, scratch, n, dm = stage
    tm = min(ROW_TILE, n)
    arrays = [a for a, _ in row_inputs] + list(consts)
    return pl.pallas_call(
        functools.partial(_stage_kernel, n_in=len(arrays), value_fn=value_fn), name=name,
        grid=(n // tm,),
        in_specs=[_row_spec(rows, a.shape[1]) for a, rows in row_inputs]
                 + [_full_spec(c) for c in consts],
        out_specs=_row_spec(tm, dm),
        out_shape=jax.ShapeDtypeStruct((n, dm), F32),
        scratch_shapes=list(scratch),
        compiler_params=_cparams(("parallel",)),
    )(*arrays)


def _route(x, xh, wr_ref, lane):
    xl = (x - xh.astype(F32)).astype(BF16)
    hh_hl = jnp.dot(xh, wr_ref[...], preferred_element_type=F32)
    logits = (hh_hl[:, :ROUTER_W] + hh_hl[:, ROUTER_W:]
              + jnp.dot(xl, wr_ref[:, :ROUTER_W], preferred_element_type=F32))
    big = float(ROUTER_W)
    lg = jnp.where(lane < N_GROUPS, logits, -jnp.inf)
    mg = jnp.max(lg, axis=-1, keepdims=True)
    gi = jnp.min(jnp.where(lg == mg, lane, big), axis=-1, keepdims=True)
    p_top = 1.0 / jnp.sum(jnp.exp(lg - mg), axis=-1, keepdims=True)
    lo = N_GROUPS + gi * EXPERTS_PER_GROUP
    le = jnp.where((lane >= lo) & (lane < lo + EXPERTS_PER_GROUP), logits, -jnp.inf)
    v1 = jnp.max(le, axis=-1, keepdims=True)
    i1 = jnp.min(jnp.where(le == v1, lane, big), axis=-1, keepdims=True)
    le2 = jnp.where(lane == i1, -jnp.inf, le)
    v2 = jnp.max(le2, axis=-1, keepdims=True)
    i2 = jnp.min(jnp.where(le2 == v2, lane, big), axis=-1, keepdims=True)
    e2 = jnp.exp(v2 - v1)
    return gi, i1, i2, p_top / (1.0 + e2), p_top * e2 / (1.0 + e2)


def _moe_ln_kernel(x_ref, wr_ref, wg_ref, wu_ref, wd_ref, g_ref, beta_ref, o_ref,
                   acc, gates, xb16, *, alpha):
    grp = pl.program_id(1)
    tm = x_ref.shape[0]
    lane = lax.broadcasted_iota(jnp.int32, (tm, ROUTER_W), 1).astype(F32)

    @pl.when(grp == 0)
    def _():
        x = x_ref[...]
        xh = x.astype(BF16)
        xb16[...] = xh
        _, i1, i2, w1, w2 = _route(x, xh, wr_ref, lane)
        gates[...] = jnp.where(lane == i1, w1, 0.0) + jnp.where(lane == i2, w2, 0.0)
        acc[...] = jnp.zeros_like(acc)

    xb = xb16[...]
    gt = gates[...]
    first = (N_GROUPS + grp * EXPERTS_PER_GROUP).astype(F32)
    for el in range(EXPERTS_PER_GROUP):
        ge = jnp.sum(jnp.where(lane == first + el, gt, 0.0), axis=-1, keepdims=True)
        h = jnp.dot(xb, wg_ref[el], preferred_element_type=F32)
        u = jnp.dot(xb, wu_ref[el], preferred_element_type=F32)
        act = (jax.nn.silu(h) * u * ge).astype(BF16)
        acc[...] += jnp.dot(act, wd_ref[el], preferred_element_type=F32)

    @pl.when(grp == pl.num_programs(1) - 1)
    def _():
        o_ref[...] = _layer_norm(alpha * x_ref[...] + acc[...], g_ref[...], beta_ref[...])


def _moe_ln(x, w, layer, g, beta, alpha):
    n, dm = x.shape
    tm = min(MOE_ROW_TILE, n)
    _, ne, _, eh = w['w_gate'].shape
    epg = EXPERTS_PER_GROUP
    return pl.pallas_call(
        functools.partial(_moe_ln_kernel, alpha=alpha), name="moe_ln",
        grid=(n // tm, ne // epg),
        in_specs=[
            pl.BlockSpec((tm, dm), lambda i, e: (i, 0)),
            pl.BlockSpec(w['w_router'].shape, lambda i, e: (0, 0)),
            pl.BlockSpec((None, epg, dm, eh), lambda i, e: (layer, e, 0, 0)),
            pl.BlockSpec((None, epg, dm, eh), lambda i, e: (layer, e, 0, 0)),
            pl.BlockSpec((None, epg, eh, dm), lambda i, e: (layer, e, 0, 0)),
            pl.BlockSpec(g.shape, lambda i, e: (0, 0)),
            pl.BlockSpec(beta.shape, lambda i, e: (0, 0)),
        ],
        out_specs=pl.BlockSpec((tm, dm), lambda i, e: (i, 0)),
        out_shape=jax.ShapeDtypeStruct((n, dm), F32),
        scratch_shapes=[pltpu.VMEM((tm, dm), F32), pltpu.VMEM((tm, ROUTER_W), F32),
                        pltpu.VMEM((tm, dm), BF16)],
        compiler_params=_cparams(("parallel", "arbitrary"), MOE_VMEM_LIMIT_BYTES),
    )(x, w['w_router'], w['w_gate'], w['w_up'], w['w_down'], g, beta)


META_W = LANES
META_GROUP = EXPERTS_PER_GROUP
META_RANK = EXPERTS_PER_GROUP + 1


def _moe_route_kernel(*refs, n_in, n_tiles, value_fn):
    in_refs = refs[:n_in]
    wr_ref, tri_ref, xg_ref, meta_ref, cnt_ref, run = refs[n_in:n_in + 6]
    tm, dm = xg_ref.shape[0], xg_ref.shape[1] - META_W
    i = pl.program_id(0)
    lane = lax.broadcasted_iota(jnp.int32, (tm, ROUTER_W), 1).astype(F32)

    @pl.when(i == 0)
    def _():
        run[...] = jnp.zeros_like(run)

    @pl.when(i < n_tiles)
    def _():
        x = value_fn(*in_refs, *refs[n_in + 6:])
        gi, i1, i2, w1, w2 = _route(x, x.astype(BF16), wr_ref, lane)
        lo = N_GROUPS + gi * EXPERTS_PER_GROUP
        in_group = lane == gi
        incl = jnp.dot(tri_ref[...], in_group.astype(BF16), preferred_element_type=F32)
        rank = jnp.sum(jnp.where(in_group, incl + run[...], 0.0), axis=-1,
                       keepdims=True) - 1.0
        run[...] += incl[tm - 1:tm, :]
        cnt_ref[...] = run[...]
        meta = (jnp.where(lane == i1 - lo, w1, 0.0) + jnp.where(lane == i2 - lo, w2, 0.0)
                + jnp.where(lane == META_GROUP, gi, 0.0)
                + jnp.where(lane == META_RANK, rank, 0.0))
        xg_ref[:, :dm] = x
        xg_ref[:, dm:] = meta
        meta_ref[...] = meta

    @pl.when(i >= n_tiles)
    def _():
        xg_ref[...] = jnp.zeros_like(xg_ref)


def _moe_route(stage, w_router, p):
    value_fn, row_inputs, consts, scratch, n, dm = stage
    tm = ROW_TILE
    n_tiles = n // tm
    tri = jnp.tril(jnp.ones((tm, tm), BF16))
    last = lambda i: (jnp.minimum(i, n_tiles - 1), 0)
    arrays = [a for a, _ in row_inputs] + list(consts)
    in_specs = ([pl.BlockSpec((rows, a.shape[1]), last) for a, rows in row_inputs]
                + [_full_spec(c) for c in consts] + [_full_spec(w_router), _full_spec(tri)])
    return pl.pallas_call(
        functools.partial(_moe_route_kernel, n_in=len(arrays), n_tiles=n_tiles,
                          value_fn=value_fn), name="moe_route",
        grid=(p // tm,),
        in_specs=in_specs,
        out_specs=[_row_spec(tm, dm + META_W), pl.BlockSpec((tm, META_W), last),
                   pl.BlockSpec((1, ROUTER_W), lambda i: (0, 0))],
        out_shape=[jax.ShapeDtypeStruct((p, dm + META_W), F32),
                   jax.ShapeDtypeStruct((n, META_W), F32),
                   jax.ShapeDtypeStruct((1, ROUTER_W), F32)],
        scratch_shapes=[pltpu.VMEM((1, ROUTER_W), F32)] + list(scratch),
        compiler_params=_cparams(("arbitrary",)),
    )(*arrays, w_router, tri)


def _moe_experts_kernel(tg_ref, nu_ref, xs_ref, wg_ref, wu_ref, wd_ref, y_ref):
    del tg_ref
    dm = y_ref.shape[1]

    @pl.when(pl.program_id(0) < nu_ref[0])
    def _():
        xb = xs_ref[:, :dm].astype(BF16)
        meta = xs_ref[:, dm:]
        acc = None
        for el in range(EXPERTS_PER_GROUP):
            h = jnp.dot(xb, wg_ref[el], preferred_element_type=F32)
            u = jnp.dot(xb, wu_ref[el], preferred_element_type=F32)
            act = (jax.nn.silu(h) * u * meta[:, el:el + 1]).astype(BF16)
            d = jnp.dot(act, wd_ref[el], preferred_element_type=F32)
            acc = d if acc is None else acc + d
        y_ref[...] = acc


def _moe_experts(xs, tile_group, n_used, w, layer):
    p, wx = xs.shape
    dm = wx - META_W
    te = MOE_EXPERT_TILE
    _, ne, _, eh = w['w_gate'].shape
    epg = EXPERTS_PER_GROUP

    def tile(i, tg, nu):
        return jnp.minimum(i, nu[0] - 1)

    wspec = lambda shape: pl.BlockSpec(
        (None, epg) + shape, lambda i, tg, nu: (layer, tg[tile(i, tg, nu)], 0, 0))
    grid_spec = pltpu.PrefetchScalarGridSpec(
        num_scalar_prefetch=2,
        grid=(p // te,),
        in_specs=[pl.BlockSpec((te, wx), lambda i, tg, nu: (tile(i, tg, nu), 0)),
                  wspec((dm, eh)), wspec((dm, eh)), wspec((eh, dm))],
        out_specs=pl.BlockSpec((te, dm), lambda i, tg, nu: (tile(i, tg, nu), 0)),
    )
    return pl.pallas_call(
        _moe_experts_kernel, name="moe_experts",
        grid_spec=grid_spec,
        out_shape=jax.ShapeDtypeStruct((p, dm), F32),
        compiler_params=_cparams(("arbitrary",)),
    )(tile_group, n_used, xs, w['w_gate'], w['w_up'], w['w_down'])


def _res_ln_kernel(xg_ref, y_ref, g_ref, beta_ref, o_ref, *, alpha):
    o_ref[...] = _layer_norm(alpha * xg_ref[...] + y_ref[...], g_ref[...], beta_ref[...])


def _res_ln(xg, y, g, beta, alpha):
    n, dm = y.shape
    tm = ROW_TILE
    return pl.pallas_call(
        functools.partial(_res_ln_kernel, alpha=alpha), name="res_ln",
        grid=(n // tm,),
        in_specs=[_row_spec(tm, dm), _row_spec(tm, dm), _full_spec(g), _full_spec(beta)],
        out_specs=_row_spec(tm, dm),
        out_shape=jax.ShapeDtypeStruct((n, dm), F32),
        compiler_params=_cparams(("parallel",)),
    )(xg, y, g, beta)


def _moe(stage, stage_name, w, layer, g, beta, alpha):
    n = stage[4]
    granule = 2 * SC_GATHER_ROWS * SC_CORES * SC_SUBCORES
    if n % granule == 0 and (N_GROUPS * MOE_EXPERT_TILE) % granule == 0:
        return _moe_routed_ln(stage, w, layer, g, beta, alpha)
    return _moe_ln(_stage_call(stage, stage_name), w, layer, g, beta, alpha)


def _moe_routed_ln(stage, w, layer, g, beta, alpha):
    n, dm = stage[4], stage[5]
    te = MOE_EXPERT_TILE
    npad = N_GROUPS * te
    p = n + npad
    xg, meta, cnt = _moe_route(stage, w['w_router'], p)
    gid = meta[:, META_GROUP].astype(jnp.int32)
    rank = meta[:, META_RANK].astype(jnp.int32)
    counts = cnt[0, :N_GROUPS].astype(jnp.int32)
    padded = (counts + te - 1) // te * te
    ends = jnp.cumsum(padded)
    starts = ends - padded

    def lookup(table, idx):
        hit = idx[:, None] == jnp.arange(table.shape[0], dtype=jnp.int32)[None, :]
        return jnp.sum(jnp.where(hit, table[None, :], 0), axis=1)

    pos = lookup(starts, gid) + rank
    gap_len = jnp.concatenate([padded - counts, p - ends[-1:]])
    gap_start = jnp.concatenate([starts + counts, ends[-1:]])
    gap_end = jnp.cumsum(gap_len)
    k = jnp.arange(npad, dtype=jnp.int32)
    seg = jnp.sum(k[:, None] >= gap_end[None, :], axis=1)
    filler_pos = lookup(gap_start, seg) + k - lookup(gap_end - gap_len, seg)
    pos_all = jnp.concatenate([pos, filler_pos]).astype(jnp.int32)
    tile_start = jnp.arange(p // te, dtype=jnp.int32) * te
    tile_group = jnp.minimum(jnp.sum(tile_start[:, None] >= ends[None, :], axis=1),
                             N_GROUPS - 1).astype(jnp.int32)
    n_used = (ends[-1:] // te).astype(jnp.int32)
    xs = _sc_scatter_rows(xg, pos_all)
    ys = _moe_experts(xs, tile_group, n_used, w, layer)
    y = _sc_gather_rows(ys, pos)
    return _res_ln(xg, y, g, beta, alpha)


SC_CORES = 2
SC_SUBCORES = 16
SC_GATHER_ROWS = 32


def _sc_gather_rows(table, idx):
    b = idx.shape[0]
    _, d = table.shape
    nw = SC_CORES * SC_SUBCORES
    ch = SC_GATHER_ROWS
    per_w = b // nw
    assert b % (nw * 2 * ch) == 0
    npair = per_w // (2 * ch)
    mesh = plsc.VectorSubcoreMesh(core_axis_name="c", subcore_axis_name="s")

    @functools.partial(
        pl.kernel, mesh=mesh, out_type=jax.ShapeDtypeStruct((b, d), table.dtype),
        scratch_types=[pltpu.VMEM((2, ch), jnp.int32), pltpu.VMEM((2, ch, d), table.dtype),
                       pltpu.SemaphoreType.DMA, pltpu.SemaphoreType.DMA],
        name="sc_gather_rows")
    def gather(table_hbm, idx_hbm, out_hbm, idx_v, rows_v, sem0, sem1):
        sems = (sem0, sem1)
        wid = lax.axis_index("s") * SC_CORES + lax.axis_index("c")
        base = wid * per_w

        def start(chunk, slot):
            off = pl.multiple_of(base + chunk * ch, SUBLANES)
            pltpu.sync_copy(idx_hbm.at[pl.ds(off, ch)], idx_v.at[slot])
            pltpu.async_copy(table_hbm.at[idx_v.at[slot]], rows_v.at[slot], sems[slot])

        def finish(chunk, slot):
            off = pl.multiple_of(base + chunk * ch, SUBLANES)
            pltpu.make_async_copy(table_hbm.at[idx_v.at[slot]], rows_v.at[slot],
                                  sems[slot]).wait()
            pltpu.sync_copy(rows_v.at[slot], out_hbm.at[pl.ds(off, ch)])

        start(0, 0)

        @pl.loop(0, npair)
        def _(p):
            c0 = 2 * p
            start(c0 + 1, 1)
            finish(c0, 0)

            @pl.when(p + 1 < npair)
            def _():
                start(c0 + 2, 0)
            finish(c0 + 1, 1)

    return gather(table, idx)


def _sc_scatter_rows(rows, idx):
    b, d = rows.shape
    nw = SC_CORES * SC_SUBCORES
    ch = SC_GATHER_ROWS
    per_w = b // nw
    assert idx.shape == (b,) and b % (nw * 2 * ch) == 0
    npair = per_w // (2 * ch)
    mesh = plsc.VectorSubcoreMesh(core_axis_name="c", subcore_axis_name="s")

    @functools.partial(
        pl.kernel, mesh=mesh, out_type=jax.ShapeDtypeStruct((b, d), rows.dtype),
        scratch_types=[pltpu.VMEM((2, ch), jnp.int32), pltpu.VMEM((2, ch, d), rows.dtype),
                       pltpu.SemaphoreType.DMA, pltpu.SemaphoreType.DMA],
        name="sc_scatter_rows")
    def scatter(rows_hbm, idx_hbm, out_hbm, idx_v, rows_v, sem0, sem1):
        sems = (sem0, sem1)
        wid = lax.axis_index("s") * SC_CORES + lax.axis_index("c")
        base = wid * per_w

        def write(slot):
            return pltpu.make_async_copy(rows_v.at[slot], out_hbm.at[idx_v.at[slot]],
                                         sems[slot])

        @pl.loop(0, npair)
        def _(p):
            for slot in range(2):
                @pl.when(p > 0)
                def _():
                    write(slot).wait()
                off = pl.multiple_of(base + (2 * p + slot) * ch, SUBLANES)
                pltpu.sync_copy(idx_hbm.at[pl.ds(off, ch)], idx_v.at[slot])
                pltpu.sync_copy(rows_hbm.at[pl.ds(off, ch)], rows_v.at[slot])
                write(slot).start()

        write(0).wait()
        write(1).wait()

    return scatter(rows, idx)


def _c_in_kernel(x_ref, w_ref, u_ref, u2_ref, us):
    u = _bdot(x_ref[...], w_ref[...])
    u_ref[...] = u
    rows = u2_ref.shape[0]
    for c in range(S5_WIDTH // LANES):
        us[c] = u[:, c * LANES:(c + 1) * LANES]
    for s in range(S5_L):
        for c in range(S5_WIDTH // LANES):
            lo = s * S5_WIDTH + c * LANES
            u2_ref[:, lo:lo + LANES] = us[c, pl.ds(s, rows, stride=S5_L), :].astype(u2_ref.dtype)


def _c_in(x, w_in):
    n, dm = x.shape
    tm = min(ROW_TILE, n)
    wu = w_in.shape[1]
    return pl.pallas_call(
        _c_in_kernel, name="c_in",
        grid=(n // tm,),
        in_specs=[_row_spec(tm, dm), _full_spec(w_in)],
        out_specs=[_row_spec(tm, wu), _row_spec(tm // S5_L, S5_L * wu)],
        out_shape=[jax.ShapeDtypeStruct((n, wu), F32),
                   jax.ShapeDtypeStruct((n // S5_L, S5_L * wu), BF16)],
        scratch_shapes=[pltpu.VMEM((wu // LANES, tm, LANES), F32)],
        compiler_params=_cparams(("parallel",)),
    )(x, w_in)


def _s5_super_rows(u_ref, q):
    tiles = [u_ref[:, s * S5_WIDTH + q * LANES:s * S5_WIDTH + (q + 1) * LANES]
             for s in range(S5_L)]
    return jnp.concatenate(tiles, axis=1).astype(BF16)


def _s5_e_kernel(u_ref, ere_ref, eim_ref, ere_out, eim_out):
    for q in range(S5_NSUPER):
        uq = _s5_super_rows(u_ref, q)
        sl = slice(q * S5_SUPER_STATE, (q + 1) * S5_SUPER_STATE)
        ere_out[:, sl] = jnp.dot(uq, ere_ref[q], preferred_element_type=F32)
        eim_out[:, sl] = jnp.dot(uq, eim_ref[q], preferred_element_type=F32)


def _const_spec(a):
    nd = a.ndim
    return pl.BlockSpec(a.shape, lambda *_: (0,) * nd, pipeline_mode=pl.Buffered(1))


def _s5_e(u2, tabs):
    rows, wu = u2.shape
    tr = min(S5_ROW_TILE, rows)
    out = jax.ShapeDtypeStruct((rows, S5_STATE_W), F32)
    return pl.pallas_call(
        _s5_e_kernel, name="s5_e",
        grid=(rows // tr,),
        in_specs=[_row_spec(tr, wu), _const_spec(tabs['e_re']), _const_spec(tabs['e_im'])],
        out_specs=[_row_spec(tr, S5_STATE_W)] * 2,
        out_shape=[out, out],
        compiler_params=_cparams(("parallel",)),
    )(u2, tabs['e_re'], tabs['e_im'])


def _s5_scan_kernel(ere_ref, eim_ref, h0r_ref, h0i_ref, stp_re_ref, stp_im_ref,
                    apw_re_ref, apw_im_ref, hpr_ref, hpi_ref, hfr_ref, hfi_ref, *, nc, last_row):
    w = ere_ref.shape[-1]
    row = lax.broadcasted_iota(jnp.int32, (SUBLANES, w), 0)
    apr = apw_re_ref[...]
    api = apw_im_ref[...]

    def tile(i, carry):
        cr, ci = carry
        start = pl.multiple_of(i * SUBLANES, SUBLANES)
        sr = ere_ref[pl.ds(start, SUBLANES), :]
        si = eim_ref[pl.ds(start, SUBLANES), :]
        for k in range(3):
            s = 1 << k
            keep = row >= s
            ar = stp_re_ref[k:k + 1, :]
            ai = stp_im_ref[k:k + 1, :]
            pr = pltpu.roll(sr, s, 0)
            pi = pltpu.roll(si, s, 0)
            sr, si = (jnp.where(keep, sr + ar * pr - ai * pi, sr),
                      jnp.where(keep, si + ar * pi + ai * pr, si))
        hr = sr + apr * cr - api * ci
        hi = si + apr * ci + api * cr
        first = row == 0
        hpr_ref[pl.ds(start, SUBLANES), :] = jnp.where(first, cr, pltpu.roll(hr, 1, 0))
        hpi_ref[pl.ds(start, SUBLANES), :] = jnp.where(first, ci, pltpu.roll(hi, 1, 0))
        return hr, hi

    def body(i, carry):
        hr, hi = tile(i, carry)
        return hr[SUBLANES - 1:SUBLANES, :], hi[SUBLANES - 1:SUBLANES, :]

    ntile = nc // SUBLANES
    carry = lax.fori_loop(0, ntile - 1, body, (h0r_ref[...], h0i_ref[...]))
    hr, hi = tile(ntile - 1, carry)
    hfr_ref[...] = hr[last_row:last_row + 1, :]
    hfi_ref[...] = hi[last_row:last_row + 1, :]


def _s5_scan(e_re, e_im, h0r, h0i, tabs, n_chunks):
    b, nc, w = e_re.shape
    last_row = (n_chunks - 1) % SUBLANES
    wb = S5_SCAN_LANES
    consts = [tabs['step_re'], tabs['step_im'], tabs['apow_re'], tabs['apow_im']]
    seq = pl.BlockSpec((None, nc, wb), lambda bi, li: (bi, 0, li))
    vec = pl.BlockSpec((None, 1, wb), lambda bi, li: (bi, 0, li))
    return pl.pallas_call(
        functools.partial(_s5_scan_kernel, nc=nc, last_row=last_row), name="s5_scan",
        grid=(b, w // wb),
        in_specs=[seq, seq, vec, vec]
                 + [pl.BlockSpec((c.shape[0], wb), lambda bi, li: (0, li)) for c in consts],
        out_specs=[seq, seq, vec, vec],
        out_shape=[jax.ShapeDtypeStruct((b, nc, w), F32)] * 2
                  + [jax.ShapeDtypeStruct((b, 1, w), F32)] * 2,
        compiler_params=_cparams(("parallel", "parallel")),
    )(e_re, e_im, h0r, h0i, *consts)


def _s5_y_kernel(u_ref, hpr_ref, hpi_ref, m_ref, fre_ref, fim_ref, y_ref):
    for q in range(S5_NSUPER):
        uq = _s5_super_rows(u_ref, q)
        sl = slice(q * S5_SUPER_STATE, (q + 1) * S5_SUPER_STATE)
        yq = (jnp.dot(uq, m_ref[q], preferred_element_type=F32)
              + _bdot(hpr_ref[:, sl], fre_ref[q])
              + _bdot(hpi_ref[:, sl], fim_ref[q]))
        for t in range(S5_L):
            y_ref[:, t * S5_WIDTH + q * LANES:t * S5_WIDTH + (q + 1) * LANES] = (
                yq[:, t * LANES:(t + 1) * LANES])


def _s5_y(u2, hp_re, hp_im, tabs):
    rows, wu = u2.shape
    tr = min(S5_ROW_TILE, rows)
    consts = [tabs['m'], tabs['f_re'], tabs['f_im']]
    return pl.pallas_call(
        _s5_y_kernel, name="s5_y",
        grid=(rows // tr,),
        in_specs=[_row_spec(tr, wu), _row_spec(tr, S5_STATE_W), _row_spec(tr, S5_STATE_W)]
                 + [_const_spec(c) for c in consts],
        out_specs=_row_spec(tr, wu),
        out_shape=jax.ShapeDtypeStruct((rows, wu), F32),
        compiler_params=_cparams(("parallel",)),
    )(u2, hp_re, hp_im, *consts)


def _c_out_value(y2_ref, u_ref, x_ref, d_ref, wglu_ref, wout_ref, g_ref, beta_ref, ys, *, alpha):
    rows = y2_ref.shape[0]
    ntile = S5_WIDTH // LANES
    for s in range(S5_L):
        for c in range(ntile):
            lo = s * S5_WIDTH + c * LANES
            ys[c, pl.ds(s, rows, stride=S5_L), :] = y2_ref[:, lo:lo + LANES]
    y = jnp.concatenate([ys[c] for c in range(ntile)], axis=1) + d_ref[...] * u_ref[...]
    vg = _bdot(jax.nn.gelu(y), wglu_ref[...])
    half = vg.shape[1] // 2
    z = vg[:, :half] * jax.nn.sigmoid(vg[:, half:])
    o = _bdot(z, wout_ref[...])
    return _layer_norm(alpha * x_ref[...] + o, g_ref[...], beta_ref[...])


def _c_out_stage(y2, u, x, w, g, beta, alpha):
    n, dm = x.shape
    tm = min(ROW_TILE, n)
    return (functools.partial(_c_out_value, alpha=alpha),
            [(y2, tm // S5_L), (u, tm), (x, tm)], [w['d'], w['w_glu'], w['w_out'], g, beta],
            [pltpu.VMEM((u.shape[1] // LANES, tm, LANES), F32)], n, dm)


def _prep_ab(w_in, q_g, kv_g, w_uq, w_ukv, conv_w, conv_b, wa, ba, wx, bx, a_param, w_out):
    dm = w_in.shape[0]
    s0 = Q_LORA
    s1 = s0 + KV_LORA
    s2 = s1 + QK_ROPE
    s3 = s2 + LRU_WIDTH
    pad_lo = jnp.zeros((dm, QK_NOPE), F32)
    pad_hi = jnp.zeros((dm, HEAD_PAD - QK_HEAD), F32)
    w_in_p = jnp.concatenate([w_in[:, :s1], w_in[:, s2:s3], w_in[:, s3:],
                              pad_lo, w_in[:, s1:s2], pad_hi], axis=1).astype(BF16)
    wq = w_uq.reshape(Q_LORA, MLA_HEADS, QK_HEAD)
    wq = jnp.pad(wq, ((0, 0), (0, 0), (0, HEAD_PAD - QK_HEAD)))
    wq = wq.reshape(Q_LORA, MLA_HEADS * HEAD_PAD).astype(BF16)
    wkv = w_ukv.reshape(KV_LORA, MLA_HEADS, QK_NOPE + V_HEAD)
    wk = jnp.pad(wkv[:, :, :QK_NOPE], ((0, 0), (0, 0), (0, HEAD_PAD - QK_NOPE)))
    wk = wk.reshape(KV_LORA, MLA_HEADS * HEAD_PAD).astype(BF16)
    wv = jnp.pad(wkv[:, :, QK_NOPE:], ((0, 0), (0, 0), (0, HEAD_PAD - V_HEAD)))
    wv = wv.reshape(KV_LORA, MLA_HEADS * HEAD_PAD).astype(BF16)
    wukt = jnp.transpose(wkv[:, :, :QK_NOPE], (1, 2, 0)).astype(BF16)
    wuv_h = jnp.transpose(wkv[:, :, QK_NOPE:], (1, 0, 2)).astype(BF16)
    eye = jnp.eye(LRU_BLOCKS, dtype=F32)

    def blockdiag(wb):
        return jnp.einsum('hij,hk->hikj', wb, eye).reshape(LRU_WIDTH, LRU_WIDTH)

    w_ax = jnp.concatenate([blockdiag(wa), blockdiag(wx)], axis=1).astype(BF16)
    b_ax = jnp.concatenate([ba, bx])[None, :]
    att_w = MLA_HEADS * V_HEAD
    return dict(w_in=w_in_p, q_g=q_g[None, :], kv_g=kv_g[None, :], w_uq=wq, w_uk=wk, w_uv=wv,
                wukt=wukt, wuv_h=wuv_h, conv_w=conv_w, conv_b=conv_b[None, :], w_ax=w_ax,
                b_ax=b_ax, a_param=jax.nn.softplus(-a_param)[None, :],
                w_out_att=w_out[:att_w].astype(BF16), w_out_rec=w_out[att_w:].astype(BF16))


def _rope_tables(row_pos, tile_pos):
    half = QK_ROPE // 2
    inv = ROPE_THETA ** (-jnp.arange(half, dtype=F32) / half)
    inv_lane = jnp.concatenate([jnp.zeros((QK_NOPE,), F32), inv, inv,
                                jnp.zeros((HEAD_PAD - QK_HEAD,), F32)])
    ang_row = row_pos.astype(F32)[:, None] * inv_lane[None, :]
    ang_tile = tile_pos.astype(F32)[:, None, None] * inv_lane
    return jnp.cos(ang_row), jnp.sin(ang_row), jnp.cos(ang_tile), jnp.sin(ang_tile)


def _prep_router(w_group, w_expert):
    dm = w_group.shape[0]
    pad = jnp.zeros((dm, ROUTER_W - N_GROUPS - N_EXPERTS), F32)
    w_router = jnp.concatenate([w_group, w_expert, pad], axis=1)
    w_router_hi = w_router.astype(BF16)
    w_router_lo = (w_router - w_router_hi.astype(F32)).astype(BF16)
    return dict(w_router=jnp.concatenate([w_router_hi, w_router_lo], axis=1))


def _cmul(ar, ai, br, bi):
    return ar * br - ai * bi, ar * bi + ai * br


def _prep_s5(lam_re, lam_im, log_step, b_re, b_im, c_re, c_im):
    hi = lax.Precision.HIGHEST
    L, G, P, C = S5_L, S5_GROUPS, S5_STATE, S5_GROUP
    lr, li = lam_re, lam_im
    dt = jnp.exp(log_step)[:, None]
    mag = jnp.exp(lr * dt)
    ar, ai = mag * jnp.cos(li * dt), mag * jnp.sin(li * dt)
    den = lr * lr + li * li
    nr = ar - 1.0
    cr = (nr * lr + ai * li) / den
    ci = (ai * lr - nr * li) / den
    bbr = cr[..., None] * b_re - ci[..., None] * b_im
    bbi = cr[..., None] * b_im + ci[..., None] * b_re
    prs, pis = [jnp.ones_like(ar)], [jnp.zeros_like(ar)]
    for _ in range(L):
        nr_, ni_ = _cmul(prs[-1], pis[-1], ar, ai)
        prs.append(nr_)
        pis.append(ni_)
    pr = jnp.stack(prs)
    pi = jnp.stack(pis)
    xr = pr[:L, :, :, None] * bbr - pi[:L, :, :, None] * bbi
    xi = pr[:L, :, :, None] * bbi + pi[:L, :, :, None] * bbr
    kk = (jnp.einsum('gcp,kgpd->kgcd', c_re, xr, precision=hi)
          - jnp.einsum('gcp,kgpd->kgcd', c_im, xi, precision=hi))
    p1r = pr[1:, :, :, None]
    p1i = pi[1:, :, :, None]
    cre_t = jnp.transpose(c_re, (0, 2, 1))[None]
    cim_t = jnp.transpose(c_im, (0, 2, 1))[None]
    f_re = cre_t * p1r - cim_t * p1i
    f_im = -cre_t * p1i - cim_t * p1r

    nq, sg = S5_NSUPER, S5_SUPER
    eye = jnp.eye(sg, dtype=F32)
    dk = jnp.einsum('kqgcd,gh->kqgdhc', kk.reshape(L, nq, sg, C, C), eye)
    dk = dk.reshape(L, nq, LANES, LANES).astype(BF16)
    lag = jnp.arange(L)[None, :] - jnp.arange(L)[:, None]
    m = jnp.where((lag >= 0)[:, :, None, None, None], dk[jnp.clip(lag, 0, L - 1)], 0)
    m = jnp.transpose(m, (2, 0, 3, 1, 4)).reshape(nq, L * LANES, L * LANES)

    row_g = (jnp.arange(L * LANES) // C) % sg
    col_g = jnp.arange(sg * P) // P
    same_group = (row_g[:, None] == col_g[None, :])[None]

    def pack_rows(a):
        a = jnp.transpose(a.reshape(L, nq, sg, P, C), (1, 0, 4, 2, 3))
        a = a.reshape(nq, L, 1, C, sg * P).astype(BF16)
        a = jnp.broadcast_to(a, (nq, L, sg, C, sg * P)).reshape(nq, L * LANES, sg * P)
        return jnp.where(same_group, a, 0)

    def pack_e(x):
        return pack_rows(x[::-1])

    def pack_f(f):
        return jnp.swapaxes(pack_rows(f), 1, 2)

    alr, ali = pr[L].reshape(1, G * P), pi[L].reshape(1, G * P)
    qr, qi = [alr], [ali]
    for _ in range(SUBLANES - 1):
        nr_, ni_ = _cmul(qr[-1], qi[-1], alr, ali)
        qr.append(nr_)
        qi.append(ni_)
    apow_re = jnp.concatenate(qr, axis=0)
    apow_im = jnp.concatenate(qi, axis=0)
    step_re = jnp.concatenate([qr[0], qr[1], qr[3]], axis=0)
    step_im = jnp.concatenate([qi[0], qi[1], qi[3]], axis=0)
    return dict(m=m, e_re=pack_e(xr), e_im=pack_e(xi), f_re=pack_f(f_re),
                f_im=pack_f(f_im), apow_re=apow_re, apow_im=apow_im,
                step_re=step_re, step_im=step_im)


def _mixer_ab(x, pos0, past_ckv, past_krope, conv0, h0, w):
    b, t, dm = x.shape
    xf = x.reshape(b * t, dm)
    tm = min(ROW_TILE, b * t)
    assert tm % t == 0 or t % tm == 0
    row_pos = jnp.arange(tm) % t
    tile_pos = pos0 + (jnp.arange(b * t // tm) * tm) % t
    q, k, v, c_new, kr_new, xb, yb = _ab_in(xf, w, _rope_tables(row_pos, tile_pos))
    if past_ckv is None:
        att = _attn_prompt(q, k, v)
    else:
        att = _attn_sample(q, c_new, kr_new, past_ckv, past_krope, w['wukt'], w['wuv_h'])
    rec, conv_new, h_new = _lru(xb.reshape(b, t, -1), yb.reshape(b, t, -1), conv0,
                                h0[:, None, :], w)
    return (att, rec.reshape(b * t, -1), c_new.reshape(b, t, -1), kr_new.reshape(b, t, -1),
            conv_new, h_new[:, 0, :])


def _mixer_c(x, h0_re, h0_im, w):
    b, t, dm = x.shape
    xf = x.reshape(b * t, dm)
    nc = t // S5_L
    assert t % S5_L == 0 and nc % SUBLANES == 0
    u, u2 = _c_in(xf, w['w_in'])
    e_re, e_im = _s5_e(u2, w['tabs'])
    hp_re, hp_im, hf_re, hf_im = _s5_scan(
        e_re.reshape(b, nc, S5_STATE_W), e_im.reshape(b, nc, S5_STATE_W),
        h0_re.reshape(b, 1, S5_STATE_W), h0_im.reshape(b, 1, S5_STATE_W), w['tabs'], nc)
    y2 = _s5_y(u2, hp_re.reshape(b * nc, S5_STATE_W), hp_im.reshape(b * nc, S5_STATE_W),
               w['tabs'])
    return (y2, u, hf_re.reshape(b, S5_GROUPS, S5_STATE), hf_im.reshape(b, S5_GROUPS, S5_STATE))


def kernel(x_prompt, x_sample, cache_mla_ckv, cache_mla_krope, state_lru_conv, state_lru_h,
           state_s5_re, state_s5_im, w_in_ab, q_norm_g, kv_norm_g, w_uq, w_ukv,
           lru_conv_w, lru_conv_b, lru_w_a, lru_b_a, lru_w_x, lru_b_x, lru_a_param, w_out_ab,
           w_in_c, s5_lam_re, s5_lam_im, s5_log_step, s5_b_re, s5_b_im, s5_c_re, s5_c_im,
           s5_d, s5_w_glu, w_out_c, ln_mix_g, ln_mix_b, ln_ffn_g, ln_ffn_b,
           moe_w_group, moe_w_expert, moe_w_gate, moe_w_up, moe_w_down):
    bp, tp, dm = x_prompt.shape
    bs, ts, _ = x_sample.shape
    past = cache_mla_ckv.shape[2]
    depth = ln_mix_g.shape[0]
    alpha = (2 * depth) ** 0.25
    assert (past + ts - 1) // CHUNK <= past // CHUNK
    hp, hs = x_prompt, x_sample
    moe_experts = dict(w_gate=moe_w_gate.astype(BF16), w_up=moe_w_up.astype(BF16),
                       w_down=moe_w_down.astype(BF16))
    outs = {k: [] for k in ('ckv_p', 'ckv_s', 'kr_p', 'kr_s', 'cv_p', 'cv_s', 'lh_p', 'lh_s',
                            's5r_p', 's5r_s', 's5i_p', 's5i_s')}
    for layer in range(depth):
        j = layer // 2
        g_mix, b_mix = ln_mix_g[layer][None, :], ln_mix_b[layer][None, :]
        g_ffn, b_ffn = ln_ffn_g[layer][None, :], ln_ffn_b[layer][None, :]
        if layer % 2 == 0:
            w = _prep_ab(w_in_ab[j], q_norm_g[j], kv_norm_g[j], w_uq[j], w_ukv[j], lru_conv_w[j],
                         lru_conv_b[j], lru_w_a[j], lru_b_a[j], lru_w_x[j], lru_b_x[j],
                         lru_a_param[j], w_out_ab[j])
            att_p, rec_p, c1, k1, v1, h1 = _mixer_ab(
                hp, 0, None, None, jnp.zeros((bp, CONV_WIDTH - 1, LRU_WIDTH), F32),
                jnp.zeros((bp, LRU_WIDTH), F32), w)
            att_s, rec_s, c2, k2, v2, h2 = _mixer_ab(
                hs, past, cache_mla_ckv[j], cache_mla_krope[j], state_lru_conv[j],
                state_lru_h[j], w)
            outs['ckv_p'].append(c1); outs['ckv_s'].append(c2)
            outs['kr_p'].append(k1); outs['kr_s'].append(k2)
            outs['cv_p'].append(v1); outs['cv_s'].append(v2)
            outs['lh_p'].append(h1); outs['lh_s'].append(h2)
            stage_p = _out_ln_stage(att_p, rec_p, hp.reshape(bp * tp, dm), w['w_out_att'],
                                    w['w_out_rec'], g_mix, b_mix, alpha)
            stage_s = _out_ln_stage(att_s, rec_s, hs.reshape(bs * ts, dm), w['w_out_att'],
                                    w['w_out_rec'], g_mix, b_mix, alpha)
            stage_name = "out_ln"
        else:
            w = dict(w_in=w_in_c[j].astype(BF16), d=s5_d[j][None, :],
                     w_glu=s5_w_glu[j].astype(BF16), w_out=w_out_c[j].astype(BF16),
                     tabs=_prep_s5(s5_lam_re[j], s5_lam_im[j], s5_log_step[j], s5_b_re[j],
                                   s5_b_im[j], s5_c_re[j], s5_c_im[j]))
            zero_state = jnp.zeros((bp, S5_GROUPS, S5_STATE), F32)
            y_p, u_p, r1, i1 = _mixer_c(hp, zero_state, zero_state, w)
            y_s, u_s, r2, i2 = _mixer_c(hs, state_s5_re[j], state_s5_im[j], w)
            outs['s5r_p'].append(r1); outs['s5r_s'].append(r2)
            outs['s5i_p'].append(i1); outs['s5i_s'].append(i2)
            stage_p = _c_out_stage(y_p, u_p, hp.reshape(bp * tp, dm), w, g_mix, b_mix, alpha)
            stage_s = _c_out_stage(y_s, u_s, hs.reshape(bs * ts, dm), w, g_mix, b_mix, alpha)
            stage_name = "c_out"
        wm = dict(moe_experts, **_prep_router(moe_w_group[layer], moe_w_expert[layer]))
        hp = _moe(stage_p, stage_name, wm, layer, g_ffn, b_ffn, alpha).reshape(bp, tp, dm)
        hs = _moe(stage_s, stage_name, wm, layer, g_ffn, b_ffn, alpha).reshape(bs, ts, dm)
    st = lambda k: jnp.stack(outs[k])
    return (hp, hs, st('ckv_p'), st('ckv_s'), st('kr_p'), st('kr_s'), st('cv_p'), st('cv_s'),
            st('lh_p'), st('lh_s'), st('s5r_p'), st('s5r_s'), st('s5i_p'), st('s5i_s'))
```

```python
import functools
import math

import jax
import jax.numpy as jnp
from jax import lax
from jax.experimental import pallas as pl
from jax.experimental.pallas import tpu as pltpu
from jax.experimental.pallas import tpu_sc as plsc

F32 = jnp.float32
BF16 = jnp.bfloat16

CHUNK = 64
MLA_HEADS = 8
QK_NOPE = 64
QK_ROPE = 32
QK_HEAD = QK_NOPE + QK_ROPE
V_HEAD = 64
Q_LORA = 256
KV_LORA = 128
ROPE_THETA = 10000.0
MLA_SCALE = QK_HEAD ** -0.5
NEG_INF = -1e30
LRU_WIDTH = 512
LRU_BLOCKS = 8
LRU_BLOCK = LRU_WIDTH // LRU_BLOCKS
CONV_WIDTH = 4
RG_C = 8.0
S5_WIDTH = 512
S5_GROUP = 16
S5_GROUPS = S5_WIDTH // S5_GROUP
S5_STATE = 64
N_GROUPS = 4
EXPERTS_PER_GROUP = 4
N_EXPERTS = N_GROUPS * EXPERTS_PER_GROUP
EXPERT_HIDDEN = 256
LN_EPS = 1e-5
RMS_EPS = 1e-6

LANES = 128
SUBLANES = 8
VMEM_LIMIT_BYTES = 48 * 1024 * 1024
MOE_VMEM_LIMIT_BYTES = 56 * 1024 * 1024

HEAD_PAD = LANES
ROW_TILE = 512
MOE_ROW_TILE = 1024
MOE_EXPERT_TILE = 512
ATTN_TQ = 1024
ATTN_TK = 512
ATTN_UNROLL = 8
Q_SCALE = MLA_SCALE * math.log2(math.e)
LRU_TILE = 512
S5_L = 8
S5_SUPER = LANES // S5_GROUP
S5_NSUPER = S5_GROUPS // S5_SUPER
S5_SUPER_IN = S5_L * LANES
S5_SUPER_STATE = S5_SUPER * S5_STATE
S5_STATE_W = S5_GROUPS * S5_STATE
S5_SCAN_LANES = 512
S5_ROW_TILE = 256
ROUTER_W = LANES


def _cparams(semantics, vmem_limit_bytes=VMEM_LIMIT_BYTES):
    return pltpu.CompilerParams(dimension_semantics=semantics,
                                vmem_limit_bytes=vmem_limit_bytes)


def _full_spec(a):
    nd = a.ndim
    return pl.BlockSpec(a.shape, lambda *_: (0,) * nd)


def _row_spec(tm, width):
    return pl.BlockSpec((tm, width), lambda i: (i, 0))


def _layer_norm(z, g, b):
    mu = jnp.mean(z, axis=-1, keepdims=True)
    zc = z - mu
    var = jnp.mean(zc * zc, axis=-1, keepdims=True)
    return zc * lax.rsqrt(var + LN_EPS) * g + b


def _rms_norm(z, g):
    ms = jnp.mean(z * z, axis=-1, keepdims=True)
    return z * lax.rsqrt(ms + RMS_EPS) * g


def _bdot(a, b):
    return jnp.dot(a.astype(BF16), b.astype(BF16), preferred_element_type=F32)


def _ab_in_kernel(x_ref, win_ref, qg_ref, kvg_ref, wuq_ref, wuk_ref, wuv_ref,
                  cosb_ref, sinb_ref, cost_ref, sint_ref,
                  q_out, k_out, v_out, c_out, kr_out, xb_out, yb_out, *, dm):
    proj = _bdot(x_ref[...], win_ref[...])
    o0 = Q_LORA
    o1 = o0 + KV_LORA
    o2 = o1 + LRU_WIDTH
    o3 = o2 + LRU_WIDTH
    q_lat = proj[:, 0:o0]
    kv_lat = proj[:, o0:o1]
    xb_out[...] = proj[:, o1:o2]
    yb_out[...] = proj[:, o2:o3]
    krp = proj[:, o3:o3 + HEAD_PAD]

    qn = _rms_norm(q_lat, qg_ref[...])
    cn = _rms_norm(kv_lat, kvg_ref[...])
    c_out[...] = cn
    q = _bdot(qn, wuq_ref[...])
    kk = _bdot(cn, wuk_ref[...])
    lane = lax.broadcasted_iota(jnp.int32, (1, MLA_HEADS * HEAD_PAD), 1)
    ones_col = (lane % HEAD_PAD == V_HEAD).astype(F32)
    v_out[...] = (_bdot(cn, wuv_ref[...]) + ones_col).astype(v_out.dtype)

    cb, sb = cosb_ref[...], sinb_ref[...]
    ct, st = cost_ref[...], sint_ref[...]
    cos = cb * ct - sb * st
    sin = sb * ct + cb * st
    half = QK_ROPE // 2
    lane1 = lax.broadcasted_iota(jnp.int32, (1, HEAD_PAD), 1)
    rc = jnp.where(lane1 < QK_HEAD, cos, 0.0)
    ra = jnp.where((lane1 >= QK_NOPE + half) & (lane1 < QK_HEAD), sin, 0.0)
    rb = jnp.where((lane1 >= QK_NOPE) & (lane1 < QK_NOPE + half), -sin, 0.0)

    def rope(z):
        return (z * rc + pltpu.roll(z, half, 1) * ra
                + pltpu.roll(z, HEAD_PAD - half, 1) * rb)

    kr = rope(krp)
    kr_out[...] = kr[:, QK_NOPE:QK_NOPE + QK_ROPE]
    for h in range(MLA_HEADS):
        sl = slice(h * HEAD_PAD, (h + 1) * HEAD_PAD)
        q_out[:, sl] = (rope(q[:, sl]) * Q_SCALE).astype(q_out.dtype)
        k_out[:, sl] = (kk[:, sl] + kr).astype(k_out.dtype)


def _ab_in(x, w, rope_tabs):
    n, dm = x.shape
    tm = min(ROW_TILE, n)
    cosb, sinb, cost, sint = rope_tabs
    consts = [w['w_in'], w['q_g'], w['kv_g'], w['w_uq'], w['w_uk'], w['w_uv'], cosb, sinb]
    tile_spec = pl.BlockSpec((None, 1, HEAD_PAD), lambda i: (i, 0, 0))
    hp = MLA_HEADS * HEAD_PAD
    out_shape = [
        jax.ShapeDtypeStruct((n, hp), BF16),
        jax.ShapeDtypeStruct((n, hp), BF16),
        jax.ShapeDtypeStruct((n, hp), BF16),
        jax.ShapeDtypeStruct((n, KV_LORA), F32),
        jax.ShapeDtypeStruct((n, QK_ROPE), F32),
        jax.ShapeDtypeStruct((n, LRU_WIDTH), F32),
        jax.ShapeDtypeStruct((n, LRU_WIDTH), F32),
    ]
    return pl.pallas_call(
        functools.partial(_ab_in_kernel, dm=dm), name="ab_in",
        grid=(n // tm,),
        in_specs=([_row_spec(tm, dm)] + [_full_spec(c) for c in consts]
                  + [tile_spec, tile_spec]),
        out_specs=[_row_spec(tm, s.shape[1]) for s in out_shape],
        out_shape=out_shape,
        compiler_params=_cparams(("parallel",)),
    )(x, *consts, cost, sint)


def _attn_prompt_kernel(q_ref, k_ref, v_ref, o_ref):
    i = pl.program_id(1)
    tq, tk = ATTN_TQ, ATTN_TK
    ndiag = tq // tk
    nfull = i * ndiag
    row_chunk = lax.broadcasted_iota(jnp.int32, (tq, tk), 0) // CHUNK
    col_chunk = lax.broadcasted_iota(jnp.int32, (tq, tk), 1) // CHUNK
    qs = [q_ref[:, hh * HEAD_PAD:(hh + 1) * HEAD_PAD] for hh in range(2)]

    def update(hh, j, m, acc, mask, row0=0):
        start = pl.multiple_of(j * tk, tk)
        k = k_ref[pl.ds(start, tk), hh * HEAD_PAD:(hh + 1) * HEAD_PAD]
        v = v_ref[pl.ds(start, tk), hh * HEAD_PAD:(hh + 1) * HEAD_PAD]
        s = lax.dot_general(qs[hh][row0:], k, (((1,), (1,)), ((), ())),
                            preferred_element_type=F32)
        if mask is not None:
            s = jnp.where(mask[row0:], s, NEG_INF)
        m_new = jnp.maximum(m, jnp.max(s, axis=-1, keepdims=True))
        alpha = jnp.exp2(m - m_new)
        p = jnp.exp2((s - m_new).astype(BF16))
        return m_new, alpha * acc + jnp.dot(p, v, preferred_element_type=F32)

    def body(j, carry):
        new = []
        for hh in range(2):
            new += update(hh, j, carry[2 * hh], carry[2 * hh + 1], None)
        return tuple(new)

    unroll = ATTN_UNROLL

    def body_unrolled(jj, carry):
        for b in range(unroll):
            carry = body(unroll * jj + b, carry)
        return carry

    init = []
    for _ in range(2):
        init += [jnp.full((tq, 1), NEG_INF, F32), jnp.zeros((tq, HEAD_PAD), F32)]
    carry = lax.fori_loop(0, nfull // unroll, body_unrolled, tuple(init))
    rem0 = nfull - nfull % unroll

    def body_rem(jj, carry):
        for b in range(ndiag):
            carry = body(rem0 + ndiag * jj + b, carry)
        return carry

    carry = list(lax.fori_loop(0, (nfull % unroll) // ndiag, body_rem, carry))
    for d in range(ndiag):
        visible = col_chunk + d * (tk // CHUNK) <= row_chunk
        r0 = d * tk
        for hh in range(2):
            m, acc = carry[2 * hh], carry[2 * hh + 1]
            m_new, acc_new = update(hh, nfull + d, m[r0:], acc[r0:], visible, r0)
            if r0:
                m_new = jnp.concatenate([m[:r0], m_new], axis=0)
                acc_new = jnp.concatenate([acc[:r0], acc_new], axis=0)
            carry[2 * hh], carry[2 * hh + 1] = m_new, acc_new
    outs = [carry[2 * hh + 1][:, :V_HEAD] / carry[2 * hh + 1][:, V_HEAD:V_HEAD + 1]
            for hh in range(2)]
    o_ref[...] = jnp.concatenate(outs, axis=-1).astype(o_ref.dtype)


def _attn_prompt(q, k, v):
    t = q.shape[0]
    pairs = MLA_HEADS // 2
    return pl.pallas_call(
        _attn_prompt_kernel, name="attn_prompt",
        grid=(pairs, t // ATTN_TQ),
        in_specs=[
            pl.BlockSpec((ATTN_TQ, 2 * HEAD_PAD), lambda p, i: (i, p)),
            pl.BlockSpec((t, 2 * HEAD_PAD), lambda p, i: (0, p)),
            pl.BlockSpec((t, 2 * HEAD_PAD), lambda p, i: (0, p)),
        ],
        out_specs=pl.BlockSpec((ATTN_TQ, 2 * V_HEAD), lambda p, i: (i, p)),
        out_shape=jax.ShapeDtypeStruct((t, MLA_HEADS * V_HEAD), BF16),
        compiler_params=_cparams(("parallel", "parallel")),
    )(q, k, v)


def _attn_sample_kernel(q_ref, cn_ref, krn_ref, cp_ref, krp_ref, wukt_ref, wuv_ref, o_ref):
    cp = cp_ref[...].astype(BF16)
    krp = krp_ref[...].astype(BF16)
    cn = cn_ref[...].astype(BF16)
    krn = krn_ref[...].astype(BF16)
    ts = cn.shape[0]
    dn = (((1,), (1,)), ((), ()))
    outs = []
    for hp in range(MLA_HEADS // 2):
        qa, qr = [], []
        for hh in range(2):
            h = 2 * hp + hh
            qh = q_ref[:, h * HEAD_PAD:(h + 1) * HEAD_PAD]
            qa.append(jnp.dot(qh[:, :QK_NOPE], wukt_ref[h], preferred_element_type=F32))
            qr.append(qh[:, QK_NOPE:QK_NOPE + QK_ROPE])
        qa = jnp.concatenate(qa, axis=0).astype(BF16)
        qr = jnp.concatenate(qr, axis=0)
        s_past = (lax.dot_general(qa, cp, dn, preferred_element_type=F32)
                  + lax.dot_general(qr, krp, dn, preferred_element_type=F32))
        s_new = (lax.dot_general(qa, cn, dn, preferred_element_type=F32)
                 + lax.dot_general(qr, krn, dn, preferred_element_type=F32))
        m = jnp.maximum(jnp.max(s_past, axis=-1, keepdims=True),
                        jnp.max(s_new, axis=-1, keepdims=True))
        p_past = jnp.exp2(s_past - m)
        p_new = jnp.exp2(s_new - m)
        l = (jnp.sum(p_past, axis=-1, keepdims=True)
             + jnp.sum(p_new, axis=-1, keepdims=True))
        o_lat = (jnp.dot(p_past.astype(BF16), cp, preferred_element_type=F32)
                 + jnp.dot(p_new.astype(BF16), cn, preferred_element_type=F32)) / l
        for hh in range(2):
            h = 2 * hp + hh
            outs.append(_bdot(o_lat[hh * ts:(hh + 1) * ts], wuv_ref[h]))
    o_ref[...] = jnp.concatenate(outs, axis=-1).astype(o_ref.dtype)


def _attn_sample(q, c_new, kr_new, c_past, kr_past, wukt, wuv):
    bs, past, _ = c_past.shape
    ts = q.shape[0] // bs
    return pl.pallas_call(
        _attn_sample_kernel, name="attn_sample",
        grid=(bs,),
        in_specs=[
            pl.BlockSpec((ts, MLA_HEADS * HEAD_PAD), lambda b: (b, 0)),
            pl.BlockSpec((ts, KV_LORA), lambda b: (b, 0)),
            pl.BlockSpec((ts, QK_ROPE), lambda b: (b, 0)),
            pl.BlockSpec((None, past, KV_LORA), lambda b: (b, 0, 0)),
            pl.BlockSpec((None, past, QK_ROPE), lambda b: (b, 0, 0)),
            _full_spec(wukt), _full_spec(wuv),
        ],
        out_specs=pl.BlockSpec((ts, MLA_HEADS * V_HEAD), lambda b: (b, 0)),
        out_shape=jax.ShapeDtypeStruct((bs * ts, MLA_HEADS * V_HEAD), BF16),
        compiler_params=_cparams(("parallel",)),
    )(q, c_new, kr_new, c_past, kr_past, wukt, wuv)


def _lru_kernel(xb_ref, yb_ref, conv0_ref, h0_ref, cw_ref, cb_ref, wax_ref, bax_ref, ap_ref,
                rec_ref, conv_out_ref, h_out_ref, xcat, hc, *, tb):
    t = pl.program_id(1)
    nt = pl.num_programs(1)
    tail = CONV_WIDTH - 1
    base = SUBLANES

    @pl.when(t == 0)
    def _():
        xcat[base - tail:base, :] = conv0_ref[...]
        hc[...] = h0_ref[...]

    @pl.when(t > 0)
    def _():
        xcat[0:base, :] = xcat[tb:tb + base, :]

    xcat[base:base + tb, :] = xb_ref[...]
    xc = cb_ref[...]
    for tap in range(CONV_WIDTH):
        xc = xc + xcat[base - tail + tap:base - tail + tap + tb, :] * cw_ref[tap:tap + 1, :]

    gates = _bdot(xc, wax_ref[...]) + bax_ref[...]
    r = jax.nn.sigmoid(gates[:, :LRU_WIDTH])
    ig = jax.nn.sigmoid(gates[:, LRU_WIDTH:])
    log_a = -RG_C * r * ap_ref[...]
    a = jnp.exp(log_a)
    u = jnp.sqrt(jnp.tanh(-log_a) * (1.0 + a * a)) * (ig * xc)

    row8 = lax.broadcasted_iota(jnp.int32, (tb, LRU_WIDTH), 0) % SUBLANES
    s = 1
    while s < SUBLANES:
        keep = row8 >= s
        u = jnp.where(keep, a * pltpu.roll(u, s, 0) + u, u)
        a = jnp.where(keep, a * pltpu.roll(a, s, 0), a)
        s *= 2
    hprev = hc[...]
    groups = []
    for g in range(tb // SUBLANES):
        sl = slice(g * SUBLANES, (g + 1) * SUBLANES)
        hg = a[sl] * hprev + u[sl]
        groups.append(hg)
        hprev = hg[SUBLANES - 1:SUBLANES, :]
    h = jnp.concatenate(groups, axis=0)
    hc[...] = hprev
    rec_ref[...] = (h * jax.nn.gelu(yb_ref[...])).astype(rec_ref.dtype)

    @pl.when(t == nt - 1)
    def _():
        conv_out_ref[...] = xcat[base + tb - tail:base + tb, :]
        h_out_ref[...] = h[tb - 1:tb, :]


def _lru(xb, yb, conv0, h0, w):
    b, t, wd = xb.shape
    tb = min(LRU_TILE, t)
    tail = CONV_WIDTH - 1
    consts = [w['conv_w'], w['conv_b'], w['w_ax'], w['b_ax'], w['a_param']]
    seq = pl.BlockSpec((None, tb, wd), lambda bi, ti: (bi, ti, 0))
    return pl.pallas_call(
        functools.partial(_lru_kernel, tb=tb), name="rg_lru",
        grid=(b, t // tb),
        in_specs=[seq, seq,
                  pl.BlockSpec((None, tail, wd), lambda bi, ti: (bi, 0, 0)),
                  pl.BlockSpec((None, 1, wd), lambda bi, ti: (bi, 0, 0))]
                 + [pl.BlockSpec(c.shape, lambda bi, ti: (0, 0)) for c in consts],
        out_specs=[seq,
                   pl.BlockSpec((None, tail, wd), lambda bi, ti: (bi, 0, 0)),
                   pl.BlockSpec((None, 1, wd), lambda bi, ti: (bi, 0, 0))],
        out_shape=[jax.ShapeDtypeStruct((b, t, wd), BF16),
                   jax.ShapeDtypeStruct((b, tail, wd), F32),
                   jax.ShapeDtypeStruct((b, 1, wd), F32)],
        scratch_shapes=[pltpu.VMEM((tb + 2 * SUBLANES, wd), F32), pltpu.VMEM((1, wd), F32)],
        compiler_params=_cparams(("parallel", "arbitrary")),
    )(xb, yb, conv0, h0, *consts)


def _out_ln_value(a_ref, b_ref, x_ref, wa_ref, wb_ref, g_ref, beta_ref, *, alpha):
    y = (jnp.dot(a_ref[...], wa_ref[...], preferred_element_type=F32)
         + jnp.dot(b_ref[...], wb_ref[...], preferred_element_type=F32))
    return _layer_norm(alpha * x_ref[...] + y, g_ref[...], beta_ref[...])


def _out_ln_stage(att, rec, x, wa, wb, g, beta, alpha):
    n, dm = x.shape
    tm = min(ROW_TILE, n)
    return (functools.partial(_out_ln_value, alpha=alpha), [(att, tm), (rec, tm), (x, tm)],
            [wa, wb, g, beta], [], n, dm)


def _stage_kernel(*refs, n_in, value_fn):
    refs[n_in][...] = value_fn(*refs[:n_in], *refs[n_in + 1:])


def _stage_call(stage, name):
    value_fn, row_inputs, consts, scratch, n, dm = stage
    tm = min(ROW_TILE, n)
    arrays = [a for a, _ in row_inputs] + list(consts)
    return pl.pallas_call(
        functools.partial(_stage_kernel, n_in=len(arrays), value_fn=value_fn), name=name,
        grid=(n // tm,),
        in_specs=[_row_spec(rows, a.shape[1]) for a, rows in row_inputs]
                 + [_full_spec(c) for c in consts],
        out_specs=_row_spec(tm, dm),
        out_shape=jax.ShapeDtypeStruct((n, dm), F32),
        scratch_shapes=list(scratch),
        compiler_params=_cparams(("parallel",)),
    )(*arrays)


def _route(x, xh, wr_ref, lane):
    xl = (x - xh.astype(F32)).astype(BF16)
    hh_hl = jnp.dot(xh, wr_ref[...], preferred_element_type=F32)
    logits = (hh_hl[:, :ROUTER_W] + hh_hl[:, ROUTER_W:]
              + jnp.dot(xl, wr_ref[:, :ROUTER_W], preferred_element_type=F32))
    big = float(ROUTER_W)
    lg = jnp.where(lane < N_GROUPS, logits, -jnp.inf)
    mg = jnp.max(lg, axis=-1, keepdims=True)
    gi = jnp.min(jnp.where(lg == mg, lane, big), axis=-1, keepdims=True)
    p_top = 1.0 / jnp.sum(jnp.exp(lg - mg), axis=-1, keepdims=True)
    lo = N_GROUPS + gi * EXPERTS_PER_GROUP
    le = jnp.where((lane >= lo) & (lane < lo + EXPERTS_PER_GROUP), logits, -jnp.inf)
    v1 = jnp.max(le, axis=-1, keepdims=True)
    i1 = jnp.min(jnp.where(le == v1, lane, big), axis=-1, keepdims=True)
    le2 = jnp.where(lane == i1, -jnp.inf, le)
    v2 = jnp.max(le2, axis=-1, keepdims=True)
    i2 = jnp.min(jnp.where(le2 == v2, lane, big), axis=-1, keepdims=True)
    e2 = jnp.exp(v2 - v1)
    return gi, i1, i2, p_top / (1.0 + e2), p_top * e2 / (1.0 + e2)


def _moe_ln_kernel(x_ref, wr_ref, wg_ref, wu_ref, wd_ref, g_ref, beta_ref, o_ref,
                   acc, gates, xb16, *, alpha):
    grp = pl.program_id(1)
    tm = x_ref.shape[0]
    lane = lax.broadcasted_iota(jnp.int32, (tm, ROUTER_W), 1).astype(F32)

    @pl.when(grp == 0)
    def _():
        x = x_ref[...]
        xh = x.astype(BF16)
        xb16[...] = xh
        _, i1, i2, w1, w2 = _route(x, xh, wr_ref, lane)
        gates[...] = jnp.where(lane == i1, w1, 0.0) + jnp.where(lane == i2, w2, 0.0)
        acc[...] = jnp.zeros_like(acc)

    xb = xb16[...]
    gt = gates[...]
    first = (N_GROUPS + grp * EXPERTS_PER_GROUP).astype(F32)
    for el in range(EXPERTS_PER_GROUP):
        ge = jnp.sum(jnp.where(lane == first + el, gt, 0.0), axis=-1, keepdims=True)
        h = jnp.dot(xb, wg_ref[el], preferred_element_type=F32)
        u = jnp.dot(xb, wu_ref[el], preferred_element_type=F32)
        act = (jax.nn.silu(h) * u * ge).astype(BF16)
        acc[...] += jnp.dot(act, wd_ref[el], preferred_element_type=F32)

    @pl.when(grp == pl.num_programs(1) - 1)
    def _():
        o_ref[...] = _layer_norm(alpha * x_ref[...] + acc[...], g_ref[...], beta_ref[...])


def _moe_ln(x, w, layer, g, beta, alpha):
    n, dm = x.shape
    tm = min(MOE_ROW_TILE, n)
    _, ne, _, eh = w['w_gate'].shape
    epg = EXPERTS_PER_GROUP
    return pl.pallas_call(
        functools.partial(_moe_ln_kernel, alpha=alpha), name="moe_ln",
        grid=(n // tm, ne // epg),
        in_specs=[
            pl.BlockSpec((tm, dm), lambda i, e: (i, 0)),
            pl.BlockSpec(w['w_router'].shape, lambda i, e: (0, 0)),
            pl.BlockSpec((None, epg, dm, eh), lambda i, e: (layer, e, 0, 0)),
            pl.BlockSpec((None, epg, dm, eh), lambda i, e: (layer, e, 0, 0)),
            pl.BlockSpec((None, epg, eh, dm), lambda i, e: (layer, e, 0, 0)),
            pl.BlockSpec(g.shape, lambda i, e: (0, 0)),
            pl.BlockSpec(beta.shape, lambda i, e: (0, 0)),
        ],
        out_specs=pl.BlockSpec((tm, dm), lambda i, e: (i, 0)),
        out_shape=jax.ShapeDtypeStruct((n, dm), F32),
        scratch_shapes=[pltpu.VMEM((tm, dm), F32), pltpu.VMEM((tm, ROUTER_W), F32),
                        pltpu.VMEM((tm, dm), BF16)],
        compiler_params=_cparams(("parallel", "arbitrary"), MOE_VMEM_LIMIT_BYTES),
    )(x, w['w_router'], w['w_gate'], w['w_up'], w['w_down'], g, beta)


META_W = LANES
META_GROUP = EXPERTS_PER_GROUP
META_RANK = EXPERTS_PER_GROUP + 1


def _moe_route_kernel(*refs, n_in, n_tiles, value_fn):
    in_refs = refs[:n_in]
    wr_ref, tri_ref, xg_ref, meta_ref, cnt_ref, run = refs[n_in:n_in + 6]
    tm, dm = xg_ref.shape[0], xg_ref.shape[1] - META_W
    i = pl.program_id(0)
    lane = lax.broadcasted_iota(jnp.int32, (tm, ROUTER_W), 1).astype(F32)

    @pl.when(i == 0)
    def _():
        run[...] = jnp.zeros_like(run)

    @pl.when(i < n_tiles)
    def _():
        x = value_fn(*in_refs, *refs[n_in + 6:])
        gi, i1, i2, w1, w2 = _route(x, x.astype(BF16), wr_ref, lane)
        lo = N_GROUPS + gi * EXPERTS_PER_GROUP
        in_group = lane == gi
        incl = jnp.dot(tri_ref[...], in_group.astype(BF16), preferred_element_type=F32)
        rank = jnp.sum(jnp.where(in_group, incl + run[...], 0.0), axis=-1,
                       keepdims=True) - 1.0
        run[...] += incl[tm - 1:tm, :]
        cnt_ref[...] = run[...]
        meta = (jnp.where(lane == i1 - lo, w1, 0.0) + jnp.where(lane == i2 - lo, w2, 0.0)
                + jnp.where(lane == META_GROUP, gi, 0.0)
                + jnp.where(lane == META_RANK, rank, 0.0))
        xg_ref[:, :dm] = x
        xg_ref[:, dm:] = meta
        meta_ref[...] = meta

    @pl.when(i >= n_tiles)
    def _():
        xg_ref[...] = jnp.zeros_like(xg_ref)


def _moe_route(stage, w_router, p):
    value_fn, row_inputs, consts, scratch, n, dm = stage
    tm = ROW_TILE
    n_tiles = n // tm
    tri = jnp.tril(jnp.ones((tm, tm), BF16))
    last = lambda i: (jnp.minimum(i, n_tiles - 1), 0)
    arrays = [a for a, _ in row_inputs] + list(consts)
    in_specs = ([pl.BlockSpec((rows, a.shape[1]), last) for a, rows in row_inputs]
                + [_full_spec(c) for c in consts] + [_full_spec(w_router), _full_spec(tri)])
    return pl.pallas_call(
        functools.partial(_moe_route_kernel, n_in=len(arrays), n_tiles=n_tiles,
                          value_fn=value_fn), name="moe_route",
        grid=(p // tm,),
        in_specs=in_specs,
        out_specs=[_row_spec(tm, dm + META_W), pl.BlockSpec((tm, META_W), last),
                   pl.BlockSpec((1, ROUTER_W), lambda i: (0, 0))],
        out_shape=[jax.ShapeDtypeStruct((p, dm + META_W), F32),
                   jax.ShapeDtypeStruct((n, META_W), F32),
                   jax.ShapeDtypeStruct((1, ROUTER_W), F32)],
        scratch_shapes=[pltpu.VMEM((1, ROUTER_W), F32)] + list(scratch),
        compiler_params=_cparams(("arbitrary",)),
    )(*arrays, w_router, tri)


def _moe_experts_kernel(tg_ref, nu_ref, xs_ref, wg_ref, wu_ref, wd_ref, y_ref):
    del tg_ref
    dm = y_ref.shape[1]

    @pl.when(pl.program_id(0) < nu_ref[0])
    def _():
        xb = xs_ref[:, :dm].astype(BF16)
        meta = xs_ref[:, dm:]
        acc = None
        for el in range(EXPERTS_PER_GROUP):
            h = jnp.dot(xb, wg_ref[el], preferred_element_type=F32)
            u = jnp.dot(xb, wu_ref[el], preferred_element_type=F32)
            act = (jax.nn.silu(h) * u * meta[:, el:el + 1]).astype(BF16)
            d = jnp.dot(act, wd_ref[el], preferred_element_type=F32)
            acc = d if acc is None else acc + d
        y_ref[...] = acc


def _moe_experts(xs, tile_group, n_used, w, layer):
    p, wx = xs.shape
    dm = wx - META_W
    te = MOE_EXPERT_TILE
    _, ne, _, eh = w['w_gate'].shape
    epg = EXPERTS_PER_GROUP

    def tile(i, tg, nu):
        return jnp.minimum(i, nu[0] - 1)

    wspec = lambda shape: pl.BlockSpec(
        (None, epg) + shape, lambda i, tg, nu: (layer, tg[tile(i, tg, nu)], 0, 0))
    grid_spec = pltpu.PrefetchScalarGridSpec(
        num_scalar_prefetch=2,
        grid=(p // te,),
        in_specs=[pl.BlockSpec((te, wx), lambda i, tg, nu: (tile(i, tg, nu), 0)),
                  wspec((dm, eh)), wspec((dm, eh)), wspec((eh, dm))],
        out_specs=pl.BlockSpec((te, dm), lambda i, tg, nu: (tile(i, tg, nu), 0)),
    )
    return pl.pallas_call(
        _moe_experts_kernel, name="moe_experts",
        grid_spec=grid_spec,
        out_shape=jax.ShapeDtypeStruct((p, dm), F32),
        compiler_params=_cparams(("arbitrary",)),
    )(tile_group, n_used, xs, w['w_gate'], w['w_up'], w['w_down'])


def _res_ln_value(xg_ref, y_ref, g_ref, beta_ref, *, alpha):
    dm = y_ref.shape[1]
    return _layer_norm(alpha * xg_ref[:, :dm] + y_ref[...], g_ref[...], beta_ref[...])


def _res_ln_stage(xg, y, g, beta, alpha):
    n, dm = y.shape
    return (functools.partial(_res_ln_value, alpha=alpha), [(xg, ROW_TILE), (y, ROW_TILE)],
            [g, beta], [], n, dm)


def _identity_value(x_ref):
    return x_ref[...]


def _array_stage(x):
    n, dm = x.shape
    return (_identity_value, [(x, min(ROW_TILE, n))], [], [], n, dm)


def _materialize(stage, name):
    if stage[0] is _identity_value:
        return stage[1][0][0]
    return _stage_call(stage, name)


def _moe(stage, stage_name, w, layer, g, beta, alpha):
    n = stage[4]
    granule = 2 * SC_GATHER_ROWS * SC_CORES * SC_SUBCORES
    if n % granule == 0 and (N_GROUPS * MOE_EXPERT_TILE) % granule == 0:
        return _moe_routed_ln(stage, w, layer, g, beta, alpha)
    return _array_stage(_moe_ln(_stage_call(stage, stage_name), w, layer, g, beta, alpha))


def _moe_routed_ln(stage, w, layer, g, beta, alpha):
    n, dm = stage[4], stage[5]
    te = MOE_EXPERT_TILE
    npad = N_GROUPS * te
    p = n + npad
    xg, meta, cnt = _moe_route(stage, w['w_router'], p)
    gid = meta[:, META_GROUP].astype(jnp.int32)
    rank = meta[:, META_RANK].astype(jnp.int32)
    counts = cnt[0, :N_GROUPS].astype(jnp.int32)
    padded = (counts + te - 1) // te * te
    ends = jnp.cumsum(padded)
    starts = ends - padded

    def lookup(table, idx):
        hit = idx[:, None] == jnp.arange(table.shape[0], dtype=jnp.int32)[None, :]
        return jnp.sum(jnp.where(hit, table[None, :], 0), axis=1)

    pos = lookup(starts, gid) + rank
    gap_len = jnp.concatenate([padded - counts, p - ends[-1:]])
    gap_start = jnp.concatenate([starts + counts, ends[-1:]])
    gap_end = jnp.cumsum(gap_len)
    k = jnp.arange(npad, dtype=jnp.int32)
    seg = jnp.sum(k[:, None] >= gap_end[None, :], axis=1)
    filler_pos = lookup(gap_start, seg) + k - lookup(gap_end - gap_len, seg)
    pos_all = jnp.concatenate([pos, filler_pos]).astype(jnp.int32)
    tile_start = jnp.arange(p // te, dtype=jnp.int32) * te
    tile_group = jnp.minimum(jnp.sum(tile_start[:, None] >= ends[None, :], axis=1),
                             N_GROUPS - 1).astype(jnp.int32)
    n_used = (ends[-1:] // te).astype(jnp.int32)
    xs = _sc_scatter_rows(xg, pos_all)
    ys = _moe_experts(xs, tile_group, n_used, w, layer)
    y = _sc_gather_rows(ys, pos)
    return _res_ln_stage(xg, y, g, beta, alpha)


SC_CORES = 2
SC_SUBCORES = 16
SC_GATHER_ROWS = 32


def _sc_gather_rows(table, idx):
    b = idx.shape[0]
    _, d = table.shape
    nw = SC_CORES * SC_SUBCORES
    ch = SC_GATHER_ROWS
    per_w = b // nw
    assert b % (nw * 2 * ch) == 0
    npair = per_w // (2 * ch)
    mesh = plsc.VectorSubcoreMesh(core_axis_name="c", subcore_axis_name="s")

    @functools.partial(
        pl.kernel, mesh=mesh, out_type=jax.ShapeDtypeStruct((b, d), table.dtype),
        scratch_types=[pltpu.VMEM((2, ch), jnp.int32), pltpu.VMEM((2, ch, d), table.dtype),
                       pltpu.SemaphoreType.DMA, pltpu.SemaphoreType.DMA],
        name="sc_gather_rows")
    def gather(table_hbm, idx_hbm, out_hbm, idx_v, rows_v, sem0, sem1):
        sems = (sem0, sem1)
        wid = lax.axis_index("s") * SC_CORES + lax.axis_index("c")
        base = wid * per_w

        def start(chunk, slot):
            off = pl.multiple_of(base + chunk * ch, SUBLANES)
            pltpu.sync_copy(idx_hbm.at[pl.ds(off, ch)], idx_v.at[slot])
            pltpu.async_copy(table_hbm.at[idx_v.at[slot]], rows_v.at[slot], sems[slot])

        def finish(chunk, slot):
            off = pl.multiple_of(base + chunk * ch, SUBLANES)
            pltpu.make_async_copy(table_hbm.at[idx_v.at[slot]], rows_v.at[slot],
                                  sems[slot]).wait()
            pltpu.sync_copy(rows_v.at[slot], out_hbm.at[pl.ds(off, ch)])

        start(0, 0)

        @pl.loop(0, npair)
        def _(p):
            c0 = 2 * p
            start(c0 + 1, 1)
            finish(c0, 0)

            @pl.when(p + 1 < npair)
            def _():
                start(c0 + 2, 0)
            finish(c0 + 1, 1)

    return gather(table, idx)


def _sc_scatter_rows(rows, idx):
    b, d = rows.shape
    nw = SC_CORES * SC_SUBCORES
    ch = SC_GATHER_ROWS
    per_w = b // nw
    assert idx.shape == (b,) and b % (nw * 2 * ch) == 0
    npair = per_w // (2 * ch)
    mesh = plsc.VectorSubcoreMesh(core_axis_name="c", subcore_axis_name="s")

    @functools.partial(
        pl.kernel, mesh=mesh, out_type=jax.ShapeDtypeStruct((b, d), rows.dtype),
        scratch_types=[pltpu.VMEM((2, ch), jnp.int32), pltpu.VMEM((2, ch, d), rows.dtype),
                       pltpu.SemaphoreType.DMA, pltpu.SemaphoreType.DMA],
        name="sc_scatter_rows")
    def scatter(rows_hbm, idx_hbm, out_hbm, idx_v, rows_v, sem0, sem1):
        sems = (sem0, sem1)
        wid = lax.axis_index("s") * SC_CORES + lax.axis_index("c")
        base = wid * per_w

        def write(slot):
            return pltpu.make_async_copy(rows_v.at[slot], out_hbm.at[idx_v.at[slot]],
                                         sems[slot])

        @pl.loop(0, npair)
        def _(p):
            for slot in range(2):
                @pl.when(p > 0)
                def _():
                    write(slot).wait()
                off = pl.multiple_of(base + (2 * p + slot) * ch, SUBLANES)
                pltpu.sync_copy(idx_hbm.at[pl.ds(off, ch)], idx_v.at[slot])
                pltpu.sync_copy(rows_hbm.at[pl.ds(off, ch)], rows_v.at[slot])
                write(slot).start()

        write(0).wait()
        write(1).wait()

    return scatter(rows, idx)


def _c_in_kernel(*refs, n_in, value_fn):
    w_ref, x_out, u_ref, u2_ref, us = refs[n_in:n_in + 5]
    x = value_fn(*refs[:n_in], *refs[n_in + 5:])
    x_out[...] = x
    u = _bdot(x, w_ref[...])
    u_ref[...] = u
    rows = u2_ref.shape[0]
    for c in range(S5_WIDTH // LANES):
        us[c] = u[:, c * LANES:(c + 1) * LANES]
    for s in range(S5_L):
        for c in range(S5_WIDTH // LANES):
            lo = s * S5_WIDTH + c * LANES
            u2_ref[:, lo:lo + LANES] = us[c, pl.ds(s, rows, stride=S5_L), :].astype(u2_ref.dtype)


def _c_in(stage, w_in):
    value_fn, row_inputs, consts, scratch, n, dm = stage
    tm = min(ROW_TILE, n)
    wu = w_in.shape[1]
    arrays = [a for a, _ in row_inputs] + list(consts)
    return pl.pallas_call(
        functools.partial(_c_in_kernel, n_in=len(arrays), value_fn=value_fn), name="c_in",
        grid=(n // tm,),
        in_specs=[_row_spec(rows, a.shape[1]) for a, rows in row_inputs]
                 + [_full_spec(c) for c in consts] + [_full_spec(w_in)],
        out_specs=[_row_spec(tm, dm), _row_spec(tm, wu), _row_spec(tm // S5_L, S5_L * wu)],
        out_shape=[jax.ShapeDtypeStruct((n, dm), F32),
                   jax.ShapeDtypeStruct((n, wu), F32),
                   jax.ShapeDtypeStruct((n // S5_L, S5_L * wu), BF16)],
        scratch_shapes=[pltpu.VMEM((wu // LANES, tm, LANES), F32)] + list(scratch),
        compiler_params=_cparams(("parallel",)),
    )(*arrays, w_in)


def _s5_super_rows(u_ref, q):
    tiles = [u_ref[:, s * S5_WIDTH + q * LANES:s * S5_WIDTH + (q + 1) * LANES]
             for s in range(S5_L)]
    return jnp.concatenate(tiles, axis=1).astype(BF16)


def _s5_e_kernel(u_ref, ere_ref, eim_ref, ere_out, eim_out):
    for q in range(S5_NSUPER):
        uq = _s5_super_rows(u_ref, q)
        sl = slice(q * S5_SUPER_STATE, (q + 1) * S5_SUPER_STATE)
        ere_out[:, sl] = jnp.dot(uq, ere_ref[q], preferred_element_type=F32)
        eim_out[:, sl] = jnp.dot(uq, eim_ref[q], preferred_element_type=F32)


def _const_spec(a):
    nd = a.ndim
    return pl.BlockSpec(a.shape, lambda *_: (0,) * nd, pipeline_mode=pl.Buffered(1))


def _s5_e(u2, tabs):
    rows, wu = u2.shape
    tr = min(S5_ROW_TILE, rows)
    out = jax.ShapeDtypeStruct((rows, S5_STATE_W), F32)
    return pl.pallas_call(
        _s5_e_kernel, name="s5_e",
        grid=(rows // tr,),
        in_specs=[_row_spec(tr, wu), _const_spec(tabs['e_re']), _const_spec(tabs['e_im'])],
        out_specs=[_row_spec(tr, S5_STATE_W)] * 2,
        out_shape=[out, out],
        compiler_params=_cparams(("parallel",)),
    )(u2, tabs['e_re'], tabs['e_im'])


def _s5_scan_kernel(ere_ref, eim_ref, h0r_ref, h0i_ref, stp_re_ref, stp_im_ref,
                    apw_re_ref, apw_im_ref, hpr_ref, hpi_ref, hfr_ref, hfi_ref, *, nc, last_row):
    w = ere_ref.shape[-1]
    row = lax.broadcasted_iota(jnp.int32, (SUBLANES, w), 0)
    apr = apw_re_ref[...]
    api = apw_im_ref[...]

    def tile(i, carry):
        cr, ci = carry
        start = pl.multiple_of(i * SUBLANES, SUBLANES)
        sr = ere_ref[pl.ds(start, SUBLANES), :]
        si = eim_ref[pl.ds(start, SUBLANES), :]
        for k in range(3):
            s = 1 << k
            keep = row >= s
            ar = stp_re_ref[k:k + 1, :]
            ai = stp_im_ref[k:k + 1, :]
            pr = pltpu.roll(sr, s, 0)
            pi = pltpu.roll(si, s, 0)
            sr, si = (jnp.where(keep, sr + ar * pr - ai * pi, sr),
                      jnp.where(keep, si + ar * pi + ai * pr, si))
        hr = sr + apr * cr - api * ci
        hi = si + apr * ci + api * cr
        first = row == 0
        hpr_ref[pl.ds(start, SUBLANES), :] = jnp.where(first, cr, pltpu.roll(hr, 1, 0))
        hpi_ref[pl.ds(start, SUBLANES), :] = jnp.where(first, ci, pltpu.roll(hi, 1, 0))
        return hr, hi

    def body(i, carry):
        hr, hi = tile(i, carry)
        return hr[SUBLANES - 1:SUBLANES, :], hi[SUBLANES - 1:SUBLANES, :]

    ntile = nc // SUBLANES
    carry = lax.fori_loop(0, ntile - 1, body, (h0r_ref[...], h0i_ref[...]))
    hr, hi = tile(ntile - 1, carry)
    hfr_ref[...] = hr[last_row:last_row + 1, :]
    hfi_ref[...] = hi[last_row:last_row + 1, :]


def _s5_scan(e_re, e_im, h0r, h0i, tabs, n_chunks):
    b, nc, w = e_re.shape
    last_row = (n_chunks - 1) % SUBLANES
    wb = S5_SCAN_LANES
    consts = [tabs['step_re'], tabs['step_im'], tabs['apow_re'], tabs['apow_im']]
    seq = pl.BlockSpec((None, nc, wb), lambda bi, li: (bi, 0, li))
    vec = pl.BlockSpec((None, 1, wb), lambda bi, li: (bi, 0, li))
    return pl.pallas_call(
        functools.partial(_s5_scan_kernel, nc=nc, last_row=last_row), name="s5_scan",
        grid=(b, w // wb),
        in_specs=[seq, seq, vec, vec]
                 + [pl.BlockSpec((c.shape[0], wb), lambda bi, li: (0, li)) for c in consts],
        out_specs=[seq, seq, vec, vec],
        out_shape=[jax.ShapeDtypeStruct((b, nc, w), F32)] * 2
                  + [jax.ShapeDtypeStruct((b, 1, w), F32)] * 2,
        compiler_params=_cparams(("parallel", "parallel")),
    )(e_re, e_im, h0r, h0i, *consts)


def _s5_y_kernel(u_ref, hpr_ref, hpi_ref, m_ref, fre_ref, fim_ref, y_ref):
    for q in range(S5_NSUPER):
        uq = _s5_super_rows(u_ref, q)
        sl = slice(q * S5_SUPER_STATE, (q + 1) * S5_SUPER_STATE)
        yq = (jnp.dot(uq, m_ref[q], preferred_element_type=F32)
              + _bdot(hpr_ref[:, sl], fre_ref[q])
              + _bdot(hpi_ref[:, sl], fim_ref[q]))
        for t in range(S5_L):
            y_ref[:, t * S5_WIDTH + q * LANES:t * S5_WIDTH + (q + 1) * LANES] = (
                yq[:, t * LANES:(t + 1) * LANES])


def _s5_y(u2, hp_re, hp_im, tabs):
    rows, wu = u2.shape
    tr = min(S5_ROW_TILE, rows)
    consts = [tabs['m'], tabs['f_re'], tabs['f_im']]
    return pl.pallas_call(
        _s5_y_kernel, name="s5_y",
        grid=(rows // tr,),
        in_specs=[_row_spec(tr, wu), _row_spec(tr, S5_STATE_W), _row_spec(tr, S5_STATE_W)]
                 + [_const_spec(c) for c in consts],
        out_specs=_row_spec(tr, wu),
        out_shape=jax.ShapeDtypeStruct((rows, wu), F32),
        compiler_params=_cparams(("parallel",)),
    )(u2, hp_re, hp_im, *consts)


def _c_out_value(y2_ref, u_ref, x_ref, d_ref, wglu_ref, wout_ref, g_ref, beta_ref, ys, *, alpha):
    rows = y2_ref.shape[0]
    ntile = S5_WIDTH // LANES
    for s in range(S5_L):
        for c in range(ntile):
            lo = s * S5_WIDTH + c * LANES
            ys[c, pl.ds(s, rows, stride=S5_L), :] = y2_ref[:, lo:lo + LANES]
    y = jnp.concatenate([ys[c] for c in range(ntile)], axis=1) + d_ref[...] * u_ref[...]
    vg = _bdot(jax.nn.gelu(y), wglu_ref[...])
    half = vg.shape[1] // 2
    z = vg[:, :half] * jax.nn.sigmoid(vg[:, half:])
    o = _bdot(z, wout_ref[...])
    return _layer_norm(alpha * x_ref[...] + o, g_ref[...], beta_ref[...])


def _c_out_stage(y2, u, x, w, g, beta, alpha):
    n, dm = x.shape
    tm = min(ROW_TILE, n)
    return (functools.partial(_c_out_value, alpha=alpha),
            [(y2, tm // S5_L), (u, tm), (x, tm)], [w['d'], w['w_glu'], w['w_out'], g, beta],
            [pltpu.VMEM((u.shape[1] // LANES, tm, LANES), F32)], n, dm)


def _prep_ab(w_in, q_g, kv_g, w_uq, w_ukv, conv_w, conv_b, wa, ba, wx, bx, a_param, w_out):
    dm = w_in.shape[0]
    s0 = Q_LORA
    s1 = s0 + KV_LORA
    s2 = s1 + QK_ROPE
    s3 = s2 + LRU_WIDTH
    pad_lo = jnp.zeros((dm, QK_NOPE), F32)
    pad_hi = jnp.zeros((dm, HEAD_PAD - QK_HEAD), F32)
    w_in_p = jnp.concatenate([w_in[:, :s1], w_in[:, s2:s3], w_in[:, s3:],
                              pad_lo, w_in[:, s1:s2], pad_hi], axis=1).astype(BF16)
    wq = w_uq.reshape(Q_LORA, MLA_HEADS, QK_HEAD)
    wq = jnp.pad(wq, ((0, 0), (0, 0), (0, HEAD_PAD - QK_HEAD)))
    wq = wq.reshape(Q_LORA, MLA_HEADS * HEAD_PAD).astype(BF16)
    wkv = w_ukv.reshape(KV_LORA, MLA_HEADS, QK_NOPE + V_HEAD)
    wk = jnp.pad(wkv[:, :, :QK_NOPE], ((0, 0), (0, 0), (0, HEAD_PAD - QK_NOPE)))
    wk = wk.reshape(KV_LORA, MLA_HEADS * HEAD_PAD).astype(BF16)
    wv = jnp.pad(wkv[:, :, QK_NOPE:], ((0, 0), (0, 0), (0, HEAD_PAD - V_HEAD)))
    wv = wv.reshape(KV_LORA, MLA_HEADS * HEAD_PAD).astype(BF16)
    wukt = jnp.transpose(wkv[:, :, :QK_NOPE], (1, 2, 0)).astype(BF16)
    wuv_h = jnp.transpose(wkv[:, :, QK_NOPE:], (1, 0, 2)).astype(BF16)
    eye = jnp.eye(LRU_BLOCKS, dtype=F32)

    def blockdiag(wb):
        return jnp.einsum('hij,hk->hikj', wb, eye).reshape(LRU_WIDTH, LRU_WIDTH)

    w_ax = jnp.concatenate([blockdiag(wa), blockdiag(wx)], axis=1).astype(BF16)
    b_ax = jnp.concatenate([ba, bx])[None, :]
    att_w = MLA_HEADS * V_HEAD
    return dict(w_in=w_in_p, q_g=q_g[None, :], kv_g=kv_g[None, :], w_uq=wq, w_uk=wk, w_uv=wv,
                wukt=wukt, wuv_h=wuv_h, conv_w=conv_w, conv_b=conv_b[None, :], w_ax=w_ax,
                b_ax=b_ax, a_param=jax.nn.softplus(-a_param)[None, :],
                w_out_att=w_out[:att_w].astype(BF16), w_out_rec=w_out[att_w:].astype(BF16))


def _rope_tables(row_pos, tile_pos):
    half = QK_ROPE // 2
    inv = ROPE_THETA ** (-jnp.arange(half, dtype=F32) / half)
    inv_lane = jnp.concatenate([jnp.zeros((QK_NOPE,), F32), inv, inv,
                                jnp.zeros((HEAD_PAD - QK_HEAD,), F32)])
    ang_row = row_pos.astype(F32)[:, None] * inv_lane[None, :]
    ang_tile = tile_pos.astype(F32)[:, None, None] * inv_lane
    return jnp.cos(ang_row), jnp.sin(ang_row), jnp.cos(ang_tile), jnp.sin(ang_tile)


def _prep_router(w_group, w_expert):
    dm = w_group.shape[0]
    pad = jnp.zeros((dm, ROUTER_W - N_GROUPS - N_EXPERTS), F32)
    w_router = jnp.concatenate([w_group, w_expert, pad], axis=1)
    w_router_hi = w_router.astype(BF16)
    w_router_lo = (w_router - w_router_hi.astype(F32)).astype(BF16)
    return dict(w_router=jnp.concatenate([w_router_hi, w_router_lo], axis=1))


def _cmul(ar, ai, br, bi):
    return ar * br - ai * bi, ar * bi + ai * br


def _prep_s5(lam_re, lam_im, log_step, b_re, b_im, c_re, c_im):
    hi = lax.Precision.HIGHEST
    L, G, P, C = S5_L, S5_GROUPS, S5_STATE, S5_GROUP
    lr, li = lam_re, lam_im
    dt = jnp.exp(log_step)[:, None]
    mag = jnp.exp(lr * dt)
    ar, ai = mag * jnp.cos(li * dt), mag * jnp.sin(li * dt)
    den = lr * lr + li * li
    nr = ar - 1.0
    cr = (nr * lr + ai * li) / den
    ci = (ai * lr - nr * li) / den
    bbr = cr[..., None] * b_re - ci[..., None] * b_im
    bbi = cr[..., None] * b_im + ci[..., None] * b_re
    prs, pis = [jnp.ones_like(ar)], [jnp.zeros_like(ar)]
    for _ in range(L):
        nr_, ni_ = _cmul(prs[-1], pis[-1], ar, ai)
        prs.append(nr_)
        pis.append(ni_)
    pr = jnp.stack(prs)
    pi = jnp.stack(pis)
    xr = pr[:L, :, :, None] * bbr - pi[:L, :, :, None] * bbi
    xi = pr[:L, :, :, None] * bbi + pi[:L, :, :, None] * bbr
    kk = (jnp.einsum('gcp,kgpd->kgcd', c_re, xr, precision=hi)
          - jnp.einsum('gcp,kgpd->kgcd', c_im, xi, precision=hi))
    p1r = pr[1:, :, :, None]
    p1i = pi[1:, :, :, None]
    cre_t = jnp.transpose(c_re, (0, 2, 1))[None]
    cim_t = jnp.transpose(c_im, (0, 2, 1))[None]
    f_re = cre_t * p1r - cim_t * p1i
    f_im = -cre_t * p1i - cim_t * p1r

    nq, sg = S5_NSUPER, S5_SUPER
    eye = jnp.eye(sg, dtype=F32)
    dk = jnp.einsum('kqgcd,gh->kqgdhc', kk.reshape(L, nq, sg, C, C), eye)
    dk = dk.reshape(L, nq, LANES, LANES).astype(BF16)
    lag = jnp.arange(L)[None, :] - jnp.arange(L)[:, None]
    m = jnp.where((lag >= 0)[:, :, None, None, None], dk[jnp.clip(lag, 0, L - 1)], 0)
    m = jnp.transpose(m, (2, 0, 3, 1, 4)).reshape(nq, L * LANES, L * LANES)

    row_g = (jnp.arange(L * LANES) // C) % sg
    col_g = jnp.arange(sg * P) // P
    same_group = (row_g[:, None] == col_g[None, :])[None]

    def pack_rows(a):
        a = jnp.transpose(a.reshape(L, nq, sg, P, C), (1, 0, 4, 2, 3))
        a = a.reshape(nq, L, 1, C, sg * P).astype(BF16)
        a = jnp.broadcast_to(a, (nq, L, sg, C, sg * P)).reshape(nq, L * LANES, sg * P)
        return jnp.where(same_group, a, 0)

    def pack_e(x):
        return pack_rows(x[::-1])

    def pack_f(f):
        return jnp.swapaxes(pack_rows(f), 1, 2)

    alr, ali = pr[L].reshape(1, G * P), pi[L].reshape(1, G * P)
    qr, qi = [alr], [ali]
    for _ in range(SUBLANES - 1):
        nr_, ni_ = _cmul(qr[-1], qi[-1], alr, ali)
        qr.append(nr_)
        qi.append(ni_)
    apow_re = jnp.concatenate(qr, axis=0)
    apow_im = jnp.concatenate(qi, axis=0)
    step_re = jnp.concatenate([qr[0], qr[1], qr[3]], axis=0)
    step_im = jnp.concatenate([qi[0], qi[1], qi[3]], axis=0)
    return dict(m=m, e_re=pack_e(xr), e_im=pack_e(xi), f_re=pack_f(f_re),
                f_im=pack_f(f_im), apow_re=apow_re, apow_im=apow_im,
                step_re=step_re, step_im=step_im)


def _mixer_ab(x, pos0, past_ckv, past_krope, conv0, h0, w):
    b, t, dm = x.shape
    xf = x.reshape(b * t, dm)
    tm = min(ROW_TILE, b * t)
    assert tm % t == 0 or t % tm == 0
    row_pos = jnp.arange(tm) % t
    tile_pos = pos0 + (jnp.arange(b * t // tm) * tm) % t
    q, k, v, c_new, kr_new, xb, yb = _ab_in(xf, w, _rope_tables(row_pos, tile_pos))
    if past_ckv is None:
        att = _attn_prompt(q, k, v)
    else:
        att = _attn_sample(q, c_new, kr_new, past_ckv, past_krope, w['wukt'], w['wuv_h'])
    rec, conv_new, h_new = _lru(xb.reshape(b, t, -1), yb.reshape(b, t, -1), conv0,
                                h0[:, None, :], w)
    return (att, rec.reshape(b * t, -1), c_new.reshape(b, t, -1), kr_new.reshape(b, t, -1),
            conv_new, h_new[:, 0, :])


def _mixer_c(x_stage, b, t, h0_re, h0_im, w):
    nc = t // S5_L
    assert t % S5_L == 0 and nc % SUBLANES == 0
    x, u, u2 = _c_in(x_stage, w['w_in'])
    e_re, e_im = _s5_e(u2, w['tabs'])
    hp_re, hp_im, hf_re, hf_im = _s5_scan(
        e_re.reshape(b, nc, S5_STATE_W), e_im.reshape(b, nc, S5_STATE_W),
        h0_re.reshape(b, 1, S5_STATE_W), h0_im.reshape(b, 1, S5_STATE_W), w['tabs'], nc)
    y2 = _s5_y(u2, hp_re.reshape(b * nc, S5_STATE_W), hp_im.reshape(b * nc, S5_STATE_W),
               w['tabs'])
    return (x, y2, u, hf_re.reshape(b, S5_GROUPS, S5_STATE),
            hf_im.reshape(b, S5_GROUPS, S5_STATE))


def kernel(x_prompt, x_sample, cache_mla_ckv, cache_mla_krope, state_lru_conv, state_lru_h,
           state_s5_re, state_s5_im, w_in_ab, q_norm_g, kv_norm_g, w_uq, w_ukv,
           lru_conv_w, lru_conv_b, lru_w_a, lru_b_a, lru_w_x, lru_b_x, lru_a_param, w_out_ab,
           w_in_c, s5_lam_re, s5_lam_im, s5_log_step, s5_b_re, s5_b_im, s5_c_re, s5_c_im,
           s5_d, s5_w_glu, w_out_c, ln_mix_g, ln_mix_b, ln_ffn_g, ln_ffn_b,
           moe_w_group, moe_w_expert, moe_w_gate, moe_w_up, moe_w_down):
    bp, tp, dm = x_prompt.shape
    bs, ts, _ = x_sample.shape
    past = cache_mla_ckv.shape[2]
    depth = ln_mix_g.shape[0]
    alpha = (2 * depth) ** 0.25
    assert (past + ts - 1) // CHUNK <= past // CHUNK
    hp = _array_stage(x_prompt.reshape(bp * tp, dm))
    hs = _array_stage(x_sample.reshape(bs * ts, dm))
    moe_experts = dict(w_gate=moe_w_gate.astype(BF16), w_up=moe_w_up.astype(BF16),
                       w_down=moe_w_down.astype(BF16))
    outs = {k: [] for k in ('ckv_p', 'ckv_s', 'kr_p', 'kr_s', 'cv_p', 'cv_s', 'lh_p', 'lh_s',
                            's5r_p', 's5r_s', 's5i_p', 's5i_s')}
    for layer in range(depth):
        j = layer // 2
        g_mix, b_mix = ln_mix_g[layer][None, :], ln_mix_b[layer][None, :]
        g_ffn, b_ffn = ln_ffn_g[layer][None, :], ln_ffn_b[layer][None, :]
        if layer % 2 == 0:
            w = _prep_ab(w_in_ab[j], q_norm_g[j], kv_norm_g[j], w_uq[j], w_ukv[j], lru_conv_w[j],
                         lru_conv_b[j], lru_w_a[j], lru_b_a[j], lru_w_x[j], lru_b_x[j],
                         lru_a_param[j], w_out_ab[j])
            xp, xs = _materialize(hp, "res_ln"), _materialize(hs, "res_ln")
            att_p, rec_p, c1, k1, v1, h1 = _mixer_ab(
                xp.reshape(bp, tp, dm), 0, None, None, jnp.zeros((bp, CONV_WIDTH - 1, LRU_WIDTH), F32),
                jnp.zeros((bp, LRU_WIDTH), F32), w)
            att_s, rec_s, c2, k2, v2, h2 = _mixer_ab(
                xs.reshape(bs, ts, dm), past, cache_mla_ckv[j], cache_mla_krope[j], state_lru_conv[j],
                state_lru_h[j], w)
            outs['ckv_p'].append(c1); outs['ckv_s'].append(c2)
            outs['kr_p'].append(k1); outs['kr_s'].append(k2)
            outs['cv_p'].append(v1); outs['cv_s'].append(v2)
            outs['lh_p'].append(h1); outs['lh_s'].append(h2)
            stage_p = _out_ln_stage(att_p, rec_p, xp, w['w_out_att'],
                                    w['w_out_rec'], g_mix, b_mix, alpha)
            stage_s = _out_ln_stage(att_s, rec_s, xs, w['w_out_att'],
                                    w['w_out_rec'], g_mix, b_mix, alpha)
            stage_name = "out_ln"
        else:
            w = dict(w_in=w_in_c[j].astype(BF16), d=s5_d[j][None, :],
                     w_glu=s5_w_glu[j].astype(BF16), w_out=w_out_c[j].astype(BF16),
                     tabs=_prep_s5(s5_lam_re[j], s5_lam_im[j], s5_log_step[j], s5_b_re[j],
                                   s5_b_im[j], s5_c_re[j], s5_c_im[j]))
            zero_state = jnp.zeros((bp, S5_GROUPS, S5_STATE), F32)
            xp, y_p, u_p, r1, i1 = _mixer_c(hp, bp, tp, zero_state, zero_state, w)
            xs, y_s, u_s, r2, i2 = _mixer_c(hs, bs, ts, state_s5_re[j], state_s5_im[j], w)
            outs['s5r_p'].append(r1); outs['s5r_s'].append(r2)
            outs['s5i_p'].append(i1); outs['s5i_s'].append(i2)
            stage_p = _c_out_stage(y_p, u_p, xp, w, g_mix, b_mix, alpha)
            stage_s = _c_out_stage(y_s, u_s, xs, w, g_mix, b_mix, alpha)
            stage_name = "c_out"
        wm = dict(moe_experts, **_prep_router(moe_w_group[layer], moe_w_expert[layer]))
        hp = _moe(stage_p, stage_name, wm, layer, g_ffn, b_ffn, alpha)
        hs = _moe(stage_s, stage_name, wm, layer, g_ffn, b_ffn, alpha)
    y_prompt = _materialize(hp, "res_ln").reshape(bp, tp, dm)
    y_sample = _materialize(hs, "res_ln").reshape(bs, ts, dm)
    st = lambda k: jnp.stack(outs[k])
    return (y_prompt, y_sample, st('ckv_p'), st('ckv_s'), st('kr_p'), st('kr_s'), st('cv_p'), st('cv_s'),
            st('lh_p'), st('lh_s'), st('s5r_p'), st('s5r_s'), st('s5i_p'), st('s5i_s'))
```

```python
import functools
import math

import jax
import jax.numpy as jnp
from jax import lax
from jax.experimental import pallas as pl
from jax.experimental.pallas import tpu as pltpu
from jax.experimental.pallas import tpu_sc as plsc

F32 = jnp.float32
BF16 = jnp.bfloat16

CHUNK = 64
MLA_HEADS = 8
QK_NOPE = 64
QK_ROPE = 32
QK_HEAD = QK_NOPE + QK_ROPE
V_HEAD = 64
Q_LORA = 256
KV_LORA = 128
ROPE_THETA = 10000.0
MLA_SCALE = QK_HEAD ** -0.5
NEG_INF = -1e30
LRU_WIDTH = 512
LRU_BLOCKS = 8
LRU_BLOCK = LRU_WIDTH // LRU_BLOCKS
CONV_WIDTH = 4
RG_C = 8.0
S5_WIDTH = 512
S5_GROUP = 16
S5_GROUPS = S5_WIDTH // S5_GROUP
S5_STATE = 64
N_GROUPS = 4
EXPERTS_PER_GROUP = 4
N_EXPERTS = N_GROUPS * EXPERTS_PER_GROUP
EXPERT_HIDDEN = 256
LN_EPS = 1e-5
RMS_EPS = 1e-6

LANES = 128
SUBLANES = 8
VMEM_LIMIT_BYTES = 48 * 1024 * 1024
MOE_VMEM_LIMIT_BYTES = 56 * 1024 * 1024

HEAD_PAD = LANES
ROW_TILE = 512
MOE_ROW_TILE = 1024
MOE_EXPERT_TILE = 512
ATTN_TQ = 1024
ATTN_TK = 512
ATTN_UNROLL = 8
Q_SCALE = MLA_SCALE * math.log2(math.e)
LRU_TILE = 512
S5_L = 8
S5_SUPER = LANES // S5_GROUP
S5_NSUPER = S5_GROUPS // S5_SUPER
S5_SUPER_IN = S5_L * LANES
S5_SUPER_STATE = S5_SUPER * S5_STATE
S5_STATE_W = S5_GROUPS * S5_STATE
S5_SCAN_LANES = 512
S5_ROW_TILE = 256
ROUTER_W = LANES


def _cparams(semantics, vmem_limit_bytes=VMEM_LIMIT_BYTES):
    return pltpu.CompilerParams(dimension_semantics=semantics,
                                vmem_limit_bytes=vmem_limit_bytes)


def _full_spec(a):
    nd = a.ndim
    return pl.BlockSpec(a.shape, lambda *_: (0,) * nd)


def _row_spec(tm, width):
    return pl.BlockSpec((tm, width), lambda i: (i, 0))


def _layer_norm(z, g, b):
    mu = jnp.mean(z, axis=-1, keepdims=True)
    zc = z - mu
    var = jnp.mean(zc * zc, axis=-1, keepdims=True)
    return zc * lax.rsqrt(var + LN_EPS) * g + b


def _rms_norm(z, g):
    ms = jnp.mean(z * z, axis=-1, keepdims=True)
    return z * lax.rsqrt(ms + RMS_EPS) * g


def _bdot(a, b):
    return jnp.dot(a.astype(BF16), b.astype(BF16), preferred_element_type=F32)


def _ab_in_kernel(x_ref, win_ref, qg_ref, kvg_ref, wuq_ref, wuk_ref, wuv_ref,
                  cosb_ref, sinb_ref, cost_ref, sint_ref,
                  q_out, k_out, v_out, c_out, kr_out, xb_out, yb_out, *, dm):
    proj = _bdot(x_ref[...], win_ref[...])
    o0 = Q_LORA
    o1 = o0 + KV_LORA
    o2 = o1 + LRU_WIDTH
    o3 = o2 + LRU_WIDTH
    q_lat = proj[:, 0:o0]
    kv_lat = proj[:, o0:o1]
    xb_out[...] = proj[:, o1:o2]
    yb_out[...] = proj[:, o2:o3]
    krp = proj[:, o3:o3 + HEAD_PAD]

    qn = _rms_norm(q_lat, qg_ref[...])
    cn = _rms_norm(kv_lat, kvg_ref[...])
    c_out[...] = cn
    q = _bdot(qn, wuq_ref[...])
    kk = _bdot(cn, wuk_ref[...])
    lane = lax.broadcasted_iota(jnp.int32, (1, MLA_HEADS * HEAD_PAD), 1)
    ones_col = (lane % HEAD_PAD == V_HEAD).astype(F32)
    v_out[...] = (_bdot(cn, wuv_ref[...]) + ones_col).astype(v_out.dtype)

    cb, sb = cosb_ref[...], sinb_ref[...]
    ct, st = cost_ref[...], sint_ref[...]
    cos = cb * ct - sb * st
    sin = sb * ct + cb * st
    half = QK_ROPE // 2
    lane1 = lax.broadcasted_iota(jnp.int32, (1, HEAD_PAD), 1)
    rc = jnp.where(lane1 < QK_HEAD, cos, 0.0)
    ra = jnp.where((lane1 >= QK_NOPE + half) & (lane1 < QK_HEAD), sin, 0.0)
    rb = jnp.where((lane1 >= QK_NOPE) & (lane1 < QK_NOPE + half), -sin, 0.0)

    def rope(z):
        return (z * rc + pltpu.roll(z, half, 1) * ra
                + pltpu.roll(z, HEAD_PAD - half, 1) * rb)

    kr = rope(krp)
    kr_out[...] = kr[:, QK_NOPE:QK_NOPE + QK_ROPE]
    for h in range(MLA_HEADS):
        sl = slice(h * HEAD_PAD, (h + 1) * HEAD_PAD)
        q_out[:, sl] = (rope(q[:, sl]) * Q_SCALE).astype(q_out.dtype)
        k_out[:, sl] = (kk[:, sl] + kr).astype(k_out.dtype)


def _ab_in(x, w, rope_tabs):
    n, dm = x.shape
    tm = min(ROW_TILE, n)
    cosb, sinb, cost, sint = rope_tabs
    consts = [w['w_in'], w['q_g'], w['kv_g'], w['w_uq'], w['w_uk'], w['w_uv'], cosb, sinb]
    tile_spec = pl.BlockSpec((None, 1, HEAD_PAD), lambda i: (i, 0, 0))
    hp = MLA_HEADS * HEAD_PAD
    out_shape = [
        jax.ShapeDtypeStruct((n, hp), BF16),
        jax.ShapeDtypeStruct((n, hp), BF16),
        jax.ShapeDtypeStruct((n, hp), BF16),
        jax.ShapeDtypeStruct((n, KV_LORA), F32),
        jax.ShapeDtypeStruct((n, QK_ROPE), F32),
        jax.ShapeDtypeStruct((n, LRU_WIDTH), F32),
        jax.ShapeDtypeStruct((n, LRU_WIDTH), F32),
    ]
    return pl.pallas_call(
        functools.partial(_ab_in_kernel, dm=dm), name="ab_in",
        grid=(n // tm,),
        in_specs=([_row_spec(tm, dm)] + [_full_spec(c) for c in consts]
                  + [tile_spec, tile_spec]),
        out_specs=[_row_spec(tm, s.shape[1]) for s in out_shape],
        out_shape=out_shape,
        compiler_params=_cparams(("parallel",)),
    )(x, *consts, cost, sint)


def _attn_prompt_kernel(q_ref, k_ref, v_ref, o_ref):
    i = pl.program_id(1)
    tq, tk = ATTN_TQ, ATTN_TK
    ndiag = tq // tk
    nfull = i * ndiag
    row_chunk = lax.broadcasted_iota(jnp.int32, (tq, tk), 0) // CHUNK
    col_chunk = lax.broadcasted_iota(jnp.int32, (tq, tk), 1) // CHUNK
    qs = [q_ref[:, hh * HEAD_PAD:(hh + 1) * HEAD_PAD] for hh in range(2)]

    def update(hh, j, m, acc, mask, row0=0):
        start = pl.multiple_of(j * tk, tk)
        k = k_ref[pl.ds(start, tk), hh * HEAD_PAD:(hh + 1) * HEAD_PAD]
        v = v_ref[pl.ds(start, tk), hh * HEAD_PAD:(hh + 1) * HEAD_PAD]
        s = lax.dot_general(qs[hh][row0:], k, (((1,), (1,)), ((), ())),
                            preferred_element_type=F32)
        if mask is not None:
            s = jnp.where(mask[row0:], s, NEG_INF)
        m_new = jnp.maximum(m, jnp.max(s, axis=-1, keepdims=True))
        alpha = jnp.exp2(m - m_new)
        p = jnp.exp2((s - m_new).astype(BF16))
        return m_new, alpha * acc + jnp.dot(p, v, preferred_element_type=F32)

    def body(j, carry):
        new = []
        for hh in range(2):
            new += update(hh, j, carry[2 * hh], carry[2 * hh + 1], None)
        return tuple(new)

    unroll = ATTN_UNROLL

    def body_unrolled(jj, carry):
        for b in range(unroll):
            carry = body(unroll * jj + b, carry)
        return carry

    init = []
    for _ in range(2):
        init += [jnp.full((tq, 1), NEG_INF, F32), jnp.zeros((tq, HEAD_PAD), F32)]
    carry = lax.fori_loop(0, nfull // unroll, body_unrolled, tuple(init))
    rem0 = nfull - nfull % unroll

    def body_rem(jj, carry):
        for b in range(ndiag):
            carry = body(rem0 + ndiag * jj + b, carry)
        return carry

    carry = list(lax.fori_loop(0, (nfull % unroll) // ndiag, body_rem, carry))
    for d in range(ndiag):
        visible = col_chunk + d * (tk // CHUNK) <= row_chunk
        r0 = d * tk
        for hh in range(2):
            m, acc = carry[2 * hh], carry[2 * hh + 1]
            m_new, acc_new = update(hh, nfull + d, m[r0:], acc[r0:], visible, r0)
            if r0:
                m_new = jnp.concatenate([m[:r0], m_new], axis=0)
                acc_new = jnp.concatenate([acc[:r0], acc_new], axis=0)
            carry[2 * hh], carry[2 * hh + 1] = m_new, acc_new
    outs = [carry[2 * hh + 1][:, :V_HEAD] / carry[2 * hh + 1][:, V_HEAD:V_HEAD + 1]
            for hh in range(2)]
    o_ref[...] = jnp.concatenate(outs, axis=-1).astype(o_ref.dtype)


def _attn_prompt(q, k, v):
    t = q.shape[0]
    pairs = MLA_HEADS // 2
    return pl.pallas_call(
        _attn_prompt_kernel, name="attn_prompt",
        grid=(pairs, t // ATTN_TQ),
        in_specs=[
            pl.BlockSpec((ATTN_TQ, 2 * HEAD_PAD), lambda p, i: (i, p)),
            pl.BlockSpec((t, 2 * HEAD_PAD), lambda p, i: (0, p)),
            pl.BlockSpec((t, 2 * HEAD_PAD), lambda p, i: (0, p)),
        ],
        out_specs=pl.BlockSpec((ATTN_TQ, 2 * V_HEAD), lambda p, i: (i, p)),
        out_shape=jax.ShapeDtypeStruct((t, MLA_HEADS * V_HEAD), BF16),
        compiler_params=_cparams(("parallel", "parallel")),
    )(q, k, v)


def _attn_sample_kernel(q_ref, cn_ref, krn_ref, cp_ref, krp_ref, wukt_ref, wuv_ref, o_ref):
    cp = cp_ref[...].astype(BF16)
    krp = krp_ref[...].astype(BF16)
    cn = cn_ref[...].astype(BF16)
    krn = krn_ref[...].astype(BF16)
    ts = cn.shape[0]
    dn = (((1,), (1,)), ((), ()))
    outs = []
    for hp in range(MLA_HEADS // 2):
        qa, qr = [], []
        for hh in range(2):
            h = 2 * hp + hh
            qh = q_ref[:, h * HEAD_PAD:(h + 1) * HEAD_PAD]
            qa.append(jnp.dot(qh[:, :QK_NOPE], wukt_ref[h], preferred_element_type=F32))
            qr.append(qh[:, QK_NOPE:QK_NOPE + QK_ROPE])
        qa = jnp.concatenate(qa, axis=0).astype(BF16)
        qr = jnp.concatenate(qr, axis=0)
        s_past = (lax.dot_general(qa, cp, dn, preferred_element_type=F32)
                  + lax.dot_general(qr, krp, dn, preferred_element_type=F32))
        s_new = (lax.dot_general(qa, cn, dn, preferred_element_type=F32)
                 + lax.dot_general(qr, krn, dn, preferred_element_type=F32))
        m = jnp.maximum(jnp.max(s_past, axis=-1, keepdims=True),
                        jnp.max(s_new, axis=-1, keepdims=True))
        p_past = jnp.exp2(s_past - m)
        p_new = jnp.exp2(s_new - m)
        l = (jnp.sum(p_past, axis=-1, keepdims=True)
             + jnp.sum(p_new, axis=-1, keepdims=True))
        o_lat = (jnp.dot(p_past.astype(BF16), cp, preferred_element_type=F32)
                 + jnp.dot(p_new.astype(BF16), cn, preferred_element_type=F32)) / l
        for hh in range(2):
            h = 2 * hp + hh
            outs.append(_bdot(o_lat[hh * ts:(hh + 1) * ts], wuv_ref[h]))
    o_ref[...] = jnp.concatenate(outs, axis=-1).astype(o_ref.dtype)


def _attn_sample(q, c_new, kr_new, c_past, kr_past, wukt, wuv):
    bs, past, _ = c_past.shape
    ts = q.shape[0] // bs
    return pl.pallas_call(
        _attn_sample_kernel, name="attn_sample",
        grid=(bs,),
        in_specs=[
            pl.BlockSpec((ts, MLA_HEADS * HEAD_PAD), lambda b: (b, 0)),
            pl.BlockSpec((ts, KV_LORA), lambda b: (b, 0)),
            pl.BlockSpec((ts, QK_ROPE), lambda b: (b, 0)),
            pl.BlockSpec((None, past, KV_LORA), lambda b: (b, 0, 0)),
            pl.BlockSpec((None, past, QK_ROPE), lambda b: (b, 0, 0)),
            _full_spec(wukt), _full_spec(wuv),
        ],
        out_specs=pl.BlockSpec((ts, MLA_HEADS * V_HEAD), lambda b: (b, 0)),
        out_shape=jax.ShapeDtypeStruct((bs * ts, MLA_HEADS * V_HEAD), BF16),
        compiler_params=_cparams(("parallel",)),
    )(q, c_new, kr_new, c_past, kr_past, wukt, wuv)


def _lru_kernel(xb_ref, yb_ref, conv0_ref, h0_ref, cw_ref, cb_ref, wax_ref, bax_ref, ap_ref,
                rec_ref, conv_out_ref, h_out_ref, xcat, hc, *, tb):
    t = pl.program_id(1)
    nt = pl.num_programs(1)
    tail = CONV_WIDTH - 1
    base = SUBLANES

    @pl.when(t == 0)
    def _():
        xcat[base - tail:base, :] = conv0_ref[...]
        hc[...] = h0_ref[...]

    @pl.when(t > 0)
    def _():
        xcat[0:base, :] = xcat[tb:tb + base, :]

    xcat[base:base + tb, :] = xb_ref[...]
    xc = cb_ref[...]
    for tap in range(CONV_WIDTH):
        xc = xc + xcat[base - tail + tap:base - tail + tap + tb, :] * cw_ref[tap:tap + 1, :]

    gates = _bdot(xc, wax_ref[...]) + bax_ref[...]
    r = jax.nn.sigmoid(gates[:, :LRU_WIDTH])
    ig = jax.nn.sigmoid(gates[:, LRU_WIDTH:])
    log_a = -RG_C * r * ap_ref[...]
    a = jnp.exp(log_a)
    u = jnp.sqrt(jnp.tanh(-log_a) * (1.0 + a * a)) * (ig * xc)

    row8 = lax.broadcasted_iota(jnp.int32, (tb, LRU_WIDTH), 0) % SUBLANES
    s = 1
    while s < SUBLANES:
        keep = row8 >= s
        u = jnp.where(keep, a * pltpu.roll(u, s, 0) + u, u)
        a = jnp.where(keep, a * pltpu.roll(a, s, 0), a)
        s *= 2
    hprev = hc[...]
    groups = []
    for g in range(tb // SUBLANES):
        sl = slice(g * SUBLANES, (g + 1) * SUBLANES)
        hg = a[sl] * hprev + u[sl]
        groups.append(hg)
        hprev = hg[SUBLANES - 1:SUBLANES, :]
    h = jnp.concatenate(groups, axis=0)
    hc[...] = hprev
    rec_ref[...] = (h * jax.nn.gelu(yb_ref[...])).astype(rec_ref.dtype)

    @pl.when(t == nt - 1)
    def _():
        conv_out_ref[...] = xcat[base + tb - tail:base + tb, :]
        h_out_ref[...] = h[tb - 1:tb, :]


def _lru(xb, yb, conv0, h0, w):
    b, t, wd = xb.shape
    tb = min(LRU_TILE, t)
    tail = CONV_WIDTH - 1
    consts = [w['conv_w'], w['conv_b'], w['w_ax'], w['b_ax'], w['a_param']]
    seq = pl.BlockSpec((None, tb, wd), lambda bi, ti: (bi, ti, 0))
    return pl.pallas_call(
        functools.partial(_lru_kernel, tb=tb), name="rg_lru",
        grid=(b, t // tb),
        in_specs=[seq, seq,
                  pl.BlockSpec((None, tail, wd), lambda bi, ti: (bi, 0, 0)),
                  pl.BlockSpec((None, 1, wd), lambda bi, ti: (bi, 0, 0))]
                 + [pl.BlockSpec(c.shape, lambda bi, ti: (0, 0)) for c in consts],
        out_specs=[seq,
                   pl.BlockSpec((None, tail, wd), lambda bi, ti: (bi, 0, 0)),
                   pl.BlockSpec((None, 1, wd), lambda bi, ti: (bi, 0, 0))],
        out_shape=[jax.ShapeDtypeStruct((b, t, wd), BF16),
                   jax.ShapeDtypeStruct((b, tail, wd), F32),
                   jax.ShapeDtypeStruct((b, 1, wd), F32)],
        scratch_shapes=[pltpu.VMEM((tb + 2 * SUBLANES, wd), F32), pltpu.VMEM((1, wd), F32)],
        compiler_params=_cparams(("parallel", "arbitrary")),
    )(xb, yb, conv0, h0, *consts)


def _out_ln_value(a_ref, b_ref, x_ref, wa_ref, wb_ref, g_ref, beta_ref, *, alpha):
    y = (jnp.dot(a_ref[...], wa_ref[...], preferred_element_type=F32)
         + jnp.dot(b_ref[...], wb_ref[...], preferred_element_type=F32))
    return _layer_norm(alpha * x_ref[...] + y, g_ref[...], beta_ref[...])


def _out_ln_stage(att, rec, x, wa, wb, g, beta, alpha):
    n, dm = x.shape
    tm = min(ROW_TILE, n)
    return (functools.partial(_out_ln_value, alpha=alpha), [(att, tm), (rec, tm), (x, tm)],
            [wa, wb, g, beta], [], n, dm)


def _stage_kernel(*refs, n_in, value_fn):
    refs[n_in][...] = value_fn(*refs[:n_in], *refs[n_in + 1:])


def _stage_call(stage, name):
    value_fn, row_inputs, consts, scratch, n, dm = stage
    tm = min(ROW_TILE, n)
    arrays = [a for a, _ in row_inputs] + list(consts)
    return pl.pallas_call(
        functools.partial(_stage_kernel, n_in=len(arrays), value_fn=value_fn), name=name,
        grid=(n // tm,),
        in_specs=[_row_spec(rows, a.shape[1]) for a, rows in row_inputs]
                 + [_full_spec(c) for c in consts],
        out_specs=_row_spec(tm, dm),
        out_shape=jax.ShapeDtypeStruct((n, dm), F32),
        scratch_shapes=list(scratch),
        compiler_params=_cparams(("parallel",)),
    )(*arrays)


def _route(x, xh, wr_ref, lane):
    xl = (x - xh.astype(F32)).astype(BF16)
    hh_hl = jnp.dot(xh, wr_ref[...], preferred_element_type=F32)
    logits = (hh_hl[:, :ROUTER_W] + hh_hl[:, ROUTER_W:]
              + jnp.dot(xl, wr_ref[:, :ROUTER_W], preferred_element_type=F32))
    big = float(ROUTER_W)
    lg = jnp.where(lane < N_GROUPS, logits, -jnp.inf)
    mg = jnp.max(lg, axis=-1, keepdims=True)
    gi = jnp.min(jnp.where(lg == mg, lane, big), axis=-1, keepdims=True)
    p_top = 1.0 / jnp.sum(jnp.exp(lg - mg), axis=-1, keepdims=True)
    lo = N_GROUPS + gi * EXPERTS_PER_GROUP
    le = jnp.where((lane >= lo) & (lane < lo + EXPERTS_PER_GROUP), logits, -jnp.inf)
    v1 = jnp.max(le, axis=-1, keepdims=True)
    i1 = jnp.min(jnp.where(le == v1, lane, big), axis=-1, keepdims=True)
    le2 = jnp.where(lane == i1, -jnp.inf, le)
    v2 = jnp.max(le2, axis=-1, keepdims=True)
    i2 = jnp.min(jnp.where(le2 == v2, lane, big), axis=-1, keepdims=True)
    e2 = jnp.exp(v2 - v1)
    return gi, i1, i2, p_top / (1.0 + e2), p_top * e2 / (1.0 + e2)


def _moe_ln_kernel(x_ref, wr_ref, wg_ref, wu_ref, wd_ref, g_ref, beta_ref, o_ref,
                   acc, gates, xb16, *, alpha):
    grp = pl.program_id(1)
    tm = x_ref.shape[0]
    lane = lax.broadcasted_iota(jnp.int32, (tm, ROUTER_W), 1).astype(F32)

    @pl.when(grp == 0)
    def _():
        x = x_ref[...]
        xh = x.astype(BF16)
        xb16[...] = xh
        _, i1, i2, w1, w2 = _route(x, xh, wr_ref, lane)
        gates[...] = jnp.where(lane == i1, w1, 0.0) + jnp.where(lane == i2, w2, 0.0)
        acc[...] = jnp.zeros_like(acc)

    xb = xb16[...]
    gt = gates[...]
    first = (N_GROUPS + grp * EXPERTS_PER_GROUP).astype(F32)
    for el in range(EXPERTS_PER_GROUP):
        ge = jnp.sum(jnp.where(lane == first + el, gt, 0.0), axis=-1, keepdims=True)
        h = jnp.dot(xb, wg_ref[el], preferred_element_type=F32)
        u = jnp.dot(xb, wu_ref[el], preferred_element_type=F32)
        act = (jax.nn.silu(h) * u * ge).astype(BF16)
        acc[...] += jnp.dot(act, wd_ref[el], preferred_element_type=F32)

    @pl.when(grp == pl.num_programs(1) - 1)
    def _():
        o_ref[...] = _layer_norm(alpha * x_ref[...] + acc[...], g_ref[...], beta_ref[...])


def _moe_ln(x, w, layer, g, beta, alpha):
    n, dm = x.shape
    tm = min(MOE_ROW_TILE, n)
    _, ne, _, eh = w['w_gate'].shape
    epg = EXPERTS_PER_GROUP
    return pl.pallas_call(
        functools.partial(_moe_ln_kernel, alpha=alpha), name="moe_ln",
        grid=(n // tm, ne // epg),
        in_specs=[
            pl.BlockSpec((tm, dm), lambda i, e: (i, 0)),
            pl.BlockSpec(w['w_router'].shape, lambda i, e: (0, 0)),
            pl.BlockSpec((None, epg, dm, eh), lambda i, e: (layer, e, 0, 0)),
            pl.BlockSpec((None, epg, dm, eh), lambda i, e: (layer, e, 0, 0)),
            pl.BlockSpec((None, epg, eh, dm), lambda i, e: (layer, e, 0, 0)),
            pl.BlockSpec(g.shape, lambda i, e: (0, 0)),
            pl.BlockSpec(beta.shape, lambda i, e: (0, 0)),
        ],
        out_specs=pl.BlockSpec((tm, dm), lambda i, e: (i, 0)),
        out_shape=jax.ShapeDtypeStruct((n, dm), F32),
        scratch_shapes=[pltpu.VMEM((tm, dm), F32), pltpu.VMEM((tm, ROUTER_W), F32),
                        pltpu.VMEM((tm, dm), BF16)],
        compiler_params=_cparams(("parallel", "arbitrary"), MOE_VMEM_LIMIT_BYTES),
    )(x, w['w_router'], w['w_gate'], w['w_up'], w['w_down'], g, beta)


META_W = LANES
META_GROUP = EXPERTS_PER_GROUP
META_RANK = EXPERTS_PER_GROUP + 1


def _moe_route_kernel(*refs, n_in, n_tiles, value_fn):
    in_refs = refs[:n_in]
    wr_ref, tri_ref, xg_ref, meta_ref, cnt_ref, run = refs[n_in:n_in + 6]
    tm, dm = xg_ref.shape[0], xg_ref.shape[1] - META_W
    i = pl.program_id(0)
    lane = lax.broadcasted_iota(jnp.int32, (tm, ROUTER_W), 1).astype(F32)

    @pl.when(i == 0)
    def _():
        run[...] = jnp.zeros_like(run)

    @pl.when(i < n_tiles)
    def _():
        x = value_fn(*in_refs, *refs[n_in + 6:])
        gi, i1, i2, w1, w2 = _route(x, x.astype(BF16), wr_ref, lane)
        lo = N_GROUPS + gi * EXPERTS_PER_GROUP
        in_group = lane == gi
        incl = jnp.dot(tri_ref[...], in_group.astype(BF16), preferred_element_type=F32)
        rank = jnp.sum(jnp.where(in_group, incl + run[...], 0.0), axis=-1,
                       keepdims=True) - 1.0
        run[...] += incl[tm - 1:tm, :]
        cnt_ref[...] = run[...]
        meta = (jnp.where(lane == i1 - lo, w1, 0.0) + jnp.where(lane == i2 - lo, w2, 0.0)
                + jnp.where(lane == META_GROUP, gi, 0.0)
                + jnp.where(lane == META_RANK, rank, 0.0))
        xg_ref[:, :dm] = x
        xg_ref[:, dm:] = meta
        meta_ref[...] = meta

    @pl.when(i >= n_tiles)
    def _():
        xg_ref[...] = jnp.zeros_like(xg_ref)


def _moe_route(stage, w_router, p):
    value_fn, row_inputs, consts, scratch, n, dm = stage
    tm = ROW_TILE
    n_tiles = n // tm
    tri = jnp.tril(jnp.ones((tm, tm), BF16))
    last = lambda i: (jnp.minimum(i, n_tiles - 1), 0)
    arrays = [a for a, _ in row_inputs] + list(consts)
    in_specs = ([pl.BlockSpec((rows, a.shape[1]), last) for a, rows in row_inputs]
                + [_full_spec(c) for c in consts] + [_full_spec(w_router), _full_spec(tri)])
    return pl.pallas_call(
        functools.partial(_moe_route_kernel, n_in=len(arrays), n_tiles=n_tiles,
                          value_fn=value_fn), name="moe_route",
        grid=(p // tm,),
        in_specs=in_specs,
        out_specs=[_row_spec(tm, dm + META_W), pl.BlockSpec((tm, META_W), last),
                   pl.BlockSpec((1, ROUTER_W), lambda i: (0, 0))],
        out_shape=[jax.ShapeDtypeStruct((p, dm + META_W), F32),
                   jax.ShapeDtypeStruct((n, META_W), F32),
                   jax.ShapeDtypeStruct((1, ROUTER_W), F32)],
        scratch_shapes=[pltpu.VMEM((1, ROUTER_W), F32)] + list(scratch),
        compiler_params=_cparams(("arbitrary",)),
    )(*arrays, w_router, tri)


def _moe_experts_kernel(tg_ref, nu_ref, xs_ref, wg_ref, wu_ref, wd_ref, y_ref):
    del tg_ref
    dm = y_ref.shape[1]

    @pl.when(pl.program_id(0) < nu_ref[0])
    def _():
        xb = xs_ref[:, :dm].astype(BF16)
        meta = xs_ref[:, dm:]
        acc = None
        for el in range(EXPERTS_PER_GROUP):
            h = jnp.dot(xb, wg_ref[el], preferred_element_type=F32)
            u = jnp.dot(xb, wu_ref[el], preferred_element_type=F32)
            act = (jax.nn.silu(h) * u * meta[:, el:el + 1]).astype(BF16)
            d = jnp.dot(act, wd_ref[el], preferred_element_type=F32)
            acc = d if acc is None else acc + d
        y_ref[...] = acc


def _moe_experts(xs, tile_group, n_used, w, layer):
    p, wx = xs.shape
    dm = wx - META_W
    te = MOE_EXPERT_TILE
    _, ne, _, eh = w['w_gate'].shape
    epg = EXPERTS_PER_GROUP

    def tile(i, tg, nu):
        return jnp.minimum(i, nu[0] - 1)

    wspec = lambda shape: pl.BlockSpec(
        (None, epg) + shape, lambda i, tg, nu: (layer, tg[tile(i, tg, nu)], 0, 0))
    grid_spec = pltpu.PrefetchScalarGridSpec(
        num_scalar_prefetch=2,
        grid=(p // te,),
        in_specs=[pl.BlockSpec((te, wx), lambda i, tg, nu: (tile(i, tg, nu), 0)),
                  wspec((dm, eh)), wspec((dm, eh)), wspec((eh, dm))],
        out_specs=pl.BlockSpec((te, dm), lambda i, tg, nu: (tile(i, tg, nu), 0)),
    )
    return pl.pallas_call(
        _moe_experts_kernel, name="moe_experts",
        grid_spec=grid_spec,
        out_shape=jax.ShapeDtypeStruct((p, dm), F32),
        compiler_params=_cparams(("arbitrary",)),
    )(tile_group, n_used, xs, w['w_gate'], w['w_up'], w['w_down'])


def _res_ln_value(xg_ref, y_ref, g_ref, beta_ref, *, alpha):
    dm = y_ref.shape[1]
    return _layer_norm(alpha * xg_ref[:, :dm] + y_ref[...], g_ref[...], beta_ref[...])


def _res_ln_stage(xg, y, g, beta, alpha):
    n, dm = y.shape
    return (functools.partial(_res_ln_value, alpha=alpha), [(xg, ROW_TILE), (y, ROW_TILE)],
            [g, beta], [], n, dm)


def _identity_value(x_ref):
    return x_ref[...]


def _array_stage(x):
    n, dm = x.shape
    return (_identity_value, [(x, min(ROW_TILE, n))], [], [], n, dm)


def _materialize(stage, name):
    if stage[0] is _identity_value:
        return stage[1][0][0]
    return _stage_call(stage, name)


def _moe(stage, stage_name, w, layer, g, beta, alpha, overlap=None):
    n = stage[4]
    granule = 2 * SC_GATHER_ROWS * SC_CORES * SC_SUBCORES
    if n % granule == 0 and (N_GROUPS * MOE_EXPERT_TILE) % granule == 0:
        return _moe_routed_ln(stage, w, layer, g, beta, alpha, overlap)
    out = _array_stage(_moe_ln(_stage_call(stage, stage_name), w, layer, g, beta, alpha))
    return out, overlap


def _moe_routed_ln(stage, w, layer, g, beta, alpha, overlap):
    n, dm = stage[4], stage[5]
    te = MOE_EXPERT_TILE
    npad = N_GROUPS * te
    p = n + npad
    xg, meta, cnt = _moe_route(stage, w['w_router'], p)
    gid = meta[:, META_GROUP].astype(jnp.int32)
    rank = meta[:, META_RANK].astype(jnp.int32)
    counts = cnt[0, :N_GROUPS].astype(jnp.int32)
    padded = (counts + te - 1) // te * te
    ends = jnp.cumsum(padded)
    starts = ends - padded

    def lookup(table, idx):
        hit = idx[:, None] == jnp.arange(table.shape[0], dtype=jnp.int32)[None, :]
        return jnp.sum(jnp.where(hit, table[None, :], 0), axis=1)

    pos = lookup(starts, gid) + rank
    gap_len = jnp.concatenate([padded - counts, p - ends[-1:]])
    gap_start = jnp.concatenate([starts + counts, ends[-1:]])
    gap_end = jnp.cumsum(gap_len)
    k = jnp.arange(npad, dtype=jnp.int32)
    seg = jnp.sum(k[:, None] >= gap_end[None, :], axis=1)
    filler_pos = lookup(gap_start, seg) + k - lookup(gap_end - gap_len, seg)
    pos_all = jnp.concatenate([pos, filler_pos]).astype(jnp.int32)
    tile_start = jnp.arange(p // te, dtype=jnp.int32) * te
    tile_group = jnp.minimum(jnp.sum(tile_start[:, None] >= ends[None, :], axis=1),
                             N_GROUPS - 1).astype(jnp.int32)
    n_used = (ends[-1:] // te).astype(jnp.int32)
    xs = _sc_scatter_rows(xg, pos_all)
    if overlap is not None:
        xs, overlap = lax.optimization_barrier((xs, overlap))
    ys = _moe_experts(xs, tile_group, n_used, w, layer)
    y = _sc_gather_rows(ys, pos)
    return _res_ln_stage(xg, y, g, beta, alpha), overlap


SC_CORES = 2
SC_SUBCORES = 16
SC_GATHER_ROWS = 32


def _sc_gather_rows(table, idx):
    b = idx.shape[0]
    _, d = table.shape
    nw = SC_CORES * SC_SUBCORES
    ch = SC_GATHER_ROWS
    per_w = b // nw
    assert b % (nw * 2 * ch) == 0
    npair = per_w // (2 * ch)
    mesh = plsc.VectorSubcoreMesh(core_axis_name="c", subcore_axis_name="s")

    @functools.partial(
        pl.kernel, mesh=mesh, out_type=jax.ShapeDtypeStruct((b, d), table.dtype),
        scratch_types=[pltpu.VMEM((2, ch), jnp.int32), pltpu.VMEM((2, ch, d), table.dtype),
                       pltpu.SemaphoreType.DMA, pltpu.SemaphoreType.DMA],
        name="sc_gather_rows")
    def gather(table_hbm, idx_hbm, out_hbm, idx_v, rows_v, sem0, sem1):
        sems = (sem0, sem1)
        wid = lax.axis_index("s") * SC_CORES + lax.axis_index("c")
        base = wid * per_w

        def start(chunk, slot):
            off = pl.multiple_of(base + chunk * ch, SUBLANES)
            pltpu.sync_copy(idx_hbm.at[pl.ds(off, ch)], idx_v.at[slot])
            pltpu.async_copy(table_hbm.at[idx_v.at[slot]], rows_v.at[slot], sems[slot])

        def finish(chunk, slot):
            off = pl.multiple_of(base + chunk * ch, SUBLANES)
            pltpu.make_async_copy(table_hbm.at[idx_v.at[slot]], rows_v.at[slot],
                                  sems[slot]).wait()
            pltpu.sync_copy(rows_v.at[slot], out_hbm.at[pl.ds(off, ch)])

        start(0, 0)

        @pl.loop(0, npair)
        def _(p):
            c0 = 2 * p
            start(c0 + 1, 1)
            finish(c0, 0)

            @pl.when(p + 1 < npair)
            def _():
                start(c0 + 2, 0)
            finish(c0 + 1, 1)

    return gather(table, idx)


def _sc_scatter_rows(rows, idx):
    b, d = rows.shape
    nw = SC_CORES * SC_SUBCORES
    ch = SC_GATHER_ROWS
    per_w = b // nw
    assert idx.shape == (b,) and b % (nw * 2 * ch) == 0
    npair = per_w // (2 * ch)
    mesh = plsc.VectorSubcoreMesh(core_axis_name="c", subcore_axis_name="s")

    @functools.partial(
        pl.kernel, mesh=mesh, out_type=jax.ShapeDtypeStruct((b, d), rows.dtype),
        scratch_types=[pltpu.VMEM((2, ch), jnp.int32), pltpu.VMEM((2, ch, d), rows.dtype),
                       pltpu.SemaphoreType.DMA, pltpu.SemaphoreType.DMA],
        name="sc_scatter_rows")
    def scatter(rows_hbm, idx_hbm, out_hbm, idx_v, rows_v, sem0, sem1):
        sems = (sem0, sem1)
        wid = lax.axis_index("s") * SC_CORES + lax.axis_index("c")
        base = wid * per_w

        def write(slot):
            return pltpu.make_async_copy(rows_v.at[slot], out_hbm.at[idx_v.at[slot]],
                                         sems[slot])

        @pl.loop(0, npair)
        def _(p):
            for slot in range(2):
                @pl.when(p > 0)
                def _():
                    write(slot).wait()
                off = pl.multiple_of(base + (2 * p + slot) * ch, SUBLANES)
                pltpu.sync_copy(idx_hbm.at[pl.ds(off, ch)], idx_v.at[slot])
                pltpu.sync_copy(rows_hbm.at[pl.ds(off, ch)], rows_v.at[slot])
                write(slot).start()

        write(0).wait()
        write(1).wait()

    return scatter(rows, idx)


def _c_in_kernel(*refs, n_in, value_fn):
    w_ref, x_out, u_ref, u2_ref, us = refs[n_in:n_in + 5]
    x = value_fn(*refs[:n_in], *refs[n_in + 5:])
    x_out[...] = x
    u = _bdot(x, w_ref[...])
    u_ref[...] = u
    rows = u2_ref.shape[0]
    for c in range(S5_WIDTH // LANES):
        us[c] = u[:, c * LANES:(c + 1) * LANES]
    for s in range(S5_L):
        for c in range(S5_WIDTH // LANES):
            lo = s * S5_WIDTH + c * LANES
            u2_ref[:, lo:lo + LANES] = us[c, pl.ds(s, rows, stride=S5_L), :].astype(u2_ref.dtype)


def _c_in(stage, w_in):
    value_fn, row_inputs, consts, scratch, n, dm = stage
    tm = min(ROW_TILE, n)
    wu = w_in.shape[1]
    arrays = [a for a, _ in row_inputs] + list(consts)
    return pl.pallas_call(
        functools.partial(_c_in_kernel, n_in=len(arrays), value_fn=value_fn), name="c_in",
        grid=(n // tm,),
        in_specs=[_row_spec(rows, a.shape[1]) for a, rows in row_inputs]
                 + [_full_spec(c) for c in consts] + [_full_spec(w_in)],
        out_specs=[_row_spec(tm, dm), _row_spec(tm, wu), _row_spec(tm // S5_L, S5_L * wu)],
        out_shape=[jax.ShapeDtypeStruct((n, dm), F32),
                   jax.ShapeDtypeStruct((n, wu), F32),
                   jax.ShapeDtypeStruct((n // S5_L, S5_L * wu), BF16)],
        scratch_shapes=[pltpu.VMEM((wu // LANES, tm, LANES), F32)] + list(scratch),
        compiler_params=_cparams(("parallel",)),
    )(*arrays, w_in)


def _s5_super_rows(u_ref, q):
    tiles = [u_ref[:, s * S5_WIDTH + q * LANES:s * S5_WIDTH + (q + 1) * LANES]
             for s in range(S5_L)]
    return jnp.concatenate(tiles, axis=1).astype(BF16)


def _s5_e_kernel(u_ref, ere_ref, eim_ref, ere_out, eim_out):
    for q in range(S5_NSUPER):
        uq = _s5_super_rows(u_ref, q)
        sl = slice(q * S5_SUPER_STATE, (q + 1) * S5_SUPER_STATE)
        ere_out[:, sl] = jnp.dot(uq, ere_ref[q], preferred_element_type=F32)
        eim_out[:, sl] = jnp.dot(uq, eim_ref[q], preferred_element_type=F32)


def _const_spec(a):
    nd = a.ndim
    return pl.BlockSpec(a.shape, lambda *_: (0,) * nd, pipeline_mode=pl.Buffered(1))


def _s5_e(u2, tabs):
    rows, wu = u2.shape
    tr = min(S5_ROW_TILE, rows)
    out = jax.ShapeDtypeStruct((rows, S5_STATE_W), F32)
    return pl.pallas_call(
        _s5_e_kernel, name="s5_e",
        grid=(rows // tr,),
        in_specs=[_row_spec(tr, wu), _const_spec(tabs['e_re']), _const_spec(tabs['e_im'])],
        out_specs=[_row_spec(tr, S5_STATE_W)] * 2,
        out_shape=[out, out],
        compiler_params=_cparams(("parallel",)),
    )(u2, tabs['e_re'], tabs['e_im'])


def _s5_scan_kernel(ere_ref, eim_ref, h0r_ref, h0i_ref, stp_re_ref, stp_im_ref,
                    apw_re_ref, apw_im_ref, hpr_ref, hpi_ref, hfr_ref, hfi_ref, *, nc, last_row):
    w = ere_ref.shape[-1]
    row = lax.broadcasted_iota(jnp.int32, (SUBLANES, w), 0)
    apr = apw_re_ref[...]
    api = apw_im_ref[...]

    def tile(i, carry):
        cr, ci = carry
        start = pl.multiple_of(i * SUBLANES, SUBLANES)
        sr = ere_ref[pl.ds(start, SUBLANES), :]
        si = eim_ref[pl.ds(start, SUBLANES), :]
        for k in range(3):
            s = 1 << k
            keep = row >= s
            ar = stp_re_ref[k:k + 1, :]
            ai = stp_im_ref[k:k + 1, :]
            pr = pltpu.roll(sr, s, 0)
            pi = pltpu.roll(si, s, 0)
            sr, si = (jnp.where(keep, sr + ar * pr - ai * pi, sr),
                      jnp.where(keep, si + ar * pi + ai * pr, si))
        hr = sr + apr * cr - api * ci
        hi = si + apr * ci + api * cr
        first = row == 0
        hpr_ref[pl.ds(start, SUBLANES), :] = jnp.where(first, cr, pltpu.roll(hr, 1, 0))
        hpi_ref[pl.ds(start, SUBLANES), :] = jnp.where(first, ci, pltpu.roll(hi, 1, 0))
        return hr, hi

    def body(i, carry):
        hr, hi = tile(i, carry)
        return hr[SUBLANES - 1:SUBLANES, :], hi[SUBLANES - 1:SUBLANES, :]

    ntile = nc // SUBLANES
    carry = lax.fori_loop(0, ntile - 1, body, (h0r_ref[...], h0i_ref[...]))
    hr, hi = tile(ntile - 1, carry)
    hfr_ref[...] = hr[last_row:last_row + 1, :]
    hfi_ref[...] = hi[last_row:last_row + 1, :]


def _s5_scan(e_re, e_im, h0r, h0i, tabs, n_chunks):
    b, nc, w = e_re.shape
    last_row = (n_chunks - 1) % SUBLANES
    wb = S5_SCAN_LANES
    consts = [tabs['step_re'], tabs['step_im'], tabs['apow_re'], tabs['apow_im']]
    seq = pl.BlockSpec((None, nc, wb), lambda bi, li: (bi, 0, li))
    vec = pl.BlockSpec((None, 1, wb), lambda bi, li: (bi, 0, li))
    return pl.pallas_call(
        functools.partial(_s5_scan_kernel, nc=nc, last_row=last_row), name="s5_scan",
        grid=(b, w // wb),
        in_specs=[seq, seq, vec, vec]
                 + [pl.BlockSpec((c.shape[0], wb), lambda bi, li: (0, li)) for c in consts],
        out_specs=[seq, seq, vec, vec],
        out_shape=[jax.ShapeDtypeStruct((b, nc, w), F32)] * 2
                  + [jax.ShapeDtypeStruct((b, 1, w), F32)] * 2,
        compiler_params=_cparams(("parallel", "parallel")),
    )(e_re, e_im, h0r, h0i, *consts)


def _s5_y_kernel(u_ref, hpr_ref, hpi_ref, m_ref, fre_ref, fim_ref, y_ref):
    for q in range(S5_NSUPER):
        uq = _s5_super_rows(u_ref, q)
        sl = slice(q * S5_SUPER_STATE, (q + 1) * S5_SUPER_STATE)
        yq = (jnp.dot(uq, m_ref[q], preferred_element_type=F32)
              + _bdot(hpr_ref[:, sl], fre_ref[q])
              + _bdot(hpi_ref[:, sl], fim_ref[q]))
        for t in range(S5_L):
            y_ref[:, t * S5_WIDTH + q * LANES:t * S5_WIDTH + (q + 1) * LANES] = (
                yq[:, t * LANES:(t + 1) * LANES])


def _s5_y(u2, hp_re, hp_im, tabs):
    rows, wu = u2.shape
    tr = min(S5_ROW_TILE, rows)
    consts = [tabs['m'], tabs['f_re'], tabs['f_im']]
    return pl.pallas_call(
        _s5_y_kernel, name="s5_y",
        grid=(rows // tr,),
        in_specs=[_row_spec(tr, wu), _row_spec(tr, S5_STATE_W), _row_spec(tr, S5_STATE_W)]
                 + [_const_spec(c) for c in consts],
        out_specs=_row_spec(tr, wu),
        out_shape=jax.ShapeDtypeStruct((rows, wu), F32),
        compiler_params=_cparams(("parallel",)),
    )(u2, hp_re, hp_im, *consts)


def _c_out_value(y2_ref, u_ref, x_ref, d_ref, wglu_ref, wout_ref, g_ref, beta_ref, ys, *, alpha):
    rows = y2_ref.shape[0]
    ntile = S5_WIDTH // LANES
    for s in range(S5_L):
        for c in range(ntile):
            lo = s * S5_WIDTH + c * LANES
            ys[c, pl.ds(s, rows, stride=S5_L), :] = y2_ref[:, lo:lo + LANES]
    y = jnp.concatenate([ys[c] for c in range(ntile)], axis=1) + d_ref[...] * u_ref[...]
    vg = _bdot(jax.nn.gelu(y), wglu_ref[...])
    half = vg.shape[1] // 2
    z = vg[:, :half] * jax.nn.sigmoid(vg[:, half:])
    o = _bdot(z, wout_ref[...])
    return _layer_norm(alpha * x_ref[...] + o, g_ref[...], beta_ref[...])


def _c_out_stage(y2, u, x, w, g, beta, alpha):
    n, dm = x.shape
    tm = min(ROW_TILE, n)
    return (functools.partial(_c_out_value, alpha=alpha),
            [(y2, tm // S5_L), (u, tm), (x, tm)], [w['d'], w['w_glu'], w['w_out'], g, beta],
            [pltpu.VMEM((u.shape[1] // LANES, tm, LANES), F32)], n, dm)


def _prep_ab(w_in, q_g, kv_g, w_uq, w_ukv, conv_w, conv_b, wa, ba, wx, bx, a_param, w_out):
    dm = w_in.shape[0]
    s0 = Q_LORA
    s1 = s0 + KV_LORA
    s2 = s1 + QK_ROPE
    s3 = s2 + LRU_WIDTH
    pad_lo = jnp.zeros((dm, QK_NOPE), F32)
    pad_hi = jnp.zeros((dm, HEAD_PAD - QK_HEAD), F32)
    w_in_p = jnp.concatenate([w_in[:, :s1], w_in[:, s2:s3], w_in[:, s3:],
                              pad_lo, w_in[:, s1:s2], pad_hi], axis=1).astype(BF16)
    wq = w_uq.reshape(Q_LORA, MLA_HEADS, QK_HEAD)
    wq = jnp.pad(wq, ((0, 0), (0, 0), (0, HEAD_PAD - QK_HEAD)))
    wq = wq.reshape(Q_LORA, MLA_HEADS * HEAD_PAD).astype(BF16)
    wkv = w_ukv.reshape(KV_LORA, MLA_HEADS, QK_NOPE + V_HEAD)
    wk = jnp.pad(wkv[:, :, :QK_NOPE], ((0, 0), (0, 0), (0, HEAD_PAD - QK_NOPE)))
    wk = wk.reshape(KV_LORA, MLA_HEADS * HEAD_PAD).astype(BF16)
    wv = jnp.pad(wkv[:, :, QK_NOPE:], ((0, 0), (0, 0), (0, HEAD_PAD - V_HEAD)))
    wv = wv.reshape(KV_LORA, MLA_HEADS * HEAD_PAD).astype(BF16)
    wukt = jnp.transpose(wkv[:, :, :QK_NOPE], (1, 2, 0)).astype(BF16)
    wuv_h = jnp.transpose(wkv[:, :, QK_NOPE:], (1, 0, 2)).astype(BF16)
    eye = jnp.eye(LRU_BLOCKS, dtype=F32)

    def blockdiag(wb):
        return jnp.einsum('hij,hk->hikj', wb, eye).reshape(LRU_WIDTH, LRU_WIDTH)

    w_ax = jnp.concatenate([blockdiag(wa), blockdiag(wx)], axis=1).astype(BF16)
    b_ax = jnp.concatenate([ba, bx])[None, :]
    att_w = MLA_HEADS * V_HEAD
    return dict(w_in=w_in_p, q_g=q_g[None, :], kv_g=kv_g[None, :], w_uq=wq, w_uk=wk, w_uv=wv,
                wukt=wukt, wuv_h=wuv_h, conv_w=conv_w, conv_b=conv_b[None, :], w_ax=w_ax,
                b_ax=b_ax, a_param=jax.nn.softplus(-a_param)[None, :],
                w_out_att=w_out[:att_w].astype(BF16), w_out_rec=w_out[att_w:].astype(BF16))


def _rope_tables(row_pos, tile_pos):
    half = QK_ROPE // 2
    inv = ROPE_THETA ** (-jnp.arange(half, dtype=F32) / half)
    inv_lane = jnp.concatenate([jnp.zeros((QK_NOPE,), F32), inv, inv,
                                jnp.zeros((HEAD_PAD - QK_HEAD,), F32)])
    ang_row = row_pos.astype(F32)[:, None] * inv_lane[None, :]
    ang_tile = tile_pos.astype(F32)[:, None, None] * inv_lane
    return jnp.cos(ang_row), jnp.sin(ang_row), jnp.cos(ang_tile), jnp.sin(ang_tile)


def _prep_router(w_group, w_expert):
    dm = w_group.shape[0]
    pad = jnp.zeros((dm, ROUTER_W - N_GROUPS - N_EXPERTS), F32)
    w_router = jnp.concatenate([w_group, w_expert, pad], axis=1)
    w_router_hi = w_router.astype(BF16)
    w_router_lo = (w_router - w_router_hi.astype(F32)).astype(BF16)
    return dict(w_router=jnp.concatenate([w_router_hi, w_router_lo], axis=1))


def _cmul(ar, ai, br, bi):
    return ar * br - ai * bi, ar * bi + ai * br


def _prep_s5(lam_re, lam_im, log_step, b_re, b_im, c_re, c_im):
    hi = lax.Precision.HIGHEST
    L, G, P, C = S5_L, S5_GROUPS, S5_STATE, S5_GROUP
    lr, li = lam_re, lam_im
    dt = jnp.exp(log_step)[:, None]
    mag = jnp.exp(lr * dt)
    ar, ai = mag * jnp.cos(li * dt), mag * jnp.sin(li * dt)
    den = lr * lr + li * li
    nr = ar - 1.0
    cr = (nr * lr + ai * li) / den
    ci = (ai * lr - nr * li) / den
    bbr = cr[..., None] * b_re - ci[..., None] * b_im
    bbi = cr[..., None] * b_im + ci[..., None] * b_re
    prs, pis = [jnp.ones_like(ar)], [jnp.zeros_like(ar)]
    for _ in range(L):
        nr_, ni_ = _cmul(prs[-1], pis[-1], ar, ai)
        prs.append(nr_)
        pis.append(ni_)
    pr = jnp.stack(prs)
    pi = jnp.stack(pis)
    xr = pr[:L, :, :, None] * bbr - pi[:L, :, :, None] * bbi
    xi = pr[:L, :, :, None] * bbi + pi[:L, :, :, None] * bbr
    kk = (jnp.einsum('gcp,kgpd->kgcd', c_re, xr, precision=hi)
          - jnp.einsum('gcp,kgpd->kgcd', c_im, xi, precision=hi))
    p1r = pr[1:, :, :, None]
    p1i = pi[1:, :, :, None]
    cre_t = jnp.transpose(c_re, (0, 2, 1))[None]
    cim_t = jnp.transpose(c_im, (0, 2, 1))[None]
    f_re = cre_t * p1r - cim_t * p1i
    f_im = -cre_t * p1i - cim_t * p1r

    nq, sg = S5_NSUPER, S5_SUPER
    eye = jnp.eye(sg, dtype=F32)
    dk = jnp.einsum('kqgcd,gh->kqgdhc', kk.reshape(L, nq, sg, C, C), eye)
    dk = dk.reshape(L, nq, LANES, LANES).astype(BF16)
    lag = jnp.arange(L)[None, :] - jnp.arange(L)[:, None]
    m = jnp.where((lag >= 0)[:, :, None, None, None], dk[jnp.clip(lag, 0, L - 1)], 0)
    m = jnp.transpose(m, (2, 0, 3, 1, 4)).reshape(nq, L * LANES, L * LANES)

    row_g = (jnp.arange(L * LANES) // C) % sg
    col_g = jnp.arange(sg * P) // P
    same_group = (row_g[:, None] == col_g[None, :])[None]

    def pack_rows(a):
        a = jnp.transpose(a.reshape(L, nq, sg, P, C), (1, 0, 4, 2, 3))
        a = a.reshape(nq, L, 1, C, sg * P).astype(BF16)
        a = jnp.broadcast_to(a, (nq, L, sg, C, sg * P)).reshape(nq, L * LANES, sg * P)
        return jnp.where(same_group, a, 0)

    def pack_e(x):
        return pack_rows(x[::-1])

    def pack_f(f):
        return jnp.swapaxes(pack_rows(f), 1, 2)

    alr, ali = pr[L].reshape(1, G * P), pi[L].reshape(1, G * P)
    qr, qi = [alr], [ali]
    for _ in range(SUBLANES - 1):
        nr_, ni_ = _cmul(qr[-1], qi[-1], alr, ali)
        qr.append(nr_)
        qi.append(ni_)
    apow_re = jnp.concatenate(qr, axis=0)
    apow_im = jnp.concatenate(qi, axis=0)
    step_re = jnp.concatenate([qr[0], qr[1], qr[3]], axis=0)
    step_im = jnp.concatenate([qi[0], qi[1], qi[3]], axis=0)
    return dict(m=m, e_re=pack_e(xr), e_im=pack_e(xi), f_re=pack_f(f_re),
                f_im=pack_f(f_im), apow_re=apow_re, apow_im=apow_im,
                step_re=step_re, step_im=step_im)


def _mixer_ab(x, pos0, past_ckv, past_krope, conv0, h0, w):
    b, t, dm = x.shape
    xf = x.reshape(b * t, dm)
    tm = min(ROW_TILE, b * t)
    assert tm % t == 0 or t % tm == 0
    row_pos = jnp.arange(tm) % t
    tile_pos = pos0 + (jnp.arange(b * t // tm) * tm) % t
    q, k, v, c_new, kr_new, xb, yb = _ab_in(xf, w, _rope_tables(row_pos, tile_pos))
    if past_ckv is None:
        att = _attn_prompt(q, k, v)
    else:
        att = _attn_sample(q, c_new, kr_new, past_ckv, past_krope, w['wukt'], w['wuv_h'])
    rec, conv_new, h_new = _lru(xb.reshape(b, t, -1), yb.reshape(b, t, -1), conv0,
                                h0[:, None, :], w)
    return (att, rec.reshape(b * t, -1), c_new.reshape(b, t, -1), kr_new.reshape(b, t, -1),
            conv_new, h_new[:, 0, :])


def _mixer_c(x_stage, b, t, h0_re, h0_im, w):
    nc = t // S5_L
    assert t % S5_L == 0 and nc % SUBLANES == 0
    x, u, u2 = _c_in(x_stage, w['w_in'])
    e_re, e_im = _s5_e(u2, w['tabs'])
    hp_re, hp_im, hf_re, hf_im = _s5_scan(
        e_re.reshape(b, nc, S5_STATE_W), e_im.reshape(b, nc, S5_STATE_W),
        h0_re.reshape(b, 1, S5_STATE_W), h0_im.reshape(b, 1, S5_STATE_W), w['tabs'], nc)
    y2 = _s5_y(u2, hp_re.reshape(b * nc, S5_STATE_W), hp_im.reshape(b * nc, S5_STATE_W),
               w['tabs'])
    return (x, y2, u, hf_re.reshape(b, S5_GROUPS, S5_STATE),
            hf_im.reshape(b, S5_GROUPS, S5_STATE))


def kernel(x_prompt, x_sample, cache_mla_ckv, cache_mla_krope, state_lru_conv, state_lru_h,
           state_s5_re, state_s5_im, w_in_ab, q_norm_g, kv_norm_g, w_uq, w_ukv,
           lru_conv_w, lru_conv_b, lru_w_a, lru_b_a, lru_w_x, lru_b_x, lru_a_param, w_out_ab,
           w_in_c, s5_lam_re, s5_lam_im, s5_log_step, s5_b_re, s5_b_im, s5_c_re, s5_c_im,
           s5_d, s5_w_glu, w_out_c, ln_mix_g, ln_mix_b, ln_ffn_g, ln_ffn_b,
           moe_w_group, moe_w_expert, moe_w_gate, moe_w_up, moe_w_down):
    bp, tp, dm = x_prompt.shape
    bs, ts, _ = x_sample.shape
    past = cache_mla_ckv.shape[2]
    depth = ln_mix_g.shape[0]
    alpha = (2 * depth) ** 0.25
    assert (past + ts - 1) // CHUNK <= past // CHUNK
    hp = _array_stage(x_prompt.reshape(bp * tp, dm))
    hs = _array_stage(x_sample.reshape(bs * ts, dm))
    moe_experts = dict(w_gate=moe_w_gate.astype(BF16), w_up=moe_w_up.astype(BF16),
                       w_down=moe_w_down.astype(BF16))
    outs = {k: [] for k in ('ckv_p', 'ckv_s', 'kr_p', 'kr_s', 'cv_p', 'cv_s', 'lh_p', 'lh_s',
                            's5r_p', 's5r_s', 's5i_p', 's5i_s')}
    for layer in range(depth):
        j = layer // 2
        g_mix, b_mix = ln_mix_g[layer][None, :], ln_mix_b[layer][None, :]
        g_ffn, b_ffn = ln_ffn_g[layer][None, :], ln_ffn_b[layer][None, :]
        if layer % 2 == 0:
            w = _prep_ab(w_in_ab[j], q_norm_g[j], kv_norm_g[j], w_uq[j], w_ukv[j], lru_conv_w[j],
                         lru_conv_b[j], lru_w_a[j], lru_b_a[j], lru_w_x[j], lru_b_x[j],
                         lru_a_param[j], w_out_ab[j])
            xp, xs = _materialize(hp, "res_ln"), _materialize(hs, "res_ln")
            att_p, rec_p, c1, k1, v1, h1 = _mixer_ab(
                xp.reshape(bp, tp, dm), 0, None, None, jnp.zeros((bp, CONV_WIDTH - 1, LRU_WIDTH), F32),
                jnp.zeros((bp, LRU_WIDTH), F32), w)
            att_s, rec_s, c2, k2, v2, h2 = _mixer_ab(
                xs.reshape(bs, ts, dm), past, cache_mla_ckv[j], cache_mla_krope[j], state_lru_conv[j],
                state_lru_h[j], w)
            outs['ckv_p'].append(c1); outs['ckv_s'].append(c2)
            outs['kr_p'].append(k1); outs['kr_s'].append(k2)
            outs['cv_p'].append(v1); outs['cv_s'].append(v2)
            outs['lh_p'].append(h1); outs['lh_s'].append(h2)
            stage_p = _out_ln_stage(att_p, rec_p, xp, w['w_out_att'],
                                    w['w_out_rec'], g_mix, b_mix, alpha)
            stage_s = _out_ln_stage(att_s, rec_s, xs, w['w_out_att'],
                                    w['w_out_rec'], g_mix, b_mix, alpha)
            wm = dict(moe_experts, **_prep_router(moe_w_group[layer], moe_w_expert[layer]))
            hp, _ = _moe(stage_p, "out_ln", wm, layer, g_ffn, b_ffn, alpha)
            hs, _ = _moe(stage_s, "out_ln", wm, layer, g_ffn, b_ffn, alpha)
        else:
            w = dict(w_in=w_in_c[j].astype(BF16), d=s5_d[j][None, :],
                     w_glu=s5_w_glu[j].astype(BF16), w_out=w_out_c[j].astype(BF16),
                     tabs=_prep_s5(s5_lam_re[j], s5_lam_im[j], s5_log_step[j], s5_b_re[j],
                                   s5_b_im[j], s5_c_re[j], s5_c_im[j]))
            wm = dict(moe_experts, **_prep_router(moe_w_group[layer], moe_w_expert[layer]))
            zero_state = jnp.zeros((bp, S5_GROUPS, S5_STATE), F32)
            xp, y_p, u_p, r1, i1 = _mixer_c(hp, bp, tp, zero_state, zero_state, w)
            stage_p = _c_out_stage(y_p, u_p, xp, w, g_mix, b_mix, alpha)
            hp, xs_in = _moe(stage_p, "c_out", wm, layer, g_ffn, b_ffn, alpha,
                             overlap=_materialize(hs, "res_ln"))
            xs, y_s, u_s, r2, i2 = _mixer_c(_array_stage(xs_in), bs, ts, state_s5_re[j],
                                            state_s5_im[j], w)
            outs['s5r_p'].append(r1); outs['s5r_s'].append(r2)
            outs['s5i_p'].append(i1); outs['s5i_s'].append(i2)
            stage_s = _c_out_stage(y_s, u_s, xs, w, g_mix, b_mix, alpha)
            hs, _ = _moe(stage_s, "c_out", wm, layer, g_ffn, b_ffn, alpha)
    y_prompt = _materialize(hp, "res_ln").reshape(bp, tp, dm)
    y_sample = _materialize(hs, "res_ln").reshape(bs, ts, dm)
    st = lambda k: jnp.stack(outs[k])
    return (y_prompt, y_sample, st('ckv_p'), st('ckv_s'), st('kr_p'), st('kr_s'), st('cv_p'), st('cv_s'),
            st('lh_p'), st('lh_s'), st('s5r_p'), st('s5r_s'), st('s5i_p'), st('s5i_s'))
```

```python
import functools
import math

import jax
import jax.numpy as jnp
from jax import lax
from jax.experimental import pallas as pl
from jax.experimental.pallas import tpu as pltpu
from jax.experimental.pallas import tpu_sc as plsc

F32 = jnp.float32
BF16 = jnp.bfloat16

CHUNK = 64
MLA_HEADS = 8
QK_NOPE = 64
QK_ROPE = 32
QK_HEAD = QK_NOPE + QK_ROPE
V_HEAD = 64
Q_LORA = 256
KV_LORA = 128
ROPE_THETA = 10000.0
MLA_SCALE = QK_HEAD ** -0.5
NEG_INF = -1e30
LRU_WIDTH = 512
LRU_BLOCKS = 8
LRU_BLOCK = LRU_WIDTH // LRU_BLOCKS
CONV_WIDTH = 4
RG_C = 8.0
S5_WIDTH = 512
S5_GROUP = 16
S5_GROUPS = S5_WIDTH // S5_GROUP
S5_STATE = 64
N_GROUPS = 4
EXPERTS_PER_GROUP = 4
N_EXPERTS = N_GROUPS * EXPERTS_PER_GROUP
EXPERT_HIDDEN = 256
LN_EPS = 1e-5
RMS_EPS = 1e-6

LANES = 128
SUBLANES = 8
VMEM_LIMIT_BYTES = 48 * 1024 * 1024
MOE_VMEM_LIMIT_BYTES = 56 * 1024 * 1024

HEAD_PAD = LANES
ROW_TILE = 512
MOE_ROW_TILE = 1024
MOE_EXPERT_TILE = 512
ATTN_TQ = 1024
ATTN_TK = 512
ATTN_UNROLL = 8
Q_SCALE = MLA_SCALE * math.log2(math.e)
LRU_TILE = 512
S5_L = 8
S5_SUPER = LANES // S5_GROUP
S5_NSUPER = S5_GROUPS // S5_SUPER
S5_SUPER_IN = S5_L * LANES
S5_SUPER_STATE = S5_SUPER * S5_STATE
S5_STATE_W = S5_GROUPS * S5_STATE
S5_SCAN_LANES = 512
S5_ROW_TILE = 256
ROUTER_W = LANES


def _cparams(semantics, vmem_limit_bytes=VMEM_LIMIT_BYTES):
    return pltpu.CompilerParams(dimension_semantics=semantics,
                                vmem_limit_bytes=vmem_limit_bytes)


def _full_spec(a):
    nd = a.ndim
    return pl.BlockSpec(a.shape, lambda *_: (0,) * nd)


def _row_spec(tm, width):
    return pl.BlockSpec((tm, width), lambda i: (i, 0))


def _layer_norm(z, g, b):
    mu = jnp.mean(z, axis=-1, keepdims=True)
    zc = z - mu
    var = jnp.mean(zc * zc, axis=-1, keepdims=True)
    return zc * lax.rsqrt(var + LN_EPS) * g + b


def _rms_norm(z, g):
    ms = jnp.mean(z * z, axis=-1, keepdims=True)
    return z * lax.rsqrt(ms + RMS_EPS) * g


def _bdot(a, b):
    return jnp.dot(a.astype(BF16), b.astype(BF16), preferred_element_type=F32)


def _ab_in_kernel(x_ref, win_ref, qg_ref, kvg_ref, wuq_ref, wuk_ref, wuv_ref,
                  cosb_ref, sinb_ref, cost_ref, sint_ref,
                  q_out, k_out, v_out, c_out, kr_out, xb_out, yb_out, *, dm):
    proj = _bdot(x_ref[...], win_ref[...])
    o0 = Q_LORA
    o1 = o0 + KV_LORA
    o2 = o1 + LRU_WIDTH
    o3 = o2 + LRU_WIDTH
    q_lat = proj[:, 0:o0]
    kv_lat = proj[:, o0:o1]
    xb_out[...] = proj[:, o1:o2]
    yb_out[...] = proj[:, o2:o3]
    krp = proj[:, o3:o3 + HEAD_PAD]

    qn = _rms_norm(q_lat, qg_ref[...])
    cn = _rms_norm(kv_lat, kvg_ref[...])
    c_out[...] = cn
    q = _bdot(qn, wuq_ref[...])
    kk = _bdot(cn, wuk_ref[...])
    lane = lax.broadcasted_iota(jnp.int32, (1, MLA_HEADS * HEAD_PAD), 1)
    ones_col = (lane % HEAD_PAD == V_HEAD).astype(F32)
    v_out[...] = (_bdot(cn, wuv_ref[...]) + ones_col).astype(v_out.dtype)

    cb, sb = cosb_ref[...], sinb_ref[...]
    ct, st = cost_ref[...], sint_ref[...]
    cos = cb * ct - sb * st
    sin = sb * ct + cb * st
    half = QK_ROPE // 2
    lane1 = lax.broadcasted_iota(jnp.int32, (1, HEAD_PAD), 1)
    rc = jnp.where(lane1 < QK_HEAD, cos, 0.0)
    ra = jnp.where((lane1 >= QK_NOPE + half) & (lane1 < QK_HEAD), sin, 0.0)
    rb = jnp.where((lane1 >= QK_NOPE) & (lane1 < QK_NOPE + half), -sin, 0.0)

    def rope(z):
        return (z * rc + pltpu.roll(z, half, 1) * ra
                + pltpu.roll(z, HEAD_PAD - half, 1) * rb)

    kr = rope(krp)
    kr_out[...] = kr[:, QK_NOPE:QK_NOPE + QK_ROPE]
    for h in range(MLA_HEADS):
        sl = slice(h * HEAD_PAD, (h + 1) * HEAD_PAD)
        q_out[:, sl] = (rope(q[:, sl]) * Q_SCALE).astype(q_out.dtype)
        k_out[:, sl] = (kk[:, sl] + kr).astype(k_out.dtype)


def _ab_in(x, w, rope_tabs):
    n, dm = x.shape
    tm = min(ROW_TILE, n)
    cosb, sinb, cost, sint = rope_tabs
    consts = [w['w_in'], w['q_g'], w['kv_g'], w['w_uq'], w['w_uk'], w['w_uv'], cosb, sinb]
    tile_spec = pl.BlockSpec((None, 1, HEAD_PAD), lambda i: (i, 0, 0))
    hp = MLA_HEADS * HEAD_PAD
    out_shape = [
        jax.ShapeDtypeStruct((n, hp), BF16),
        jax.ShapeDtypeStruct((n, hp), BF16),
        jax.ShapeDtypeStruct((n, hp), BF16),
        jax.ShapeDtypeStruct((n, KV_LORA), F32),
        jax.ShapeDtypeStruct((n, QK_ROPE), F32),
        jax.ShapeDtypeStruct((n, LRU_WIDTH), F32),
        jax.ShapeDtypeStruct((n, LRU_WIDTH), F32),
    ]
    return pl.pallas_call(
        functools.partial(_ab_in_kernel, dm=dm), name="ab_in",
        grid=(n // tm,),
        in_specs=([_row_spec(tm, dm)] + [_full_spec(c) for c in consts]
                  + [tile_spec, tile_spec]),
        out_specs=[_row_spec(tm, s.shape[1]) for s in out_shape],
        out_shape=out_shape,
        compiler_params=_cparams(("parallel",)),
    )(x, *consts, cost, sint)


def _attn_prompt_kernel(q_ref, k_ref, v_ref, o_ref):
    i = pl.program_id(1)
    tq, tk = ATTN_TQ, ATTN_TK
    ndiag = tq // tk
    nfull = i * ndiag
    row_chunk = lax.broadcasted_iota(jnp.int32, (tq, tk), 0) // CHUNK
    col_chunk = lax.broadcasted_iota(jnp.int32, (tq, tk), 1) // CHUNK
    qs = [q_ref[:, hh * HEAD_PAD:(hh + 1) * HEAD_PAD] for hh in range(2)]

    def update(hh, j, m, acc, mask, row0=0):
        start = pl.multiple_of(j * tk, tk)
        k = k_ref[pl.ds(start, tk), hh * HEAD_PAD:(hh + 1) * HEAD_PAD]
        v = v_ref[pl.ds(start, tk), hh * HEAD_PAD:(hh + 1) * HEAD_PAD]
        s = lax.dot_general(qs[hh][row0:], k, (((1,), (1,)), ((), ())),
                            preferred_element_type=F32)
        if mask is not None:
            s = jnp.where(mask[row0:], s, NEG_INF)
        m_new = jnp.maximum(m, jnp.max(s, axis=-1, keepdims=True))
        alpha = jnp.exp2(m - m_new)
        p = jnp.exp2((s - m_new).astype(BF16))
        return m_new, alpha * acc + jnp.dot(p, v, preferred_element_type=F32)

    def body(j, carry):
        new = []
        for hh in range(2):
            new += update(hh, j, carry[2 * hh], carry[2 * hh + 1], None)
        return tuple(new)

    unroll = ATTN_UNROLL

    def body_unrolled(jj, carry):
        for b in range(unroll):
            carry = body(unroll * jj + b, carry)
        return carry

    init = []
    for _ in range(2):
        init += [jnp.full((tq, 1), NEG_INF, F32), jnp.zeros((tq, HEAD_PAD), F32)]
    carry = lax.fori_loop(0, nfull // unroll, body_unrolled, tuple(init))
    rem0 = nfull - nfull % unroll

    def body_rem(jj, carry):
        for b in range(ndiag):
            carry = body(rem0 + ndiag * jj + b, carry)
        return carry

    carry = list(lax.fori_loop(0, (nfull % unroll) // ndiag, body_rem, carry))
    for d in range(ndiag):
        visible = col_chunk + d * (tk // CHUNK) <= row_chunk
        r0 = d * tk
        for hh in range(2):
            m, acc = carry[2 * hh], carry[2 * hh + 1]
            m_new, acc_new = update(hh, nfull + d, m[r0:], acc[r0:], visible, r0)
            if r0:
                m_new = jnp.concatenate([m[:r0], m_new], axis=0)
                acc_new = jnp.concatenate([acc[:r0], acc_new], axis=0)
            carry[2 * hh], carry[2 * hh + 1] = m_new, acc_new
    outs = [carry[2 * hh + 1][:, :V_HEAD] / carry[2 * hh + 1][:, V_HEAD:V_HEAD + 1]
            for hh in range(2)]
    o_ref[...] = jnp.concatenate(outs, axis=-1).astype(o_ref.dtype)


def _attn_prompt(q, k, v):
    t = q.shape[0]
    pairs = MLA_HEADS // 2
    return pl.pallas_call(
        _attn_prompt_kernel, name="attn_prompt",
        grid=(pairs, t // ATTN_TQ),
        in_specs=[
            pl.BlockSpec((ATTN_TQ, 2 * HEAD_PAD), lambda p, i: (i, p)),
            pl.BlockSpec((t, 2 * HEAD_PAD), lambda p, i: (0, p)),
            pl.BlockSpec((t, 2 * HEAD_PAD), lambda p, i: (0, p)),
        ],
        out_specs=pl.BlockSpec((ATTN_TQ, 2 * V_HEAD), lambda p, i: (i, p)),
        out_shape=jax.ShapeDtypeStruct((t, MLA_HEADS * V_HEAD), BF16),
        compiler_params=_cparams(("parallel", "parallel")),
    )(q, k, v)


def _attn_sample_kernel(q_ref, cn_ref, krn_ref, cp_ref, krp_ref, wukt_ref, wuv_ref, o_ref):
    cp = cp_ref[...].astype(BF16)
    krp = krp_ref[...].astype(BF16)
    cn = cn_ref[...].astype(BF16)
    krn = krn_ref[...].astype(BF16)
    ts = cn.shape[0]
    dn = (((1,), (1,)), ((), ()))
    outs = []
    for hp in range(MLA_HEADS // 2):
        qa, qr = [], []
        for hh in range(2):
            h = 2 * hp + hh
            qh = q_ref[:, h * HEAD_PAD:(h + 1) * HEAD_PAD]
            qa.append(jnp.dot(qh[:, :QK_NOPE], wukt_ref[h], preferred_element_type=F32))
            qr.append(qh[:, QK_NOPE:QK_NOPE + QK_ROPE])
        qa = jnp.concatenate(qa, axis=0).astype(BF16)
        qr = jnp.concatenate(qr, axis=0)
        s_past = (lax.dot_general(qa, cp, dn, preferred_element_type=F32)
                  + lax.dot_general(qr, krp, dn, preferred_element_type=F32))
        s_new = (lax.dot_general(qa, cn, dn, preferred_element_type=F32)
                 + lax.dot_general(qr, krn, dn, preferred_element_type=F32))
        m = jnp.maximum(jnp.max(s_past, axis=-1, keepdims=True),
                        jnp.max(s_new, axis=-1, keepdims=True))
        p_past = jnp.exp2(s_past - m)
        p_new = jnp.exp2(s_new - m)
        l = (jnp.sum(p_past, axis=-1, keepdims=True)
             + jnp.sum(p_new, axis=-1, keepdims=True))
        o_lat = (jnp.dot(p_past.astype(BF16), cp, preferred_element_type=F32)
                 + jnp.dot(p_new.astype(BF16), cn, preferred_element_type=F32)) / l
        for hh in range(2):
            h = 2 * hp + hh
            outs.append(_bdot(o_lat[hh * ts:(hh + 1) * ts], wuv_ref[h]))
    o_ref[...] = jnp.concatenate(outs, axis=-1).astype(o_ref.dtype)


def _attn_sample(q, c_new, kr_new, c_past, kr_past, wukt, wuv):
    bs, past, _ = c_past.shape
    ts = q.shape[0] // bs
    return pl.pallas_call(
        _attn_sample_kernel, name="attn_sample",
        grid=(bs,),
        in_specs=[
            pl.BlockSpec((ts, MLA_HEADS * HEAD_PAD), lambda b: (b, 0)),
            pl.BlockSpec((ts, KV_LORA), lambda b: (b, 0)),
            pl.BlockSpec((ts, QK_ROPE), lambda b: (b, 0)),
            pl.BlockSpec((None, past, KV_LORA), lambda b: (b, 0, 0)),
            pl.BlockSpec((None, past, QK_ROPE), lambda b: (b, 0, 0)),
            _full_spec(wukt), _full_spec(wuv),
        ],
        out_specs=pl.BlockSpec((ts, MLA_HEADS * V_HEAD), lambda b: (b, 0)),
        out_shape=jax.ShapeDtypeStruct((bs * ts, MLA_HEADS * V_HEAD), BF16),
        compiler_params=_cparams(("parallel",)),
    )(q, c_new, kr_new, c_past, kr_past, wukt, wuv)


def _lru_kernel(xb_ref, yb_ref, conv0_ref, h0_ref, cw_ref, cb_ref, wax_ref, bax_ref, ap_ref,
                rec_ref, conv_out_ref, h_out_ref, xcat, hc, *, tb):
    t = pl.program_id(1)
    nt = pl.num_programs(1)
    tail = CONV_WIDTH - 1
    base = SUBLANES

    @pl.when(t == 0)
    def _():
        xcat[base - tail:base, :] = conv0_ref[...]
        hc[...] = h0_ref[...]

    @pl.when(t > 0)
    def _():
        xcat[0:base, :] = xcat[tb:tb + base, :]

    xcat[base:base + tb, :] = xb_ref[...]
    xc = cb_ref[...]
    for tap in range(CONV_WIDTH):
        xc = xc + xcat[base - tail + tap:base - tail + tap + tb, :] * cw_ref[tap:tap + 1, :]

    gates = _bdot(xc, wax_ref[...]) + bax_ref[...]
    r = jax.nn.sigmoid(gates[:, :LRU_WIDTH])
    ig = jax.nn.sigmoid(gates[:, LRU_WIDTH:])
    log_a = -RG_C * r * ap_ref[...]
    a = jnp.exp(log_a)
    u = jnp.sqrt(jnp.tanh(-log_a) * (1.0 + a * a)) * (ig * xc)

    row8 = lax.broadcasted_iota(jnp.int32, (tb, LRU_WIDTH), 0) % SUBLANES
    s = 1
    while s < SUBLANES:
        keep = row8 >= s
        u = jnp.where(keep, a * pltpu.roll(u, s, 0) + u, u)
        a = jnp.where(keep, a * pltpu.roll(a, s, 0), a)
        s *= 2
    hprev = hc[...]
    groups = []
    for g in range(tb // SUBLANES):
        sl = slice(g * SUBLANES, (g + 1) * SUBLANES)
        hg = a[sl] * hprev + u[sl]
        groups.append(hg)
        hprev = hg[SUBLANES - 1:SUBLANES, :]
    h = jnp.concatenate(groups, axis=0)
    hc[...] = hprev
    rec_ref[...] = (h * jax.nn.gelu(yb_ref[...])).astype(rec_ref.dtype)

    @pl.when(t == nt - 1)
    def _():
        conv_out_ref[...] = xcat[base + tb - tail:base + tb, :]
        h_out_ref[...] = h[tb - 1:tb, :]


def _lru(xb, yb, conv0, h0, w):
    b, t, wd = xb.shape
    tb = min(LRU_TILE, t)
    tail = CONV_WIDTH - 1
    consts = [w['conv_w'], w['conv_b'], w['w_ax'], w['b_ax'], w['a_param']]
    seq = pl.BlockSpec((None, tb, wd), lambda bi, ti: (bi, ti, 0))
    return pl.pallas_call(
        functools.partial(_lru_kernel, tb=tb), name="rg_lru",
        grid=(b, t // tb),
        in_specs=[seq, seq,
                  pl.BlockSpec((None, tail, wd), lambda bi, ti: (bi, 0, 0)),
                  pl.BlockSpec((None, 1, wd), lambda bi, ti: (bi, 0, 0))]
                 + [pl.BlockSpec(c.shape, lambda bi, ti: (0, 0)) for c in consts],
        out_specs=[seq,
                   pl.BlockSpec((None, tail, wd), lambda bi, ti: (bi, 0, 0)),
                   pl.BlockSpec((None, 1, wd), lambda bi, ti: (bi, 0, 0))],
        out_shape=[jax.ShapeDtypeStruct((b, t, wd), BF16),
                   jax.ShapeDtypeStruct((b, tail, wd), F32),
                   jax.ShapeDtypeStruct((b, 1, wd), F32)],
        scratch_shapes=[pltpu.VMEM((tb + 2 * SUBLANES, wd), F32), pltpu.VMEM((1, wd), F32)],
        compiler_params=_cparams(("parallel", "arbitrary")),
    )(xb, yb, conv0, h0, *consts)


def _out_ln_value(a_ref, b_ref, x_ref, wa_ref, wb_ref, g_ref, beta_ref, *, alpha):
    y = (jnp.dot(a_ref[...], wa_ref[...], preferred_element_type=F32)
         + jnp.dot(b_ref[...], wb_ref[...], preferred_element_type=F32))
    return _layer_norm(alpha * x_ref[...] + y, g_ref[...], beta_ref[...])


def _out_ln_stage(att, rec, x, wa, wb, g, beta, alpha):
    n, dm = x.shape
    tm = min(ROW_TILE, n)
    return (functools.partial(_out_ln_value, alpha=alpha), [(att, tm), (rec, tm), (x, tm)],
            [wa, wb, g, beta], [], n, dm)


def _stage_kernel(*refs, n_in, value_fn):
    refs[n_in][...] = value_fn(*refs[:n_in], *refs[n_in + 1:])


def _stage_call(stage, name):
    value_fn, row_inputs, consts, scratch, n, dm = stage
    tm = min(ROW_TILE, n)
    arrays = [a for a, _ in row_inputs] + list(consts)
    return pl.pallas_call(
        functools.partial(_stage_kernel, n_in=len(arrays), value_fn=value_fn), name=name,
        grid=(n // tm,),
        in_specs=[_row_spec(rows, a.shape[1]) for a, rows in row_inputs]
                 + [_full_spec(c) for c in consts],
        out_specs=_row_spec(tm, dm),
        out_shape=jax.ShapeDtypeStruct((n, dm), F32),
        scratch_shapes=list(scratch),
        compiler_params=_cparams(("parallel",)),
    )(*arrays)


def _route(x, xh, wr_ref, lane):
    xl = (x - xh.astype(F32)).astype(BF16)
    hh_hl = jnp.dot(xh, wr_ref[...], preferred_element_type=F32)
    logits = (hh_hl[:, :ROUTER_W] + hh_hl[:, ROUTER_W:]
              + jnp.dot(xl, wr_ref[:, :ROUTER_W], preferred_element_type=F32))
    big = float(ROUTER_W)
    lg = jnp.where(lane < N_GROUPS, logits, -jnp.inf)
    mg = jnp.max(lg, axis=-1, keepdims=True)
    gi = jnp.min(jnp.where(lg == mg, lane, big), axis=-1, keepdims=True)
    p_top = 1.0 / jnp.sum(jnp.exp(lg - mg), axis=-1, keepdims=True)
    lo = N_GROUPS + gi * EXPERTS_PER_GROUP
    le = jnp.where((lane >= lo) & (lane < lo + EXPERTS_PER_GROUP), logits, -jnp.inf)
    v1 = jnp.max(le, axis=-1, keepdims=True)
    i1 = jnp.min(jnp.where(le == v1, lane, big), axis=-1, keepdims=True)
    le2 = jnp.where(lane == i1, -jnp.inf, le)
    v2 = jnp.max(le2, axis=-1, keepdims=True)
    i2 = jnp.min(jnp.where(le2 == v2, lane, big), axis=-1, keepdims=True)
    e2 = jnp.exp(v2 - v1)
    return gi, i1, i2, p_top / (1.0 + e2), p_top * e2 / (1.0 + e2)


def _moe_ln_kernel(x_ref, wr_ref, wg_ref, wu_ref, wd_ref, g_ref, beta_ref, o_ref,
                   acc, gates, xb16, *, alpha):
    grp = pl.program_id(1)
    tm = x_ref.shape[0]
    lane = lax.broadcasted_iota(jnp.int32, (tm, ROUTER_W), 1).astype(F32)

    @pl.when(grp == 0)
    def _():
        x = x_ref[...]
        xh = x.astype(BF16)
        xb16[...] = xh
        _, i1, i2, w1, w2 = _route(x, xh, wr_ref, lane)
        gates[...] = jnp.where(lane == i1, w1, 0.0) + jnp.where(lane == i2, w2, 0.0)
        acc[...] = jnp.zeros_like(acc)

    xb = xb16[...]
    gt = gates[...]
    first = (N_GROUPS + grp * EXPERTS_PER_GROUP).astype(F32)
    for el in range(EXPERTS_PER_GROUP):
        ge = jnp.sum(jnp.where(lane == first + el, gt, 0.0), axis=-1, keepdims=True)
        h = jnp.dot(xb, wg_ref[el], preferred_element_type=F32)
        u = jnp.dot(xb, wu_ref[el], preferred_element_type=F32)
        act = (jax.nn.silu(h) * u * ge).astype(BF16)
        acc[...] += jnp.dot(act, wd_ref[el], preferred_element_type=F32)

    @pl.when(grp == pl.num_programs(1) - 1)
    def _():
        o_ref[...] = _layer_norm(alpha * x_ref[...] + acc[...], g_ref[...], beta_ref[...])


def _moe_ln(x, w, layer, g, beta, alpha):
    n, dm = x.shape
    tm = min(MOE_ROW_TILE, n)
    _, ne, _, eh = w['w_gate'].shape
    epg = EXPERTS_PER_GROUP
    return pl.pallas_call(
        functools.partial(_moe_ln_kernel, alpha=alpha), name="moe_ln",
        grid=(n // tm, ne // epg),
        in_specs=[
            pl.BlockSpec((tm, dm), lambda i, e: (i, 0)),
            pl.BlockSpec(w['w_router'].shape, lambda i, e: (0, 0)),
            pl.BlockSpec((None, epg, dm, eh), lambda i, e: (layer, e, 0, 0)),
            pl.BlockSpec((None, epg, dm, eh), lambda i, e: (layer, e, 0, 0)),
            pl.BlockSpec((None, epg, eh, dm), lambda i, e: (layer, e, 0, 0)),
            pl.BlockSpec(g.shape, lambda i, e: (0, 0)),
            pl.BlockSpec(beta.shape, lambda i, e: (0, 0)),
        ],
        out_specs=pl.BlockSpec((tm, dm), lambda i, e: (i, 0)),
        out_shape=jax.ShapeDtypeStruct((n, dm), F32),
        scratch_shapes=[pltpu.VMEM((tm, dm), F32), pltpu.VMEM((tm, ROUTER_W), F32),
                        pltpu.VMEM((tm, dm), BF16)],
        compiler_params=_cparams(("parallel", "arbitrary"), MOE_VMEM_LIMIT_BYTES),
    )(x, w['w_router'], w['w_gate'], w['w_up'], w['w_down'], g, beta)


META_W = LANES
META_GROUP = EXPERTS_PER_GROUP
META_RANK = EXPERTS_PER_GROUP + 1


def _moe_route_kernel(*refs, n_in, n_tiles, value_fn):
    in_refs = refs[:n_in]
    wr_ref, tri_ref, xg_ref, meta_ref, cnt_ref, run = refs[n_in:n_in + 6]
    tm, dm = xg_ref.shape[0], xg_ref.shape[1] - META_W
    i = pl.program_id(0)
    lane = lax.broadcasted_iota(jnp.int32, (tm, ROUTER_W), 1).astype(F32)

    @pl.when(i == 0)
    def _():
        run[...] = jnp.zeros_like(run)

    @pl.when(i < n_tiles)
    def _():
        x = value_fn(*in_refs, *refs[n_in + 6:])
        gi, i1, i2, w1, w2 = _route(x, x.astype(BF16), wr_ref, lane)
        lo = N_GROUPS + gi * EXPERTS_PER_GROUP
        in_group = lane == gi
        incl = jnp.dot(tri_ref[...], in_group.astype(BF16), preferred_element_type=F32)
        rank = jnp.sum(jnp.where(in_group, incl + run[...], 0.0), axis=-1,
                       keepdims=True) - 1.0
        run[...] += incl[tm - 1:tm, :]
        cnt_ref[...] = run[...]
        meta = (jnp.where(lane == i1 - lo, w1, 0.0) + jnp.where(lane == i2 - lo, w2, 0.0)
                + jnp.where(lane == META_GROUP, gi, 0.0)
                + jnp.where(lane == META_RANK, rank, 0.0))
        xg_ref[:, :dm] = x
        xg_ref[:, dm:] = meta
        meta_ref[...] = meta

    @pl.when(i >= n_tiles)
    def _():
        xg_ref[...] = jnp.zeros_like(xg_ref)


def _moe_route(stage, w_router, p):
    value_fn, row_inputs, consts, scratch, n, dm = stage
    tm = ROW_TILE
    n_tiles = n // tm
    tri = jnp.tril(jnp.ones((tm, tm), BF16))
    last = lambda i: (jnp.minimum(i, n_tiles - 1), 0)
    arrays = [a for a, _ in row_inputs] + list(consts)
    in_specs = ([pl.BlockSpec((rows, a.shape[1]), last) for a, rows in row_inputs]
                + [_full_spec(c) for c in consts] + [_full_spec(w_router), _full_spec(tri)])
    return pl.pallas_call(
        functools.partial(_moe_route_kernel, n_in=len(arrays), n_tiles=n_tiles,
                          value_fn=value_fn), name="moe_route",
        grid=(p // tm,),
        in_specs=in_specs,
        out_specs=[_row_spec(tm, dm + META_W), pl.BlockSpec((tm, META_W), last),
                   pl.BlockSpec((1, ROUTER_W), lambda i: (0, 0))],
        out_shape=[jax.ShapeDtypeStruct((p, dm + META_W), F32),
                   jax.ShapeDtypeStruct((n, META_W), F32),
                   jax.ShapeDtypeStruct((1, ROUTER_W), F32)],
        scratch_shapes=[pltpu.VMEM((1, ROUTER_W), F32)] + list(scratch),
        compiler_params=_cparams(("arbitrary",)),
    )(*arrays, w_router, tri)


def _moe_experts_kernel(tg_ref, nu_ref, xs_ref, wg_ref, wu_ref, wd_ref, y_ref):
    del tg_ref
    dm = y_ref.shape[1]

    @pl.when(pl.program_id(0) < nu_ref[0])
    def _():
        xb = xs_ref[:, :dm].astype(BF16)
        meta = xs_ref[:, dm:]
        acc = None
        for el in range(EXPERTS_PER_GROUP):
            h = jnp.dot(xb, wg_ref[el], preferred_element_type=F32)
            u = jnp.dot(xb, wu_ref[el], preferred_element_type=F32)
            act = (jax.nn.silu(h) * u * meta[:, el:el + 1]).astype(BF16)
            d = jnp.dot(act, wd_ref[el], preferred_element_type=F32)
            acc = d if acc is None else acc + d
        y_ref[...] = acc


def _moe_experts(xs, tile_group, n_used, w, layer):
    p, wx = xs.shape
    dm = wx - META_W
    te = MOE_EXPERT_TILE
    _, ne, _, eh = w['w_gate'].shape
    epg = EXPERTS_PER_GROUP

    def tile(i, tg, nu):
        return jnp.minimum(i, nu[0] - 1)

    wspec = lambda shape: pl.BlockSpec(
        (None, epg) + shape, lambda i, tg, nu: (layer, tg[tile(i, tg, nu)], 0, 0))
    grid_spec = pltpu.PrefetchScalarGridSpec(
        num_scalar_prefetch=2,
        grid=(p // te,),
        in_specs=[pl.BlockSpec((te, wx), lambda i, tg, nu: (tile(i, tg, nu), 0)),
                  wspec((dm, eh)), wspec((dm, eh)), wspec((eh, dm))],
        out_specs=pl.BlockSpec((te, dm), lambda i, tg, nu: (tile(i, tg, nu), 0)),
    )
    return pl.pallas_call(
        _moe_experts_kernel, name="moe_experts",
        grid_spec=grid_spec,
        out_shape=jax.ShapeDtypeStruct((p, dm), F32),
        compiler_params=_cparams(("arbitrary",)),
    )(tile_group, n_used, xs, w['w_gate'], w['w_up'], w['w_down'])


def _res_ln_value(xg_ref, y_ref, g_ref, beta_ref, *, alpha):
    dm = y_ref.shape[1]
    return _layer_norm(alpha * xg_ref[:, :dm] + y_ref[...], g_ref[...], beta_ref[...])


def _res_ln_stage(xg, y, g, beta, alpha):
    n, dm = y.shape
    return (functools.partial(_res_ln_value, alpha=alpha), [(xg, ROW_TILE), (y, ROW_TILE)],
            [g, beta], [], n, dm)


def _identity_value(x_ref):
    return x_ref[...]


def _array_stage(x):
    n, dm = x.shape
    return (_identity_value, [(x, min(ROW_TILE, n))], [], [], n, dm)


def _materialize(stage, name):
    if stage[0] is _identity_value:
        return stage[1][0][0]
    return _stage_call(stage, name)


def _moe(stage, stage_name, w, layer, g, beta, alpha, overlap=None):
    n = stage[4]
    granule = 2 * SC_GATHER_ROWS * SC_CORES * SC_SUBCORES
    if n % granule == 0 and (N_GROUPS * MOE_EXPERT_TILE) % granule == 0:
        return _moe_routed_ln(stage, w, layer, g, beta, alpha, overlap)
    out = _array_stage(_moe_ln(_stage_call(stage, stage_name), w, layer, g, beta, alpha))
    return out, overlap


def _moe_routed_ln(stage, w, layer, g, beta, alpha, overlap):
    n, dm = stage[4], stage[5]
    te = MOE_EXPERT_TILE
    npad = N_GROUPS * te
    p = n + npad
    xg, meta, cnt = _moe_route(stage, w['w_router'], p)
    gid = meta[:, META_GROUP].astype(jnp.int32)
    rank = meta[:, META_RANK].astype(jnp.int32)
    counts = cnt[0, :N_GROUPS].astype(jnp.int32)
    padded = (counts + te - 1) // te * te
    ends = jnp.cumsum(padded)
    starts = ends - padded

    def lookup(table, idx):
        out = jnp.broadcast_to(table[0], idx.shape)
        for entry in range(1, table.shape[0]):
            out = jnp.where(idx >= entry, table[entry], out)
        return out

    pos = lookup(starts, gid) + rank
    gap_len = jnp.concatenate([padded - counts, p - ends[-1:]])
    gap_start = jnp.concatenate([starts + counts, ends[-1:]])
    gap_end = jnp.cumsum(gap_len)
    k = jnp.arange(npad, dtype=jnp.int32)
    seg = jnp.sum(k[:, None] >= gap_end[None, :], axis=1)
    filler_pos = lookup(gap_start, seg) + k - lookup(gap_end - gap_len, seg)
    pos_all = jnp.concatenate([pos, filler_pos]).astype(jnp.int32)
    tile_start = jnp.arange(p // te, dtype=jnp.int32) * te
    tile_group = jnp.minimum(jnp.sum(tile_start[:, None] >= ends[None, :], axis=1),
                             N_GROUPS - 1).astype(jnp.int32)
    n_used = (ends[-1:] // te).astype(jnp.int32)
    xs = _sc_scatter_rows(xg, pos_all)
    if overlap is not None:
        xs, overlap = lax.optimization_barrier((xs, overlap))
    ys = _moe_experts(xs, tile_group, n_used, w, layer)
    y = _sc_gather_rows(ys, pos)
    return _res_ln_stage(xg, y, g, beta, alpha), overlap


SC_CORES = 2
SC_SUBCORES = 16
SC_GATHER_ROWS = 32


def _sc_gather_rows(table, idx):
    b = idx.shape[0]
    _, d = table.shape
    nw = SC_CORES * SC_SUBCORES
    ch = SC_GATHER_ROWS
    per_w = b // nw
    assert b % (nw * 2 * ch) == 0
    npair = per_w // (2 * ch)
    mesh = plsc.VectorSubcoreMesh(core_axis_name="c", subcore_axis_name="s")

    @functools.partial(
        pl.kernel, mesh=mesh, out_type=jax.ShapeDtypeStruct((b, d), table.dtype),
        scratch_types=[pltpu.VMEM((2, ch), jnp.int32), pltpu.VMEM((2, ch, d), table.dtype),
                       pltpu.SemaphoreType.DMA, pltpu.SemaphoreType.DMA],
        name="sc_gather_rows")
    def gather(table_hbm, idx_hbm, out_hbm, idx_v, rows_v, sem0, sem1):
        sems = (sem0, sem1)
        wid = lax.axis_index("s") * SC_CORES + lax.axis_index("c")
        base = wid * per_w

        def start(chunk, slot):
            off = pl.multiple_of(base + chunk * ch, SUBLANES)
            pltpu.sync_copy(idx_hbm.at[pl.ds(off, ch)], idx_v.at[slot])
            pltpu.async_copy(table_hbm.at[idx_v.at[slot]], rows_v.at[slot], sems[slot])

        def finish(chunk, slot):
            off = pl.multiple_of(base + chunk * ch, SUBLANES)
            pltpu.make_async_copy(table_hbm.at[idx_v.at[slot]], rows_v.at[slot],
                                  sems[slot]).wait()
            pltpu.sync_copy(rows_v.at[slot], out_hbm.at[pl.ds(off, ch)])

        start(0, 0)

        @pl.loop(0, npair)
        def _(p):
            c0 = 2 * p
            start(c0 + 1, 1)
            finish(c0, 0)

            @pl.when(p + 1 < npair)
            def _():
                start(c0 + 2, 0)
            finish(c0 + 1, 1)

    return gather(table, idx)


def _sc_scatter_rows(rows, idx):
    b, d = rows.shape
    nw = SC_CORES * SC_SUBCORES
    ch = SC_GATHER_ROWS
    per_w = b // nw
    assert idx.shape == (b,) and b % (nw * 2 * ch) == 0
    npair = per_w // (2 * ch)
    mesh = plsc.VectorSubcoreMesh(core_axis_name="c", subcore_axis_name="s")

    @functools.partial(
        pl.kernel, mesh=mesh, out_type=jax.ShapeDtypeStruct((b, d), rows.dtype),
        scratch_types=[pltpu.VMEM((2, ch), jnp.int32), pltpu.VMEM((2, ch, d), rows.dtype),
                       pltpu.SemaphoreType.DMA, pltpu.SemaphoreType.DMA],
        name="sc_scatter_rows")
    def scatter(rows_hbm, idx_hbm, out_hbm, idx_v, rows_v, sem0, sem1):
        sems = (sem0, sem1)
        wid = lax.axis_index("s") * SC_CORES + lax.axis_index("c")
        base = wid * per_w

        def write(slot):
            return pltpu.make_async_copy(rows_v.at[slot], out_hbm.at[idx_v.at[slot]],
                                         sems[slot])

        @pl.loop(0, npair)
        def _(p):
            for slot in range(2):
                @pl.when(p > 0)
                def _():
                    write(slot).wait()
                off = pl.multiple_of(base + (2 * p + slot) * ch, SUBLANES)
                pltpu.sync_copy(idx_hbm.at[pl.ds(off, ch)], idx_v.at[slot])
                pltpu.sync_copy(rows_hbm.at[pl.ds(off, ch)], rows_v.at[slot])
                write(slot).start()

        write(0).wait()
        write(1).wait()

    return scatter(rows, idx)


def _c_in_kernel(*refs, n_in, value_fn):
    w_ref, x_out, u_ref, u2_ref, us = refs[n_in:n_in + 5]
    x = value_fn(*refs[:n_in], *refs[n_in + 5:])
    x_out[...] = x
    u = _bdot(x, w_ref[...])
    u_ref[...] = u
    rows = u2_ref.shape[0]
    for c in range(S5_WIDTH // LANES):
        us[c] = u[:, c * LANES:(c + 1) * LANES]
    for s in range(S5_L):
        for c in range(S5_WIDTH // LANES):
            lo = s * S5_WIDTH + c * LANES
            u2_ref[:, lo:lo + LANES] = us[c, pl.ds(s, rows, stride=S5_L), :].astype(u2_ref.dtype)


def _c_in(stage, w_in):
    value_fn, row_inputs, consts, scratch, n, dm = stage
    tm = min(ROW_TILE, n)
    wu = w_in.shape[1]
    arrays = [a for a, _ in row_inputs] + list(consts)
    return pl.pallas_call(
        functools.partial(_c_in_kernel, n_in=len(arrays), value_fn=value_fn), name="c_in",
        grid=(n // tm,),
        in_specs=[_row_spec(rows, a.shape[1]) for a, rows in row_inputs]
                 + [_full_spec(c) for c in consts] + [_full_spec(w_in)],
        out_specs=[_row_spec(tm, dm), _row_spec(tm, wu), _row_spec(tm // S5_L, S5_L * wu)],
        out_shape=[jax.ShapeDtypeStruct((n, dm), F32),
                   jax.ShapeDtypeStruct((n, wu), F32),
                   jax.ShapeDtypeStruct((n // S5_L, S5_L * wu), BF16)],
        scratch_shapes=[pltpu.VMEM((wu // LANES, tm, LANES), F32)] + list(scratch),
        compiler_params=_cparams(("parallel",)),
    )(*arrays, w_in)


def _s5_super_rows(u_ref, q):
    tiles = [u_ref[:, s * S5_WIDTH + q * LANES:s * S5_WIDTH + (q + 1) * LANES]
             for s in range(S5_L)]
    return jnp.concatenate(tiles, axis=1).astype(BF16)


def _s5_e_kernel(u_ref, ere_ref, eim_ref, ere_out, eim_out):
    for q in range(S5_NSUPER):
        uq = _s5_super_rows(u_ref, q)
        sl = slice(q * S5_SUPER_STATE, (q + 1) * S5_SUPER_STATE)
        ere_out[:, sl] = jnp.dot(uq, ere_ref[q], preferred_element_type=F32)
        eim_out[:, sl] = jnp.dot(uq, eim_ref[q], preferred_element_type=F32)


def _const_spec(a):
    nd = a.ndim
    return pl.BlockSpec(a.shape, lambda *_: (0,) * nd, pipeline_mode=pl.Buffered(1))


def _s5_e(u2, tabs):
    rows, wu = u2.shape
    tr = min(S5_ROW_TILE, rows)
    out = jax.ShapeDtypeStruct((rows, S5_STATE_W), F32)
    return pl.pallas_call(
        _s5_e_kernel, name="s5_e",
        grid=(rows // tr,),
        in_specs=[_row_spec(tr, wu), _const_spec(tabs['e_re']), _const_spec(tabs['e_im'])],
        out_specs=[_row_spec(tr, S5_STATE_W)] * 2,
        out_shape=[out, out],
        compiler_params=_cparams(("parallel",)),
    )(u2, tabs['e_re'], tabs['e_im'])


def _s5_scan_kernel(ere_ref, eim_ref, h0r_ref, h0i_ref, stp_re_ref, stp_im_ref,
                    apw_re_ref, apw_im_ref, hpr_ref, hpi_ref, hfr_ref, hfi_ref, *, nc, last_row):
    w = ere_ref.shape[-1]
    row = lax.broadcasted_iota(jnp.int32, (SUBLANES, w), 0)
    apr = apw_re_ref[...]
    api = apw_im_ref[...]

    def tile(i, carry):
        cr, ci = carry
        start = pl.multiple_of(i * SUBLANES, SUBLANES)
        sr = ere_ref[pl.ds(start, SUBLANES), :]
        si = eim_ref[pl.ds(start, SUBLANES), :]
        for k in range(3):
            s = 1 << k
            keep = row >= s
            ar = stp_re_ref[k:k + 1, :]
            ai = stp_im_ref[k:k + 1, :]
            pr = pltpu.roll(sr, s, 0)
            pi = pltpu.roll(si, s, 0)
            sr, si = (jnp.where(keep, sr + ar * pr - ai * pi, sr),
                      jnp.where(keep, si + ar * pi + ai * pr, si))
        hr = sr + apr * cr - api * ci
        hi = si + apr * ci + api * cr
        first = row == 0
        hpr_ref[pl.ds(start, SUBLANES), :] = jnp.where(first, cr, pltpu.roll(hr, 1, 0))
        hpi_ref[pl.ds(start, SUBLANES), :] = jnp.where(first, ci, pltpu.roll(hi, 1, 0))
        return hr, hi

    def body(i, carry):
        hr, hi = tile(i, carry)
        return hr[SUBLANES - 1:SUBLANES, :], hi[SUBLANES - 1:SUBLANES, :]

    ntile = nc // SUBLANES
    carry = lax.fori_loop(0, ntile - 1, body, (h0r_ref[...], h0i_ref[...]))
    hr, hi = tile(ntile - 1, carry)
    hfr_ref[...] = hr[last_row:last_row + 1, :]
    hfi_ref[...] = hi[last_row:last_row + 1, :]


def _s5_scan(e_re, e_im, h0r, h0i, tabs, n_chunks):
    b, nc, w = e_re.shape
    last_row = (n_chunks - 1) % SUBLANES
    wb = S5_SCAN_LANES
    consts = [tabs['step_re'], tabs['step_im'], tabs['apow_re'], tabs['apow_im']]
    seq = pl.BlockSpec((None, nc, wb), lambda bi, li: (bi, 0, li))
    vec = pl.BlockSpec((None, 1, wb), lambda bi, li: (bi, 0, li))
    return pl.pallas_call(
        functools.partial(_s5_scan_kernel, nc=nc, last_row=last_row), name="s5_scan",
        grid=(b, w // wb),
        in_specs=[seq, seq, vec, vec]
                 + [pl.BlockSpec((c.shape[0], wb), lambda bi, li: (0, li)) for c in consts],
        out_specs=[seq, seq, vec, vec],
        out_shape=[jax.ShapeDtypeStruct((b, nc, w), F32)] * 2
                  + [jax.ShapeDtypeStruct((b, 1, w), F32)] * 2,
        compiler_params=_cparams(("parallel", "parallel")),
    )(e_re, e_im, h0r, h0i, *consts)


def _s5_y_kernel(u_ref, hpr_ref, hpi_ref, m_ref, fre_ref, fim_ref, y_ref):
    for q in range(S5_NSUPER):
        uq = _s5_super_rows(u_ref, q)
        sl = slice(q * S5_SUPER_STATE, (q + 1) * S5_SUPER_STATE)
        yq = (jnp.dot(uq, m_ref[q], preferred_element_type=F32)
              + _bdot(hpr_ref[:, sl], fre_ref[q])
              + _bdot(hpi_ref[:, sl], fim_ref[q]))
        for t in range(S5_L):
            y_ref[:, t * S5_WIDTH + q * LANES:t * S5_WIDTH + (q + 1) * LANES] = (
                yq[:, t * LANES:(t + 1) * LANES])


def _s5_y(u2, hp_re, hp_im, tabs):
    rows, wu = u2.shape
    tr = min(S5_ROW_TILE, rows)
    consts = [tabs['m'], tabs['f_re'], tabs['f_im']]
    return pl.pallas_call(
        _s5_y_kernel, name="s5_y",
        grid=(rows // tr,),
        in_specs=[_row_spec(tr, wu), _row_spec(tr, S5_STATE_W), _row_spec(tr, S5_STATE_W)]
                 + [_const_spec(c) for c in consts],
        out_specs=_row_spec(tr, wu),
        out_shape=jax.ShapeDtypeStruct((rows, wu), F32),
        compiler_params=_cparams(("parallel",)),
    )(u2, hp_re, hp_im, *consts)


def _c_out_value(y2_ref, u_ref, x_ref, d_ref, wglu_ref, wout_ref, g_ref, beta_ref, ys, *, alpha):
    rows = y2_ref.shape[0]
    ntile = S5_WIDTH // LANES
    for s in range(S5_L):
        for c in range(ntile):
            lo = s * S5_WIDTH + c * LANES
            ys[c, pl.ds(s, rows, stride=S5_L), :] = y2_ref[:, lo:lo + LANES]
    y = jnp.concatenate([ys[c] for c in range(ntile)], axis=1) + d_ref[...] * u_ref[...]
    vg = _bdot(jax.nn.gelu(y), wglu_ref[...])
    half = vg.shape[1] // 2
    z = vg[:, :half] * jax.nn.sigmoid(vg[:, half:])
    o = _bdot(z, wout_ref[...])
    return _layer_norm(alpha * x_ref[...] + o, g_ref[...], beta_ref[...])


def _c_out_stage(y2, u, x, w, g, beta, alpha):
    n, dm = x.shape
    tm = min(ROW_TILE, n)
    return (functools.partial(_c_out_value, alpha=alpha),
            [(y2, tm // S5_L), (u, tm), (x, tm)], [w['d'], w['w_glu'], w['w_out'], g, beta],
            [pltpu.VMEM((u.shape[1] // LANES, tm, LANES), F32)], n, dm)


def _prep_ab(w_in, q_g, kv_g, w_uq, w_ukv, conv_w, conv_b, wa, ba, wx, bx, a_param, w_out):
    dm = w_in.shape[0]
    s0 = Q_LORA
    s1 = s0 + KV_LORA
    s2 = s1 + QK_ROPE
    s3 = s2 + LRU_WIDTH
    pad_lo = jnp.zeros((dm, QK_NOPE), F32)
    pad_hi = jnp.zeros((dm, HEAD_PAD - QK_HEAD), F32)
    w_in_p = jnp.concatenate([w_in[:, :s1], w_in[:, s2:s3], w_in[:, s3:],
                              pad_lo, w_in[:, s1:s2], pad_hi], axis=1).astype(BF16)
    wq = w_uq.reshape(Q_LORA, MLA_HEADS, QK_HEAD)
    wq = jnp.pad(wq, ((0, 0), (0, 0), (0, HEAD_PAD - QK_HEAD)))
    wq = wq.reshape(Q_LORA, MLA_HEADS * HEAD_PAD).astype(BF16)
    wkv = w_ukv.reshape(KV_LORA, MLA_HEADS, QK_NOPE + V_HEAD)
    wk = jnp.pad(wkv[:, :, :QK_NOPE], ((0, 0), (0, 0), (0, HEAD_PAD - QK_NOPE)))
    wk = wk.reshape(KV_LORA, MLA_HEADS * HEAD_PAD).astype(BF16)
    wv = jnp.pad(wkv[:, :, QK_NOPE:], ((0, 0), (0, 0), (0, HEAD_PAD - V_HEAD)))
    wv = wv.reshape(KV_LORA, MLA_HEADS * HEAD_PAD).astype(BF16)
    wukt = jnp.transpose(wkv[:, :, :QK_NOPE], (1, 2, 0)).astype(BF16)
    wuv_h = jnp.transpose(wkv[:, :, QK_NOPE:], (1, 0, 2)).astype(BF16)
    eye = jnp.eye(LRU_BLOCKS, dtype=F32)

    def blockdiag(wb):
        return jnp.einsum('hij,hk->hikj', wb, eye).reshape(LRU_WIDTH, LRU_WIDTH)

    w_ax = jnp.concatenate([blockdiag(wa), blockdiag(wx)], axis=1).astype(BF16)
    b_ax = jnp.concatenate([ba, bx])[None, :]
    att_w = MLA_HEADS * V_HEAD
    return dict(w_in=w_in_p, q_g=q_g[None, :], kv_g=kv_g[None, :], w_uq=wq, w_uk=wk, w_uv=wv,
                wukt=wukt, wuv_h=wuv_h, conv_w=conv_w, conv_b=conv_b[None, :], w_ax=w_ax,
                b_ax=b_ax, a_param=jax.nn.softplus(-a_param)[None, :],
                w_out_att=w_out[:att_w].astype(BF16), w_out_rec=w_out[att_w:].astype(BF16))


def _rope_tables(row_pos, tile_pos):
    half = QK_ROPE // 2
    inv = ROPE_THETA ** (-jnp.arange(half, dtype=F32) / half)
    inv_lane = jnp.concatenate([jnp.zeros((QK_NOPE,), F32), inv, inv,
                                jnp.zeros((HEAD_PAD - QK_HEAD,), F32)])
    ang_row = row_pos.astype(F32)[:, None] * inv_lane[None, :]
    ang_tile = tile_pos.astype(F32)[:, None, None] * inv_lane
    return jnp.cos(ang_row), jnp.sin(ang_row), jnp.cos(ang_tile), jnp.sin(ang_tile)


def _prep_router(w_group, w_expert):
    dm = w_group.shape[0]
    pad = jnp.zeros((dm, ROUTER_W - N_GROUPS - N_EXPERTS), F32)
    w_router = jnp.concatenate([w_group, w_expert, pad], axis=1)
    w_router_hi = w_router.astype(BF16)
    w_router_lo = (w_router - w_router_hi.astype(F32)).astype(BF16)
    return dict(w_router=jnp.concatenate([w_router_hi, w_router_lo], axis=1))


def _cmul(ar, ai, br, bi):
    return ar * br - ai * bi, ar * bi + ai * br


def _prep_s5(lam_re, lam_im, log_step, b_re, b_im, c_re, c_im):
    hi = lax.Precision.HIGHEST
    L, G, P, C = S5_L, S5_GROUPS, S5_STATE, S5_GROUP
    lr, li = lam_re, lam_im
    dt = jnp.exp(log_step)[:, None]
    mag = jnp.exp(lr * dt)
    ar, ai = mag * jnp.cos(li * dt), mag * jnp.sin(li * dt)
    den = lr * lr + li * li
    nr = ar - 1.0
    cr = (nr * lr + ai * li) / den
    ci = (ai * lr - nr * li) / den
    bbr = cr[..., None] * b_re - ci[..., None] * b_im
    bbi = cr[..., None] * b_im + ci[..., None] * b_re
    prs, pis = [jnp.ones_like(ar)], [jnp.zeros_like(ar)]
    for _ in range(L):
        nr_, ni_ = _cmul(prs[-1], pis[-1], ar, ai)
        prs.append(nr_)
        pis.append(ni_)
    pr = jnp.stack(prs)
    pi = jnp.stack(pis)
    xr = pr[:L, :, :, None] * bbr - pi[:L, :, :, None] * bbi
    xi = pr[:L, :, :, None] * bbi + pi[:L, :, :, None] * bbr
    kk = (jnp.einsum('gcp,kgpd->kgcd', c_re, xr, precision=hi)
          - jnp.einsum('gcp,kgpd->kgcd', c_im, xi, precision=hi))
    p1r = pr[1:, :, :, None]
    p1i = pi[1:, :, :, None]
    cre_t = jnp.transpose(c_re, (0, 2, 1))[None]
    cim_t = jnp.transpose(c_im, (0, 2, 1))[None]
    f_re = cre_t * p1r - cim_t * p1i
    f_im = -cre_t * p1i - cim_t * p1r

    nq, sg = S5_NSUPER, S5_SUPER
    eye = jnp.eye(sg, dtype=F32)
    dk = jnp.einsum('kqgcd,gh->kqgdhc', kk.reshape(L, nq, sg, C, C), eye)
    dk = dk.reshape(L, nq, LANES, LANES).astype(BF16)
    lag = jnp.arange(L)[None, :] - jnp.arange(L)[:, None]
    m = jnp.where((lag >= 0)[:, :, None, None, None], dk[jnp.clip(lag, 0, L - 1)], 0)
    m = jnp.transpose(m, (2, 0, 3, 1, 4)).reshape(nq, L * LANES, L * LANES)

    row_g = (jnp.arange(L * LANES) // C) % sg
    col_g = jnp.arange(sg * P) // P
    same_group = (row_g[:, None] == col_g[None, :])[None]

    def pack_rows(a):
        a = jnp.transpose(a.reshape(L, nq, sg, P, C), (1, 0, 4, 2, 3))
        a = a.reshape(nq, L, 1, C, sg * P).astype(BF16)
        a = jnp.broadcast_to(a, (nq, L, sg, C, sg * P)).reshape(nq, L * LANES, sg * P)
        return jnp.where(same_group, a, 0)

    def pack_e(x):
        return pack_rows(x[::-1])

    def pack_f(f):
        return jnp.swapaxes(pack_rows(f), 1, 2)

    alr, ali = pr[L].reshape(1, G * P), pi[L].reshape(1, G * P)
    qr, qi = [alr], [ali]
    for _ in range(SUBLANES - 1):
        nr_, ni_ = _cmul(qr[-1], qi[-1], alr, ali)
        qr.append(nr_)
        qi.append(ni_)
    apow_re = jnp.concatenate(qr, axis=0)
    apow_im = jnp.concatenate(qi, axis=0)
    step_re = jnp.concatenate([qr[0], qr[1], qr[3]], axis=0)
    step_im = jnp.concatenate([qi[0], qi[1], qi[3]], axis=0)
    return dict(m=m, e_re=pack_e(xr), e_im=pack_e(xi), f_re=pack_f(f_re),
                f_im=pack_f(f_im), apow_re=apow_re, apow_im=apow_im,
                step_re=step_re, step_im=step_im)


def _mixer_ab(x, pos0, past_ckv, past_krope, conv0, h0, w):
    b, t, dm = x.shape
    xf = x.reshape(b * t, dm)
    tm = min(ROW_TILE, b * t)
    assert tm % t == 0 or t % tm == 0
    row_pos = jnp.arange(tm) % t
    tile_pos = pos0 + (jnp.arange(b * t // tm) * tm) % t
    q, k, v, c_new, kr_new, xb, yb = _ab_in(xf, w, _rope_tables(row_pos, tile_pos))
    if past_ckv is None:
        att = _attn_prompt(q, k, v)
    else:
        att = _attn_sample(q, c_new, kr_new, past_ckv, past_krope, w['wukt'], w['wuv_h'])
    rec, conv_new, h_new = _lru(xb.reshape(b, t, -1), yb.reshape(b, t, -1), conv0,
                                h0[:, None, :], w)
    return (att, rec.reshape(b * t, -1), c_new.reshape(b, t, -1), kr_new.reshape(b, t, -1),
            conv_new, h_new[:, 0, :])


def _mixer_c(x_stage, b, t, h0_re, h0_im, w):
    nc = t // S5_L
    assert t % S5_L == 0 and nc % SUBLANES == 0
    x, u, u2 = _c_in(x_stage, w['w_in'])
    e_re, e_im = _s5_e(u2, w['tabs'])
    hp_re, hp_im, hf_re, hf_im = _s5_scan(
        e_re.reshape(b, nc, S5_STATE_W), e_im.reshape(b, nc, S5_STATE_W),
        h0_re.reshape(b, 1, S5_STATE_W), h0_im.reshape(b, 1, S5_STATE_W), w['tabs'], nc)
    y2 = _s5_y(u2, hp_re.reshape(b * nc, S5_STATE_W), hp_im.reshape(b * nc, S5_STATE_W),
               w['tabs'])
    return (x, y2, u, hf_re.reshape(b, S5_GROUPS, S5_STATE),
            hf_im.reshape(b, S5_GROUPS, S5_STATE))


def kernel(x_prompt, x_sample, cache_mla_ckv, cache_mla_krope, state_lru_conv, state_lru_h,
           state_s5_re, state_s5_im, w_in_ab, q_norm_g, kv_norm_g, w_uq, w_ukv,
           lru_conv_w, lru_conv_b, lru_w_a, lru_b_a, lru_w_x, lru_b_x, lru_a_param, w_out_ab,
           w_in_c, s5_lam_re, s5_lam_im, s5_log_step, s5_b_re, s5_b_im, s5_c_re, s5_c_im,
           s5_d, s5_w_glu, w_out_c, ln_mix_g, ln_mix_b, ln_ffn_g, ln_ffn_b,
           moe_w_group, moe_w_expert, moe_w_gate, moe_w_up, moe_w_down):
    bp, tp, dm = x_prompt.shape
    bs, ts, _ = x_sample.shape
    past = cache_mla_ckv.shape[2]
    depth = ln_mix_g.shape[0]
    alpha = (2 * depth) ** 0.25
    assert (past + ts - 1) // CHUNK <= past // CHUNK
    hp = _array_stage(x_prompt.reshape(bp * tp, dm))
    hs = _array_stage(x_sample.reshape(bs * ts, dm))
    moe_experts = dict(w_gate=moe_w_gate.astype(BF16), w_up=moe_w_up.astype(BF16),
                       w_down=moe_w_down.astype(BF16))
    outs = {k: [] for k in ('ckv_p', 'ckv_s', 'kr_p', 'kr_s', 'cv_p', 'cv_s', 'lh_p', 'lh_s',
                            's5r_p', 's5r_s', 's5i_p', 's5i_s')}
    for layer in range(depth):
        j = layer // 2
        g_mix, b_mix = ln_mix_g[layer][None, :], ln_mix_b[layer][None, :]
        g_ffn, b_ffn = ln_ffn_g[layer][None, :], ln_ffn_b[layer][None, :]
        if layer % 2 == 0:
            w = _prep_ab(w_in_ab[j], q_norm_g[j], kv_norm_g[j], w_uq[j], w_ukv[j], lru_conv_w[j],
                         lru_conv_b[j], lru_w_a[j], lru_b_a[j], lru_w_x[j], lru_b_x[j],
                         lru_a_param[j], w_out_ab[j])
            xp, xs = _materialize(hp, "res_ln"), _materialize(hs, "res_ln")
            att_p, rec_p, c1, k1, v1, h1 = _mixer_ab(
                xp.reshape(bp, tp, dm), 0, None, None, jnp.zeros((bp, CONV_WIDTH - 1, LRU_WIDTH), F32),
                jnp.zeros((bp, LRU_WIDTH), F32), w)
            att_s, rec_s, c2, k2, v2, h2 = _mixer_ab(
                xs.reshape(bs, ts, dm), past, cache_mla_ckv[j], cache_mla_krope[j], state_lru_conv[j],
                state_lru_h[j], w)
            outs['ckv_p'].append(c1); outs['ckv_s'].append(c2)
            outs['kr_p'].append(k1); outs['kr_s'].append(k2)
            outs['cv_p'].append(v1); outs['cv_s'].append(v2)
            outs['lh_p'].append(h1); outs['lh_s'].append(h2)
            stage_p = _out_ln_stage(att_p, rec_p, xp, w['w_out_att'],
                                    w['w_out_rec'], g_mix, b_mix, alpha)
            stage_s = _out_ln_stage(att_s, rec_s, xs, w['w_out_att'],
                                    w['w_out_rec'], g_mix, b_mix, alpha)
            wm = dict(moe_experts, **_prep_router(moe_w_group[layer], moe_w_expert[layer]))
            hp, _ = _moe(stage_p, "out_ln", wm, layer, g_ffn, b_ffn, alpha)
            hs, _ = _moe(stage_s, "out_ln", wm, layer, g_ffn, b_ffn, alpha)
        else:
            w = dict(w_in=w_in_c[j].astype(BF16), d=s5_d[j][None, :],
                     w_glu=s5_w_glu[j].astype(BF16), w_out=w_out_c[j].astype(BF16),
                     tabs=_prep_s5(s5_lam_re[j], s5_lam_im[j], s5_log_step[j], s5_b_re[j],
                                   s5_b_im[j], s5_c_re[j], s5_c_im[j]))
            wm = dict(moe_experts, **_prep_router(moe_w_group[layer], moe_w_expert[layer]))
            zero_state = jnp.zeros((bp, S5_GROUPS, S5_STATE), F32)
            xp, y_p, u_p, r1, i1 = _mixer_c(hp, bp, tp, zero_state, zero_state, w)
            stage_p = _c_out_stage(y_p, u_p, xp, w, g_mix, b_mix, alpha)
            hp, xs_in = _moe(stage_p, "c_out", wm, layer, g_ffn, b_ffn, alpha,
                             overlap=_materialize(hs, "res_ln"))
            xs, y_s, u_s, r2, i2 = _mixer_c(_array_stage(xs_in), bs, ts, state_s5_re[j],
                                            state_s5_im[j], w)
            outs['s5r_p'].append(r1); outs['s5r_s'].append(r2)
            outs['s5i_p'].append(i1); outs['s5i_s'].append(i2)
            stage_s = _c_out_stage(y_s, u_s, xs, w, g_mix, b_mix, alpha)
            hs, _ = _moe(stage_s, "c_out", wm, layer, g_ffn, b_ffn, alpha)
    y_prompt = _materialize(hp, "res_ln").reshape(bp, tp, dm)
    y_sample = _materialize(hs, "res_ln").reshape(bs, ts, dm)
    st = lambda k: jnp.stack(outs[k])
    return (y_prompt, y_sample, st('ckv_p'), st('ckv_s'), st('kr_p'), st('kr_s'), st('cv_p'), st('cv_s'),
            st('lh_p'), st('lh_s'), st('s5r_p'), st('s5r_s'), st('s5i_p'), st('s5i_s'))
```

```python
import functools
import math

import jax
import jax.numpy as jnp
from jax import lax
from jax.experimental import pallas as pl
from jax.experimental.pallas import tpu as pltpu
from jax.experimental.pallas import tpu_sc as plsc

F32 = jnp.float32
BF16 = jnp.bfloat16

CHUNK = 64
MLA_HEADS = 8
QK_NOPE = 64
QK_ROPE = 32
QK_HEAD = QK_NOPE + QK_ROPE
V_HEAD = 64
Q_LORA = 256
KV_LORA = 128
ROPE_THETA = 10000.0
MLA_SCALE = QK_HEAD ** -0.5
NEG_INF = -1e30
LRU_WIDTH = 512
LRU_BLOCKS = 8
LRU_BLOCK = LRU_WIDTH // LRU_BLOCKS
CONV_WIDTH = 4
RG_C = 8.0
S5_WIDTH = 512
S5_GROUP = 16
S5_GROUPS = S5_WIDTH // S5_GROUP
S5_STATE = 64
N_GROUPS = 4
EXPERTS_PER_GROUP = 4
N_EXPERTS = N_GROUPS * EXPERTS_PER_GROUP
EXPERT_HIDDEN = 256
LN_EPS = 1e-5
RMS_EPS = 1e-6

LANES = 128
SUBLANES = 8
VMEM_LIMIT_BYTES = 48 * 1024 * 1024
MOE_VMEM_LIMIT_BYTES = 56 * 1024 * 1024

HEAD_PAD = LANES
ROW_TILE = 512
MOE_ROW_TILE = 1024
MOE_EXPERT_TILE = 512
ATTN_TQ = 1024
ATTN_TK = 512
ATTN_UNROLL = 8
Q_SCALE = MLA_SCALE * math.log2(math.e)
LRU_TILE = 512
S5_L = 8
S5_SUPER = LANES // S5_GROUP
S5_NSUPER = S5_GROUPS // S5_SUPER
S5_SUPER_IN = S5_L * LANES
S5_SUPER_STATE = S5_SUPER * S5_STATE
S5_STATE_W = S5_GROUPS * S5_STATE
S5_SCAN_LANES = 512
S5_ROW_TILE = 256
ROUTER_W = LANES


def _cparams(semantics, vmem_limit_bytes=VMEM_LIMIT_BYTES):
    return pltpu.CompilerParams(dimension_semantics=semantics,
                                vmem_limit_bytes=vmem_limit_bytes)


def _full_spec(a):
    nd = a.ndim
    return pl.BlockSpec(a.shape, lambda *_: (0,) * nd)


def _row_spec(tm, width):
    return pl.BlockSpec((tm, width), lambda i: (i, 0))


def _layer_norm(z, g, b):
    mu = jnp.mean(z, axis=-1, keepdims=True)
    zc = z - mu
    var = jnp.mean(zc * zc, axis=-1, keepdims=True)
    return zc * lax.rsqrt(var + LN_EPS) * g + b


def _rms_norm(z, g):
    ms = jnp.mean(z * z, axis=-1, keepdims=True)
    return z * lax.rsqrt(ms + RMS_EPS) * g


def _bdot(a, b):
    return jnp.dot(a.astype(BF16), b.astype(BF16), preferred_element_type=F32)


def _ab_in_kernel(x_ref, win_ref, qg_ref, kvg_ref, wuq_ref, wuk_ref, wuv_ref,
                  cosb_ref, sinb_ref, cost_ref, sint_ref,
                  q_out, k_out, v_out, c_out, kr_out, xb_out, yb_out, *, dm):
    proj = _bdot(x_ref[...], win_ref[...])
    o0 = Q_LORA
    o1 = o0 + KV_LORA
    o2 = o1 + LRU_WIDTH
    o3 = o2 + LRU_WIDTH
    q_lat = proj[:, 0:o0]
    kv_lat = proj[:, o0:o1]
    xb_out[...] = proj[:, o1:o2]
    yb_out[...] = proj[:, o2:o3]
    krp = proj[:, o3:o3 + HEAD_PAD]

    qn = _rms_norm(q_lat, qg_ref[...])
    cn = _rms_norm(kv_lat, kvg_ref[...])
    c_out[...] = cn
    q = _bdot(qn, wuq_ref[...])
    kk = _bdot(cn, wuk_ref[...])
    lane = lax.broadcasted_iota(jnp.int32, (1, MLA_HEADS * HEAD_PAD), 1)
    ones_col = (lane % HEAD_PAD == V_HEAD).astype(F32)
    v_out[...] = (_bdot(cn, wuv_ref[...]) + ones_col).astype(v_out.dtype)

    cb, sb = cosb_ref[...], sinb_ref[...]
    ct, st = cost_ref[...], sint_ref[...]
    cos = cb * ct - sb * st
    sin = sb * ct + cb * st
    half = QK_ROPE // 2
    lane1 = lax.broadcasted_iota(jnp.int32, (1, HEAD_PAD), 1)
    rc = jnp.where(lane1 < QK_HEAD, cos, 0.0)
    ra = jnp.where((lane1 >= QK_NOPE + half) & (lane1 < QK_HEAD), sin, 0.0)
    rb = jnp.where((lane1 >= QK_NOPE) & (lane1 < QK_NOPE + half), -sin, 0.0)

    def rope(z):
        return (z * rc + pltpu.roll(z, half, 1) * ra
                + pltpu.roll(z, HEAD_PAD - half, 1) * rb)

    kr = rope(krp)
    kr_out[...] = kr[:, QK_NOPE:QK_NOPE + QK_ROPE]
    for h in range(MLA_HEADS):
        sl = slice(h * HEAD_PAD, (h + 1) * HEAD_PAD)
        q_out[:, sl] = (rope(q[:, sl]) * Q_SCALE).astype(q_out.dtype)
        k_out[:, sl] = (kk[:, sl] + kr).astype(k_out.dtype)


def _ab_in(x, w, rope_tabs):
    n, dm = x.shape
    tm = min(ROW_TILE, n)
    cosb, sinb, cost, sint = rope_tabs
    consts = [w['w_in'], w['q_g'], w['kv_g'], w['w_uq'], w['w_uk'], w['w_uv'], cosb, sinb]
    tile_spec = pl.BlockSpec((None, 1, HEAD_PAD), lambda i: (i, 0, 0))
    hp = MLA_HEADS * HEAD_PAD
    out_shape = [
        jax.ShapeDtypeStruct((n, hp), BF16),
        jax.ShapeDtypeStruct((n, hp), BF16),
        jax.ShapeDtypeStruct((n, hp), BF16),
        jax.ShapeDtypeStruct((n, KV_LORA), F32),
        jax.ShapeDtypeStruct((n, QK_ROPE), F32),
        jax.ShapeDtypeStruct((n, LRU_WIDTH), F32),
        jax.ShapeDtypeStruct((n, LRU_WIDTH), F32),
    ]
    return pl.pallas_call(
        functools.partial(_ab_in_kernel, dm=dm), name="ab_in",
        grid=(n // tm,),
        in_specs=([_row_spec(tm, dm)] + [_full_spec(c) for c in consts]
                  + [tile_spec, tile_spec]),
        out_specs=[_row_spec(tm, s.shape[1]) for s in out_shape],
        out_shape=out_shape,
        compiler_params=_cparams(("parallel",)),
    )(x, *consts, cost, sint)


def _attn_prompt_kernel(q_ref, k_ref, v_ref, o_ref):
    i = pl.program_id(1)
    tq, tk = ATTN_TQ, ATTN_TK
    ndiag = tq // tk
    nfull = i * ndiag
    row_chunk = lax.broadcasted_iota(jnp.int32, (tq, tk), 0) // CHUNK
    col_chunk = lax.broadcasted_iota(jnp.int32, (tq, tk), 1) // CHUNK
    qs = [q_ref[:, hh * HEAD_PAD:(hh + 1) * HEAD_PAD] for hh in range(2)]

    def update(hh, j, m, acc, mask, row0=0):
        start = pl.multiple_of(j * tk, tk)
        k = k_ref[pl.ds(start, tk), hh * HEAD_PAD:(hh + 1) * HEAD_PAD]
        v = v_ref[pl.ds(start, tk), hh * HEAD_PAD:(hh + 1) * HEAD_PAD]
        s = lax.dot_general(qs[hh][row0:], k, (((1,), (1,)), ((), ())),
                            preferred_element_type=F32)
        if mask is not None:
            s = jnp.where(mask[row0:], s, NEG_INF)
        m_new = jnp.maximum(m, jnp.max(s, axis=-1, keepdims=True))
        alpha = jnp.exp2(m - m_new)
        p = jnp.exp2((s - m_new).astype(BF16))
        return m_new, alpha * acc + jnp.dot(p, v, preferred_element_type=F32)

    def body(j, carry):
        new = []
        for hh in range(2):
            new += update(hh, j, carry[2 * hh], carry[2 * hh + 1], None)
        return tuple(new)

    unroll = ATTN_UNROLL

    def body_unrolled(jj, carry):
        for b in range(unroll):
            carry = body(unroll * jj + b, carry)
        return carry

    init = []
    for _ in range(2):
        init += [jnp.full((tq, 1), NEG_INF, F32), jnp.zeros((tq, HEAD_PAD), F32)]
    carry = lax.fori_loop(0, nfull // unroll, body_unrolled, tuple(init))
    rem0 = nfull - nfull % unroll

    def body_rem(jj, carry):
        for b in range(ndiag):
            carry = body(rem0 + ndiag * jj + b, carry)
        return carry

    carry = list(lax.fori_loop(0, (nfull % unroll) // ndiag, body_rem, carry))
    for d in range(ndiag):
        visible = col_chunk + d * (tk // CHUNK) <= row_chunk
        r0 = d * tk
        for hh in range(2):
            m, acc = carry[2 * hh], carry[2 * hh + 1]
            m_new, acc_new = update(hh, nfull + d, m[r0:], acc[r0:], visible, r0)
            if r0:
                m_new = jnp.concatenate([m[:r0], m_new], axis=0)
                acc_new = jnp.concatenate([acc[:r0], acc_new], axis=0)
            carry[2 * hh], carry[2 * hh + 1] = m_new, acc_new
    outs = [carry[2 * hh + 1][:, :V_HEAD] / carry[2 * hh + 1][:, V_HEAD:V_HEAD + 1]
            for hh in range(2)]
    o_ref[...] = jnp.concatenate(outs, axis=-1).astype(o_ref.dtype)


def _attn_prompt(q, k, v):
    t = q.shape[0]
    pairs = MLA_HEADS // 2
    return pl.pallas_call(
        _attn_prompt_kernel, name="attn_prompt",
        grid=(pairs, t // ATTN_TQ),
        in_specs=[
            pl.BlockSpec((ATTN_TQ, 2 * HEAD_PAD), lambda p, i: (i, p)),
            pl.BlockSpec((t, 2 * HEAD_PAD), lambda p, i: (0, p)),
            pl.BlockSpec((t, 2 * HEAD_PAD), lambda p, i: (0, p)),
        ],
        out_specs=pl.BlockSpec((ATTN_TQ, 2 * V_HEAD), lambda p, i: (i, p)),
        out_shape=jax.ShapeDtypeStruct((t, MLA_HEADS * V_HEAD), BF16),
        compiler_params=_cparams(("parallel", "parallel")),
    )(q, k, v)


def _attn_sample_kernel(q_ref, cn_ref, krn_ref, cp_ref, krp_ref, wukt_ref, wuv_ref, o_ref):
    cp = cp_ref[...].astype(BF16)
    krp = krp_ref[...].astype(BF16)
    cn = cn_ref[...].astype(BF16)
    krn = krn_ref[...].astype(BF16)
    ts = cn.shape[0]
    dn = (((1,), (1,)), ((), ()))
    outs = []
    for hp in range(MLA_HEADS // 2):
        qa, qr = [], []
        for hh in range(2):
            h = 2 * hp + hh
            qh = q_ref[:, h * HEAD_PAD:(h + 1) * HEAD_PAD]
            qa.append(jnp.dot(qh[:, :QK_NOPE], wukt_ref[h], preferred_element_type=F32))
            qr.append(qh[:, QK_NOPE:QK_NOPE + QK_ROPE])
        qa = jnp.concatenate(qa, axis=0).astype(BF16)
        qr = jnp.concatenate(qr, axis=0)
        s_past = (lax.dot_general(qa, cp, dn, preferred_element_type=F32)
                  + lax.dot_general(qr, krp, dn, preferred_element_type=F32))
        s_new = (lax.dot_general(qa, cn, dn, preferred_element_type=F32)
                 + lax.dot_general(qr, krn, dn, preferred_element_type=F32))
        m = jnp.maximum(jnp.max(s_past, axis=-1, keepdims=True),
                        jnp.max(s_new, axis=-1, keepdims=True))
        p_past = jnp.exp2(s_past - m)
        p_new = jnp.exp2(s_new - m)
        l = (jnp.sum(p_past, axis=-1, keepdims=True)
             + jnp.sum(p_new, axis=-1, keepdims=True))
        o_lat = (jnp.dot(p_past.astype(BF16), cp, preferred_element_type=F32)
                 + jnp.dot(p_new.astype(BF16), cn, preferred_element_type=F32)) / l
        for hh in range(2):
            h = 2 * hp + hh
            outs.append(_bdot(o_lat[hh * ts:(hh + 1) * ts], wuv_ref[h]))
    o_ref[...] = jnp.concatenate(outs, axis=-1).astype(o_ref.dtype)


def _attn_sample(q, c_new, kr_new, c_past, kr_past, wukt, wuv):
    bs, past, _ = c_past.shape
    ts = q.shape[0] // bs
    return pl.pallas_call(
        _attn_sample_kernel, name="attn_sample",
        grid=(bs,),
        in_specs=[
            pl.BlockSpec((ts, MLA_HEADS * HEAD_PAD), lambda b: (b, 0)),
            pl.BlockSpec((ts, KV_LORA), lambda b: (b, 0)),
            pl.BlockSpec((ts, QK_ROPE), lambda b: (b, 0)),
            pl.BlockSpec((None, past, KV_LORA), lambda b: (b, 0, 0)),
            pl.BlockSpec((None, past, QK_ROPE), lambda b: (b, 0, 0)),
            _full_spec(wukt), _full_spec(wuv),
        ],
        out_specs=pl.BlockSpec((ts, MLA_HEADS * V_HEAD), lambda b: (b, 0)),
        out_shape=jax.ShapeDtypeStruct((bs * ts, MLA_HEADS * V_HEAD), BF16),
        compiler_params=_cparams(("parallel",)),
    )(q, c_new, kr_new, c_past, kr_past, wukt, wuv)


def _lru_kernel(xb_ref, yb_ref, conv0_ref, h0_ref, cw_ref, cb_ref, wax_ref, bax_ref, ap_ref,
                rec_ref, conv_out_ref, h_out_ref, xcat, hc, *, tb):
    t = pl.program_id(1)
    nt = pl.num_programs(1)
    tail = CONV_WIDTH - 1
    base = SUBLANES

    @pl.when(t == 0)
    def _():
        xcat[base - tail:base, :] = conv0_ref[...]
        hc[...] = h0_ref[...]

    @pl.when(t > 0)
    def _():
        xcat[0:base, :] = xcat[tb:tb + base, :]

    xcat[base:base + tb, :] = xb_ref[...]
    xc = cb_ref[...]
    for tap in range(CONV_WIDTH):
        xc = xc + xcat[base - tail + tap:base - tail + tap + tb, :] * cw_ref[tap:tap + 1, :]

    gates = _bdot(xc, wax_ref[...]) + bax_ref[...]
    r = jax.nn.sigmoid(gates[:, :LRU_WIDTH])
    ig = jax.nn.sigmoid(gates[:, LRU_WIDTH:])
    log_a = -RG_C * r * ap_ref[...]
    a = jnp.exp(log_a)
    u = jnp.sqrt(jnp.tanh(-log_a) * (1.0 + a * a)) * (ig * xc)

    row8 = lax.broadcasted_iota(jnp.int32, (tb, LRU_WIDTH), 0) % SUBLANES
    s = 1
    while s < SUBLANES:
        keep = row8 >= s
        u = jnp.where(keep, a * pltpu.roll(u, s, 0) + u, u)
        a = jnp.where(keep, a * pltpu.roll(a, s, 0), a)
        s *= 2
    hprev = hc[...]
    groups = []
    for g in range(tb // SUBLANES):
        sl = slice(g * SUBLANES, (g + 1) * SUBLANES)
        hg = a[sl] * hprev + u[sl]
        groups.append(hg)
        hprev = hg[SUBLANES - 1:SUBLANES, :]
    h = jnp.concatenate(groups, axis=0)
    hc[...] = hprev
    rec_ref[...] = (h * jax.nn.gelu(yb_ref[...])).astype(rec_ref.dtype)

    @pl.when(t == nt - 1)
    def _():
        conv_out_ref[...] = xcat[base + tb - tail:base + tb, :]
        h_out_ref[...] = h[tb - 1:tb, :]


def _lru(xb, yb, conv0, h0, w):
    b, t, wd = xb.shape
    tb = min(LRU_TILE, t)
    tail = CONV_WIDTH - 1
    consts = [w['conv_w'], w['conv_b'], w['w_ax'], w['b_ax'], w['a_param']]
    seq = pl.BlockSpec((None, tb, wd), lambda bi, ti: (bi, ti, 0))
    return pl.pallas_call(
        functools.partial(_lru_kernel, tb=tb), name="rg_lru",
        grid=(b, t // tb),
        in_specs=[seq, seq,
                  pl.BlockSpec((None, tail, wd), lambda bi, ti: (bi, 0, 0)),
                  pl.BlockSpec((None, 1, wd), lambda bi, ti: (bi, 0, 0))]
                 + [pl.BlockSpec(c.shape, lambda bi, ti: (0, 0)) for c in consts],
        out_specs=[seq,
                   pl.BlockSpec((None, tail, wd), lambda bi, ti: (bi, 0, 0)),
                   pl.BlockSpec((None, 1, wd), lambda bi, ti: (bi, 0, 0))],
        out_shape=[jax.ShapeDtypeStruct((b, t, wd), BF16),
                   jax.ShapeDtypeStruct((b, tail, wd), F32),
                   jax.ShapeDtypeStruct((b, 1, wd), F32)],
        scratch_shapes=[pltpu.VMEM((tb + 2 * SUBLANES, wd), F32), pltpu.VMEM((1, wd), F32)],
        compiler_params=_cparams(("parallel", "arbitrary")),
    )(xb, yb, conv0, h0, *consts)


def _out_ln_value(a_ref, b_ref, x_ref, wa_ref, wb_ref, g_ref, beta_ref, *, alpha):
    y = (jnp.dot(a_ref[...], wa_ref[...], preferred_element_type=F32)
         + jnp.dot(b_ref[...], wb_ref[...], preferred_element_type=F32))
    return _layer_norm(alpha * x_ref[...] + y, g_ref[...], beta_ref[...])


def _out_ln_stage(att, rec, x, wa, wb, g, beta, alpha):
    n, dm = x.shape
    tm = min(ROW_TILE, n)
    return (functools.partial(_out_ln_value, alpha=alpha), [(att, tm), (rec, tm), (x, tm)],
            [wa, wb, g, beta], [], n, dm)


def _stage_kernel(*refs, n_in, value_fn):
    refs[n_in][...] = value_fn(*refs[:n_in], *refs[n_in + 1:])


def _stage_call(stage, name):
    value_fn, row_inputs, consts, scratch, n, dm = stage
    tm = min(ROW_TILE, n)
    arrays = [a for a, _ in row_inputs] + list(consts)
    return pl.pallas_call(
        functools.partial(_stage_kernel, n_in=len(arrays), value_fn=value_fn), name=name,
        grid=(n // tm,),
        in_specs=[_row_spec(rows, a.shape[1]) for a, rows in row_inputs]
                 + [_full_spec(c) for c in consts],
        out_specs=_row_spec(tm, dm),
        out_shape=jax.ShapeDtypeStruct((n, dm), F32),
        scratch_shapes=list(scratch),
        compiler_params=_cparams(("parallel",)),
    )(*arrays)


def _route(x, xh, wr_ref, lane):
    xl = (x - xh.astype(F32)).astype(BF16)
    hh_hl = jnp.dot(xh, wr_ref[...], preferred_element_type=F32)
    logits = (hh_hl[:, :ROUTER_W] + hh_hl[:, ROUTER_W:]
              + jnp.dot(xl, wr_ref[:, :ROUTER_W], preferred_element_type=F32))
    big = float(ROUTER_W)
    lg = jnp.where(lane < N_GROUPS, logits, -jnp.inf)
    mg = jnp.max(lg, axis=-1, keepdims=True)
    gi = jnp.min(jnp.where(lg == mg, lane, big), axis=-1, keepdims=True)
    p_top = 1.0 / jnp.sum(jnp.exp(lg - mg), axis=-1, keepdims=True)
    lo = N_GROUPS + gi * EXPERTS_PER_GROUP
    le = jnp.where((lane >= lo) & (lane < lo + EXPERTS_PER_GROUP), logits, -jnp.inf)
    v1 = jnp.max(le, axis=-1, keepdims=True)
    i1 = jnp.min(jnp.where(le == v1, lane, big), axis=-1, keepdims=True)
    le2 = jnp.where(lane == i1, -jnp.inf, le)
    v2 = jnp.max(le2, axis=-1, keepdims=True)
    i2 = jnp.min(jnp.where(le2 == v2, lane, big), axis=-1, keepdims=True)
    e2 = jnp.exp(v2 - v1)
    return gi, i1, i2, p_top / (1.0 + e2), p_top * e2 / (1.0 + e2)


def _moe_ln_kernel(x_ref, wr_ref, wg_ref, wu_ref, wd_ref, g_ref, beta_ref, o_ref,
                   acc, gates, xb16, *, alpha):
    grp = pl.program_id(1)
    tm = x_ref.shape[0]
    lane = lax.broadcasted_iota(jnp.int32, (tm, ROUTER_W), 1).astype(F32)

    @pl.when(grp == 0)
    def _():
        x = x_ref[...]
        xh = x.astype(BF16)
        xb16[...] = xh
        _, i1, i2, w1, w2 = _route(x, xh, wr_ref, lane)
        gates[...] = jnp.where(lane == i1, w1, 0.0) + jnp.where(lane == i2, w2, 0.0)
        acc[...] = jnp.zeros_like(acc)

    xb = xb16[...]
    gt = gates[...]
    first = (N_GROUPS + grp * EXPERTS_PER_GROUP).astype(F32)
    for el in range(EXPERTS_PER_GROUP):
        ge = jnp.sum(jnp.where(lane == first + el, gt, 0.0), axis=-1, keepdims=True)
        h = jnp.dot(xb, wg_ref[el], preferred_element_type=F32)
        u = jnp.dot(xb, wu_ref[el], preferred_element_type=F32)
        act = (jax.nn.silu(h) * u * ge).astype(BF16)
        acc[...] += jnp.dot(act, wd_ref[el], preferred_element_type=F32)

    @pl.when(grp == pl.num_programs(1) - 1)
    def _():
        o_ref[...] = _layer_norm(alpha * x_ref[...] + acc[...], g_ref[...], beta_ref[...])


def _moe_ln(x, w, layer, g, beta, alpha):
    n, dm = x.shape
    tm = min(MOE_ROW_TILE, n)
    _, ne, _, eh = w['w_gate'].shape
    epg = EXPERTS_PER_GROUP
    return pl.pallas_call(
        functools.partial(_moe_ln_kernel, alpha=alpha), name="moe_ln",
        grid=(n // tm, ne // epg),
        in_specs=[
            pl.BlockSpec((tm, dm), lambda i, e: (i, 0)),
            pl.BlockSpec(w['w_router'].shape, lambda i, e: (0, 0)),
            pl.BlockSpec((None, epg, dm, eh), lambda i, e: (layer, e, 0, 0)),
            pl.BlockSpec((None, epg, dm, eh), lambda i, e: (layer, e, 0, 0)),
            pl.BlockSpec((None, epg, eh, dm), lambda i, e: (layer, e, 0, 0)),
            pl.BlockSpec(g.shape, lambda i, e: (0, 0)),
            pl.BlockSpec(beta.shape, lambda i, e: (0, 0)),
        ],
        out_specs=pl.BlockSpec((tm, dm), lambda i, e: (i, 0)),
        out_shape=jax.ShapeDtypeStruct((n, dm), F32),
        scratch_shapes=[pltpu.VMEM((tm, dm), F32), pltpu.VMEM((tm, ROUTER_W), F32),
                        pltpu.VMEM((tm, dm), BF16)],
        compiler_params=_cparams(("parallel", "arbitrary"), MOE_VMEM_LIMIT_BYTES),
    )(x, w['w_router'], w['w_gate'], w['w_up'], w['w_down'], g, beta)


META_W = LANES
META_GROUP = EXPERTS_PER_GROUP
META_RANK = EXPERTS_PER_GROUP + 1


def _moe_route_kernel(*refs, n_in, n_tiles, value_fn):
    in_refs = refs[:n_in]
    wr_ref, tri_ref, xg_ref, meta_ref, cnt_ref, run = refs[n_in:n_in + 6]
    tm, dm = xg_ref.shape[0], xg_ref.shape[1] - META_W
    i = pl.program_id(0)
    lane = lax.broadcasted_iota(jnp.int32, (tm, ROUTER_W), 1).astype(F32)

    @pl.when(i == 0)
    def _():
        run[...] = jnp.zeros_like(run)

    @pl.when(i < n_tiles)
    def _():
        x = value_fn(*in_refs, *refs[n_in + 6:])
        gi, i1, i2, w1, w2 = _route(x, x.astype(BF16), wr_ref, lane)
        lo = N_GROUPS + gi * EXPERTS_PER_GROUP
        in_group = lane == gi
        incl = jnp.dot(tri_ref[...], in_group.astype(BF16), preferred_element_type=F32)
        rank = jnp.sum(jnp.where(in_group, incl + run[...], 0.0), axis=-1,
                       keepdims=True) - 1.0
        run[...] += incl[tm - 1:tm, :]
        cnt_ref[...] = run[...]
        meta = (jnp.where(lane == i1 - lo, w1, 0.0) + jnp.where(lane == i2 - lo, w2, 0.0)
                + jnp.where(lane == META_GROUP, gi, 0.0)
                + jnp.where(lane == META_RANK, rank, 0.0))
        xg_ref[:, :dm] = x
        xg_ref[:, dm:] = meta
        meta_ref[...] = meta

    @pl.when(i >= n_tiles)
    def _():
        xg_ref[...] = jnp.zeros_like(xg_ref)


def _moe_route(stage, w_router, p):
    value_fn, row_inputs, consts, scratch, n, dm = stage
    tm = ROW_TILE
    n_tiles = n // tm
    tri = jnp.tril(jnp.ones((tm, tm), BF16))
    last = lambda i: (jnp.minimum(i, n_tiles - 1), 0)
    arrays = [a for a, _ in row_inputs] + list(consts)
    in_specs = ([pl.BlockSpec((rows, a.shape[1]), last) for a, rows in row_inputs]
                + [_full_spec(c) for c in consts] + [_full_spec(w_router), _full_spec(tri)])
    return pl.pallas_call(
        functools.partial(_moe_route_kernel, n_in=len(arrays), n_tiles=n_tiles,
                          value_fn=value_fn), name="moe_route",
        grid=(p // tm,),
        in_specs=in_specs,
        out_specs=[_row_spec(tm, dm + META_W), pl.BlockSpec((tm, META_W), last),
                   pl.BlockSpec((1, ROUTER_W), lambda i: (0, 0))],
        out_shape=[jax.ShapeDtypeStruct((p, dm + META_W), F32),
                   jax.ShapeDtypeStruct((n, META_W), F32),
                   jax.ShapeDtypeStruct((1, ROUTER_W), F32)],
        scratch_shapes=[pltpu.VMEM((1, ROUTER_W), F32)] + list(scratch),
        compiler_params=_cparams(("arbitrary",)),
    )(*arrays, w_router, tri)


def _moe_experts_kernel(tg_ref, nu_ref, xs_ref, wg_ref, wu_ref, wd_ref, y_ref):
    del tg_ref
    dm = y_ref.shape[1]

    @pl.when(pl.program_id(0) < nu_ref[0])
    def _():
        xb = xs_ref[:, :dm].astype(BF16)
        meta = xs_ref[:, dm:]
        acc = None
        for el in range(EXPERTS_PER_GROUP):
            h = jnp.dot(xb, wg_ref[el], preferred_element_type=F32)
            u = jnp.dot(xb, wu_ref[el], preferred_element_type=F32)
            act = (jax.nn.silu(h) * u * meta[:, el:el + 1]).astype(BF16)
            d = jnp.dot(act, wd_ref[el], preferred_element_type=F32)
            acc = d if acc is None else acc + d
        y_ref[...] = acc


def _moe_experts(xs, tile_group, n_used, w, layer):
    p, wx = xs.shape
    dm = wx - META_W
    te = MOE_EXPERT_TILE
    _, ne, _, eh = w['w_gate'].shape
    epg = EXPERTS_PER_GROUP

    def tile(i, tg, nu):
        return jnp.minimum(i, nu[0] - 1)

    wspec = lambda shape: pl.BlockSpec(
        (None, epg) + shape, lambda i, tg, nu: (layer, tg[tile(i, tg, nu)], 0, 0))
    grid_spec = pltpu.PrefetchScalarGridSpec(
        num_scalar_prefetch=2,
        grid=(p // te,),
        in_specs=[pl.BlockSpec((te, wx), lambda i, tg, nu: (tile(i, tg, nu), 0)),
                  wspec((dm, eh)), wspec((dm, eh)), wspec((eh, dm))],
        out_specs=pl.BlockSpec((te, dm), lambda i, tg, nu: (tile(i, tg, nu), 0)),
    )
    return pl.pallas_call(
        _moe_experts_kernel, name="moe_experts",
        grid_spec=grid_spec,
        out_shape=jax.ShapeDtypeStruct((p, dm), F32),
        compiler_params=_cparams(("arbitrary",)),
    )(tile_group, n_used, xs, w['w_gate'], w['w_up'], w['w_down'])


def _res_ln_value(xg_ref, y_ref, g_ref, beta_ref, *, alpha):
    dm = y_ref.shape[1]
    return _layer_norm(alpha * xg_ref[:, :dm] + y_ref[...], g_ref[...], beta_ref[...])


def _res_ln_stage(xg, y, g, beta, alpha):
    n, dm = y.shape
    return (functools.partial(_res_ln_value, alpha=alpha), [(xg, ROW_TILE), (y, ROW_TILE)],
            [g, beta], [], n, dm)


def _identity_value(x_ref):
    return x_ref[...]


def _array_stage(x):
    n, dm = x.shape
    return (_identity_value, [(x, min(ROW_TILE, n))], [], [], n, dm)


def _materialize(stage, name):
    if stage[0] is _identity_value:
        return stage[1][0][0]
    return _stage_call(stage, name)


def _moe(stage, stage_name, w, layer, g, beta, alpha, overlap=None):
    n = stage[4]
    granule = 2 * SC_GATHER_ROWS * SC_CORES * SC_SUBCORES
    if n % granule == 0 and (N_GROUPS * MOE_EXPERT_TILE) % granule == 0:
        return _moe_routed_ln(stage, w, layer, g, beta, alpha, overlap)
    out = _array_stage(_moe_ln(_stage_call(stage, stage_name), w, layer, g, beta, alpha))
    return out, overlap


def _moe_routed_ln(stage, w, layer, g, beta, alpha, overlap):
    n, dm = stage[4], stage[5]
    te = MOE_EXPERT_TILE
    npad = N_GROUPS * te
    p = n + npad
    xg, meta, cnt = _moe_route(stage, w['w_router'], p)
    gid = meta[:, META_GROUP].astype(jnp.int32)
    rank = meta[:, META_RANK].astype(jnp.int32)
    counts = cnt[0, :N_GROUPS].astype(jnp.int32)
    padded = (counts + te - 1) // te * te
    ends = jnp.cumsum(padded)
    starts = ends - padded

    def lookup(table, idx):
        hit = idx[:, None] == jnp.arange(table.shape[0], dtype=jnp.int32)[None, :]
        return jnp.sum(jnp.where(hit, table[None, :], 0), axis=1)

    pos = lookup(starts, gid) + rank
    gap_len = jnp.concatenate([padded - counts, p - ends[-1:]])
    gap_start = jnp.concatenate([starts + counts, ends[-1:]])
    gap_end = jnp.cumsum(gap_len)
    k = jnp.arange(npad, dtype=jnp.int32)
    seg = jnp.sum(k[:, None] >= gap_end[None, :], axis=1)
    filler_pos = lookup(gap_start, seg) + k - lookup(gap_end - gap_len, seg)
    pos_all = jnp.concatenate([pos, filler_pos]).astype(jnp.int32)
    tile_start = jnp.arange(p // te, dtype=jnp.int32) * te
    tile_group = jnp.minimum(jnp.sum(tile_start[:, None] >= ends[None, :], axis=1),
                             N_GROUPS - 1).astype(jnp.int32)
    n_used = (ends[-1:] // te).astype(jnp.int32)
    xs = _sc_scatter_rows(xg, pos_all)
    if overlap is not None:
        xs, overlap = lax.optimization_barrier((xs, overlap))
    ys = _moe_experts(xs, tile_group, n_used, w, layer)
    y = _sc_gather_rows(ys, pos)
    return _res_ln_stage(xg, y, g, beta, alpha), overlap


SC_CORES = 2
SC_SUBCORES = 16
SC_GATHER_ROWS = 32


def _sc_gather_rows(table, idx):
    b = idx.shape[0]
    _, d = table.shape
    nw = SC_CORES * SC_SUBCORES
    ch = SC_GATHER_ROWS
    per_w = b // nw
    assert b % (nw * 2 * ch) == 0
    npair = per_w // (2 * ch)
    mesh = plsc.VectorSubcoreMesh(core_axis_name="c", subcore_axis_name="s")

    @functools.partial(
        pl.kernel, mesh=mesh, out_type=jax.ShapeDtypeStruct((b, d), table.dtype),
        scratch_types=[pltpu.VMEM((2, ch), jnp.int32), pltpu.VMEM((2, ch, d), table.dtype),
                       pltpu.SemaphoreType.DMA, pltpu.SemaphoreType.DMA],
        name="sc_gather_rows")
    def gather(table_hbm, idx_hbm, out_hbm, idx_v, rows_v, sem0, sem1):
        sems = (sem0, sem1)
        wid = lax.axis_index("s") * SC_CORES + lax.axis_index("c")
        base = wid * per_w

        def start(chunk, slot):
            off = pl.multiple_of(base + chunk * ch, SUBLANES)
            pltpu.sync_copy(idx_hbm.at[pl.ds(off, ch)], idx_v.at[slot])
            pltpu.async_copy(table_hbm.at[idx_v.at[slot]], rows_v.at[slot], sems[slot])

        def finish(chunk, slot):
            off = pl.multiple_of(base + chunk * ch, SUBLANES)
            pltpu.make_async_copy(table_hbm.at[idx_v.at[slot]], rows_v.at[slot],
                                  sems[slot]).wait()
            pltpu.sync_copy(rows_v.at[slot], out_hbm.at[pl.ds(off, ch)])

        start(0, 0)

        @pl.loop(0, npair)
        def _(p):
            c0 = 2 * p
            start(c0 + 1, 1)
            finish(c0, 0)

            @pl.when(p + 1 < npair)
            def _():
                start(c0 + 2, 0)
            finish(c0 + 1, 1)

    return gather(table, idx)


def _sc_scatter_rows(rows, idx):
    b, d = rows.shape
    nw = SC_CORES * SC_SUBCORES
    ch = SC_GATHER_ROWS
    per_w = b // nw
    assert idx.shape == (b,) and b % (nw * 2 * ch) == 0
    npair = per_w // (2 * ch)
    mesh = plsc.VectorSubcoreMesh(core_axis_name="c", subcore_axis_name="s")

    @functools.partial(
        pl.kernel, mesh=mesh, out_type=jax.ShapeDtypeStruct((b, d), rows.dtype),
        scratch_types=[pltpu.VMEM((2, ch), jnp.int32), pltpu.VMEM((2, ch, d), rows.dtype),
                       pltpu.SemaphoreType.DMA, pltpu.SemaphoreType.DMA],
        name="sc_scatter_rows")
    def scatter(rows_hbm, idx_hbm, out_hbm, idx_v, rows_v, sem0, sem1):
        sems = (sem0, sem1)
        wid = lax.axis_index("s") * SC_CORES + lax.axis_index("c")
        base = wid * per_w

        def write(slot):
            return pltpu.make_async_copy(rows_v.at[slot], out_hbm.at[idx_v.at[slot]],
                                         sems[slot])

        @pl.loop(0, npair)
        def _(p):
            for slot in range(2):
                @pl.when(p > 0)
                def _():
                    write(slot).wait()
                off = pl.multiple_of(base + (2 * p + slot) * ch, SUBLANES)
                pltpu.sync_copy(idx_hbm.at[pl.ds(off, ch)], idx_v.at[slot])
                pltpu.sync_copy(rows_hbm.at[pl.ds(off, ch)], rows_v.at[slot])
                write(slot).start()

        write(0).wait()
        write(1).wait()

    return scatter(rows, idx)


def _c_in_kernel(*refs, n_in, value_fn):
    w_ref, x_out, u_ref, u2_ref, us = refs[n_in:n_in + 5]
    x = value_fn(*refs[:n_in], *refs[n_in + 5:])
    x_out[...] = x
    u = _bdot(x, w_ref[...])
    u_ref[...] = u
    rows = u2_ref.shape[0]
    for c in range(S5_WIDTH // LANES):
        us[c] = u[:, c * LANES:(c + 1) * LANES]
    for s in range(S5_L):
        for c in range(S5_WIDTH // LANES):
            lo = s * S5_WIDTH + c * LANES
            u2_ref[:, lo:lo + LANES] = us[c, pl.ds(s, rows, stride=S5_L), :].astype(u2_ref.dtype)


def _c_in(stage, w_in):
    value_fn, row_inputs, consts, scratch, n, dm = stage
    tm = min(ROW_TILE, n)
    wu = w_in.shape[1]
    arrays = [a for a, _ in row_inputs] + list(consts)
    return pl.pallas_call(
        functools.partial(_c_in_kernel, n_in=len(arrays), value_fn=value_fn), name="c_in",
        grid=(n // tm,),
        in_specs=[_row_spec(rows, a.shape[1]) for a, rows in row_inputs]
                 + [_full_spec(c) for c in consts] + [_full_spec(w_in)],
        out_specs=[_row_spec(tm, dm), _row_spec(tm, wu), _row_spec(tm // S5_L, S5_L * wu)],
        out_shape=[jax.ShapeDtypeStruct((n, dm), F32),
                   jax.ShapeDtypeStruct((n, wu), F32),
                   jax.ShapeDtypeStruct((n // S5_L, S5_L * wu), BF16)],
        scratch_shapes=[pltpu.VMEM((wu // LANES, tm, LANES), F32)] + list(scratch),
        compiler_params=_cparams(("parallel",)),
    )(*arrays, w_in)


def _s5_super_rows(u_ref, q):
    tiles = [u_ref[:, s * S5_WIDTH + q * LANES:s * S5_WIDTH + (q + 1) * LANES]
             for s in range(S5_L)]
    return jnp.concatenate(tiles, axis=1).astype(BF16)


def _s5_e_kernel(u_ref, ere_ref, eim_ref, ere_out, eim_out):
    for q in range(S5_NSUPER):
        uq = _s5_super_rows(u_ref, q)
        sl = slice(q * S5_SUPER_STATE, (q + 1) * S5_SUPER_STATE)
        ere_out[:, sl] = jnp.dot(uq, ere_ref[q], preferred_element_type=F32)
        eim_out[:, sl] = jnp.dot(uq, eim_ref[q], preferred_element_type=F32)


def _const_spec(a):
    nd = a.ndim
    return pl.BlockSpec(a.shape, lambda *_: (0,) * nd, pipeline_mode=pl.Buffered(1))


def _s5_e(u2, tabs):
    rows, wu = u2.shape
    tr = min(S5_ROW_TILE, rows)
    out = jax.ShapeDtypeStruct((rows, S5_STATE_W), F32)
    return pl.pallas_call(
        _s5_e_kernel, name="s5_e",
        grid=(rows // tr,),
        in_specs=[_row_spec(tr, wu), _const_spec(tabs['e_re']), _const_spec(tabs['e_im'])],
        out_specs=[_row_spec(tr, S5_STATE_W)] * 2,
        out_shape=[out, out],
        compiler_params=_cparams(("parallel",)),
    )(u2, tabs['e_re'], tabs['e_im'])


def _s5_scan_kernel(ere_ref, eim_ref, h0r_ref, h0i_ref, stp_re_ref, stp_im_ref,
                    apw_re_ref, apw_im_ref, hpr_ref, hpi_ref, hfr_ref, hfi_ref, *, nc, last_row):
    w = ere_ref.shape[-1]
    row = lax.broadcasted_iota(jnp.int32, (SUBLANES, w), 0)
    apr = apw_re_ref[...]
    api = apw_im_ref[...]

    def tile(i, carry):
        cr, ci = carry
        start = pl.multiple_of(i * SUBLANES, SUBLANES)
        sr = ere_ref[pl.ds(start, SUBLANES), :]
        si = eim_ref[pl.ds(start, SUBLANES), :]
        for k in range(3):
            s = 1 << k
            keep = row >= s
            ar = stp_re_ref[k:k + 1, :]
            ai = stp_im_ref[k:k + 1, :]
            pr = pltpu.roll(sr, s, 0)
            pi = pltpu.roll(si, s, 0)
            sr, si = (jnp.where(keep, sr + ar * pr - ai * pi, sr),
                      jnp.where(keep, si + ar * pi + ai * pr, si))
        hr = sr + apr * cr - api * ci
        hi = si + apr * ci + api * cr
        first = row == 0
        hpr_ref[pl.ds(start, SUBLANES), :] = jnp.where(first, cr, pltpu.roll(hr, 1, 0))
        hpi_ref[pl.ds(start, SUBLANES), :] = jnp.where(first, ci, pltpu.roll(hi, 1, 0))
        return hr, hi

    def body(i, carry):
        hr, hi = tile(i, carry)
        return hr[SUBLANES - 1:SUBLANES, :], hi[SUBLANES - 1:SUBLANES, :]

    ntile = nc // SUBLANES
    carry = lax.fori_loop(0, ntile - 1, body, (h0r_ref[...], h0i_ref[...]))
    hr, hi = tile(ntile - 1, carry)
    hfr_ref[...] = hr[last_row:last_row + 1, :]
    hfi_ref[...] = hi[last_row:last_row + 1, :]


def _s5_scan(e_re, e_im, h0r, h0i, tabs, n_chunks):
    b, nc, w = e_re.shape
    last_row = (n_chunks - 1) % SUBLANES
    wb = S5_SCAN_LANES
    consts = [tabs['step_re'], tabs['step_im'], tabs['apow_re'], tabs['apow_im']]
    seq = pl.BlockSpec((None, nc, wb), lambda bi, li: (bi, 0, li))
    vec = pl.BlockSpec((None, 1, wb), lambda bi, li: (bi, 0, li))
    return pl.pallas_call(
        functools.partial(_s5_scan_kernel, nc=nc, last_row=last_row), name="s5_scan",
        grid=(b, w // wb),
        in_specs=[seq, seq, vec, vec]
                 + [pl.BlockSpec((c.shape[0], wb), lambda bi, li: (0, li)) for c in consts],
        out_specs=[seq, seq, vec, vec],
        out_shape=[jax.ShapeDtypeStruct((b, nc, w), F32)] * 2
                  + [jax.ShapeDtypeStruct((b, 1, w), F32)] * 2,
        compiler_params=_cparams(("parallel", "parallel")),
    )(e_re, e_im, h0r, h0i, *consts)


def _s5_y_kernel(u_ref, hpr_ref, hpi_ref, m_ref, fre_ref, fim_ref, y_ref):
    for q in range(S5_NSUPER):
        uq = _s5_super_rows(u_ref, q)
        sl = slice(q * S5_SUPER_STATE, (q + 1) * S5_SUPER_STATE)
        yq = (jnp.dot(uq, m_ref[q], preferred_element_type=F32)
              + _bdot(hpr_ref[:, sl], fre_ref[q])
              + _bdot(hpi_ref[:, sl], fim_ref[q]))
        for t in range(S5_L):
            y_ref[:, t * S5_WIDTH + q * LANES:t * S5_WIDTH + (q + 1) * LANES] = (
                yq[:, t * LANES:(t + 1) * LANES])


def _s5_y(u2, hp_re, hp_im, tabs):
    rows, wu = u2.shape
    tr = min(S5_ROW_TILE, rows)
    consts = [tabs['m'], tabs['f_re'], tabs['f_im']]
    return pl.pallas_call(
        _s5_y_kernel, name="s5_y",
        grid=(rows // tr,),
        in_specs=[_row_spec(tr, wu), _row_spec(tr, S5_STATE_W), _row_spec(tr, S5_STATE_W)]
                 + [_const_spec(c) for c in consts],
        out_specs=_row_spec(tr, wu),
        out_shape=jax.ShapeDtypeStruct((rows, wu), F32),
        compiler_params=_cparams(("parallel",)),
    )(u2, hp_re, hp_im, *consts)


def _c_out_value(y2_ref, u_ref, x_ref, d_ref, wglu_ref, wout_ref, g_ref, beta_ref, ys, *, alpha):
    rows = y2_ref.shape[0]
    ntile = S5_WIDTH // LANES
    for s in range(S5_L):
        for c in range(ntile):
            lo = s * S5_WIDTH + c * LANES
            ys[c, pl.ds(s, rows, stride=S5_L), :] = y2_ref[:, lo:lo + LANES]
    y = jnp.concatenate([ys[c] for c in range(ntile)], axis=1) + d_ref[...] * u_ref[...]
    vg = _bdot(jax.nn.gelu(y), wglu_ref[...])
    half = vg.shape[1] // 2
    z = vg[:, :half] * jax.nn.sigmoid(vg[:, half:])
    o = _bdot(z, wout_ref[...])
    return _layer_norm(alpha * x_ref[...] + o, g_ref[...], beta_ref[...])


def _c_out_stage(y2, u, x, w, g, beta, alpha):
    n, dm = x.shape
    tm = min(ROW_TILE, n)
    return (functools.partial(_c_out_value, alpha=alpha),
            [(y2, tm // S5_L), (u, tm), (x, tm)], [w['d'], w['w_glu'], w['w_out'], g, beta],
            [pltpu.VMEM((u.shape[1] // LANES, tm, LANES), F32)], n, dm)


def _prep_ab(w_in, q_g, kv_g, w_uq, w_ukv, conv_w, conv_b, wa, ba, wx, bx, a_param, w_out):
    dm = w_in.shape[0]
    s0 = Q_LORA
    s1 = s0 + KV_LORA
    s2 = s1 + QK_ROPE
    s3 = s2 + LRU_WIDTH
    pad_lo = jnp.zeros((dm, QK_NOPE), F32)
    pad_hi = jnp.zeros((dm, HEAD_PAD - QK_HEAD), F32)
    w_in_p = jnp.concatenate([w_in[:, :s1], w_in[:, s2:s3], w_in[:, s3:],
                              pad_lo, w_in[:, s1:s2], pad_hi], axis=1).astype(BF16)
    wq = w_uq.reshape(Q_LORA, MLA_HEADS, QK_HEAD)
    wq = jnp.pad(wq, ((0, 0), (0, 0), (0, HEAD_PAD - QK_HEAD)))
    wq = wq.reshape(Q_LORA, MLA_HEADS * HEAD_PAD).astype(BF16)
    wkv = w_ukv.reshape(KV_LORA, MLA_HEADS, QK_NOPE + V_HEAD)
    wk = jnp.pad(wkv[:, :, :QK_NOPE], ((0, 0), (0, 0), (0, HEAD_PAD - QK_NOPE)))
    wk = wk.reshape(KV_LORA, MLA_HEADS * HEAD_PAD).astype(BF16)
    wv = jnp.pad(wkv[:, :, QK_NOPE:], ((0, 0), (0, 0), (0, HEAD_PAD - V_HEAD)))
    wv = wv.reshape(KV_LORA, MLA_HEADS * HEAD_PAD).astype(BF16)
    wukt = jnp.transpose(wkv[:, :, :QK_NOPE], (1, 2, 0)).astype(BF16)
    wuv_h = jnp.transpose(wkv[:, :, QK_NOPE:], (1, 0, 2)).astype(BF16)
    eye = jnp.eye(LRU_BLOCKS, dtype=F32)

    def blockdiag(wb):
        return jnp.einsum('hij,hk->hikj', wb, eye).reshape(LRU_WIDTH, LRU_WIDTH)

    w_ax = jnp.concatenate([blockdiag(wa), blockdiag(wx)], axis=1).astype(BF16)
    b_ax = jnp.concatenate([ba, bx])[None, :]
    att_w = MLA_HEADS * V_HEAD
    return dict(w_in=w_in_p, q_g=q_g[None, :], kv_g=kv_g[None, :], w_uq=wq, w_uk=wk, w_uv=wv,
                wukt=wukt, wuv_h=wuv_h, conv_w=conv_w, conv_b=conv_b[None, :], w_ax=w_ax,
                b_ax=b_ax, a_param=jax.nn.softplus(-a_param)[None, :],
                w_out_att=w_out[:att_w].astype(BF16), w_out_rec=w_out[att_w:].astype(BF16))


def _rope_tables(row_pos, tile_pos):
    half = QK_ROPE // 2
    inv = ROPE_THETA ** (-jnp.arange(half, dtype=F32) / half)
    inv_lane = jnp.concatenate([jnp.zeros((QK_NOPE,), F32), inv, inv,
                                jnp.zeros((HEAD_PAD - QK_HEAD,), F32)])
    ang_row = row_pos.astype(F32)[:, None] * inv_lane[None, :]
    ang_tile = tile_pos.astype(F32)[:, None, None] * inv_lane
    return jnp.cos(ang_row), jnp.sin(ang_row), jnp.cos(ang_tile), jnp.sin(ang_tile)


def _prep_router(w_group, w_expert):
    dm = w_group.shape[0]
    pad = jnp.zeros((dm, ROUTER_W - N_GROUPS - N_EXPERTS), F32)
    w_router = jnp.concatenate([w_group, w_expert, pad], axis=1)
    w_router_hi = w_router.astype(BF16)
    w_router_lo = (w_router - w_router_hi.astype(F32)).astype(BF16)
    return dict(w_router=jnp.concatenate([w_router_hi, w_router_lo], axis=1))


def _cmul(ar, ai, br, bi):
    return ar * br - ai * bi, ar * bi + ai * br


def _prep_s5(lam_re, lam_im, log_step, b_re, b_im, c_re, c_im):
    hi = lax.Precision.HIGHEST
    L, G, P, C = S5_L, S5_GROUPS, S5_STATE, S5_GROUP
    lr, li = lam_re, lam_im
    dt = jnp.exp(log_step)[:, None]
    mag = jnp.exp(lr * dt)
    ar, ai = mag * jnp.cos(li * dt), mag * jnp.sin(li * dt)
    den = lr * lr + li * li
    nr = ar - 1.0
    cr = (nr * lr + ai * li) / den
    ci = (ai * lr - nr * li) / den
    bbr = cr[..., None] * b_re - ci[..., None] * b_im
    bbi = cr[..., None] * b_im + ci[..., None] * b_re
    prs, pis = [jnp.ones_like(ar)], [jnp.zeros_like(ar)]
    for _ in range(L):
        nr_, ni_ = _cmul(prs[-1], pis[-1], ar, ai)
        prs.append(nr_)
        pis.append(ni_)
    pr = jnp.stack(prs)
    pi = jnp.stack(pis)
    xr = pr[:L, :, :, None] * bbr - pi[:L, :, :, None] * bbi
    xi = pr[:L, :, :, None] * bbi + pi[:L, :, :, None] * bbr
    kk = (jnp.einsum('gcp,kgpd->kgcd', c_re, xr, precision=hi)
          - jnp.einsum('gcp,kgpd->kgcd', c_im, xi, precision=hi))
    p1r = pr[1:, :, :, None]
    p1i = pi[1:, :, :, None]
    cre_t = jnp.transpose(c_re, (0, 2, 1))[None]
    cim_t = jnp.transpose(c_im, (0, 2, 1))[None]
    f_re = cre_t * p1r - cim_t * p1i
    f_im = -cre_t * p1i - cim_t * p1r

    nq, sg = S5_NSUPER, S5_SUPER
    eye = jnp.eye(sg, dtype=F32)
    dk = jnp.einsum('kqgcd,gh->kqgdhc', kk.reshape(L, nq, sg, C, C), eye)
    dk = dk.reshape(L, nq, LANES, LANES).astype(BF16)
    zero = jnp.zeros((nq, LANES, LANES), BF16)
    m = jnp.concatenate(
        [jnp.concatenate([dk[t - s] if t >= s else zero for t in range(L)], axis=2)
         for s in range(L)], axis=1)

    row_g = (jnp.arange(L * LANES) // C) % sg
    col_g = jnp.arange(sg * P) // P
    same_group = (row_g[:, None] == col_g[None, :])[None]

    def pack_rows(a):
        a = jnp.transpose(a.reshape(L, nq, sg, P, C), (1, 0, 4, 2, 3))
        a = a.reshape(nq, L, 1, C, sg * P).astype(BF16)
        a = jnp.broadcast_to(a, (nq, L, sg, C, sg * P)).reshape(nq, L * LANES, sg * P)
        return jnp.where(same_group, a, 0)

    def pack_e(x):
        return pack_rows(x[::-1])

    def pack_f(f):
        return jnp.swapaxes(pack_rows(f), 1, 2)

    alr, ali = pr[L].reshape(1, G * P), pi[L].reshape(1, G * P)
    qr, qi = [alr], [ali]
    for _ in range(SUBLANES - 1):
        nr_, ni_ = _cmul(qr[-1], qi[-1], alr, ali)
        qr.append(nr_)
        qi.append(ni_)
    apow_re = jnp.concatenate(qr, axis=0)
    apow_im = jnp.concatenate(qi, axis=0)
    step_re = jnp.concatenate([qr[0], qr[1], qr[3]], axis=0)
    step_im = jnp.concatenate([qi[0], qi[1], qi[3]], axis=0)
    return dict(m=m, e_re=pack_e(xr), e_im=pack_e(xi), f_re=pack_f(f_re),
                f_im=pack_f(f_im), apow_re=apow_re, apow_im=apow_im,
                step_re=step_re, step_im=step_im)


def _mixer_ab(x, pos0, past_ckv, past_krope, conv0, h0, w):
    b, t, dm = x.shape
    xf = x.reshape(b * t, dm)
    tm = min(ROW_TILE, b * t)
    assert tm % t == 0 or t % tm == 0
    row_pos = jnp.arange(tm) % t
    tile_pos = pos0 + (jnp.arange(b * t // tm) * tm) % t
    q, k, v, c_new, kr_new, xb, yb = _ab_in(xf, w, _rope_tables(row_pos, tile_pos))
    if past_ckv is None:
        att = _attn_prompt(q, k, v)
    else:
        att = _attn_sample(q, c_new, kr_new, past_ckv, past_krope, w['wukt'], w['wuv_h'])
    rec, conv_new, h_new = _lru(xb.reshape(b, t, -1), yb.reshape(b, t, -1), conv0,
                                h0[:, None, :], w)
    return (att, rec.reshape(b * t, -1), c_new.reshape(b, t, -1), kr_new.reshape(b, t, -1),
            conv_new, h_new[:, 0, :])


def _mixer_c(x_stage, b, t, h0_re, h0_im, w):
    nc = t // S5_L
    assert t % S5_L == 0 and nc % SUBLANES == 0
    x, u, u2 = _c_in(x_stage, w['w_in'])
    e_re, e_im = _s5_e(u2, w['tabs'])
    hp_re, hp_im, hf_re, hf_im = _s5_scan(
        e_re.reshape(b, nc, S5_STATE_W), e_im.reshape(b, nc, S5_STATE_W),
        h0_re.reshape(b, 1, S5_STATE_W), h0_im.reshape(b, 1, S5_STATE_W), w['tabs'], nc)
    y2 = _s5_y(u2, hp_re.reshape(b * nc, S5_STATE_W), hp_im.reshape(b * nc, S5_STATE_W),
               w['tabs'])
    return (x, y2, u, hf_re.reshape(b, S5_GROUPS, S5_STATE),
            hf_im.reshape(b, S5_GROUPS, S5_STATE))


def kernel(x_prompt, x_sample, cache_mla_ckv, cache_mla_krope, state_lru_conv, state_lru_h,
           state_s5_re, state_s5_im, w_in_ab, q_norm_g, kv_norm_g, w_uq, w_ukv,
           lru_conv_w, lru_conv_b, lru_w_a, lru_b_a, lru_w_x, lru_b_x, lru_a_param, w_out_ab,
           w_in_c, s5_lam_re, s5_lam_im, s5_log_step, s5_b_re, s5_b_im, s5_c_re, s5_c_im,
           s5_d, s5_w_glu, w_out_c, ln_mix_g, ln_mix_b, ln_ffn_g, ln_ffn_b,
           moe_w_group, moe_w_expert, moe_w_gate, moe_w_up, moe_w_down):
    bp, tp, dm = x_prompt.shape
    bs, ts, _ = x_sample.shape
    past = cache_mla_ckv.shape[2]
    depth = ln_mix_g.shape[0]
    alpha = (2 * depth) ** 0.25
    assert (past + ts - 1) // CHUNK <= past // CHUNK
    hp = _array_stage(x_prompt.reshape(bp * tp, dm))
    hs = _array_stage(x_sample.reshape(bs * ts, dm))
    moe_experts = dict(w_gate=moe_w_gate.astype(BF16), w_up=moe_w_up.astype(BF16),
                       w_down=moe_w_down.astype(BF16))
    outs = {k: [] for k in ('ckv_p', 'ckv_s', 'kr_p', 'kr_s', 'cv_p', 'cv_s', 'lh_p', 'lh_s',
                            's5r_p', 's5r_s', 's5i_p', 's5i_s')}
    for layer in range(depth):
        j = layer // 2
        g_mix, b_mix = ln_mix_g[layer][None, :], ln_mix_b[layer][None, :]
        g_ffn, b_ffn = ln_ffn_g[layer][None, :], ln_ffn_b[layer][None, :]
        if layer % 2 == 0:
            w = _prep_ab(w_in_ab[j], q_norm_g[j], kv_norm_g[j], w_uq[j], w_ukv[j], lru_conv_w[j],
                         lru_conv_b[j], lru_w_a[j], lru_b_a[j], lru_w_x[j], lru_b_x[j],
                         lru_a_param[j], w_out_ab[j])
            xp, xs = _materialize(hp, "res_ln"), _materialize(hs, "res_ln")
            att_p, rec_p, c1, k1, v1, h1 = _mixer_ab(
                xp.reshape(bp, tp, dm), 0, None, None, jnp.zeros((bp, CONV_WIDTH - 1, LRU_WIDTH), F32),
                jnp.zeros((bp, LRU_WIDTH), F32), w)
            att_s, rec_s, c2, k2, v2, h2 = _mixer_ab(
                xs.reshape(bs, ts, dm), past, cache_mla_ckv[j], cache_mla_krope[j], state_lru_conv[j],
                state_lru_h[j], w)
            outs['ckv_p'].append(c1); outs['ckv_s'].append(c2)
            outs['kr_p'].append(k1); outs['kr_s'].append(k2)
            outs['cv_p'].append(v1); outs['cv_s'].append(v2)
            outs['lh_p'].append(h1); outs['lh_s'].append(h2)
            stage_p = _out_ln_stage(att_p, rec_p, xp, w['w_out_att'],
                                    w['w_out_rec'], g_mix, b_mix, alpha)
            stage_s = _out_ln_stage(att_s, rec_s, xs, w['w_out_att'],
                                    w['w_out_rec'], g_mix, b_mix, alpha)
            wm = dict(moe_experts, **_prep_router(moe_w_group[layer], moe_w_expert[layer]))
            hp, _ = _moe(stage_p, "out_ln", wm, layer, g_ffn, b_ffn, alpha)
            hs, _ = _moe(stage_s, "out_ln", wm, layer, g_ffn, b_ffn, alpha)
        else:
            w = dict(w_in=w_in_c[j].astype(BF16), d=s5_d[j][None, :],
                     w_glu=s5_w_glu[j].astype(BF16), w_out=w_out_c[j].astype(BF16),
                     tabs=_prep_s5(s5_lam_re[j], s5_lam_im[j], s5_log_step[j], s5_b_re[j],
                                   s5_b_im[j], s5_c_re[j], s5_c_im[j]))
            wm = dict(moe_experts, **_prep_router(moe_w_group[layer], moe_w_expert[layer]))
            zero_state = jnp.zeros((bp, S5_GROUPS, S5_STATE), F32)
            xp, y_p, u_p, r1, i1 = _mixer_c(hp, bp, tp, zero_state, zero_state, w)
            stage_p = _c_out_stage(y_p, u_p, xp, w, g_mix, b_mix, alpha)
            hp, xs_in = _moe(stage_p, "c_out", wm, layer, g_ffn, b_ffn, alpha,
                             overlap=_materialize(hs, "res_ln"))
            xs, y_s, u_s, r2, i2 = _mixer_c(_array_stage(xs_in), bs, ts, state_s5_re[j],
                                            state_s5_im[j], w)
            outs['s5r_p'].append(r1); outs['s5r_s'].append(r2)
            outs['s5i_p'].append(i1); outs['s5i_s'].append(i2)
            stage_s = _c_out_stage(y_s, u_s, xs, w, g_mix, b_mix, alpha)
            hs, _ = _moe(stage_s, "c_out", wm, layer, g_ffn, b_ffn, alpha)
    y_prompt = _materialize(hp, "res_ln").reshape(bp, tp, dm)
    y_sample = _materialize(hs, "res_ln").reshape(bs, ts, dm)
    st = lambda k: jnp.stack(outs[k])
    return (y_prompt, y_sample, st('ckv_p'), st('ckv_s'), st('kr_p'), st('kr_s'), st('cv_p'), st('cv_s'),
            st('lh_p'), st('lh_s'), st('s5r_p'), st('s5r_s'), st('s5i_p'), st('s5i_s'))
```
